```python
import jax, jax.numpy as jnp
from jax import lax
import numpy as np

D_MODEL = 2048
BATCH = 16
SEQ = 2048
DEPTH = 2
DEC_BATCH = 8
DEC_SEQ = 64
PAST_LEN = 4096

CHUNK = 64
N_EVEN = (DEPTH + 1) // 2
N_ODD = DEPTH // 2
D_LRU = D_MODEL // 2
LRU_BLOCKS = 16
LRU_BLOCK = D_LRU // LRU_BLOCKS
CONV_W = 4
LRU_C = 8.0
D_RWKV = D_MODEL // 2
RWKV_HEAD = 64
RWKV_HEADS = D_RWKV // RWKV_HEAD
DECAY_LORA = 64
AAA_LORA = 64
GATE_LORA = 128
D_SHIFT = 3 * D_RWKV + DECAY_LORA + AAA_LORA + GATE_LORA
D_IN_EVEN = 2 * D_LRU + D_SHIFT
D_MIX_EVEN = D_LRU + D_RWKV
N_HEADS_C = 16
HEAD_DIM_C = 128
N_KV_C = 4
N_IDX_HEADS = 16
D_IDX = 64
TOPK_MAX = 256
D_MIX_ODD = N_HEADS_C * HEAD_DIM_C
ODD_SPLITS = [D_MIX_ODD, N_KV_C * HEAD_DIM_C, N_KV_C * HEAD_DIM_C, N_IDX_HEADS * D_IDX, D_IDX, N_IDX_HEADS]
D_IN_ODD = sum(ODD_SPLITS)
N_EXPERTS = 16
N_GROUPS = 4
EXPERTS_PER_GROUP = N_EXPERTS // N_GROUPS
TOP_K = 2
D_FF = D_MODEL // 2
MOE_BLOCK = 128
DN_ALPHA = (2 * DEPTH) ** 0.25
DN_BETA = (8 * DEPTH) ** -0.25
LN_EPS = 1e-5
GN_EPS = 64e-5

kernel_name = "hybrid_streaming_encoder_step"


def _cuts(sizes):
    return np.cumsum(sizes)[:-1].tolist()


def layer_norm(x, g, b):
    xf = x.astype(jnp.float32)
    mu = xf.mean(-1, keepdims=True)
    xc = xf - mu
    var = (xc * xc).mean(-1, keepdims=True)
    return (xc * lax.rsqrt(var + LN_EPS) * g.astype(jnp.float32) + b.astype(jnp.float32)).astype(x.dtype)


def rg_lru(xa, h0, gr_w, gr_b, gi_w, gi_b, lam):
    B, T, _ = xa.shape
    f32 = jnp.float32
    xb = xa.reshape(B, T, LRU_BLOCKS, LRU_BLOCK)
    r = jax.nn.sigmoid((jnp.einsum('btgi,gio->btgo', xb, gr_w).reshape(B, T, D_LRU) + gr_b).astype(f32))
    i = jax.nn.sigmoid((jnp.einsum('btgi,gio->btgo', xb, gi_w).reshape(B, T, D_LRU) + gi_b).astype(f32))
    log_a = -LRU_C * r * jax.nn.softplus(-lam.astype(f32))
    a = jnp.exp(log_a)
    u = jnp.sqrt(-jnp.expm1(2.0 * log_a)) * (i * xa.astype(f32))

    def step(h, au):
        h = au[0] * h + au[1]
        return h, h

    h_T, hs = lax.scan(step, h0.astype(f32), (jnp.moveaxis(a, 1, 0), jnp.moveaxis(u, 1, 0)))
    return jnp.moveaxis(hs, 0, 1), h_T


def rwkv7_mix(zs, S0, w0, w2, a0, a2, g2, k_k, k_a, r_k, gn_g, gn_b):
    B, T, _ = zs.shape
    f32 = jnp.float32
    r, k, v, wl, al, gl = jnp.split(zs, _cuts([D_RWKV, D_RWKV, D_RWKV, DECAY_LORA, AAA_LORA, GATE_LORA]), axis=-1)
    w_log = -jax.nn.softplus(-(w0 + jnp.tanh(wl) @ w2).astype(f32)) - 0.5
    decay = jnp.exp(-jnp.exp(w_log))
    a = jax.nn.sigmoid((a0 + al @ a2).astype(f32))
    g = jax.nn.sigmoid(gl) @ g2
    heads = lambda t: t.astype(f32).reshape(B, T, RWKV_HEADS, RWKV_HEAD)
    kk = heads(k * k_k)
    kk = kk / jnp.maximum(jnp.sqrt(jnp.sum(kk * kk, -1, keepdims=True)), 1e-12)
    k_mod = k.astype(f32) * (1.0 + (a - 1.0) * k_a.astype(f32))
    rh, kh, vh, ah, wh = heads(r), heads(k_mod), heads(v), heads(a), heads(decay)

    def step(S, inp):
        r_t, w_t, k_t, v_t, kk_t, a_t = inp
        sa = jnp.einsum('bhvk,bhk->bhv', S, -kk_t)
        S = S * w_t[:, :, None, :] + sa[..., None] * (kk_t * a_t)[:, :, None, :] + v_t[..., None] * k_t[:, :, None, :]
        return S, jnp.einsum('bhvk,bhk->bhv', S, r_t)

    tm = lambda t: jnp.moveaxis(t, 1, 0)
    S_T, ys = lax.scan(step, S0.astype(f32), (tm(rh), tm(wh), tm(kh), tm(vh), tm(kk), tm(ah)))
    y = tm(ys)
    mu = y.mean(-1, keepdims=True)
    var = jnp.square(y - mu).mean(-1, keepdims=True)
    y = (y - mu) * lax.rsqrt(var + GN_EPS) * gn_g.astype(f32).reshape(RWKV_HEADS, RWKV_HEAD) + gn_b.astype(f32).reshape(RWKV_HEADS, RWKV_HEAD)
    y = y + jnp.sum(rh * kh * r_k.astype(f32), -1, keepdims=True) * vh
    y = y.reshape(B, T, D_RWKV) * g.astype(f32)
    return y.astype(zs.dtype), S_T.astype(zs.dtype)


def even_layer(x, lru_h, lru_conv, wkv, shift, w_in, conv_w, conv_b, gr_w, gr_b, gi_w, gi_b, lam, shift_mu,
               w0, w2, a0, a2, g2, k_k, k_a, r_k, gn_g, gn_b, w_out):
    T = x.shape[1]
    z = x @ w_in
    ax, agate, zb = jnp.split(z, [D_LRU, 2 * D_LRU], axis=-1)
    xp = jnp.concatenate([lru_conv.astype(x.dtype), ax], axis=1)
    xa = conv_b + sum(conv_w[j] * xp[:, j:j + T] for j in range(CONV_W))
    hs, h_T = rg_lru(xa, lru_h, gr_w, gr_b, gi_w, gi_b, lam)
    y_a = hs.astype(x.dtype) * jax.nn.gelu(agate, approximate=True)
    zprev = jnp.concatenate([shift.astype(x.dtype), zb[:, :-1]], axis=1)
    zs = zb + shift_mu * (zprev - zb)
    y_b, S_T = rwkv7_mix(zs, wkv, w0, w2, a0, a2, g2, k_k, k_a, r_k, gn_g, gn_b)
    out = jnp.concatenate([y_a, y_b], axis=-1) @ w_out
    return out, (h_T.astype(x.dtype), xp[:, T:], S_T, zb[:, -1:])


def dsa_attend(q, qi, wi, keys, vals, kidx, limit, n_sel):
    B, Q = q.shape[:2]
    L = keys.shape[1]
    rel = jax.nn.relu(jnp.einsum('bqhd,bsd->bqhs', qi, kidx).astype(jnp.float32))
    iscore = jnp.einsum('bqh,bqhs->bqs', wi.astype(jnp.float32), rel)
    iscore = jnp.where(jnp.arange(L) < limit, iscore, -jnp.inf)
    _, sel = lax.top_k(iscore, n_sel)
    valid = sel < limit
    gather = jax.vmap(lambda t, idx: t[idx])
    ks = gather(keys, sel)
    vs = gather(vals, sel)
    qg = q.reshape(B, Q, N_KV_C, N_HEADS_C // N_KV_C, HEAD_DIM_C)
    logits = jnp.einsum('bqkgd,bqnkd->bqkgn', qg, ks).astype(jnp.float32) * (HEAD_DIM_C ** -0.5)
    logits = jnp.where(valid[:, :, None, None, :], logits, -jnp.inf)
    p = jax.nn.softmax(logits, axis=-1).astype(vs.dtype)
    o = jnp.einsum('bqkgn,bqnkd->bqkgd', p, vs)
    return o.reshape(B, Q, D_MIX_ODD)


def odd_layer(x, cache_k, cache_v, cache_kidx, w_in, w_out):
    B, T, _ = x.shape
    q, k, v, qi, ki, wi = jnp.split(x @ w_in, _cuts(ODD_SPLITS), axis=-1)
    q = q.reshape(B, T, N_HEADS_C, HEAD_DIM_C)
    k = k.reshape(B, T, N_KV_C, HEAD_DIM_C)
    v = v.reshape(B, T, N_KV_C, HEAD_DIM_C)
    qi = qi.reshape(B, T, N_IDX_HEADS, D_IDX)
    wi = wi * ((N_IDX_HEADS * D_IDX) ** -0.5)
    if cache_k is None:
        n_blk = T // CHUNK
        n_sel = min(TOPK_MAX, T // 4)
        blk = lambda t: jnp.moveaxis(t.reshape((B, n_blk, CHUNK) + t.shape[2:]), 1, 0)

        def one_block(args):
            c, qb, qib, wib = args
            return dsa_attend(qb, qib, wib, k, v, ki, (c + 1) * CHUNK, n_sel)

        o = lax.map(one_block, (jnp.arange(n_blk, dtype=jnp.int32), blk(q), blk(qi), blk(wi)))
        o = jnp.moveaxis(o, 0, 1).reshape(B, T, D_MIX_ODD)
    else:
        keys = jnp.concatenate([cache_k.astype(k.dtype), k], axis=1)
        vals = jnp.concatenate([cache_v.astype(v.dtype), v], axis=1)
        kidx = jnp.concatenate([cache_kidx.astype(ki.dtype), ki], axis=1)
        L = keys.shape[1]
        o = dsa_attend(q, qi, wi, keys, vals, kidx, L, min(TOPK_MAX, L // 4))
    return o @ w_out, (k, v, ki)


def route(xt, router_w, router_b):
    scores = jax.nn.sigmoid(xt.astype(jnp.float32) @ router_w.astype(jnp.float32))
    biased = scores + router_b.astype(jnp.float32)
    grp = biased.reshape(-1, N_GROUPS, EXPERTS_PER_GROUP)
    grp_score = lax.top_k(grp, TOP_K)[0].sum(-1)
    g = jnp.argmax(grp_score, axis=-1)
    in_grp = jnp.take_along_axis(grp, g[:, None, None], axis=1)[:, 0]
    _, j = lax.top_k(in_grp, TOP_K)
    expert = (g[:, None] * EXPERTS_PER_GROUP + j).astype(jnp.int32)
    w = jnp.take_along_axis(scores, expert, axis=1)
    return expert, w / w.sum(-1, keepdims=True)


def moe_ffn(x, router_w, router_b, w_gate, w_up, w_down):
    shp = x.shape
    xt = x.reshape(-1, shp[-1])
    T = xt.shape[0]
    expert, gate_w = route(xt, router_w, router_b)
    n_assign = T * TOP_K
    n_blocks = -(-n_assign // MOE_BLOCK) + N_EXPERTS
    rows = n_blocks * MOE_BLOCK
    flat_e = expert.reshape(-1)
    flat_t = jnp.arange(n_assign, dtype=jnp.int32) // TOP_K
    flat_w = gate_w.reshape(-1).astype(xt.dtype)
    order = jnp.argsort(flat_e)
    se = flat_e[order]
    counts = jnp.zeros(N_EXPERTS, jnp.int32).at[flat_e].add(1)
    padded = (counts + MOE_BLOCK - 1) // MOE_BLOCK * MOE_BLOCK
    pad_end = jnp.cumsum(padded)
    pad_start = pad_end - padded
    seg_start = jnp.cumsum(counts) - counts
    dest = pad_start[se] + jnp.arange(n_assign, dtype=jnp.int32) - seg_start[se]
    row_tok = jnp.zeros(rows, jnp.int32).at[dest].set(flat_t[order])
    row_w = jnp.zeros(rows, xt.dtype).at[dest].set(flat_w[order])
    block_exp = jnp.minimum(jnp.searchsorted(pad_end, jnp.arange(n_blocks, dtype=jnp.int32) * MOE_BLOCK, side='right'), N_EXPERTS - 1)

    def run_block(args):
        tok, wt, e = args
        xb = xt[tok]
        h = jax.nn.silu(xb @ w_gate[e]) * (xb @ w_up[e])
        return (h @ w_down[e]) * wt[:, None]

    yb = lax.map(run_block, (row_tok.reshape(n_blocks, MOE_BLOCK), row_w.reshape(n_blocks, MOE_BLOCK), block_exp))
    y = jax.ops.segment_sum(yb.reshape(rows, shp[-1]), row_tok, num_segments=T)
    return y.reshape(shp)


def run_trunk(x, even_states, odd_caches, even_w, odd_w, norm_w, moe_w):
    ln1_g, ln1_b, ln2_g, ln2_b = norm_w
    router_w, router_b, wg, wu, wd = moe_w
    new_even, new_odd = [], []
    for layer in range(DEPTH):
        j = layer // 2
        if layer % 2 == 0:
            out, new = even_layer(x, *(s[j] for s in even_states), *(w[j] for w in even_w))
            new_even.append(new)
        else:
            cache = (None, None, None) if odd_caches is None else tuple(c[j] for c in odd_caches)
            out, new = odd_layer(x, *cache, *(w[j] for w in odd_w))
            new_odd.append(new)
        x = layer_norm(DN_ALPHA * x + out, ln1_g[layer], ln1_b[layer])
        x = layer_norm(DN_ALPHA * x + moe_ffn(x, router_w, router_b, wg[layer], wu[layer], wd[layer]), ln2_g[layer], ln2_b[layer])
    new_even = [jnp.stack(t) for t in zip(*new_even)]
    new_odd = [jnp.stack(t) for t in zip(*new_odd)]
    return x, new_even, new_odd


def setup_inputs(seed: int = 0) -> dict:
    key = jax.random.key(seed)
    keys = jax.random.split(key, 64)
    counter = [0]

    def nxt():
        counter[0] += 1
        return keys[counter[0] - 1]

    nrm = lambda shape, scale=1.0: scale * jax.random.normal(nxt(), shape, jnp.float32)
    uni = lambda shape, lo, hi: jax.random.uniform(nxt(), shape, jnp.float32, lo, hi)
    u = uni((N_EVEN, D_LRU), 0.9, 0.999) ** (1.0 / LRU_C)
    lam = jnp.log(u) - jnp.log1p(-u)
    return {
        'x_prompt': nrm((BATCH, SEQ, D_MODEL)),
        'x_sample': nrm((DEC_BATCH, DEC_SEQ, D_MODEL)),
        'state_lru_h': nrm((N_EVEN, DEC_BATCH, D_LRU), 0.5),
        'state_lru_conv': nrm((N_EVEN, DEC_BATCH, CONV_W - 1, D_LRU)),
        'state_rwkv_wkv': nrm((N_EVEN, DEC_BATCH, RWKV_HEADS, RWKV_HEAD, RWKV_HEAD), 0.3),
        'state_rwkv_shift': nrm((N_EVEN, DEC_BATCH, 1, D_SHIFT)),
        'cache_k': nrm((N_ODD, DEC_BATCH, PAST_LEN, N_KV_C, HEAD_DIM_C)),
        'cache_v': nrm((N_ODD, DEC_BATCH, PAST_LEN, N_KV_C, HEAD_DIM_C)),
        'cache_kidx': nrm((N_ODD, DEC_BATCH, PAST_LEN, D_IDX)),
        'ev_w_in': nrm((N_EVEN, D_MODEL, D_IN_EVEN), D_MODEL ** -0.5),
        'ev_conv_w': nrm((N_EVEN, CONV_W, D_LRU), 0.5),
        'ev_conv_b': nrm((N_EVEN, D_LRU), 0.02),
        'ev_gate_r_w': nrm((N_EVEN, LRU_BLOCKS, LRU_BLOCK, LRU_BLOCK), LRU_BLOCK ** -0.5),
        'ev_gate_r_b': nrm((N_EVEN, D_LRU), 0.02),
        'ev_gate_i_w': nrm((N_EVEN, LRU_BLOCKS, LRU_BLOCK, LRU_BLOCK), LRU_BLOCK ** -0.5),
        'ev_gate_i_b': nrm((N_EVEN, D_LRU), 0.02),
        'ev_lambda': lam,
        'ev_shift_mu': uni((N_EVEN, D_SHIFT), 0.0, 1.0),
        'ev_w0': uni((N_EVEN, D_RWKV), -6.0, 1.0),
        'ev_w2': nrm((N_EVEN, DECAY_LORA, D_RWKV), 0.1),
        'ev_a0': nrm((N_EVEN, D_RWKV), 0.1),
        'ev_a2': nrm((N_EVEN, AAA_LORA, D_RWKV), 0.1),
        'ev_g2': nrm((N_EVEN, GATE_LORA, D_RWKV), GATE_LORA ** -0.5),
        'ev_k_k': 0.85 + nrm((N_EVEN, D_RWKV), 0.02),
        'ev_k_a': 1.0 + nrm((N_EVEN, D_RWKV), 0.02),
        'ev_r_k': nrm((N_EVEN, RWKV_HEADS, RWKV_HEAD), 0.1),
        'ev_gn_g': 1.0 + nrm((N_EVEN, D_RWKV), 0.02),
        'ev_gn_b': nrm((N_EVEN, D_RWKV), 0.02),
        'ev_w_out': nrm((N_EVEN, D_MIX_EVEN, D_MODEL), D_MIX_EVEN ** -0.5 * DN_BETA),
        'od_w_in': nrm((N_ODD, D_MODEL, D_IN_ODD), D_MODEL ** -0.5),
        'od_w_out': nrm((N_ODD, D_MIX_ODD, D_MODEL), D_MIX_ODD ** -0.5 * DN_BETA),
        'ln1_g': 1.0 + nrm((DEPTH, D_MODEL), 0.02),
        'ln1_b': nrm((DEPTH, D_MODEL), 0.02),
        'ln2_g': 1.0 + nrm((DEPTH, D_MODEL), 0.02),
        'ln2_b': nrm((DEPTH, D_MODEL), 0.02),
        'router_w': nrm((D_MODEL, N_EXPERTS), D_MODEL ** -0.5),
        'router_b': nrm((N_EXPERTS,), 0.01),
        'moe_w_gate': nrm((DEPTH, N_EXPERTS, D_MODEL, D_FF), D_MODEL ** -0.5),
        'moe_w_up': nrm((DEPTH, N_EXPERTS, D_MODEL, D_FF), D_MODEL ** -0.5),
        'moe_w_down': nrm((DEPTH, N_EXPERTS, D_FF, D_MODEL), D_FF ** -0.5 * DN_BETA),
    }


def reference(x_prompt, x_sample, state_lru_h, state_lru_conv, state_rwkv_wkv, state_rwkv_shift,
              cache_k, cache_v, cache_kidx,
              ev_w_in, ev_conv_w, ev_conv_b, ev_gate_r_w, ev_gate_r_b, ev_gate_i_w, ev_gate_i_b, ev_lambda,
              ev_shift_mu, ev_w0, ev_w2, ev_a0, ev_a2, ev_g2, ev_k_k, ev_k_a, ev_r_k, ev_gn_g, ev_gn_b, ev_w_out,
              od_w_in, od_w_out, ln1_g, ln1_b, ln2_g, ln2_b, router_w, router_b, moe_w_gate, moe_w_up, moe_w_down):
    even_w = (ev_w_in, ev_conv_w, ev_conv_b, ev_gate_r_w, ev_gate_r_b, ev_gate_i_w, ev_gate_i_b, ev_lambda,
              ev_shift_mu, ev_w0, ev_w2, ev_a0, ev_a2, ev_g2, ev_k_k, ev_k_a, ev_r_k, ev_gn_g, ev_gn_b, ev_w_out)
    odd_w = (od_w_in, od_w_out)
    norm_w = (ln1_g, ln1_b, ln2_g, ln2_b)
    moe_w = (router_w, router_b, moe_w_gate, moe_w_up, moe_w_down)
    B = x_prompt.shape[0]
    dt = x_prompt.dtype
    fresh = (jnp.zeros((N_EVEN, B, D_LRU), dt),
             jnp.zeros((N_EVEN, B, CONV_W - 1, D_LRU), dt),
             jnp.zeros((N_EVEN, B, RWKV_HEADS, RWKV_HEAD, RWKV_HEAD), dt),
             jnp.zeros((N_EVEN, B, 1, D_SHIFT), dt))
    y_prompt, pe, po = run_trunk(x_prompt, fresh, None, even_w, odd_w, norm_w, moe_w)
    y_sample, se, so = run_trunk(x_sample, (state_lru_h, state_lru_conv, state_rwkv_wkv, state_rwkv_shift),
                                 (cache_k, cache_v, cache_kidx), even_w, odd_w, norm_w, moe_w)
    p_h, p_conv, p_wkv, p_shift = pe
    p_k, p_v, p_ki = po
    s_h, s_conv, s_wkv, s_shift = se
    s_k, s_v, s_ki = so
    return (y_prompt, y_sample, p_h, p_conv, p_wkv, p_shift, p_k, p_v, p_ki, s_h, s_conv, s_wkv, s_shift, s_k, s_v, s_ki)
```

```python
import functools
import math

import jax
import jax.numpy as jnp
from jax import lax
from jax.experimental import pallas as pl
from jax.experimental.pallas import tpu as pltpu

F32 = jnp.float32
BF16 = jnp.bfloat16

CHUNK = 64
LRU_BLOCKS = 16
CONV_W = 4
LRU_C = 8.0
RWKV_HEAD = 64
DECAY_LORA = 64
AAA_LORA = 64
GATE_LORA = 128
N_HEADS_C = 16
HEAD_DIM_C = 128
N_KV_C = 4
N_IDX_HEADS = 16
D_IDX = 64
TOPK_MAX = 256
N_EXPERTS = 16
N_GROUPS = 4
EXPERTS_PER_GROUP = N_EXPERTS // N_GROUPS
TOP_K = 2
DEPTH = 2
DN_ALPHA = (2 * DEPTH) ** 0.25
LN_EPS = 1e-5
GN_EPS = 64e-5

LANES = 128
SUBLANES = 8
VMEM_LIMIT = 56 * 1024 * 1024
MOE_TM = 256


def _params(sem):
    return pltpu.CompilerParams(dimension_semantics=sem, vmem_limit_bytes=VMEM_LIMIT)


def _mm_kernel(x_ref, w_ref, o_ref, xb_ref):
    @pl.when(pl.program_id(1) == 0)
    def _():
        xb_ref[...] = x_ref[...].astype(BF16)

    o_ref[...] = jnp.dot(xb_ref[...], w_ref[...], preferred_element_type=F32)


def _pick(n, cands):
    for c in cands:
        if n % c == 0:
            return c
    return n


def matmul(x, w):
    M, K = x.shape
    N = w.shape[1]
    tm = _pick(M, (1024, 512, 256, 128, 64, 32, 16, 8))
    tn = _pick(N, (768, 1024, 1408, 512, 384, 256, 128))
    return pl.pallas_call(
        _mm_kernel,
        grid=(M // tm, N // tn),
        in_specs=[pl.BlockSpec((tm, K), lambda i, j: (i, 0)),
                  pl.BlockSpec((K, tn), lambda i, j: (0, j))],
        out_specs=pl.BlockSpec((tm, tn), lambda i, j: (i, j)),
        out_shape=jax.ShapeDtypeStruct((M, N), F32),
        scratch_shapes=[pltpu.VMEM((tm, K), BF16)],
        compiler_params=_params(("arbitrary", "arbitrary")),
    )(x, w)


def _expm1(x):
    e = jnp.exp(x)
    safe = jnp.where((e == 1.0) | (e == 0.0), 2.0, e)
    return jnp.where(e == 1.0, x, jnp.where(e == 0.0, -1.0, (e - 1.0) * x / jnp.log(safe)))


def _lru_kernel(xa_ref, rp_ref, ip_ref, c_ref, h0_ref, hs_ref, hT_ref, a_scr, u_scr, h_scr):
    tc = pl.program_id(1)

    @pl.when(tc == 0)
    def _():
        h_scr[...] = h0_ref[...]

    xa = xa_ref[...]
    r = jax.nn.sigmoid(rp_ref[...])
    i = jax.nn.sigmoid(ip_ref[...])
    log_a = c_ref[...][None] * r
    a_scr[...] = jnp.exp(log_a)
    u_scr[...] = jnp.sqrt(-_expm1(2.0 * log_a)) * (i * xa)

    def step(t, h):
        h = a_scr[t] * h + u_scr[t]
        hs_ref[t] = h
        return h

    h = lax.fori_loop(0, xa_ref.shape[0], step, h_scr[...])
    h_scr[...] = h
    hT_ref[...] = h


def lru_scan(xa, rp, ip, c, h0):
    T, B, C = xa.shape
    tc = _pick(T, (64, 32, 16, 8))
    cc = _pick(C, (512, 256, 128))
    blk = pl.BlockSpec((tc, B, cc), lambda j, t: (t, 0, j))
    st = pl.BlockSpec((B, cc), lambda j, t: (0, j))
    return pl.pallas_call(
        _lru_kernel,
        grid=(C // cc, T // tc),
        in_specs=[blk, blk, blk, pl.BlockSpec((1, cc), lambda j, t: (0, j)), st],
        out_specs=[blk, st],
        out_shape=[jax.ShapeDtypeStruct((T, B, C), F32), jax.ShapeDtypeStruct((B, C), F32)],
        scratch_shapes=[pltpu.VMEM((tc, B, cc), F32), pltpu.VMEM((tc, B, cc), F32),
                        pltpu.VMEM((B, cc), F32)],
        compiler_params=_params(("arbitrary", "arbitrary")),
    )(xa, rp, ip, c, h0)


def _colsum(x):
    n = x.shape[0] // SUBLANES
    p = x.reshape(n, SUBLANES, x.shape[1]).sum(axis=0)
    p = p + pltpu.roll(p, 4, axis=0)
    p = p + pltpu.roll(p, 2, axis=0)
    p = p + pltpu.roll(p, 1, axis=0)
    return p


def _bcast_rows(p, n):
    return jnp.broadcast_to(p[None], (n // SUBLANES,) + p.shape).reshape(n, p.shape[1])


def _rwkv_kernel(r_ref, w_ref, k_ref, v_ref, kk_ref, kka_ref, s0_ref, y_ref, sT_ref, s_scr):
    tc = pl.program_id(1)
    N = RWKV_HEAD

    @pl.when(tc == 0)
    def _():
        s_scr[...] = s0_ref[...]

    sub = lax.broadcasted_iota(jnp.int32, (SUBLANES, LANES), 0)

    def step(t, carry):
        r = r_ref[t]
        w = w_ref[t]
        k = k_ref[t]
        nkk = -kk_ref[t]
        kka = kka_ref[t]

        def vgroup(g, c2):
            yacc = jnp.zeros((SUBLANES, LANES), F32)
            for j in range(SUBLANES):
                vi = g * SUBLANES + j
                S = s_scr[vi]
                sa = _bcast_rows(_colsum(S * nkk), N)
                vb = jnp.broadcast_to(v_ref[t, pl.ds(vi, 1), :], (N, LANES))
                S = S * w + sa * kka + vb * k
                s_scr[vi] = S
                yacc = jnp.where(sub == j, _colsum(S * r), yacc)
            y_ref[t, pl.ds(pl.multiple_of(g * SUBLANES, SUBLANES), SUBLANES), :] = yacc
            return c2

        lax.fori_loop(0, N // SUBLANES, vgroup, 0)
        return carry

    lax.fori_loop(0, r_ref.shape[0], step, 0)

    @pl.when(tc == pl.num_programs(1) - 1)
    def _():
        sT_ref[...] = s_scr[...]


def rwkv_scan(r, w, k, v, kk, kka, s0):
    T, N, P = r.shape
    tc = _pick(T, (32, 16, 8))
    blk = pl.BlockSpec((tc, N, LANES), lambda p, t: (t, 0, p))
    st = pl.BlockSpec((N, N, LANES), lambda p, t: (0, 0, p))
    return pl.pallas_call(
        _rwkv_kernel,
        grid=(P // LANES, T // tc),
        in_specs=[blk] * 6 + [st],
        out_specs=[blk, st],
        out_shape=[jax.ShapeDtypeStruct((T, N, P), F32), jax.ShapeDtypeStruct((N, N, P), F32)],
        scratch_shapes=[pltpu.VMEM((N, N, LANES), F32)],
        compiler_params=_params(("arbitrary", "arbitrary")),
    )(r, w, k, v, kk, kka, s0)


def _dsa_kernel(q_ref, qi_ref, wi_ref, k_ref, v_ref, ki_ref, o_ref, *, past, n_sel, scale):
    c = pl.program_id(1)
    L = k_ref.shape[1]
    Q = q_ref.shape[1]
    limit = past + (c + 1) * Q

    qi = qi_ref[0].astype(BF16)
    kidx = ki_ref[0]
    wi = wi_ref[0]
    isc = jnp.zeros((Q, L), F32)
    for h in range(N_IDX_HEADS):
        s = lax.dot_general(qi[:, h * D_IDX:(h + 1) * D_IDX], kidx, (((1,), (1,)), ((), ())),
                            preferred_element_type=F32)
        isc = isc + wi[:, h:h + 1] * jnp.maximum(s, 0.0)
    col = lax.broadcasted_iota(jnp.int32, (Q, L), 1)
    ok = col < limit
    isc = jnp.where(ok, isc + 0.0, -jnp.inf)

    bits = pltpu.bitcast(isc, jnp.int32)
    key = bits ^ ((bits >> 31) & jnp.int32(0x7FFFFFFF))
    int_min = jnp.int32(-2 ** 31)

    def search(i, t):
        cand = t | lax.shift_left(jnp.int32(1), 31 - i)
        cnt = jnp.sum((key >= (cand ^ int_min)).astype(jnp.int32), axis=1, keepdims=True)
        return jnp.where(cnt >= n_sel, cand, t)

    thr = lax.fori_loop(0, 32, search, jnp.zeros((Q, 1), jnp.int32)) ^ int_min
    bias = jnp.where((key >= thr) & ok, 0.0, -jnp.inf).astype(F32)

    G = N_HEADS_C // N_KV_C
    q = q_ref[0]
    for j in range(N_KV_C):
        qg = jnp.concatenate(
            [q[:, (j * G + g) * HEAD_DIM_C:(j * G + g + 1) * HEAD_DIM_C] for g in range(G)], axis=0)
        kj = k_ref[0, :, j * HEAD_DIM_C:(j + 1) * HEAD_DIM_C]
        vj = v_ref[0, :, j * HEAD_DIM_C:(j + 1) * HEAD_DIM_C]
        lg = lax.dot_general(qg.astype(BF16), kj, (((1,), (1,)), ((), ())),
                             preferred_element_type=F32) * scale
        lg = (lg.reshape(G, Q, L) + bias[None]).reshape(G * Q, L)
        m = jnp.max(lg, axis=1, keepdims=True)
        e = jnp.exp(lg - m)
        den = jnp.sum(e, axis=1, keepdims=True)
        og = jnp.dot(e.astype(BF16), vj, preferred_element_type=F32) / den
        for g in range(G):
            o_ref[0, :, (j * G + g) * HEAD_DIM_C:(j * G + g + 1) * HEAD_DIM_C] = og[g * Q:(g + 1) * Q]


def dsa_attention(q, qi, wi, keys, vals, kidx, past, n_sel):
    B, T, _ = q.shape
    L = keys.shape[1]
    kern = functools.partial(_dsa_kernel, past=past, n_sel=n_sel, scale=HEAD_DIM_C ** -0.5)
    qspec = lambda d: pl.BlockSpec((1, CHUNK, d), lambda b, c: (b, c, 0))
    kspec = lambda d: pl.BlockSpec((1, L, d), lambda b, c: (b, 0, 0))
    return pl.pallas_call(
        kern,
        grid=(B, T // CHUNK),
        in_specs=[qspec(q.shape[2]), qspec(qi.shape[2]), qspec(wi.shape[2]),
                  kspec(keys.shape[2]), kspec(vals.shape[2]), kspec(kidx.shape[2])],
        out_specs=qspec(q.shape[2]),
        out_shape=jax.ShapeDtypeStruct(q.shape, F32),
        compiler_params=_params(("arbitrary", "arbitrary")),
    )(q, qi, wi, keys, vals, kidx)


def _router_kernel(x_ref, w_ref, b_ref, e_ref, g_ref):
    logits = lax.dot_general(w_ref[...], x_ref[...], (((1,), (1,)), ((), ())),
                             preferred_element_type=F32, precision=lax.Precision.HIGHEST)
    scores = jax.nn.sigmoid(logits)
    biased = scores + b_ref[...]
    tm = scores.shape[1]
    P = EXPERTS_PER_GROUP

    def ranks(rows):
        out = []
        for i in range(len(rows)):
            rk = jnp.zeros((1, tm), jnp.int32)
            for j in range(len(rows)):
                if j == i:
                    continue
                beat = (rows[j] > rows[i]) | ((rows[j] == rows[i]) & (j < i))
                rk = rk + beat.astype(jnp.int32)
            out.append(rk)
        return out

    b = [biased[i:i + 1] for i in range(N_EXPERTS)]
    s = [scores[i:i + 1] for i in range(N_EXPERTS)]
    grp_rank, grp_score = [], []
    for g in range(N_GROUPS):
        rows = b[g * P:(g + 1) * P]
        rk = ranks(rows)
        grp_rank.append(rk)
        top1 = sum(jnp.where(rk[i] == 0, rows[i], 0.0) for i in range(P))
        top2 = sum(jnp.where(rk[i] == 1, rows[i], 0.0) for i in range(P))
        grp_score.append(top1 + top2)
    gr = ranks(grp_score)
    e1 = jnp.zeros((1, tm), jnp.int32)
    e2 = jnp.zeros((1, tm), jnp.int32)
    w1 = jnp.zeros((1, tm), F32)
    w2 = jnp.zeros((1, tm), F32)
    for g in range(N_GROUPS):
        best = gr[g] == 0
        for i in range(P):
            is1 = best & (grp_rank[g][i] == 0)
            is2 = best & (grp_rank[g][i] == 1)
            e1 = jnp.where(is1, g * P + i, e1)
            e2 = jnp.where(is2, g * P + i, e2)
            w1 = jnp.where(is1, s[g * P + i], w1)
            w2 = jnp.where(is2, s[g * P + i], w2)
    tot = w1 + w2
    e_ref[...] = jnp.concatenate([e1, e2], axis=0)
    g_ref[...] = jnp.concatenate([w1 / tot, w2 / tot], axis=0)


def route(xt, router_w, router_b):
    T, D = xt.shape
    tm = _pick(T, (512, 256, 128))
    return pl.pallas_call(
        _router_kernel,
        grid=(T // tm,),
        in_specs=[pl.BlockSpec((tm, D), lambda i: (i, 0)),
                  pl.BlockSpec((N_EXPERTS, D), lambda i: (0, 0)),
                  pl.BlockSpec((N_EXPERTS, 1), lambda i: (0, 0))],
        out_specs=[pl.BlockSpec((TOP_K, tm), lambda i: (0, i)),
                   pl.BlockSpec((TOP_K, tm), lambda i: (0, i))],
        out_shape=[jax.ShapeDtypeStruct((TOP_K, T), jnp.int32),
                   jax.ShapeDtypeStruct((TOP_K, T), F32)],
        compiler_params=_params(("arbitrary",)),
    )(xt, router_w.T, router_b.reshape(N_EXPERTS, 1))


def _ffn_kernel(be_ref, x_ref, g_ref, wg_ref, wu_ref, wd_ref, o_ref):
    del be_ref
    x = x_ref[...].astype(BF16)
    a = jnp.dot(x, wg_ref[0], preferred_element_type=F32)
    u = jnp.dot(x, wu_ref[0], preferred_element_type=F32)
    h = (a * jax.nn.sigmoid(a)) * u
    o_ref[...] = jnp.dot(h.astype(BF16), wd_ref[0], preferred_element_type=F32) * g_ref[...]


def expert_ffn(xs, row_w, block_exp, wg, wu, wd):
    R, D = xs.shape
    F = wg.shape[2]
    nb = R // MOE_TM
    grid_spec = pltpu.PrefetchScalarGridSpec(
        num_scalar_prefetch=1,
        grid=(nb,),
        in_specs=[pl.BlockSpec((MOE_TM, D), lambda i, be: (i, 0)),
                  pl.BlockSpec((MOE_TM, 1), lambda i, be: (i, 0)),
                  pl.BlockSpec((1, D, F), lambda i, be: (be[i], 0, 0)),
                  pl.BlockSpec((1, D, F), lambda i, be: (be[i], 0, 0)),
                  pl.BlockSpec((1, F, D), lambda i, be: (be[i], 0, 0))],
        out_specs=pl.BlockSpec((MOE_TM, D), lambda i, be: (i, 0)),
    )
    return pl.pallas_call(
        _ffn_kernel,
        grid_spec=grid_spec,
        out_shape=jax.ShapeDtypeStruct((R, D), F32),
        compiler_params=_params(("arbitrary",)),
    )(block_exp, xs, row_w, wg, wu, wd)


def moe_ffn(x, router_w, router_b, wg, wu, wd):
    shp = x.shape
    xt = x.reshape(-1, shp[-1])
    T = xt.shape[0]
    expert, gate = route(xt, router_w, router_b)
    flat_e = expert.T.reshape(-1)
    flat_w = gate.T.reshape(-1)
    n_assign = T * TOP_K
    onehot = (flat_e[:, None] == jnp.arange(N_EXPERTS, dtype=jnp.int32)[None]).astype(jnp.int32)
    csum = jnp.cumsum(onehot, axis=0)
    rank = jnp.take_along_axis(csum, flat_e[:, None], axis=1)[:, 0] - 1
    counts = csum[-1]
    padded = (counts + MOE_TM - 1) // MOE_TM * MOE_TM
    pad_end = jnp.cumsum(padded)
    pad_start = pad_end - padded
    dest = pad_start[flat_e] + rank
    nb = -(-n_assign // MOE_TM) + N_EXPERTS
    rows = nb * MOE_TM
    row_tok = jnp.zeros(rows, jnp.int32).at[dest].set(jnp.arange(n_assign, dtype=jnp.int32) // TOP_K)
    row_w = jnp.zeros(rows, F32).at[dest].set(flat_w)
    block_exp = jnp.minimum(
        jnp.searchsorted(pad_end, jnp.arange(nb, dtype=jnp.int32) * MOE_TM, side='right'),
        N_EXPERTS - 1).astype(jnp.int32)
    yb = expert_ffn(xt[row_tok], row_w[:, None], block_exp, wg, wu, wd)
    d2 = dest.reshape(T, TOP_K)
    y = yb[d2[:, 0]] + yb[d2[:, 1]]
    return y.reshape(shp)


def layer_norm(x, g, b):
    mu = x.mean(-1, keepdims=True)
    xc = x - mu
    var = (xc * xc).mean(-1, keepdims=True)
    return xc * lax.rsqrt(var + LN_EPS) * g + b


def _block_diag(w):
    G, I, O = w.shape
    eye = jnp.eye(G, dtype=w.dtype)
    return (eye[:, None, :, None] * w[:, :, None, :]).reshape(G * I, G * O)


def even_layer(x, lru_h, lru_conv, wkv, shift, p):
    B, T, D = x.shape
    D_LRU = lru_h.shape[-1]
    D_RWKV = p['w0'].shape[0]
    H = D_RWKV // RWKV_HEAD
    N = RWKV_HEAD
    z = matmul(x.reshape(B * T, D), p['w_in']).reshape(B, T, -1)
    ax, agate, zb = z[..., :D_LRU], z[..., D_LRU:2 * D_LRU], z[..., 2 * D_LRU:]

    xp = jnp.concatenate([lru_conv, ax], axis=1)
    xa = p['conv_b'] + sum(p['conv_w'][j] * xp[:, j:j + T] for j in range(CONV_W))
    gates = matmul(xa.reshape(B * T, D_LRU), p['gate_w']).reshape(B, T, 2 * D_LRU)
    rp = gates[..., :D_LRU] + p['gate_r_b']
    ip = gates[..., D_LRU:] + p['gate_i_b']
    tmaj = lambda t: jnp.swapaxes(t, 0, 1)
    hs, h_T = lru_scan(tmaj(xa), tmaj(rp), tmaj(ip), p['lru_c'], lru_h)
    y_a = tmaj(hs) * jax.nn.gelu(agate, approximate=True)

    zprev = jnp.concatenate([shift, zb[:, :-1]], axis=1)
    zs = zb + p['shift_mu'] * (zprev - zb)
    c = 3 * D_RWKV
    r, k, v = zs[..., :D_RWKV], zs[..., D_RWKV:2 * D_RWKV], zs[..., 2 * D_RWKV:c]
    wl = zs[..., c:c + DECAY_LORA]
    al = zs[..., c + DECAY_LORA:c + DECAY_LORA + AAA_LORA]
    gl = zs[..., c + DECAY_LORA + AAA_LORA:]
    lora_in = jnp.concatenate([jnp.tanh(wl), al, jax.nn.sigmoid(gl)], axis=-1)
    lora = matmul(lora_in.reshape(B * T, -1), p['lora_w']).reshape(B, T, 3 * D_RWKV)
    w_log = -jax.nn.softplus(-(p['w0'] + lora[..., :D_RWKV])) - 0.5
    decay = jnp.exp(-jnp.exp(w_log))
    a = jax.nn.sigmoid(p['a0'] + lora[..., D_RWKV:2 * D_RWKV])
    g = lora[..., 2 * D_RWKV:]
    heads = lambda t: t.reshape(B, T, H, N)
    kk = heads(k * p['k_k'])
    kk = kk / jnp.maximum(jnp.sqrt(jnp.sum(kk * kk, -1, keepdims=True)), 1e-12)
    k_mod = k * (1.0 + (a - 1.0) * p['k_a'])
    rh, kh, vh, ah, wh = heads(r), heads(k_mod), heads(v), heads(a), heads(decay)
    lanes = lambda t: jnp.transpose(t, (1, 3, 0, 2)).reshape(T, N, B * H)
    s0 = jnp.transpose(wkv, (2, 3, 0, 1)).reshape(N, N, B * H)
    y, s_T = rwkv_scan(lanes(rh), lanes(wh), lanes(kh), lanes(vh), lanes(kk), lanes(kk * ah), s0)
    y = jnp.transpose(y.reshape(T, N, B, H), (2, 0, 3, 1))
    S_T = jnp.transpose(s_T.reshape(N, N, B, H), (2, 3, 0, 1))
    mu = y.mean(-1, keepdims=True)
    var = jnp.square(y - mu).mean(-1, keepdims=True)
    y = (y - mu) * lax.rsqrt(var + GN_EPS) * p['gn_g'].reshape(H, N) + p['gn_b'].reshape(H, N)
    y = y + jnp.sum(rh * kh * p['r_k'], -1, keepdims=True) * vh
    y_b = y.reshape(B, T, D_RWKV) * g

    mix = jnp.concatenate([y_a, y_b], axis=-1)
    out = matmul(mix.reshape(B * T, -1), p['w_out']).reshape(B, T, D)
    return out, (h_T, xp[:, T:], S_T, zb[:, -1:])


def odd_layer(x, cache, p):
    B, T, D = x.shape
    dq = N_HEADS_C * HEAD_DIM_C
    dkv = N_KV_C * HEAD_DIM_C
    dqi = N_IDX_HEADS * D_IDX
    z = matmul(x.reshape(B * T, D), p['w_in']).reshape(B, T, -1)
    q = z[..., :dq]
    k = z[..., dq:dq + dkv]
    v = z[..., dq + dkv:dq + 2 * dkv]
    qi = z[..., dq + 2 * dkv:dq + 2 * dkv + dqi]
    o = dq + 2 * dkv + dqi
    ki = z[..., o:o + D_IDX]
    wi = z[..., o + D_IDX:o + D_IDX + N_IDX_HEADS] * (dqi ** -0.5)
    if cache is None:
        keys, vals, kidx, past = k, v, ki, 0
        n_sel = min(TOPK_MAX, T // 4)
    else:
        ck, cv, cki = cache
        past = ck.shape[1]
        L = past + T
        n_sel = min(TOPK_MAX, L // 4)
        pad = (-L) % LANES
        cat = lambda c, n: jnp.concatenate(
            [c.reshape(B, past, -1), n, jnp.zeros((B, pad, n.shape[-1]), n.dtype)], axis=1)
        keys, vals, kidx = cat(ck, k), cat(cv, v), cat(cki, ki)
    o_att = dsa_attention(q, qi, wi, keys.astype(BF16), vals.astype(BF16), kidx.astype(BF16), past, n_sel)
    out = matmul(o_att.reshape(B * T, dq), p['w_out']).reshape(B, T, D)
    new = (k.reshape(B, T, N_KV_C, HEAD_DIM_C), v.reshape(B, T, N_KV_C, HEAD_DIM_C), ki)
    return out, new


def run_trunk(x, even_state, odd_cache, ev, od, norm_w, moe_w):
    ln1_g, ln1_b, ln2_g, ln2_b = norm_w
    router_w, router_b, wg, wu, wd = moe_w
    out, new_even = even_layer(x, *even_state, ev)
    x = layer_norm(DN_ALPHA * x + out, ln1_g[0], ln1_b[0])
    x = layer_norm(DN_ALPHA * x + moe_ffn(x, router_w, router_b, wg[0], wu[0], wd[0]), ln2_g[0], ln2_b[0])
    out, new_odd = odd_layer(x, odd_cache, od)
    x = layer_norm(DN_ALPHA * x + out, ln1_g[1], ln1_b[1])
    x = layer_norm(DN_ALPHA * x + moe_ffn(x, router_w, router_b, wg[1], wu[1], wd[1]), ln2_g[1], ln2_b[1])
    return x, new_even, new_odd


def kernel(x_prompt, x_sample, state_lru_h, state_lru_conv, state_rwkv_wkv, state_rwkv_shift, cache_k, cache_v, cache_kidx, ev_w_in, ev_conv_w, ev_conv_b, ev_gate_r_w, ev_gate_r_b, ev_gate_i_w, ev_gate_i_b, ev_lambda, ev_shift_mu, ev_w0, ev_w2, ev_a0, ev_a2, ev_g2, ev_k_k, ev_k_a, ev_r_k, ev_gn_g, ev_gn_b, ev_w_out, od_w_in, od_w_out, ln1_g, ln1_b, ln2_g, ln2_b, router_w, router_b, moe_w_gate, moe_w_up, moe_w_down):
    D_RWKV = ev_w0.shape[1]
    zero = lambda r, c: jnp.zeros((r, c), F32)
    lora_w = jnp.concatenate([
        jnp.concatenate([ev_w2[0], zero(DECAY_LORA, 2 * D_RWKV)], axis=1),
        jnp.concatenate([zero(AAA_LORA, D_RWKV), ev_a2[0], zero(AAA_LORA, D_RWKV)], axis=1),
        jnp.concatenate([zero(GATE_LORA, 2 * D_RWKV), ev_g2[0]], axis=1)], axis=0)
    ev = dict(
        w_in=ev_w_in[0].astype(BF16), conv_w=ev_conv_w[0], conv_b=ev_conv_b[0],
        gate_w=jnp.concatenate([_block_diag(ev_gate_r_w[0]), _block_diag(ev_gate_i_w[0])], axis=1).astype(BF16),
        gate_r_b=ev_gate_r_b[0], gate_i_b=ev_gate_i_b[0],
        lru_c=(-LRU_C * jax.nn.softplus(-ev_lambda[0]))[None],
        shift_mu=ev_shift_mu[0], w0=ev_w0[0], a0=ev_a0[0], lora_w=lora_w.astype(BF16),
        k_k=ev_k_k[0], k_a=ev_k_a[0], r_k=ev_r_k[0], gn_g=ev_gn_g[0], gn_b=ev_gn_b[0],
        w_out=ev_w_out[0].astype(BF16))
    n_in = od_w_in.shape[2]
    od = dict(w_in=jnp.pad(od_w_in[0], ((0, 0), (0, (-n_in) % LANES))).astype(BF16),
              w_out=od_w_out[0].astype(BF16))
    norm_w = (ln1_g, ln1_b, ln2_g, ln2_b)
    moe_w = (router_w, router_b, moe_w_gate.astype(BF16), moe_w_up.astype(BF16), moe_w_down.astype(BF16))

    B = x_prompt.shape[0]
    fresh = (jnp.zeros((B,) + state_lru_h.shape[2:], F32),
             jnp.zeros((B,) + state_lru_conv.shape[2:], F32),
             jnp.zeros((B,) + state_rwkv_wkv.shape[2:], F32),
             jnp.zeros((B,) + state_rwkv_shift.shape[2:], F32))
    y_p, pe, po = run_trunk(x_prompt, fresh, None, ev, od, norm_w, moe_w)
    y_s, se, so = run_trunk(
        x_sample, (state_lru_h[0], state_lru_conv[0], state_rwkv_wkv[0], state_rwkv_shift[0]),
        (cache_k[0], cache_v[0], cache_kidx[0]), ev, od, norm_w, moe_w)
    st = lambda t: t[None]
    return (y_p, y_s, *map(st, pe), *map(st, po), *map(st, se), *map(st, so))
```

```python
import functools
import math

import jax
import jax.numpy as jnp
from jax import lax
from jax.experimental import pallas as pl
from jax.experimental.pallas import tpu as pltpu

F32 = jnp.float32
BF16 = jnp.bfloat16

CHUNK = 64
LRU_BLOCKS = 16
CONV_W = 4
LRU_C = 8.0
RWKV_HEAD = 64
DECAY_LORA = 64
AAA_LORA = 64
GATE_LORA = 128
N_HEADS_C = 16
HEAD_DIM_C = 128
N_KV_C = 4
N_IDX_HEADS = 16
D_IDX = 64
TOPK_MAX = 256
N_EXPERTS = 16
N_GROUPS = 4
EXPERTS_PER_GROUP = N_EXPERTS // N_GROUPS
TOP_K = 2
DEPTH = 2
DN_ALPHA = (2 * DEPTH) ** 0.25
LN_EPS = 1e-5
GN_EPS = 64e-5

LANES = 128
SUBLANES = 8
VMEM_LIMIT = 56 * 1024 * 1024
MOE_TM = 256


def _params(sem):
    return pltpu.CompilerParams(dimension_semantics=sem, vmem_limit_bytes=VMEM_LIMIT)


def _mm_kernel(x_ref, w_ref, o_ref, xb_ref):
    @pl.when(pl.program_id(1) == 0)
    def _():
        xb_ref[...] = x_ref[...].astype(BF16)

    o_ref[...] = jnp.dot(xb_ref[...], w_ref[...], preferred_element_type=F32)


def _pick(n, cands):
    for c in cands:
        if n % c == 0:
            return c
    return n


def matmul(x, w):
    M, K = x.shape
    N = w.shape[1]
    tm = _pick(M, (1024, 512, 256, 128, 64, 32, 16, 8))
    tn = _pick(N, (768, 1024, 1408, 512, 384, 256, 128))
    return pl.pallas_call(
        _mm_kernel,
        grid=(M // tm, N // tn),
        in_specs=[pl.BlockSpec((tm, K), lambda i, j: (i, 0)),
                  pl.BlockSpec((K, tn), lambda i, j: (0, j))],
        out_specs=pl.BlockSpec((tm, tn), lambda i, j: (i, j)),
        out_shape=jax.ShapeDtypeStruct((M, N), F32),
        scratch_shapes=[pltpu.VMEM((tm, K), BF16)],
        compiler_params=_params(("arbitrary", "arbitrary")),
    )(x, w)


def _mm_ln_kernel(x_ref, w_ref, r_ref, g_ref, b_ref, o_ref):
    y = jnp.dot(x_ref[...].astype(BF16), w_ref[...], preferred_element_type=F32)
    h = DN_ALPHA * r_ref[...] + y
    mu = jnp.mean(h, axis=-1, keepdims=True)
    hc = h - mu
    var = jnp.mean(hc * hc, axis=-1, keepdims=True)
    o_ref[...] = hc * lax.rsqrt(var + LN_EPS) * g_ref[...] + b_ref[...]


def matmul_ln(x, w, resid, g, b):
    M, K = x.shape
    N = w.shape[1]
    tm = _pick(M, (256, 128, 64, 32, 16, 8))
    row = pl.BlockSpec((1, N), lambda i: (0, 0))
    return pl.pallas_call(
        _mm_ln_kernel,
        grid=(M // tm,),
        in_specs=[pl.BlockSpec((tm, K), lambda i: (i, 0)), pl.BlockSpec((K, N), lambda i: (0, 0)),
                  pl.BlockSpec((tm, N), lambda i: (i, 0)), row, row],
        out_specs=pl.BlockSpec((tm, N), lambda i: (i, 0)),
        out_shape=jax.ShapeDtypeStruct((M, N), F32),
        compiler_params=_params(("arbitrary",)),
    )(x, w, resid, g.reshape(1, N), b.reshape(1, N))


def _expm1(x):
    e = jnp.exp(x)
    safe = jnp.where((e == 1.0) | (e == 0.0), 2.0, e)
    return jnp.where(e == 1.0, x, jnp.where(e == 0.0, -1.0, (e - 1.0) * x / jnp.log(safe)))


def _lru_kernel(xa_ref, rp_ref, ip_ref, c_ref, h0_ref, hs_ref, hT_ref, a_scr, u_scr, h_scr):
    tc = pl.program_id(1)

    @pl.when(tc == 0)
    def _():
        h_scr[...] = h0_ref[...]

    xa = xa_ref[...]
    r = jax.nn.sigmoid(rp_ref[...])
    i = jax.nn.sigmoid(ip_ref[...])
    log_a = c_ref[...][None] * r
    a_scr[...] = jnp.exp(log_a)
    u_scr[...] = jnp.sqrt(-_expm1(2.0 * log_a)) * (i * xa)

    def step(t, h):
        h = a_scr[t] * h + u_scr[t]
        hs_ref[t] = h
        return h

    h = lax.fori_loop(0, xa_ref.shape[0], step, h_scr[...])
    h_scr[...] = h
    hT_ref[...] = h


def lru_scan(xa, rp, ip, c, h0):
    T, B, C = xa.shape
    tc = _pick(T, (64, 32, 16, 8))
    cc = _pick(C, (512, 256, 128))
    blk = pl.BlockSpec((tc, B, cc), lambda j, t: (t, 0, j))
    st = pl.BlockSpec((B, cc), lambda j, t: (0, j))
    return pl.pallas_call(
        _lru_kernel,
        grid=(C // cc, T // tc),
        in_specs=[blk, blk, blk, pl.BlockSpec((1, cc), lambda j, t: (0, j)), st],
        out_specs=[blk, st],
        out_shape=[jax.ShapeDtypeStruct((T, B, C), F32), jax.ShapeDtypeStruct((B, C), F32)],
        scratch_shapes=[pltpu.VMEM((tc, B, cc), F32), pltpu.VMEM((tc, B, cc), F32),
                        pltpu.VMEM((B, cc), F32)],
        compiler_params=_params(("arbitrary", "arbitrary")),
    )(xa, rp, ip, c, h0)


def _colsum(x):
    n = x.shape[0] // SUBLANES
    p = x.reshape(n, SUBLANES, x.shape[1]).sum(axis=0)
    p = p + pltpu.roll(p, 4, axis=0)
    p = p + pltpu.roll(p, 2, axis=0)
    p = p + pltpu.roll(p, 1, axis=0)
    return p


def _bcast_rows(p, n):
    return jnp.broadcast_to(p[None], (n // SUBLANES,) + p.shape).reshape(n, p.shape[1])


def _rwkv_kernel(r_ref, w_ref, k_ref, v_ref, kk_ref, kka_ref, s0_ref, y_ref, sT_ref, s_scr):
    tc = pl.program_id(1)
    N = RWKV_HEAD

    @pl.when(tc == 0)
    def _():
        s_scr[...] = s0_ref[...]

    sub = lax.broadcasted_iota(jnp.int32, (SUBLANES, LANES), 0)

    def step(t, carry):
        r = r_ref[t]
        w = w_ref[t]
        k = k_ref[t]
        nkk = -kk_ref[t]
        kka = kka_ref[t]

        def vgroup(g, c2):
            yacc = jnp.zeros((SUBLANES, LANES), F32)
            for j in range(SUBLANES):
                vi = g * SUBLANES + j
                S = s_scr[vi]
                sa = _bcast_rows(_colsum(S * nkk), N)
                vb = jnp.broadcast_to(v_ref[t, pl.ds(vi, 1), :], (N, LANES))
                S = S * w + sa * kka + vb * k
                s_scr[vi] = S
                yacc = jnp.where(sub == j, _colsum(S * r), yacc)
            y_ref[t, pl.ds(pl.multiple_of(g * SUBLANES, SUBLANES), SUBLANES), :] = yacc
            return c2

        lax.fori_loop(0, N // SUBLANES, vgroup, 0)
        return carry

    lax.fori_loop(0, r_ref.shape[0], step, 0)

    @pl.when(tc == pl.num_programs(1) - 1)
    def _():
        sT_ref[...] = s_scr[...]


def rwkv_scan(r, w, k, v, kk, kka, s0):
    T, N, P = r.shape
    tc = _pick(T, (32, 16, 8))
    blk = pl.BlockSpec((tc, N, LANES), lambda p, t: (t, 0, p))
    st = pl.BlockSpec((N, N, LANES), lambda p, t: (0, 0, p))
    return pl.pallas_call(
        _rwkv_kernel,
        grid=(P // LANES, T // tc),
        in_specs=[blk] * 6 + [st],
        out_specs=[blk, st],
        out_shape=[jax.ShapeDtypeStruct((T, N, P), F32), jax.ShapeDtypeStruct((N, N, P), F32)],
        scratch_shapes=[pltpu.VMEM((N, N, LANES), F32)],
        compiler_params=_params(("arbitrary", "arbitrary")),
    )(r, w, k, v, kk, kka, s0)


MASKED = -1e30


KEY_NEG_INF = -2139095041
QK_SPLIT = 2


def _tree_sum(parts):
    while len(parts) > 1:
        parts = [parts[i] + parts[i + 1] for i in range(0, len(parts) - 1, 2)] + parts[len(parts) & ~1:]
    return parts[0]


def _dsa_kernel(q_ref, qi_ref, wi_ref, k_ref, vt_ref, ki_ref, o_ref,
                key_scr, m_scr, l_scr, acc_scr, *, past, n_sel, scale, tk):
    qb = pl.program_id(1)
    QB = q_ref.shape[1]
    L = k_ref.shape[1]
    NP = N_HEADS_C // 2
    lane_chunk = lax.broadcasted_iota(jnp.int32, (1, QB), 1) // CHUNK
    limit = past + (qb * (QB // CHUNK) + lane_chunk + 1) * CHUNK
    n_tiles = (past + (qb + 1) * QB + tk - 1) // tk
    u = 2 if (L // tk) % 2 == 0 else 1
    n_count = (n_tiles + u - 1) // u
    row = lax.broadcasted_iota(jnp.int32, (tk, QB), 0)
    tile = lambda kt: pl.ds(pl.multiple_of(kt * tk, tk), tk)

    qi = qi_ref[0].astype(BF16)
    qi_pair = [jnp.concatenate([qi[:, (2 * p) * D_IDX:(2 * p + 1) * D_IDX],
                                qi[:, (2 * p + 1) * D_IDX:(2 * p + 2) * D_IDX]], axis=0)
               for p in range(N_IDX_HEADS // 2)]
    wi = wi_ref[0]

    def score_tile(kt, c):
        @pl.when(kt < n_tiles)
        def _():
            kidx = ki_ref[0, tile(kt), :]
            terms = []
            for p in range(N_IDX_HEADS // 2):
                s = lax.dot_general(kidx, qi_pair[p], (((1,), (1,)), ((), ())),
                                    preferred_element_type=F32)
                for e in range(2):
                    h = 2 * p + e
                    terms.append(wi[h:h + 1, :] * jnp.maximum(s[:, e * QB:(e + 1) * QB], 0.0))
            isc = terms[0]
            for t in terms[1:]:
                isc = isc + t
            ok = row + kt * tk < limit
            bits = lax.bitcast_convert_type(jnp.where(ok, isc + 0.0, -jnp.inf), jnp.int32)
            key_scr[tile(kt), :] = bits ^ ((bits >> 31) & jnp.int32(0x7FFFFFFF))

        @pl.when(kt >= n_tiles)
        def _():
            key_scr[tile(kt), :] = jnp.full((tk, QB), KEY_NEG_INF, jnp.int32)

        return c

    lax.fori_loop(0, n_count * u, score_tile, 0)

    int_min = jnp.int32(-2 ** 31)
    ct = tk * u

    def search(i, t):
        cand = t | lax.shift_left(jnp.int32(1), 31 - i)
        cs = cand ^ int_min

        def count_tile(kt, acc):
            keys = key_scr[pl.ds(pl.multiple_of(kt * ct, ct), ct), :]
            hit = jnp.where(keys >= cs, 1.0, 0.0)
            return acc + _tree_sum([hit[r * SUBLANES:(r + 1) * SUBLANES] for r in range(ct // SUBLANES)])

        acc = lax.fori_loop(0, n_count, count_tile, jnp.zeros((SUBLANES, QB), F32))
        cnt = jnp.sum(acc, axis=0, keepdims=True)
        return jnp.where(cnt >= n_sel, cand, t)

    thr = lax.fori_loop(0, 32, search, jnp.zeros((1, QB), jnp.int32)) ^ int_min

    G = N_HEADS_C // N_KV_C
    dh = HEAD_DIM_C
    q = q_ref[0]
    q_pair = [(jnp.concatenate([q[:, (2 * p) * dh:(2 * p + 1) * dh],
                                q[:, (2 * p + 1) * dh:(2 * p + 2) * dh]], axis=0) * scale).astype(BF16)
              for p in range(NP)]
    m_scr[...] = jnp.full(m_scr.shape, MASKED, F32)
    l_scr[...] = jnp.zeros(l_scr.shape, F32)
    acc_scr[...] = jnp.zeros(acc_scr.shape, F32)

    def att_tile(kt, c):
        sel = (key_scr[tile(kt), :] >= thr) & (row + kt * tk < limit)
        b = jnp.where(sel, 0.0, MASKED)
        b2 = jnp.concatenate([b, b], axis=1)
        def logits(p):
            j = (2 * p) // G
            kj = k_ref[0, tile(kt), j * dh:(j + 1) * dh]
            part = tk // QK_SPLIT
            return jnp.concatenate(
                [lax.dot_general(kj[r * part:(r + 1) * part], q_pair[p], (((1,), (1,)), ((), ())),
                                 preferred_element_type=F32) for r in range(QK_SPLIT)], axis=0)

        ahead = 2
        lg = {p: logits(p) for p in range(ahead)}
        for p in range(NP):
            j = (2 * p) // G
            x = lg.pop(p) + b2
            m = m_scr[p:p + 1, :]
            m_new = jnp.maximum(m, jnp.max(x, axis=0, keepdims=True))
            alpha = jnp.exp2(m - m_new)
            pr = jnp.exp2(x - m_new)
            l_scr[p:p + 1, :] = alpha * l_scr[p:p + 1, :] + jnp.sum(pr, axis=0, keepdims=True)
            m_scr[p:p + 1, :] = m_new
            vt = vt_ref[0, kt, j * dh:(j + 1) * dh, :]
            pv = jnp.dot(vt, pr.astype(BF16), preferred_element_type=F32)
            if p + ahead < NP:
                lg[p + ahead] = logits(p + ahead)
            acc_scr[p] = acc_scr[p] * alpha + pv
        return c

    lax.fori_loop(0, n_tiles, att_tile, 0)
    for p in range(NP):
        o2 = acc_scr[p] / l_scr[p:p + 1, :]
        for s in range(2):
            o_ref[0, :, (2 * p + s) * dh:(2 * p + s + 1) * dh] = o2[:, s * QB:(s + 1) * QB].T


def dsa_attention(z, q_col, qi_col, wi_t, keys, vals_t, kidx, past, n_sel, tk):
    B, T, _ = z.shape
    L = keys.shape[1]
    QB = LANES
    dq = N_HEADS_C * HEAD_DIM_C
    dqi = N_IDX_HEADS * D_IDX
    kern = functools.partial(_dsa_kernel, past=past, n_sel=n_sel,
                             scale=HEAD_DIM_C ** -0.5 * math.log2(math.e), tk=tk)
    return pl.pallas_call(
        kern,
        grid=(B, T // QB),
        in_specs=[pl.BlockSpec((1, QB, dq), lambda b, c: (b, c, q_col)),
                  pl.BlockSpec((1, QB, dqi), lambda b, c: (b, c, qi_col)),
                  pl.BlockSpec((1, N_IDX_HEADS, QB), lambda b, c: (b, 0, c)),
                  pl.BlockSpec((1, L, keys.shape[2]), lambda b, c: (b, 0, 0)),
                  pl.BlockSpec((1,) + vals_t.shape[1:], lambda b, c: (b, 0, 0, 0)),
                  pl.BlockSpec((1, L, kidx.shape[2]), lambda b, c: (b, 0, 0))],
        out_specs=pl.BlockSpec((1, QB, dq), lambda b, c: (b, c, 0)),
        out_shape=jax.ShapeDtypeStruct((B, T, dq), F32),
        scratch_shapes=[pltpu.VMEM((L, QB), jnp.int32),
                        pltpu.VMEM((N_HEADS_C // 2, 2 * QB), F32),
                        pltpu.VMEM((N_HEADS_C // 2, 2 * QB), F32),
                        pltpu.VMEM((N_HEADS_C // 2, HEAD_DIM_C, 2 * QB), F32)],
        compiler_params=_params(("arbitrary", "arbitrary")),
    )(z, z, wi_t, keys, vals_t, kidx)


def _router_kernel(x_ref, w_ref, b_ref, e_ref, g_ref):
    logits = lax.dot_general(w_ref[...], x_ref[...], (((1,), (1,)), ((), ())),
                             preferred_element_type=F32, precision=lax.Precision.HIGHEST)
    scores = jax.nn.sigmoid(logits)
    biased = scores + b_ref[...]
    tm = scores.shape[1]
    P = EXPERTS_PER_GROUP

    def ranks(rows):
        out = []
        for i in range(len(rows)):
            rk = jnp.zeros((1, tm), jnp.int32)
            for j in range(len(rows)):
                if j == i:
                    continue
                beat = (rows[j] > rows[i]) | ((rows[j] == rows[i]) & (j < i))
                rk = rk + beat.astype(jnp.int32)
            out.append(rk)
        return out

    b = [biased[i:i + 1] for i in range(N_EXPERTS)]
    s = [scores[i:i + 1] for i in range(N_EXPERTS)]
    grp_rank, grp_score = [], []
    for g in range(N_GROUPS):
        rows = b[g * P:(g + 1) * P]
        rk = ranks(rows)
        grp_rank.append(rk)
        top1 = sum(jnp.where(rk[i] == 0, rows[i], 0.0) for i in range(P))
        top2 = sum(jnp.where(rk[i] == 1, rows[i], 0.0) for i in range(P))
        grp_score.append(top1 + top2)
    gr = ranks(grp_score)
    e1 = jnp.zeros((1, tm), jnp.int32)
    e2 = jnp.zeros((1, tm), jnp.int32)
    w1 = jnp.zeros((1, tm), F32)
    w2 = jnp.zeros((1, tm), F32)
    for g in range(N_GROUPS):
        best = gr[g] == 0
        for i in range(P):
            is1 = best & (grp_rank[g][i] == 0)
            is2 = best & (grp_rank[g][i] == 1)
            e1 = jnp.where(is1, g * P + i, e1)
            e2 = jnp.where(is2, g * P + i, e2)
            w1 = jnp.where(is1, s[g * P + i], w1)
            w2 = jnp.where(is2, s[g * P + i], w2)
    tot = w1 + w2
    e_ref[...] = jnp.concatenate([e1, e2], axis=0)
    g_ref[...] = jnp.concatenate([w1 / tot, w2 / tot], axis=0)


def route(xt, router_w, router_b):
    T, D = xt.shape
    tm = _pick(T, (512, 256, 128))
    return pl.pallas_call(
        _router_kernel,
        grid=(T // tm,),
        in_specs=[pl.BlockSpec((tm, D), lambda i: (i, 0)),
                  pl.BlockSpec((N_EXPERTS, D), lambda i: (0, 0)),
                  pl.BlockSpec((N_EXPERTS, 1), lambda i: (0, 0))],
        out_specs=[pl.BlockSpec((TOP_K, tm), lambda i: (0, i)),
                   pl.BlockSpec((TOP_K, tm), lambda i: (0, i))],
        out_shape=[jax.ShapeDtypeStruct((TOP_K, T), jnp.int32),
                   jax.ShapeDtypeStruct((TOP_K, T), F32)],
        compiler_params=_params(("arbitrary",)),
    )(xt, router_w.T, router_b.reshape(N_EXPERTS, 1))


def _ffn_kernel(be_ref, x_ref, g_ref, wg_ref, wu_ref, wd_ref, o_ref):
    del be_ref
    x = x_ref[...].astype(BF16)
    a = jnp.dot(x, wg_ref[0], preferred_element_type=F32)
    u = jnp.dot(x, wu_ref[0], preferred_element_type=F32)
    h = (a * jax.nn.sigmoid(a)) * u
    o_ref[...] = jnp.dot(h.astype(BF16), wd_ref[0], preferred_element_type=F32) * g_ref[...]


def expert_ffn(xs, row_w, block_exp, wg, wu, wd):
    R, D = xs.shape
    F = wg.shape[2]
    nb = R // MOE_TM
    grid_spec = pltpu.PrefetchScalarGridSpec(
        num_scalar_prefetch=1,
        grid=(nb,),
        in_specs=[pl.BlockSpec((MOE_TM, D), lambda i, be: (i, 0)),
                  pl.BlockSpec((MOE_TM, 1), lambda i, be: (i, 0)),
                  pl.BlockSpec((1, D, F), lambda i, be: (be[i], 0, 0)),
                  pl.BlockSpec((1, D, F), lambda i, be: (be[i], 0, 0)),
                  pl.BlockSpec((1, F, D), lambda i, be: (be[i], 0, 0))],
        out_specs=pl.BlockSpec((MOE_TM, D), lambda i, be: (i, 0)),
    )
    return pl.pallas_call(
        _ffn_kernel,
        grid_spec=grid_spec,
        out_shape=jax.ShapeDtypeStruct((R, D), F32),
        compiler_params=_params(("arbitrary",)),
    )(block_exp, xs, row_w, wg, wu, wd)


def moe_ffn(x, router_w, router_b, wg, wu, wd):
    shp = x.shape
    xt = x.reshape(-1, shp[-1])
    T = xt.shape[0]
    expert, gate = route(xt, router_w, router_b)
    flat_e = expert.T.reshape(-1)
    flat_w = gate.T.reshape(-1)
    n_assign = T * TOP_K
    onehot = (flat_e[:, None] == jnp.arange(N_EXPERTS, dtype=jnp.int32)[None]).astype(jnp.int32)
    csum = jnp.cumsum(onehot, axis=0)
    rank = jnp.take_along_axis(csum, flat_e[:, None], axis=1)[:, 0] - 1
    counts = csum[-1]
    padded = (counts + MOE_TM - 1) // MOE_TM * MOE_TM
    pad_end = jnp.cumsum(padded)
    pad_start = pad_end - padded
    dest = pad_start[flat_e] + rank
    nb = -(-n_assign // MOE_TM) + N_EXPERTS
    rows = nb * MOE_TM
    row_tok = jnp.zeros(rows, jnp.int32).at[dest].set(jnp.arange(n_assign, dtype=jnp.int32) // TOP_K)
    row_w = jnp.zeros(rows, F32).at[dest].set(flat_w)
    block_exp = jnp.minimum(
        jnp.searchsorted(pad_end, jnp.arange(nb, dtype=jnp.int32) * MOE_TM, side='right'),
        N_EXPERTS - 1).astype(jnp.int32)
    yb = expert_ffn(xt[row_tok], row_w[:, None], block_exp, wg, wu, wd)
    d2 = dest.reshape(T, TOP_K)
    y = yb[d2[:, 0]] + yb[d2[:, 1]]
    return y.reshape(shp)


def layer_norm(x, g, b):
    mu = x.mean(-1, keepdims=True)
    xc = x - mu
    var = (xc * xc).mean(-1, keepdims=True)
    return xc * lax.rsqrt(var + LN_EPS) * g + b


def _block_diag(w):
    G, I, O = w.shape
    eye = jnp.eye(G, dtype=w.dtype)
    return (eye[:, None, :, None] * w[:, :, None, :]).reshape(G * I, G * O)


def even_layer(x, lru_h, lru_conv, wkv, shift, p, ln_g, ln_b):
    B, T, D = x.shape
    D_LRU = lru_h.shape[-1]
    D_RWKV = p['w0'].shape[0]
    H = D_RWKV // RWKV_HEAD
    N = RWKV_HEAD
    z = matmul(x.reshape(B * T, D), p['w_in']).reshape(B, T, -1)
    ax, agate, zb = z[..., :D_LRU], z[..., D_LRU:2 * D_LRU], z[..., 2 * D_LRU:]

    xp = jnp.concatenate([lru_conv, ax], axis=1)
    xa = p['conv_b'] + sum(p['conv_w'][j] * xp[:, j:j + T] for j in range(CONV_W))
    gates = matmul(xa.reshape(B * T, D_LRU), p['gate_w']).reshape(B, T, 2 * D_LRU)
    rp = gates[..., :D_LRU] + p['gate_r_b']
    ip = gates[..., D_LRU:] + p['gate_i_b']
    tmaj = lambda t: jnp.swapaxes(t, 0, 1)
    hs, h_T = lru_scan(tmaj(xa), tmaj(rp), tmaj(ip), p['lru_c'], lru_h)
    y_a = tmaj(hs) * jax.nn.gelu(agate, approximate=True)

    zprev = jnp.concatenate([shift, zb[:, :-1]], axis=1)
    zs = zb + p['shift_mu'] * (zprev - zb)
    c = 3 * D_RWKV
    r, k, v = zs[..., :D_RWKV], zs[..., D_RWKV:2 * D_RWKV], zs[..., 2 * D_RWKV:c]
    wl = zs[..., c:c + DECAY_LORA]
    al = zs[..., c + DECAY_LORA:c + DECAY_LORA + AAA_LORA]
    gl = zs[..., c + DECAY_LORA + AAA_LORA:]
    lora_in = jnp.concatenate([jnp.tanh(wl), al, jax.nn.sigmoid(gl)], axis=-1)
    lora = matmul(lora_in.reshape(B * T, -1), p['lora_w']).reshape(B, T, 3 * D_RWKV)
    w_log = -jax.nn.softplus(-(p['w0'] + lora[..., :D_RWKV])) - 0.5
    decay = jnp.exp(-jnp.exp(w_log))
    a = jax.nn.sigmoid(p['a0'] + lora[..., D_RWKV:2 * D_RWKV])
    g = lora[..., 2 * D_RWKV:]
    heads = lambda t: t.reshape(B, T, H, N)
    kk = heads(k * p['k_k'])
    kk = kk / jnp.maximum(jnp.sqrt(jnp.sum(kk * kk, -1, keepdims=True)), 1e-12)
    k_mod = k * (1.0 + (a - 1.0) * p['k_a'])
    rh, kh, vh, ah, wh = heads(r), heads(k_mod), heads(v), heads(a), heads(decay)
    lanes = lambda t: jnp.transpose(t, (1, 3, 0, 2)).reshape(T, N, B * H)
    s0 = jnp.transpose(wkv, (2, 3, 0, 1)).reshape(N, N, B * H)
    y, s_T = rwkv_scan(lanes(rh), lanes(wh), lanes(kh), lanes(vh), lanes(kk), lanes(kk * ah), s0)
    y = jnp.transpose(y.reshape(T, N, B, H), (2, 0, 3, 1))
    S_T = jnp.transpose(s_T.reshape(N, N, B, H), (2, 3, 0, 1))
    mu = y.mean(-1, keepdims=True)
    var = jnp.square(y - mu).mean(-1, keepdims=True)
    y = (y - mu) * lax.rsqrt(var + GN_EPS) * p['gn_g'].reshape(H, N) + p['gn_b'].reshape(H, N)
    y = y + jnp.sum(rh * kh * p['r_k'], -1, keepdims=True) * vh
    y_b = y.reshape(B, T, D_RWKV) * g

    mix = jnp.concatenate([y_a, y_b], axis=-1)
    x_new = matmul_ln(mix.reshape(B * T, -1), p['w_out'], x.reshape(B * T, D), ln_g, ln_b)
    return x_new.reshape(B, T, D), (h_T, xp[:, T:], S_T, zb[:, -1:])


def odd_layer(x, cache, p, ln_g, ln_b):
    B, T, D = x.shape
    dq = N_HEADS_C * HEAD_DIM_C
    dkv = N_KV_C * HEAD_DIM_C
    dqi = N_IDX_HEADS * D_IDX
    z = matmul(x.reshape(B * T, D), p['w_in']).reshape(B, T, -1)
    k = z[..., dq:dq + dkv]
    v = z[..., dq + dkv:dq + 2 * dkv]
    o = dq + 2 * dkv + dqi
    ki = z[..., o:o + D_IDX]
    wi = z[..., o + D_IDX:o + D_IDX + N_IDX_HEADS] * (dqi ** -0.5)
    Tp = -(-T // LANES) * LANES
    if cache is None:
        past = 0
        n_sel = min(TOPK_MAX, T // 4)
        parts = lambda c, n: [n]
    else:
        past = cache[0].shape[1]
        n_sel = min(TOPK_MAX, (past + T) // 4)
        parts = lambda c, n: [c.reshape(B, past, -1), n]
    L = -(-(past + Tp) // LANES) * LANES
    tk = _pick(L, (256, 384, 128))
    cat = lambda c, n: jnp.concatenate(
        parts(c, n) + [jnp.zeros((B, L - past - T, n.shape[-1]), n.dtype)], axis=1).astype(BF16)
    ck, cv, cki = cache if cache is not None else (None, None, None)
    keys, vals, kidx = cat(ck, k), cat(cv, v), cat(cki, ki)
    vals_t = jnp.swapaxes(vals.reshape(B, L // tk, tk, dkv), 2, 3)
    rowpad = lambda t: jnp.pad(t, ((0, 0), (0, Tp - T), (0, 0)))
    wi_t = jnp.swapaxes(rowpad(wi), 1, 2)
    o_att = dsa_attention(rowpad(z), 0, (dq + 2 * dkv) // dqi, wi_t, keys, vals_t, kidx, past, n_sel, tk)
    x_new = matmul_ln(o_att[:, :T].reshape(B * T, dq), p['w_out'], x.reshape(B * T, D), ln_g, ln_b)
    new = (k.reshape(B, T, N_KV_C, HEAD_DIM_C), v.reshape(B, T, N_KV_C, HEAD_DIM_C), ki)
    return x_new.reshape(B, T, D), new


def run_trunk(x, even_state, odd_cache, ev, od, norm_w, moe_w):
    ln1_g, ln1_b, ln2_g, ln2_b = norm_w
    router_w, router_b, wg, wu, wd = moe_w
    x, new_even = even_layer(x, *even_state, ev, ln1_g[0], ln1_b[0])
    x = layer_norm(DN_ALPHA * x + moe_ffn(x, router_w, router_b, wg[0], wu[0], wd[0]), ln2_g[0], ln2_b[0])
    x, new_odd = odd_layer(x, odd_cache, od, ln1_g[1], ln1_b[1])
    x = layer_norm(DN_ALPHA * x + moe_ffn(x, router_w, router_b, wg[1], wu[1], wd[1]), ln2_g[1], ln2_b[1])
    return x, new_even, new_odd


def kernel(x_prompt, x_sample, state_lru_h, state_lru_conv, state_rwkv_wkv, state_rwkv_shift, cache_k, cache_v, cache_kidx, ev_w_in, ev_conv_w, ev_conv_b, ev_gate_r_w, ev_gate_r_b, ev_gate_i_w, ev_gate_i_b, ev_lambda, ev_shift_mu, ev_w0, ev_w2, ev_a0, ev_a2, ev_g2, ev_k_k, ev_k_a, ev_r_k, ev_gn_g, ev_gn_b, ev_w_out, od_w_in, od_w_out, ln1_g, ln1_b, ln2_g, ln2_b, router_w, router_b, moe_w_gate, moe_w_up, moe_w_down):
    D_RWKV = ev_w0.shape[1]
    zero = lambda r, c: jnp.zeros((r, c), F32)
    lora_w = jnp.concatenate([
        jnp.concatenate([ev_w2[0], zero(DECAY_LORA, 2 * D_RWKV)], axis=1),
        jnp.concatenate([zero(AAA_LORA, D_RWKV), ev_a2[0], zero(AAA_LORA, D_RWKV)], axis=1),
        jnp.concatenate([zero(GATE_LORA, 2 * D_RWKV), ev_g2[0]], axis=1)], axis=0)
    ev = dict(
        w_in=ev_w_in[0].astype(BF16), conv_w=ev_conv_w[0], conv_b=ev_conv_b[0],
        gate_w=jnp.concatenate([_block_diag(ev_gate_r_w[0]), _block_diag(ev_gate_i_w[0])], axis=1).astype(BF16),
        gate_r_b=ev_gate_r_b[0], gate_i_b=ev_gate_i_b[0],
        lru_c=(-LRU_C * jax.nn.softplus(-ev_lambda[0]))[None],
        shift_mu=ev_shift_mu[0], w0=ev_w0[0], a0=ev_a0[0], lora_w=lora_w.astype(BF16),
        k_k=ev_k_k[0], k_a=ev_k_a[0], r_k=ev_r_k[0], gn_g=ev_gn_g[0], gn_b=ev_gn_b[0],
        w_out=ev_w_out[0].astype(BF16))
    n_in = od_w_in.shape[2]
    od = dict(w_in=jnp.pad(od_w_in[0], ((0, 0), (0, (-n_in) % LANES))).astype(BF16),
              w_out=od_w_out[0].astype(BF16))
    norm_w = (ln1_g, ln1_b, ln2_g, ln2_b)
    moe_w = (router_w, router_b, moe_w_gate.astype(BF16), moe_w_up.astype(BF16), moe_w_down.astype(BF16))

    B = x_prompt.shape[0]
    fresh = (jnp.zeros((B,) + state_lru_h.shape[2:], F32),
             jnp.zeros((B,) + state_lru_conv.shape[2:], F32),
             jnp.zeros((B,) + state_rwkv_wkv.shape[2:], F32),
             jnp.zeros((B,) + state_rwkv_shift.shape[2:], F32))
    y_p, pe, po = run_trunk(x_prompt, fresh, None, ev, od, norm_w, moe_w)
    y_s, se, so = run_trunk(
        x_sample, (state_lru_h[0], state_lru_conv[0], state_rwkv_wkv[0], state_rwkv_shift[0]),
        (cache_k[0], cache_v[0], cache_kidx[0]), ev, od, norm_w, moe_w)
    st = lambda t: t[None]
    return (y_p, y_s, *map(st, pe), *map(st, po), *map(st, se), *map(st, so))
```

```python
import functools
import math

import jax
import jax.numpy as jnp
from jax import lax
from jax.experimental import pallas as pl
from jax.experimental.pallas import tpu as pltpu

F32 = jnp.float32
BF16 = jnp.bfloat16

CHUNK = 64
LRU_BLOCKS = 16
CONV_W = 4
LRU_C = 8.0
RWKV_HEAD = 64
DECAY_LORA = 64
AAA_LORA = 64
GATE_LORA = 128
N_HEADS_C = 16
HEAD_DIM_C = 128
N_KV_C = 4
N_IDX_HEADS = 16
D_IDX = 64
TOPK_MAX = 256
N_EXPERTS = 16
N_GROUPS = 4
EXPERTS_PER_GROUP = N_EXPERTS // N_GROUPS
TOP_K = 2
DEPTH = 2
DN_ALPHA = (2 * DEPTH) ** 0.25
LN_EPS = 1e-5
GN_EPS = 64e-5

LANES = 128
SUBLANES = 8
VMEM_LIMIT = 56 * 1024 * 1024
MOE_TM = 256


def _params(sem):
    return pltpu.CompilerParams(dimension_semantics=sem, vmem_limit_bytes=VMEM_LIMIT)


def _mm_kernel(x_ref, w_ref, o_ref, xb_ref):
    @pl.when(pl.program_id(1) == 0)
    def _():
        xb_ref[...] = x_ref[...].astype(BF16)

    o_ref[...] = jnp.dot(xb_ref[...], w_ref[...], preferred_element_type=F32)


def _pick(n, cands):
    for c in cands:
        if n % c == 0:
            return c
    return n


def matmul(x, w):
    M, K = x.shape
    N = w.shape[1]
    tm = _pick(M, (1024, 512, 256, 128, 64, 32, 16, 8))
    tn = _pick(N, (768, 1024, 1408, 512, 384, 256, 128))
    return pl.pallas_call(
        _mm_kernel,
        grid=(M // tm, N // tn),
        in_specs=[pl.BlockSpec((tm, K), lambda i, j: (i, 0)),
                  pl.BlockSpec((K, tn), lambda i, j: (0, j))],
        out_specs=pl.BlockSpec((tm, tn), lambda i, j: (i, j)),
        out_shape=jax.ShapeDtypeStruct((M, N), F32),
        scratch_shapes=[pltpu.VMEM((tm, K), BF16)],
        compiler_params=_params(("arbitrary", "arbitrary")),
    )(x, w)


def _deepnorm(resid, y, g, b):
    h = DN_ALPHA * resid + y
    mu = jnp.mean(h, axis=-1, keepdims=True)
    hc = h - mu
    var = jnp.mean(hc * hc, axis=-1, keepdims=True)
    return hc * lax.rsqrt(var + LN_EPS) * g + b


def _mm_ln_kernel(x_ref, w_ref, r_ref, g_ref, b_ref, o_ref):
    y = jnp.dot(x_ref[...].astype(BF16), w_ref[...], preferred_element_type=F32)
    o_ref[...] = _deepnorm(r_ref[...], y, g_ref[...], b_ref[...])


def _mix_ln_kernel(ya_ref, yb_ref, gate_ref, w_ref, r_ref, g_ref, b_ref, o_ref):
    mix = jnp.concatenate([ya_ref[...], yb_ref[...] * gate_ref[...]], axis=1).astype(BF16)
    y = jnp.dot(mix, w_ref[...], preferred_element_type=F32)
    o_ref[...] = _deepnorm(r_ref[...], y, g_ref[...], b_ref[...])


def matmul_ln(x, w, resid, g, b):
    M, K = x.shape
    N = w.shape[1]
    tm = _pick(M, (256, 128, 64, 32, 16, 8))
    row = pl.BlockSpec((1, N), lambda i: (0, 0))
    return pl.pallas_call(
        _mm_ln_kernel,
        grid=(M // tm,),
        in_specs=[pl.BlockSpec((tm, K), lambda i: (i, 0)), pl.BlockSpec((K, N), lambda i: (0, 0)),
                  pl.BlockSpec((tm, N), lambda i: (i, 0)), row, row],
        out_specs=pl.BlockSpec((tm, N), lambda i: (i, 0)),
        out_shape=jax.ShapeDtypeStruct((M, N), F32),
        compiler_params=_params(("arbitrary",)),
    )(x, w, resid, g.reshape(1, N), b.reshape(1, N))


def mix_ln(ya, yb, gate, w, resid, g, b):
    M, Kh = ya.shape
    N = w.shape[1]
    tm = _pick(M, (256, 128, 64, 32, 16, 8))
    row = pl.BlockSpec((1, N), lambda i: (0, 0))
    half = pl.BlockSpec((tm, Kh), lambda i: (i, 0))
    return pl.pallas_call(
        _mix_ln_kernel,
        grid=(M // tm,),
        in_specs=[half, half, half, pl.BlockSpec((2 * Kh, N), lambda i: (0, 0)),
                  pl.BlockSpec((tm, N), lambda i: (i, 0)), row, row],
        out_specs=pl.BlockSpec((tm, N), lambda i: (i, 0)),
        out_shape=jax.ShapeDtypeStruct((M, N), F32),
        compiler_params=_params(("arbitrary",)),
    )(ya, yb, gate, w, resid, g.reshape(1, N), b.reshape(1, N))


def _expm1(x):
    e = jnp.exp(x)
    safe = jnp.where((e == 1.0) | (e == 0.0), 2.0, e)
    return jnp.where(e == 1.0, x, jnp.where(e == 0.0, -1.0, (e - 1.0) * x / jnp.log(safe)))


def _softplus(x):
    return jnp.maximum(x, 0.0) + jnp.log(1.0 + jnp.exp(-jnp.abs(x)))


HALO = SUBLANES


def _even_pre_kernel(ax_ref, ag_ref, zr_ref, zk_ref, zv_ref, zt_ref,
                     conv_ref, sr_ref, sk_ref, sv_ref, st_ref, h0_ref,
                     cw_ref, cb_ref, gw_ref, gb_ref, lc_ref, mu_ref,
                     w0_ref, a0_ref, w2_ref, a2_ref, g2_ref, kk_ref, ka_ref,
                     ya_ref, r_ref, w_ref, k_ref, v_ref, kkr_ref, a_ref, g_ref, hT_ref,
                     xp_scr, pr_scr, pk_scr, pv_scr, pt_scr, a_scr, u_scr, hs_scr, h_scr):
    i = pl.program_id(1)
    tm = ax_ref.shape[1]
    C = ax_ref.shape[2]
    prev = ((xp_scr, None), (pr_scr, sr_ref), (pk_scr, sk_ref), (pv_scr, sv_ref), (pt_scr, st_ref))

    @pl.when(i == 0)
    def _():
        xp_scr[0:HALO, :] = jnp.zeros((HALO, C), F32)
        xp_scr[HALO - (CONV_W - 1):HALO, :] = conv_ref[0]
        for scr, ref in prev[1:]:
            scr[0:HALO, :] = jnp.broadcast_to(ref[0], (HALO, scr.shape[1]))
        h_scr[...] = jnp.broadcast_to(h0_ref[0], h_scr.shape)

    cur = pl.ds(HALO, tm)
    xp_scr[cur, :] = ax_ref[0]
    pr_scr[cur, :] = zr_ref[0]
    pk_scr[cur, :] = zk_ref[0]
    pv_scr[cur, :] = zv_ref[0]
    pt_scr[cur, :] = zt_ref[0]

    xa = cb_ref[...]
    for j in range(CONV_W):
        xa = xa + cw_ref[j:j + 1, :] * xp_scr[pl.ds(HALO - (CONV_W - 1) + j, tm), :]
    gates = jnp.dot(xa.astype(BF16), gw_ref[...], preferred_element_type=F32) + gb_ref[...]
    log_a = lc_ref[...] * jax.nn.sigmoid(gates[:, :C])
    a_scr[...] = jnp.exp(log_a)
    u_scr[...] = jnp.sqrt(-_expm1(2.0 * log_a)) * (jax.nn.sigmoid(gates[:, C:]) * xa)

    def step(t, h):
        h = a_scr[pl.ds(t, 1), :] * h + u_scr[pl.ds(t, 1), :]
        hs_scr[pl.ds(t, 1), :] = h
        return h

    h = lax.fori_loop(0, tm, step, h_scr[0:1, :], unroll=8)
    h_scr[...] = jnp.broadcast_to(h, h_scr.shape)
    hT_ref[0] = h
    ya_ref[0] = hs_scr[...] * jax.nn.gelu(ag_ref[0], approximate=True)

    def shifted(scr, lo, hi):
        zb = scr[cur, :]
        return zb + mu_ref[:, lo:hi] * (scr[pl.ds(HALO - 1, tm), :] - zb)

    r = shifted(pr_scr, 0, C)
    k = shifted(pk_scr, C, 2 * C)
    v = shifted(pv_scr, 2 * C, 3 * C)
    tail = shifted(pt_scr, 3 * C, 3 * C + pt_scr.shape[1])
    wl = jnp.tanh(tail[:, :DECAY_LORA])
    al = tail[:, DECAY_LORA:DECAY_LORA + AAA_LORA]
    gl = jax.nn.sigmoid(tail[:, DECAY_LORA + AAA_LORA:])
    lora = lambda x, w_ref_: jnp.dot(x.astype(BF16), w_ref_[...], preferred_element_type=F32)
    w_log = -_softplus(-(w0_ref[...] + lora(wl, w2_ref))) - 0.5
    a = jax.nn.sigmoid(a0_ref[...] + lora(al, a2_ref))
    r_ref[0] = r
    w_ref[0] = jnp.exp(-jnp.exp(w_log))
    k_ref[0] = k * (1.0 + (a - 1.0) * ka_ref[...])
    v_ref[0] = v
    kkr_ref[0] = k * kk_ref[...]
    a_ref[0] = a
    g_ref[0] = lora(gl, g2_ref)

    for scr, _ in prev:
        scr[0:HALO, :] = scr[pl.ds(tm, HALO), :]


def even_pre(z, lru_conv, shift, lru_h, p):
    B, T, _ = z.shape
    C = lru_h.shape[-1]
    tail = DECAY_LORA + AAA_LORA + GATE_LORA
    tm = _pick(T, (128, 64, 32, 16, 8))
    zblk = lambda w, col: pl.BlockSpec((1, tm, w), lambda b, i: (b, i, col))
    sblk = lambda rows, w, col: pl.BlockSpec((1, rows, w), lambda b, i: (b, 0, col))
    full = lambda a: pl.BlockSpec(a.shape, lambda b, i: (0,) * a.ndim)
    weights = [p['conv_w'], p['conv_b'], p['gate_w'], p['gate_b'], p['lru_c'], p['shift_mu'],
               p['w0'], p['a0'], p['w2'], p['a2'], p['g2'], p['k_k'], p['k_a']]
    tok = jax.ShapeDtypeStruct((B, T, C), F32)
    return pl.pallas_call(
        _even_pre_kernel,
        grid=(B, T // tm),
        in_specs=[zblk(C, 0), zblk(C, 1), zblk(C, 2), zblk(C, 3), zblk(C, 4), zblk(tail, 5 * C // tail),
                  sblk(CONV_W - 1, C, 0), sblk(1, C, 0), sblk(1, C, 1), sblk(1, C, 2),
                  sblk(1, tail, 3 * C // tail), sblk(1, C, 0)] + [full(w) for w in weights],
        out_specs=[zblk(C, 0)] * 8 + [sblk(1, C, 0)],
        out_shape=[tok] * 8 + [jax.ShapeDtypeStruct((B, 1, C), F32)],
        scratch_shapes=[pltpu.VMEM((tm + HALO, C), F32)] * 4 + [pltpu.VMEM((tm + HALO, tail), F32)]
                       + [pltpu.VMEM((tm, C), F32)] * 3 + [pltpu.VMEM((SUBLANES, C), F32)],
        compiler_params=_params(("arbitrary", "arbitrary")),
    )(z, z, z, z, z, z, lru_conv, shift, shift, shift, shift, lru_h[:, None, :], *weights)


def _colsum(x):
    n = x.shape[0] // SUBLANES
    p = x.reshape(n, SUBLANES, x.shape[1]).sum(axis=0)
    p = p + pltpu.roll(p, 4, axis=0)
    p = p + pltpu.roll(p, 2, axis=0)
    p = p + pltpu.roll(p, 1, axis=0)
    return p


def _bcast_rows(p, n):
    return jnp.broadcast_to(p[None], (n // SUBLANES,) + p.shape).reshape(n, p.shape[1])


def _rwkv_kernel(r_ref, w_ref, k_ref, v_ref, kk_ref, a_ref, gng_ref, gnb_ref, rk_ref, s0_ref,
                 y_ref, sT_ref, s_scr):
    tc = pl.program_id(1)
    N = RWKV_HEAD

    @pl.when(tc == 0)
    def _():
        s_scr[...] = s0_ref[...]

    sub = lax.broadcasted_iota(jnp.int32, (SUBLANES, LANES), 0)

    def step(t, carry):
        r = r_ref[t]
        w = w_ref[t]
        k = k_ref[t]
        kk = kk_ref[t]
        kk = kk / jnp.maximum(_bcast_rows(jnp.sqrt(_colsum(kk * kk)), N), 1e-12)
        nkk = -kk
        kka = kk * a_ref[t]

        def vgroup(g, c2):
            yacc = jnp.zeros((SUBLANES, LANES), F32)
            for j in range(SUBLANES):
                vi = g * SUBLANES + j
                S = s_scr[vi]
                sa = _bcast_rows(_colsum(S * nkk), N)
                vb = jnp.broadcast_to(v_ref[t, pl.ds(vi, 1), :], (N, LANES))
                S = S * w + sa * kka + vb * k
                s_scr[vi] = S
                yacc = jnp.where(sub == j, _colsum(S * r), yacc)
            y_ref[t, pl.ds(pl.multiple_of(g * SUBLANES, SUBLANES), SUBLANES), :] = yacc
            return c2

        lax.fori_loop(0, N // SUBLANES, vgroup, 0)

        y = y_ref[t]
        d = y - _bcast_rows(_colsum(y), N) * (1.0 / N)
        var = _bcast_rows(_colsum(d * d), N) * (1.0 / N)
        bonus = _bcast_rows(_colsum(r * k * rk_ref[...]), N)
        y_ref[t] = d * lax.rsqrt(var + GN_EPS) * gng_ref[...] + gnb_ref[...] + bonus * v_ref[t]
        return carry

    lax.fori_loop(0, r_ref.shape[0], step, 0)

    @pl.when(tc == pl.num_programs(1) - 1)
    def _():
        sT_ref[...] = s_scr[...]


def rwkv_scan(r, w, k, v, kk_raw, a, gn_g, gn_b, r_k, s0):
    T, N, P = r.shape
    tc = _pick(T, (32, 16, 8))
    blk = pl.BlockSpec((tc, N, LANES), lambda p, t: (t, 0, p))
    par = pl.BlockSpec((N, LANES), lambda p, t: (0, p))
    st = pl.BlockSpec((N, N, LANES), lambda p, t: (0, 0, p))
    return pl.pallas_call(
        _rwkv_kernel,
        grid=(P // LANES, T // tc),
        in_specs=[blk] * 6 + [par] * 3 + [st],
        out_specs=[blk, st],
        out_shape=[jax.ShapeDtypeStruct((T, N, P), F32), jax.ShapeDtypeStruct((N, N, P), F32)],
        scratch_shapes=[pltpu.VMEM((N, N, LANES), F32)],
        compiler_params=_params(("arbitrary", "arbitrary")),
    )(r, w, k, v, kk_raw, a, gn_g, gn_b, r_k, s0)


MASKED = -1e30


KEY_NEG_INF = -2139095041
QK_SPLIT = 2


def _tree_sum(parts):
    while len(parts) > 1:
        parts = [parts[i] + parts[i + 1] for i in range(0, len(parts) - 1, 2)] + parts[len(parts) & ~1:]
    return parts[0]


def _dsa_kernel(q_ref, qi_ref, wi_ref, k_ref, vt_ref, ki_ref, o_ref,
                key_scr, m_scr, l_scr, acc_scr, *, past, n_sel, scale, tk):
    qb = pl.program_id(1)
    QB = q_ref.shape[1]
    L = k_ref.shape[1]
    NP = N_HEADS_C // 2
    lane_chunk = lax.broadcasted_iota(jnp.int32, (1, QB), 1) // CHUNK
    limit = past + (qb * (QB // CHUNK) + lane_chunk + 1) * CHUNK
    n_tiles = (past + (qb + 1) * QB + tk - 1) // tk
    u = 2 if (L // tk) % 2 == 0 else 1
    n_count = (n_tiles + u - 1) // u
    row = lax.broadcasted_iota(jnp.int32, (tk, QB), 0)
    tile = lambda kt: pl.ds(pl.multiple_of(kt * tk, tk), tk)

    qi = qi_ref[0].astype(BF16)
    qi_pair = [jnp.concatenate([qi[:, (2 * p) * D_IDX:(2 * p + 1) * D_IDX],
                                qi[:, (2 * p + 1) * D_IDX:(2 * p + 2) * D_IDX]], axis=0)
               for p in range(N_IDX_HEADS // 2)]
    wi = wi_ref[0]

    def score_tile(kt, c):
        @pl.when(kt < n_tiles)
        def _():
            kidx = ki_ref[0, tile(kt), :]
            terms = []
            for p in range(N_IDX_HEADS // 2):
                s = lax.dot_general(kidx, qi_pair[p], (((1,), (1,)), ((), ())),
                                    preferred_element_type=F32)
                for e in range(2):
                    h = 2 * p + e
                    terms.append(wi[h:h + 1, :] * jnp.maximum(s[:, e * QB:(e + 1) * QB], 0.0))
            isc = terms[0]
            for t in terms[1:]:
                isc = isc + t
            ok = row + kt * tk < limit
            bits = lax.bitcast_convert_type(jnp.where(ok, isc + 0.0, -jnp.inf), jnp.int32)
            key_scr[tile(kt), :] = bits ^ ((bits >> 31) & jnp.int32(0x7FFFFFFF))

        @pl.when(kt >= n_tiles)
        def _():
            key_scr[tile(kt), :] = jnp.full((tk, QB), KEY_NEG_INF, jnp.int32)

        return c

    lax.fori_loop(0, n_count * u, score_tile, 0)

    int_min = jnp.int32(-2 ** 31)
    ct = tk * u

    def search(i, t):
        cand = t | lax.shift_left(jnp.int32(1), 31 - i)
        cs = cand ^ int_min

        def count_tile(kt, acc):
            keys = key_scr[pl.ds(pl.multiple_of(kt * ct, ct), ct), :]
            hit = jnp.where(keys >= cs, 1.0, 0.0)
            return acc + _tree_sum([hit[r * SUBLANES:(r + 1) * SUBLANES] for r in range(ct // SUBLANES)])

        acc = lax.fori_loop(0, n_count, count_tile, jnp.zeros((SUBLANES, QB), F32))
        cnt = jnp.sum(acc, axis=0, keepdims=True)
        return jnp.where(cnt >= n_sel, cand, t)

    thr = lax.fori_loop(0, 32, search, jnp.zeros((1, QB), jnp.int32)) ^ int_min

    G = N_HEADS_C // N_KV_C
    dh = HEAD_DIM_C
    q = q_ref[0]
    q_pair = [(jnp.concatenate([q[:, (2 * p) * dh:(2 * p + 1) * dh],
                                q[:, (2 * p + 1) * dh:(2 * p + 2) * dh]], axis=0) * scale).astype(BF16)
              for p in range(NP)]
    m_scr[...] = jnp.full(m_scr.shape, MASKED, F32)
    l_scr[...] = jnp.zeros(l_scr.shape, F32)
    acc_scr[...] = jnp.zeros(acc_scr.shape, F32)

    def att_tile(kt, c):
        sel = (key_scr[tile(kt), :] >= thr) & (row + kt * tk < limit)
        b = jnp.where(sel, 0.0, MASKED)
        b2 = jnp.concatenate([b, b], axis=1)
        def logits(p):
            j = (2 * p) // G
            kj = k_ref[0, tile(kt), j * dh:(j + 1) * dh]
            part = tk // QK_SPLIT
            return jnp.concatenate(
                [lax.dot_general(kj[r * part:(r + 1) * part], q_pair[p], (((1,), (1,)), ((), ())),
                                 preferred_element_type=F32) for r in range(QK_SPLIT)], axis=0)

        ahead = 2
        lg = {p: logits(p) for p in range(ahead)}
        for p in range(NP):
            j = (2 * p) // G
            x = lg.pop(p) + b2
            m = m_scr[p:p + 1, :]
            m_new = jnp.maximum(m, jnp.max(x, axis=0, keepdims=True))
            alpha = jnp.exp2(m - m_new)
            pr = jnp.exp2(x - m_new)
            l_scr[p:p + 1, :] = alpha * l_scr[p:p + 1, :] + jnp.sum(pr, axis=0, keepdims=True)
            m_scr[p:p + 1, :] = m_new
            vt = vt_ref[0, kt, j * dh:(j + 1) * dh, :]
            pv = jnp.dot(vt, pr.astype(BF16), preferred_element_type=F32)
            if p + ahead < NP:
                lg[p + ahead] = logits(p + ahead)
            acc_scr[p] = acc_scr[p] * alpha + pv
        return c

    lax.fori_loop(0, n_tiles, att_tile, 0)
    for p in range(NP):
        o2 = acc_scr[p] / l_scr[p:p + 1, :]
        for s in range(2):
            o_ref[0, :, (2 * p + s) * dh:(2 * p + s + 1) * dh] = o2[:, s * QB:(s + 1) * QB].T


def dsa_attention(z, q_col, qi_col, wi_t, keys, vals_t, kidx, past, n_sel, tk):
    B, T, _ = z.shape
    L = keys.shape[1]
    QB = LANES
    dq = N_HEADS_C * HEAD_DIM_C
    dqi = N_IDX_HEADS * D_IDX
    kern = functools.partial(_dsa_kernel, past=past, n_sel=n_sel,
                             scale=HEAD_DIM_C ** -0.5 * math.log2(math.e), tk=tk)
    return pl.pallas_call(
        kern,
        grid=(B, T // QB),
        in_specs=[pl.BlockSpec((1, QB, dq), lambda b, c: (b, c, q_col)),
                  pl.BlockSpec((1, QB, dqi), lambda b, c: (b, c, qi_col)),
                  pl.BlockSpec((1, N_IDX_HEADS, QB), lambda b, c: (b, 0, c)),
                  pl.BlockSpec((1, L, keys.shape[2]), lambda b, c: (b, 0, 0)),
                  pl.BlockSpec((1,) + vals_t.shape[1:], lambda b, c: (b, 0, 0, 0)),
                  pl.BlockSpec((1, L, kidx.shape[2]), lambda b, c: (b, 0, 0))],
        out_specs=pl.BlockSpec((1, QB, dq), lambda b, c: (b, c, 0)),
        out_shape=jax.ShapeDtypeStruct((B, T, dq), F32),
        scratch_shapes=[pltpu.VMEM((L, QB), jnp.int32),
                        pltpu.VMEM((N_HEADS_C // 2, 2 * QB), F32),
                        pltpu.VMEM((N_HEADS_C // 2, 2 * QB), F32),
                        pltpu.VMEM((N_HEADS_C // 2, HEAD_DIM_C, 2 * QB), F32)],
        compiler_params=_params(("arbitrary", "arbitrary")),
    )(z, z, wi_t, keys, vals_t, kidx)


def _router_kernel(x_ref, w_ref, b_ref, e_ref, g_ref):
    logits = lax.dot_general(w_ref[...], x_ref[...], (((1,), (1,)), ((), ())),
                             preferred_element_type=F32, precision=lax.Precision.HIGHEST)
    scores = jax.nn.sigmoid(logits)
    biased = scores + b_ref[...]
    tm = scores.shape[1]
    P = EXPERTS_PER_GROUP

    def ranks(rows):
        out = []
        for i in range(len(rows)):
            rk = jnp.zeros((1, tm), jnp.int32)
            for j in range(len(rows)):
                if j == i:
                    continue
                beat = (rows[j] > rows[i]) | ((rows[j] == rows[i]) & (j < i))
                rk = rk + beat.astype(jnp.int32)
            out.append(rk)
        return out

    b = [biased[i:i + 1] for i in range(N_EXPERTS)]
    s = [scores[i:i + 1] for i in range(N_EXPERTS)]
    grp_rank, grp_score = [], []
    for g in range(N_GROUPS):
        rows = b[g * P:(g + 1) * P]
        rk = ranks(rows)
        grp_rank.append(rk)
        top1 = sum(jnp.where(rk[i] == 0, rows[i], 0.0) for i in range(P))
        top2 = sum(jnp.where(rk[i] == 1, rows[i], 0.0) for i in range(P))
        grp_score.append(top1 + top2)
    gr = ranks(grp_score)
    e1 = jnp.zeros((1, tm), jnp.int32)
    e2 = jnp.zeros((1, tm), jnp.int32)
    w1 = jnp.zeros((1, tm), F32)
    w2 = jnp.zeros((1, tm), F32)
    for g in range(N_GROUPS):
        best = gr[g] == 0
        for i in range(P):
            is1 = best & (grp_rank[g][i] == 0)
            is2 = best & (grp_rank[g][i] == 1)
            e1 = jnp.where(is1, g * P + i, e1)
            e2 = jnp.where(is2, g * P + i, e2)
            w1 = jnp.where(is1, s[g * P + i], w1)
            w2 = jnp.where(is2, s[g * P + i], w2)
    tot = w1 + w2
    e_ref[...] = jnp.concatenate([e1, e2], axis=0)
    g_ref[...] = jnp.concatenate([w1 / tot, w2 / tot], axis=0)


def route(xt, router_w, router_b):
    T, D = xt.shape
    tm = _pick(T, (512, 256, 128))
    return pl.pallas_call(
        _router_kernel,
        grid=(T // tm,),
        in_specs=[pl.BlockSpec((tm, D), lambda i: (i, 0)),
                  pl.BlockSpec((N_EXPERTS, D), lambda i: (0, 0)),
                  pl.BlockSpec((N_EXPERTS, 1), lambda i: (0, 0))],
        out_specs=[pl.BlockSpec((TOP_K, tm), lambda i: (0, i)),
                   pl.BlockSpec((TOP_K, tm), lambda i: (0, i))],
        out_shape=[jax.ShapeDtypeStruct((TOP_K, T), jnp.int32),
                   jax.ShapeDtypeStruct((TOP_K, T), F32)],
        compiler_params=_params(("arbitrary",)),
    )(xt, router_w.T, router_b.reshape(N_EXPERTS, 1))


def _ffn_kernel(be_ref, x_ref, wg_ref, wu_ref, wd_ref, o_ref):
    del be_ref
    x = x_ref[...].astype(BF16)
    a = jnp.dot(x, wg_ref[0], preferred_element_type=F32)
    u = jnp.dot(x, wu_ref[0], preferred_element_type=F32)
    h = (a * jax.nn.sigmoid(a)) * u
    o_ref[...] = jnp.dot(h.astype(BF16), wd_ref[0], preferred_element_type=F32)


def expert_ffn(xs, block_exp, wg, wu, wd):
    R, D = xs.shape
    F = wg.shape[2]
    nb = R // MOE_TM
    grid_spec = pltpu.PrefetchScalarGridSpec(
        num_scalar_prefetch=1,
        grid=(nb,),
        in_specs=[pl.BlockSpec((MOE_TM, D), lambda i, be: (i, 0)),
                  pl.BlockSpec((1, D, F), lambda i, be: (be[i], 0, 0)),
                  pl.BlockSpec((1, D, F), lambda i, be: (be[i], 0, 0)),
                  pl.BlockSpec((1, F, D), lambda i, be: (be[i], 0, 0))],
        out_specs=pl.BlockSpec((MOE_TM, D), lambda i, be: (i, 0)),
    )
    return pl.pallas_call(
        _ffn_kernel,
        grid_spec=grid_spec,
        out_shape=jax.ShapeDtypeStruct((R, D), F32),
        compiler_params=_params(("arbitrary",)),
    )(block_exp, xs, wg, wu, wd)


def moe_ffn(x, router_w, router_b, wg, wu, wd):
    shp = x.shape
    xt = x.reshape(-1, shp[-1])
    T = xt.shape[0]
    expert, gate = route(xt, router_w, router_b)
    flat_e = expert.T.reshape(-1)
    n_assign = T * TOP_K
    onehot = (flat_e[:, None] == jnp.arange(N_EXPERTS, dtype=jnp.int32)[None]).astype(jnp.int32)
    csum = jnp.cumsum(onehot, axis=0)
    rank = jnp.take_along_axis(csum, flat_e[:, None], axis=1)[:, 0] - 1
    counts = csum[-1]
    padded = (counts + MOE_TM - 1) // MOE_TM * MOE_TM
    pad_end = jnp.cumsum(padded)
    pad_start = pad_end - padded
    dest = pad_start[flat_e] + rank
    nb = -(-n_assign // MOE_TM) + N_EXPERTS
    rows = nb * MOE_TM
    row_tok = jnp.zeros(rows, jnp.int32).at[dest].set(jnp.arange(n_assign, dtype=jnp.int32) // TOP_K)
    block_exp = jnp.minimum(
        jnp.searchsorted(pad_end, jnp.arange(nb, dtype=jnp.int32) * MOE_TM, side='right'),
        N_EXPERTS - 1).astype(jnp.int32)
    yb = expert_ffn(xt[row_tok], block_exp, wg, wu, wd)
    d2 = dest.reshape(T, TOP_K)
    y = yb[d2[:, 0]] * gate[0][:, None] + yb[d2[:, 1]] * gate[1][:, None]
    return y.reshape(shp)


def layer_norm(x, g, b):
    mu = x.mean(-1, keepdims=True)
    xc = x - mu
    var = (xc * xc).mean(-1, keepdims=True)
    return xc * lax.rsqrt(var + LN_EPS) * g + b


def _block_diag(w):
    G, I, O = w.shape
    eye = jnp.eye(G, dtype=w.dtype)
    return (eye[:, None, :, None] * w[:, :, None, :]).reshape(G * I, G * O)


def even_layer(x, lru_h, lru_conv, wkv, shift, p, ln_g, ln_b):
    B, T, D = x.shape
    D_LRU = lru_h.shape[-1]
    D_RWKV = p['w0'].shape[1]
    H = D_RWKV // RWKV_HEAD
    N = RWKV_HEAD
    z = matmul(x.reshape(B * T, D), p['w_in']).reshape(B, T, -1)
    y_a, r, decay, k_mod, v, kk_raw, a, g, h_T = even_pre(z, lru_conv, shift, lru_h, p)

    lanes = lambda t: jnp.transpose(t.reshape(B, T, H, N), (1, 3, 0, 2)).reshape(T, N, B * H)
    per_chain = lambda w: jnp.tile(w.reshape(H, N).T, (1, B))
    s0 = jnp.transpose(wkv, (2, 3, 0, 1)).reshape(N, N, B * H)
    y, s_T = rwkv_scan(lanes(r), lanes(decay), lanes(k_mod), lanes(v), lanes(kk_raw), lanes(a),
                       per_chain(p['gn_g']), per_chain(p['gn_b']), per_chain(p['r_k']), s0)
    y = jnp.transpose(y.reshape(T, N, B, H), (2, 0, 3, 1)).reshape(B * T, D_RWKV)
    S_T = jnp.transpose(s_T.reshape(N, N, B, H), (2, 3, 0, 1))

    x_new = mix_ln(y_a.reshape(B * T, D_LRU), y, g.reshape(B * T, D_RWKV), p['w_out'],
                   x.reshape(B * T, D), ln_g, ln_b)
    conv_state = jnp.concatenate([lru_conv, z[..., :D_LRU]], axis=1)[:, T:]
    return x_new.reshape(B, T, D), (h_T[:, 0], conv_state, S_T, z[:, -1:, 2 * D_LRU:])


def odd_layer(x, cache, p, ln_g, ln_b):
    B, T, D = x.shape
    dq = N_HEADS_C * HEAD_DIM_C
    dkv = N_KV_C * HEAD_DIM_C
    dqi = N_IDX_HEADS * D_IDX
    z = matmul(x.reshape(B * T, D), p['w_in']).reshape(B, T, -1)
    k = z[..., dq:dq + dkv]
    v = z[..., dq + dkv:dq + 2 * dkv]
    o = dq + 2 * dkv + dqi
    ki = z[..., o:o + D_IDX]
    wi = z[..., o + D_IDX:o + D_IDX + N_IDX_HEADS] * (dqi ** -0.5)
    Tp = -(-T // LANES) * LANES
    if cache is None:
        past = 0
        n_sel = min(TOPK_MAX, T // 4)
        parts = lambda c, n: [n]
    else:
        past = cache[0].shape[1]
        n_sel = min(TOPK_MAX, (past + T) // 4)
        parts = lambda c, n: [c.reshape(B, past, -1), n]
    L = -(-(past + Tp) // LANES) * LANES
    tk = _pick(L, (256, 384, 128))
    cat = lambda c, n: jnp.concatenate(
        parts(c, n) + [jnp.zeros((B, L - past - T, n.shape[-1]), n.dtype)], axis=1).astype(BF16)
    ck, cv, cki = cache if cache is not None else (None, None, None)
    keys, vals, kidx = cat(ck, k), cat(cv, v), cat(cki, ki)
    vals_t = jnp.swapaxes(vals.reshape(B, L // tk, tk, dkv), 2, 3)
    rowpad = lambda t: jnp.pad(t, ((0, 0), (0, Tp - T), (0, 0)))
    wi_t = jnp.swapaxes(rowpad(wi), 1, 2)
    o_att = dsa_attention(rowpad(z), 0, (dq + 2 * dkv) // dqi, wi_t, keys, vals_t, kidx, past, n_sel, tk)
    x_new = matmul_ln(o_att[:, :T].reshape(B * T, dq), p['w_out'], x.reshape(B * T, D), ln_g, ln_b)
    new = (k.reshape(B, T, N_KV_C, HEAD_DIM_C), v.reshape(B, T, N_KV_C, HEAD_DIM_C), ki)
    return x_new.reshape(B, T, D), new


def run_trunk(x, even_state, odd_cache, ev, od, norm_w, moe_w):
    ln1_g, ln1_b, ln2_g, ln2_b = norm_w
    router_w, router_b, wg, wu, wd = moe_w
    x, new_even = even_layer(x, *even_state, ev, ln1_g[0], ln1_b[0])
    x = layer_norm(DN_ALPHA * x + moe_ffn(x, router_w, router_b, wg[0], wu[0], wd[0]), ln2_g[0], ln2_b[0])
    x, new_odd = odd_layer(x, odd_cache, od, ln1_g[1], ln1_b[1])
    x = layer_norm(DN_ALPHA * x + moe_ffn(x, router_w, router_b, wg[1], wu[1], wd[1]), ln2_g[1], ln2_b[1])
    return x, new_even, new_odd


def kernel(x_prompt, x_sample, state_lru_h, state_lru_conv, state_rwkv_wkv, state_rwkv_shift, cache_k, cache_v, cache_kidx, ev_w_in, ev_conv_w, ev_conv_b, ev_gate_r_w, ev_gate_r_b, ev_gate_i_w, ev_gate_i_b, ev_lambda, ev_shift_mu, ev_w0, ev_w2, ev_a0, ev_a2, ev_g2, ev_k_k, ev_k_a, ev_r_k, ev_gn_g, ev_gn_b, ev_w_out, od_w_in, od_w_out, ln1_g, ln1_b, ln2_g, ln2_b, router_w, router_b, moe_w_gate, moe_w_up, moe_w_down):
    ev = dict(
        w_in=ev_w_in[0].astype(BF16), conv_w=ev_conv_w[0], conv_b=ev_conv_b[0][None],
        gate_w=jnp.concatenate([_block_diag(ev_gate_r_w[0]), _block_diag(ev_gate_i_w[0])], axis=1).astype(BF16),
        gate_b=jnp.concatenate([ev_gate_r_b[0], ev_gate_i_b[0]])[None],
        lru_c=(-LRU_C * jax.nn.softplus(-ev_lambda[0]))[None],
        shift_mu=ev_shift_mu[0][None], w0=ev_w0[0][None], a0=ev_a0[0][None],
        w2=ev_w2[0].astype(BF16), a2=ev_a2[0].astype(BF16), g2=ev_g2[0].astype(BF16),
        k_k=ev_k_k[0][None], k_a=ev_k_a[0][None], r_k=ev_r_k[0], gn_g=ev_gn_g[0], gn_b=ev_gn_b[0],
        w_out=ev_w_out[0].astype(BF16))
    n_in = od_w_in.shape[2]
    od = dict(w_in=jnp.pad(od_w_in[0], ((0, 0), (0, (-n_in) % LANES))).astype(BF16),
              w_out=od_w_out[0].astype(BF16))
    norm_w = (ln1_g, ln1_b, ln2_g, ln2_b)
    moe_w = (router_w, router_b, moe_w_gate.astype(BF16), moe_w_up.astype(BF16), moe_w_down.astype(BF16))

    B = x_prompt.shape[0]
    fresh = (jnp.zeros((B,) + state_lru_h.shape[2:], F32),
             jnp.zeros((B,) + state_lru_conv.shape[2:], F32),
             jnp.zeros((B,) + state_rwkv_wkv.shape[2:], F32),
             jnp.zeros((B,) + state_rwkv_shift.shape[2:], F32))
    y_p, pe, po = run_trunk(x_prompt, fresh, None, ev, od, norm_w, moe_w)
    y_s, se, so = run_trunk(
        x_sample, (state_lru_h[0], state_lru_conv[0], state_rwkv_wkv[0], state_rwkv_shift[0]),
        (cache_k[0], cache_v[0], cache_kidx[0]), ev, od, norm_w, moe_w)
    st = lambda t: t[None]
    return (y_p, y_s, *map(st, pe), *map(st, po), *map(st, se), *map(st, so))
```

```python
import functools
import math

import jax
import jax.numpy as jnp
from jax import lax
from jax.experimental import pallas as pl
from jax.experimental.pallas import tpu as pltpu

F32 = jnp.float32
BF16 = jnp.bfloat16

CHUNK = 64
LRU_BLOCKS = 16
CONV_W = 4
LRU_C = 8.0
RWKV_HEAD = 64
DECAY_LORA = 64
AAA_LORA = 64
GATE_LORA = 128
N_HEADS_C = 16
HEAD_DIM_C = 128
N_KV_C = 4
N_IDX_HEADS = 16
D_IDX = 64
TOPK_MAX = 256
N_EXPERTS = 16
N_GROUPS = 4
EXPERTS_PER_GROUP = N_EXPERTS // N_GROUPS
TOP_K = 2
DEPTH = 2
DN_ALPHA = (2 * DEPTH) ** 0.25
LN_EPS = 1e-5
GN_EPS = 64e-5

LANES = 128
SUBLANES = 8
VMEM_LIMIT = 56 * 1024 * 1024
MOE_TM = 256


def _params(sem):
    return pltpu.CompilerParams(dimension_semantics=sem, vmem_limit_bytes=VMEM_LIMIT)


def _mm_kernel(x_ref, w_ref, o_ref, xb_ref):
    @pl.when(pl.program_id(1) == 0)
    def _():
        xb_ref[...] = x_ref[...].astype(BF16)

    o_ref[...] = jnp.dot(xb_ref[...], w_ref[...], preferred_element_type=F32)


def _pick(n, cands):
    for c in cands:
        if n % c == 0:
            return c
    return n


def matmul(x, w):
    M, K = x.shape
    N = w.shape[1]
    tm = _pick(M, (1024, 512, 256, 128, 64, 32, 16, 8))
    tn = _pick(N, (768, 1024, 1408, 512, 384, 256, 128))
    return pl.pallas_call(
        _mm_kernel,
        grid=(M // tm, N // tn),
        in_specs=[pl.BlockSpec((tm, K), lambda i, j: (i, 0)),
                  pl.BlockSpec((K, tn), lambda i, j: (0, j))],
        out_specs=pl.BlockSpec((tm, tn), lambda i, j: (i, j)),
        out_shape=jax.ShapeDtypeStruct((M, N), F32),
        scratch_shapes=[pltpu.VMEM((tm, K), BF16)],
        compiler_params=_params(("arbitrary", "arbitrary")),
    )(x, w)


def _deepnorm(resid, y, g, b):
    h = DN_ALPHA * resid + y
    mu = jnp.mean(h, axis=-1, keepdims=True)
    hc = h - mu
    var = jnp.mean(hc * hc, axis=-1, keepdims=True)
    return hc * lax.rsqrt(var + LN_EPS) * g + b


def _mm_ln_kernel(x_ref, w_ref, r_ref, g_ref, b_ref, o_ref):
    y = jnp.dot(x_ref[...].astype(BF16), w_ref[...], preferred_element_type=F32)
    o_ref[...] = _deepnorm(r_ref[...], y, g_ref[...], b_ref[...])


def _mix_ln_kernel(ya_ref, yb_ref, gate_ref, w_ref, r_ref, g_ref, b_ref, o_ref):
    mix = jnp.concatenate([ya_ref[...], yb_ref[...] * gate_ref[...]], axis=1).astype(BF16)
    y = jnp.dot(mix, w_ref[...], preferred_element_type=F32)
    o_ref[...] = _deepnorm(r_ref[...], y, g_ref[...], b_ref[...])


def matmul_ln(x, w, resid, g, b):
    M, K = x.shape
    N = w.shape[1]
    tm = _pick(M, (256, 128, 64, 32, 16, 8))
    row = pl.BlockSpec((1, N), lambda i: (0, 0))
    return pl.pallas_call(
        _mm_ln_kernel,
        grid=(M // tm,),
        in_specs=[pl.BlockSpec((tm, K), lambda i: (i, 0)), pl.BlockSpec((K, N), lambda i: (0, 0)),
                  pl.BlockSpec((tm, N), lambda i: (i, 0)), row, row],
        out_specs=pl.BlockSpec((tm, N), lambda i: (i, 0)),
        out_shape=jax.ShapeDtypeStruct((M, N), F32),
        compiler_params=_params(("arbitrary",)),
    )(x, w, resid, g.reshape(1, N), b.reshape(1, N))


def mix_ln(ya, yb, gate, w, resid, g, b):
    M, Kh = ya.shape
    N = w.shape[1]
    tm = _pick(M, (256, 128, 64, 32, 16, 8))
    row = pl.BlockSpec((1, N), lambda i: (0, 0))
    half = pl.BlockSpec((tm, Kh), lambda i: (i, 0))
    return pl.pallas_call(
        _mix_ln_kernel,
        grid=(M // tm,),
        in_specs=[half, half, half, pl.BlockSpec((2 * Kh, N), lambda i: (0, 0)),
                  pl.BlockSpec((tm, N), lambda i: (i, 0)), row, row],
        out_specs=pl.BlockSpec((tm, N), lambda i: (i, 0)),
        out_shape=jax.ShapeDtypeStruct((M, N), F32),
        compiler_params=_params(("arbitrary",)),
    )(ya, yb, gate, w, resid, g.reshape(1, N), b.reshape(1, N))


def _expm1(x):
    e = jnp.exp(x)
    safe = jnp.where((e == 1.0) | (e == 0.0), 2.0, e)
    return jnp.where(e == 1.0, x, jnp.where(e == 0.0, -1.0, (e - 1.0) * x / jnp.log(safe)))


def _softplus(x):
    return jnp.maximum(x, 0.0) + jnp.log(1.0 + jnp.exp(-jnp.abs(x)))


HALO = SUBLANES


def _even_pre_kernel(ax_ref, ag_ref, zr_ref, zk_ref, zv_ref, zt_ref,
                     conv_ref, sr_ref, sk_ref, sv_ref, st_ref, h0_ref,
                     cw_ref, cb_ref, gw_ref, gb_ref, lc_ref, mu_ref,
                     w0_ref, a0_ref, w2_ref, a2_ref, g2_ref,
                     ya_ref, r_ref, w_ref, k_ref, v_ref, a_ref, g_ref, hT_ref,
                     xp_scr, pr_scr, pk_scr, pv_scr, pt_scr, a_scr, u_scr, hs_scr, h_scr):
    i = pl.program_id(1)
    tm = ax_ref.shape[1]
    C = ax_ref.shape[2]
    prev = ((xp_scr, None), (pr_scr, sr_ref), (pk_scr, sk_ref), (pv_scr, sv_ref), (pt_scr, st_ref))

    @pl.when(i == 0)
    def _():
        xp_scr[0:HALO, :] = jnp.zeros((HALO, C), F32)
        xp_scr[HALO - (CONV_W - 1):HALO, :] = conv_ref[0]
        for scr, ref in prev[1:]:
            scr[0:HALO, :] = jnp.broadcast_to(ref[0], (HALO, scr.shape[1]))
        h_scr[...] = jnp.broadcast_to(h0_ref[0], h_scr.shape)

    cur = pl.ds(HALO, tm)
    xp_scr[cur, :] = ax_ref[0]
    pr_scr[cur, :] = zr_ref[0]
    pk_scr[cur, :] = zk_ref[0]
    pv_scr[cur, :] = zv_ref[0]
    pt_scr[cur, :] = zt_ref[0]

    xa = cb_ref[...]
    for j in range(CONV_W):
        xa = xa + cw_ref[j:j + 1, :] * xp_scr[pl.ds(HALO - (CONV_W - 1) + j, tm), :]
    gates = jnp.dot(xa.astype(BF16), gw_ref[...], preferred_element_type=F32) + gb_ref[...]
    log_a = lc_ref[...] * jax.nn.sigmoid(gates[:, :C])
    a_scr[...] = jnp.exp(log_a)
    u_scr[...] = jnp.sqrt(-_expm1(2.0 * log_a)) * (jax.nn.sigmoid(gates[:, C:]) * xa)

    def step(t, h):
        h = a_scr[pl.ds(t, 1), :] * h + u_scr[pl.ds(t, 1), :]
        hs_scr[pl.ds(t, 1), :] = h
        return h

    h = lax.fori_loop(0, tm, step, h_scr[0:1, :], unroll=8)
    h_scr[...] = jnp.broadcast_to(h, h_scr.shape)
    hT_ref[0] = h
    ya_ref[0] = hs_scr[...] * jax.nn.gelu(ag_ref[0], approximate=True)

    def shifted(scr, lo, hi):
        zb = scr[cur, :]
        return zb + mu_ref[:, lo:hi] * (scr[pl.ds(HALO - 1, tm), :] - zb)

    r = shifted(pr_scr, 0, C)
    k = shifted(pk_scr, C, 2 * C)
    v = shifted(pv_scr, 2 * C, 3 * C)
    tail = shifted(pt_scr, 3 * C, 3 * C + pt_scr.shape[1])
    wl = jnp.tanh(tail[:, :DECAY_LORA])
    al = tail[:, DECAY_LORA:DECAY_LORA + AAA_LORA]
    gl = jax.nn.sigmoid(tail[:, DECAY_LORA + AAA_LORA:])
    lora = lambda x, w_ref_: jnp.dot(x.astype(BF16), w_ref_[...], preferred_element_type=F32)
    w_log = -_softplus(-(w0_ref[...] + lora(wl, w2_ref))) - 0.5
    a = jax.nn.sigmoid(a0_ref[...] + lora(al, a2_ref))
    r_ref[0] = r
    w_ref[0] = jnp.exp(-jnp.exp(w_log))
    k_ref[0] = k
    v_ref[0] = v
    a_ref[0] = a
    g_ref[0] = lora(gl, g2_ref)

    for scr, _ in prev:
        scr[0:HALO, :] = scr[pl.ds(tm, HALO), :]


def even_pre(z, lru_conv, shift, lru_h, p):
    B, T, _ = z.shape
    C = lru_h.shape[-1]
    tail = DECAY_LORA + AAA_LORA + GATE_LORA
    tm = _pick(T, (128, 64, 32, 16, 8))
    zblk = lambda w, col: pl.BlockSpec((1, tm, w), lambda b, i: (b, i, col))
    sblk = lambda rows, w, col: pl.BlockSpec((1, rows, w), lambda b, i: (b, 0, col))
    full = lambda a: pl.BlockSpec(a.shape, lambda b, i: (0,) * a.ndim)
    weights = [p['conv_w'], p['conv_b'], p['gate_w'], p['gate_b'], p['lru_c'], p['shift_mu'],
               p['w0'], p['a0'], p['w2'], p['a2'], p['g2']]
    tok = jax.ShapeDtypeStruct((B, T, C), F32)
    return pl.pallas_call(
        _even_pre_kernel,
        grid=(B, T // tm),
        in_specs=[zblk(C, 0), zblk(C, 1), zblk(C, 2), zblk(C, 3), zblk(C, 4), zblk(tail, 5 * C // tail),
                  sblk(CONV_W - 1, C, 0), sblk(1, C, 0), sblk(1, C, 1), sblk(1, C, 2),
                  sblk(1, tail, 3 * C // tail), sblk(1, C, 0)] + [full(w) for w in weights],
        out_specs=[zblk(C, 0)] * 7 + [sblk(1, C, 0)],
        out_shape=[tok] * 7 + [jax.ShapeDtypeStruct((B, 1, C), F32)],
        scratch_shapes=[pltpu.VMEM((tm + HALO, C), F32)] * 4 + [pltpu.VMEM((tm + HALO, tail), F32)]
                       + [pltpu.VMEM((tm, C), F32)] * 3 + [pltpu.VMEM((SUBLANES, C), F32)],
        compiler_params=_params(("arbitrary", "arbitrary")),
    )(z, z, z, z, z, z, lru_conv, shift, shift, shift, shift, lru_h[:, None, :], *weights)


def _colsum(x):
    n = x.shape[0] // SUBLANES
    p = x.reshape(n, SUBLANES, x.shape[1]).sum(axis=0)
    p = p + pltpu.roll(p, 4, axis=0)
    p = p + pltpu.roll(p, 2, axis=0)
    p = p + pltpu.roll(p, 1, axis=0)
    return p


def _bcast_rows(p, n):
    return jnp.broadcast_to(p[None], (n // SUBLANES,) + p.shape).reshape(n, p.shape[1])


def _rwkv_kernel(r_ref, w_ref, k_ref, v_ref, a_ref, kk_ref, ka_ref, gng_ref, gnb_ref, rk_ref, s0_ref,
                 y_ref, sT_ref, s_scr):
    tc = pl.program_id(1)
    N = RWKV_HEAD

    @pl.when(tc == 0)
    def _():
        s_scr[...] = s0_ref[...]

    sub = lax.broadcasted_iota(jnp.int32, (SUBLANES, LANES), 0)

    def step(t, carry):
        r = r_ref[t]
        w = w_ref[t]
        k_raw = k_ref[t]
        a = a_ref[t]
        kk = k_raw * kk_ref[...]
        kk = kk / jnp.maximum(_bcast_rows(jnp.sqrt(_colsum(kk * kk)), N), 1e-12)
        nkk = -kk
        kka = kk * a
        k = k_raw * (1.0 + (a - 1.0) * ka_ref[...])

        def vgroup(g, c2):
            yacc = jnp.zeros((SUBLANES, LANES), F32)
            for j in range(SUBLANES):
                vi = g * SUBLANES + j
                S = s_scr[vi]
                sa = _bcast_rows(_colsum(S * nkk), N)
                vb = jnp.broadcast_to(v_ref[t, pl.ds(vi, 1), :], (N, LANES))
                S = S * w + sa * kka + vb * k
                s_scr[vi] = S
                yacc = jnp.where(sub == j, _colsum(S * r), yacc)
            y_ref[t, pl.ds(pl.multiple_of(g * SUBLANES, SUBLANES), SUBLANES), :] = yacc
            return c2

        lax.fori_loop(0, N // SUBLANES, vgroup, 0)

        y = y_ref[t]
        d = y - _bcast_rows(_colsum(y), N) * (1.0 / N)
        var = _bcast_rows(_colsum(d * d), N) * (1.0 / N)
        bonus = _bcast_rows(_colsum(r * k * rk_ref[...]), N)
        y_ref[t] = d * lax.rsqrt(var + GN_EPS) * gng_ref[...] + gnb_ref[...] + bonus * v_ref[t]
        return carry

    lax.fori_loop(0, r_ref.shape[0], step, 0)

    @pl.when(tc == pl.num_programs(1) - 1)
    def _():
        sT_ref[...] = s_scr[...]


def rwkv_scan(r, w, k, v, a, k_k, k_a, gn_g, gn_b, r_k, s0):
    T, N, P = r.shape
    tc = _pick(T, (32, 16, 8))
    blk = pl.BlockSpec((tc, N, LANES), lambda p, t: (t, 0, p))
    par = pl.BlockSpec((N, LANES), lambda p, t: (0, p))
    st = pl.BlockSpec((N, N, LANES), lambda p, t: (0, 0, p))
    return pl.pallas_call(
        _rwkv_kernel,
        grid=(P // LANES, T // tc),
        in_specs=[blk] * 5 + [par] * 5 + [st],
        out_specs=[blk, st],
        out_shape=[jax.ShapeDtypeStruct((T, N, P), F32), jax.ShapeDtypeStruct((N, N, P), F32)],
        scratch_shapes=[pltpu.VMEM((N, N, LANES), F32)],
        compiler_params=_params(("arbitrary", "arbitrary")),
    )(r, w, k, v, a, k_k, k_a, gn_g, gn_b, r_k, s0)


MASKED = -1e30


KEY_NEG_INF = -2139095041
QK_SPLIT = 2


def _tree_sum(parts):
    while len(parts) > 1:
        parts = [parts[i] + parts[i + 1] for i in range(0, len(parts) - 1, 2)] + parts[len(parts) & ~1:]
    return parts[0]


def _dsa_kernel(q_ref, qi_ref, wi_ref, k_ref, vt_ref, ki_ref, o_ref,
                key_scr, m_scr, l_scr, acc_scr, *, past, n_sel, scale, tk):
    qb = pl.program_id(1)
    QB = q_ref.shape[1]
    L = k_ref.shape[1]
    NP = N_HEADS_C // 2
    lane_chunk = lax.broadcasted_iota(jnp.int32, (1, QB), 1) // CHUNK
    limit = past + (qb * (QB // CHUNK) + lane_chunk + 1) * CHUNK
    n_tiles = (past + (qb + 1) * QB + tk - 1) // tk
    u = 2 if (tk < 512 and (L // tk) % 2 == 0) else 1
    n_count = (n_tiles + u - 1) // u
    row = lax.broadcasted_iota(jnp.int32, (tk, QB), 0)
    tile = lambda kt: pl.ds(pl.multiple_of(kt * tk, tk), tk)

    qi = qi_ref[0].astype(BF16)
    qi_pair = [jnp.concatenate([qi[:, (2 * p) * D_IDX:(2 * p + 1) * D_IDX],
                                qi[:, (2 * p + 1) * D_IDX:(2 * p + 2) * D_IDX]], axis=0)
               for p in range(N_IDX_HEADS // 2)]
    wi = wi_ref[0]

    def score_tile(kt, c):
        @pl.when(kt < n_tiles)
        def _():
            kidx = ki_ref[0, tile(kt), :]
            terms = []
            for p in range(N_IDX_HEADS // 2):
                s = lax.dot_general(kidx, qi_pair[p], (((1,), (1,)), ((), ())),
                                    preferred_element_type=F32)
                for e in range(2):
                    h = 2 * p + e
                    terms.append(wi[h:h + 1, :] * jnp.maximum(s[:, e * QB:(e + 1) * QB], 0.0))
            isc = terms[0]
            for t in terms[1:]:
                isc = isc + t
            ok = row + kt * tk < limit
            bits = lax.bitcast_convert_type(jnp.where(ok, isc + 0.0, -jnp.inf), jnp.int32)
            key_scr[tile(kt), :] = bits ^ ((bits >> 31) & jnp.int32(0x7FFFFFFF))

        @pl.when(kt >= n_tiles)
        def _():
            key_scr[tile(kt), :] = jnp.full((tk, QB), KEY_NEG_INF, jnp.int32)

        return c

    lax.fori_loop(0, n_count * u, score_tile, 0)

    int_min = jnp.int32(-2 ** 31)
    ct = tk * u

    def search(i, t):
        cand = t | lax.shift_left(jnp.int32(1), 31 - i)
        cs = cand ^ int_min

        def count_tile(kt, acc):
            keys = key_scr[pl.ds(pl.multiple_of(kt * ct, ct), ct), :]
            hit = jnp.where(keys >= cs, 1.0, 0.0)
            return acc + _tree_sum([hit[r * SUBLANES:(r + 1) * SUBLANES] for r in range(ct // SUBLANES)])

        acc = lax.fori_loop(0, n_count, count_tile, jnp.zeros((SUBLANES, QB), F32))
        cnt = jnp.sum(acc, axis=0, keepdims=True)
        return jnp.where(cnt >= n_sel, cand, t)

    thr = lax.fori_loop(0, 32, search, jnp.zeros((1, QB), jnp.int32)) ^ int_min

    G = N_HEADS_C // N_KV_C
    dh = HEAD_DIM_C
    q = q_ref[0]
    q_pair = [(jnp.concatenate([q[:, (2 * p) * dh:(2 * p + 1) * dh],
                                q[:, (2 * p + 1) * dh:(2 * p + 2) * dh]], axis=0) * scale).astype(BF16)
              for p in range(NP)]
    m_scr[...] = jnp.full(m_scr.shape, MASKED, F32)
    l_scr[...] = jnp.zeros(l_scr.shape, F32)
    acc_scr[...] = jnp.zeros(acc_scr.shape, F32)

    def att_tile(kt, c):
        sel = (key_scr[tile(kt), :] >= thr) & (row + kt * tk < limit)
        b = jnp.where(sel, 0.0, MASKED)
        b2 = jnp.concatenate([b, b], axis=1)
        def logits(p):
            j = (2 * p) // G
            kj = k_ref[0, tile(kt), j * dh:(j + 1) * dh]
            part = tk // QK_SPLIT
            return jnp.concatenate(
                [lax.dot_general(kj[r * part:(r + 1) * part], q_pair[p], (((1,), (1,)), ((), ())),
                                 preferred_element_type=F32) for r in range(QK_SPLIT)], axis=0)

        ahead = 2
        lg = {p: logits(p) for p in range(ahead)}
        for p in range(NP):
            j = (2 * p) // G
            x = lg.pop(p) + b2
            m = m_scr[p:p + 1, :]
            m_new = jnp.maximum(m, jnp.max(x, axis=0, keepdims=True))
            alpha = jnp.exp2(m - m_new)
            pr = jnp.exp2(x - m_new)
            l_scr[p:p + 1, :] = alpha * l_scr[p:p + 1, :] + jnp.sum(pr, axis=0, keepdims=True)
            m_scr[p:p + 1, :] = m_new
            vt = vt_ref[0, kt, j * dh:(j + 1) * dh, :]
            pv = jnp.dot(vt, pr.astype(BF16), preferred_element_type=F32)
            if p + ahead < NP:
                lg[p + ahead] = logits(p + ahead)
            acc_scr[p] = acc_scr[p] * alpha + pv
        return c

    lax.fori_loop(0, n_tiles, att_tile, 0)
    for p in range(NP):
        o2 = acc_scr[p] / l_scr[p:p + 1, :]
        for s in range(2):
            o_ref[0, :, (2 * p + s) * dh:(2 * p + s + 1) * dh] = o2[:, s * QB:(s + 1) * QB].T


def dsa_attention(z, q_col, qi_col, wi_t, keys, vals_t, kidx, past, n_sel, tk):
    B, T, _ = z.shape
    L = keys.shape[1]
    QB = LANES
    dq = N_HEADS_C * HEAD_DIM_C
    dqi = N_IDX_HEADS * D_IDX
    kern = functools.partial(_dsa_kernel, past=past, n_sel=n_sel,
                             scale=HEAD_DIM_C ** -0.5 * math.log2(math.e), tk=tk)
    return pl.pallas_call(
        kern,
        grid=(B, T // QB),
        in_specs=[pl.BlockSpec((1, QB, dq), lambda b, c: (b, c, q_col)),
                  pl.BlockSpec((1, QB, dqi), lambda b, c: (b, c, qi_col)),
                  pl.BlockSpec((1, N_IDX_HEADS, QB), lambda b, c: (b, 0, c)),
                  pl.BlockSpec((1, L, keys.shape[2]), lambda b, c: (b, 0, 0)),
                  pl.BlockSpec((1,) + vals_t.shape[1:], lambda b, c: (b, 0, 0, 0)),
                  pl.BlockSpec((1, L, kidx.shape[2]), lambda b, c: (b, 0, 0))],
        out_specs=pl.BlockSpec((1, QB, dq), lambda b, c: (b, c, 0)),
        out_shape=jax.ShapeDtypeStruct((B, T, dq), F32),
        scratch_shapes=[pltpu.VMEM((L, QB), jnp.int32),
                        pltpu.VMEM((N_HEADS_C // 2, 2 * QB), F32),
                        pltpu.VMEM((N_HEADS_C // 2, 2 * QB), F32),
                        pltpu.VMEM((N_HEADS_C // 2, HEAD_DIM_C, 2 * QB), F32)],
        compiler_params=_params(("arbitrary", "arbitrary")),
    )(z, z, wi_t, keys, vals_t, kidx)


def _router_kernel(x_ref, w_ref, b_ref, e_ref, g_ref):
    logits = lax.dot_general(w_ref[...], x_ref[...], (((1,), (1,)), ((), ())),
                             preferred_element_type=F32, precision=lax.Precision.HIGHEST)
    scores = jax.nn.sigmoid(logits)
    biased = scores + b_ref[...]
    tm = scores.shape[1]
    P = EXPERTS_PER_GROUP

    def ranks(rows):
        out = []
        for i in range(len(rows)):
            rk = jnp.zeros((1, tm), jnp.int32)
            for j in range(len(rows)):
                if j == i:
                    continue
                beat = (rows[j] > rows[i]) | ((rows[j] == rows[i]) & (j < i))
                rk = rk + beat.astype(jnp.int32)
            out.append(rk)
        return out

    b = [biased[i:i + 1] for i in range(N_EXPERTS)]
    s = [scores[i:i + 1] for i in range(N_EXPERTS)]
    grp_rank, grp_score = [], []
    for g in range(N_GROUPS):
        rows = b[g * P:(g + 1) * P]
        rk = ranks(rows)
        grp_rank.append(rk)
        top1 = sum(jnp.where(rk[i] == 0, rows[i], 0.0) for i in range(P))
        top2 = sum(jnp.where(rk[i] == 1, rows[i], 0.0) for i in range(P))
        grp_score.append(top1 + top2)
    gr = ranks(grp_score)
    e1 = jnp.zeros((1, tm), jnp.int32)
    e2 = jnp.zeros((1, tm), jnp.int32)
    w1 = jnp.zeros((1, tm), F32)
    w2 = jnp.zeros((1, tm), F32)
    for g in range(N_GROUPS):
        best = gr[g] == 0
        for i in range(P):
            is1 = best & (grp_rank[g][i] == 0)
            is2 = best & (grp_rank[g][i] == 1)
            e1 = jnp.where(is1, g * P + i, e1)
            e2 = jnp.where(is2, g * P + i, e2)
            w1 = jnp.where(is1, s[g * P + i], w1)
            w2 = jnp.where(is2, s[g * P + i], w2)
    tot = w1 + w2
    e_ref[...] = jnp.concatenate([e1, e2], axis=0)
    g_ref[...] = jnp.concatenate([w1 / tot, w2 / tot], axis=0)


def route(xt, router_w, router_b):
    T, D = xt.shape
    tm = _pick(T, (512, 256, 128))
    return pl.pallas_call(
        _router_kernel,
        grid=(T // tm,),
        in_specs=[pl.BlockSpec((tm, D), lambda i: (i, 0)),
                  pl.BlockSpec((N_EXPERTS, D), lambda i: (0, 0)),
                  pl.BlockSpec((N_EXPERTS, 1), lambda i: (0, 0))],
        out_specs=[pl.BlockSpec((TOP_K, tm), lambda i: (0, i)),
                   pl.BlockSpec((TOP_K, tm), lambda i: (0, i))],
        out_shape=[jax.ShapeDtypeStruct((TOP_K, T), jnp.int32),
                   jax.ShapeDtypeStruct((TOP_K, T), F32)],
        compiler_params=_params(("arbitrary",)),
    )(xt, router_w.T, router_b.reshape(N_EXPERTS, 1))


def _ffn_kernel(be_ref, x_ref, wg_ref, wu_ref, wd_ref, o_ref):
    del be_ref
    x = x_ref[...].astype(BF16)
    a = jnp.dot(x, wg_ref[0], preferred_element_type=F32)
    u = jnp.dot(x, wu_ref[0], preferred_element_type=F32)
    h = (a * jax.nn.sigmoid(a)) * u
    o_ref[...] = jnp.dot(h.astype(BF16), wd_ref[0], preferred_element_type=F32)


def expert_ffn(xs, block_exp, wg, wu, wd):
    R, D = xs.shape
    F = wg.shape[2]
    nb = R // MOE_TM
    grid_spec = pltpu.PrefetchScalarGridSpec(
        num_scalar_prefetch=1,
        grid=(nb,),
        in_specs=[pl.BlockSpec((MOE_TM, D), lambda i, be: (i, 0)),
                  pl.BlockSpec((1, D, F), lambda i, be: (be[i], 0, 0)),
                  pl.BlockSpec((1, D, F), lambda i, be: (be[i], 0, 0)),
                  pl.BlockSpec((1, F, D), lambda i, be: (be[i], 0, 0))],
        out_specs=pl.BlockSpec((MOE_TM, D), lambda i, be: (i, 0)),
    )
    return pl.pallas_call(
        _ffn_kernel,
        grid_spec=grid_spec,
        out_shape=jax.ShapeDtypeStruct((R, D), F32),
        compiler_params=_params(("arbitrary",)),
    )(block_exp, xs, wg, wu, wd)


def _combine_ln_kernel(ya_ref, yb_ref, ga_ref, gb_ref, r_ref, g_ref, b_ref, o_ref):
    y = ya_ref[...] * ga_ref[...] + yb_ref[...] * gb_ref[...]
    o_ref[...] = _deepnorm(r_ref[...], y, g_ref[...], b_ref[...])


def combine_ln(ya, yb, ga, gb, resid, g, b):
    M, N = ya.shape
    tm = _pick(M, (256, 128, 64, 32, 16, 8))
    blk = pl.BlockSpec((tm, N), lambda i: (i, 0))
    col = pl.BlockSpec((tm, 1), lambda i: (i, 0))
    row = pl.BlockSpec((1, N), lambda i: (0, 0))
    return pl.pallas_call(
        _combine_ln_kernel,
        grid=(M // tm,),
        in_specs=[blk, blk, col, col, blk, row, row],
        out_specs=blk,
        out_shape=jax.ShapeDtypeStruct((M, N), F32),
        compiler_params=_params(("arbitrary",)),
    )(ya, yb, ga, gb, resid, g.reshape(1, N), b.reshape(1, N))


def moe_layer(x, router_w, router_b, wg, wu, wd, ln_g, ln_b):
    shp = x.shape
    xt = x.reshape(-1, shp[-1])
    T = xt.shape[0]
    expert, gate = route(xt, router_w, router_b)
    flat_e = expert.T.reshape(-1)
    n_assign = T * TOP_K
    onehot = (flat_e[:, None] == jnp.arange(N_EXPERTS, dtype=jnp.int32)[None]).astype(jnp.int32)
    csum = jnp.cumsum(onehot, axis=0)
    rank = jnp.take_along_axis(csum, flat_e[:, None], axis=1)[:, 0] - 1
    counts = csum[-1]
    padded = (counts + MOE_TM - 1) // MOE_TM * MOE_TM
    pad_end = jnp.cumsum(padded)
    pad_start = pad_end - padded
    dest = pad_start[flat_e] + rank
    nb = -(-n_assign // MOE_TM) + N_EXPERTS
    rows = nb * MOE_TM
    row_tok = jnp.zeros(rows, jnp.int32).at[dest].set(jnp.arange(n_assign, dtype=jnp.int32) // TOP_K)
    block_exp = jnp.minimum(
        jnp.searchsorted(pad_end, jnp.arange(nb, dtype=jnp.int32) * MOE_TM, side='right'),
        N_EXPERTS - 1).astype(jnp.int32)
    yb = expert_ffn(xt[row_tok], block_exp, wg, wu, wd)
    d2 = dest.reshape(T, TOP_K)
    out = combine_ln(yb[d2[:, 0]], yb[d2[:, 1]], gate[0][:, None], gate[1][:, None], xt, ln_g, ln_b)
    return out.reshape(shp)


def _block_diag(w):
    G, I, O = w.shape
    eye = jnp.eye(G, dtype=w.dtype)
    return (eye[:, None, :, None] * w[:, :, None, :]).reshape(G * I, G * O)


def even_layer(x, lru_h, lru_conv, wkv, shift, p, ln_g, ln_b):
    B, T, D = x.shape
    D_LRU = lru_h.shape[-1]
    D_RWKV = p['w0'].shape[1]
    H = D_RWKV // RWKV_HEAD
    N = RWKV_HEAD
    z = matmul(x.reshape(B * T, D), p['w_in']).reshape(B, T, -1)
    y_a, r, decay, k, v, a, g, h_T = even_pre(z, lru_conv, shift, lru_h, p)

    lanes = lambda t: jnp.transpose(t.reshape(B, T, H, N), (1, 3, 0, 2)).reshape(T, N, B * H)
    per_chain = lambda w: jnp.tile(w.reshape(H, N).T, (1, B))
    s0 = jnp.transpose(wkv, (2, 3, 0, 1)).reshape(N, N, B * H)
    y, s_T = rwkv_scan(lanes(r), lanes(decay), lanes(k), lanes(v), lanes(a),
                       per_chain(p['k_k']), per_chain(p['k_a']), per_chain(p['gn_g']),
                       per_chain(p['gn_b']), per_chain(p['r_k']), s0)
    y = jnp.transpose(y.reshape(T, N, B, H), (2, 0, 3, 1)).reshape(B * T, D_RWKV)
    S_T = jnp.transpose(s_T.reshape(N, N, B, H), (2, 3, 0, 1))

    x_new = mix_ln(y_a.reshape(B * T, D_LRU), y, g.reshape(B * T, D_RWKV), p['w_out'],
                   x.reshape(B * T, D), ln_g, ln_b)
    conv_state = jnp.concatenate([lru_conv, z[..., :D_LRU]], axis=1)[:, T:]
    return x_new.reshape(B, T, D), (h_T[:, 0], conv_state, S_T, z[:, -1:, 2 * D_LRU:])


def odd_layer(x, cache, p, ln_g, ln_b):
    B, T, D = x.shape
    dq = N_HEADS_C * HEAD_DIM_C
    dkv = N_KV_C * HEAD_DIM_C
    dqi = N_IDX_HEADS * D_IDX
    z = matmul(x.reshape(B * T, D), p['w_in']).reshape(B, T, -1)
    k = z[..., dq:dq + dkv]
    v = z[..., dq + dkv:dq + 2 * dkv]
    o = dq + 2 * dkv + dqi
    ki = z[..., o:o + D_IDX]
    wi = z[..., o + D_IDX:o + D_IDX + N_IDX_HEADS] * (dqi ** -0.5)
    Tp = -(-T // LANES) * LANES
    if cache is None:
        past = 0
        n_sel = min(TOPK_MAX, T // 4)
        parts = lambda c, n: [n]
    else:
        past = cache[0].shape[1]
        n_sel = min(TOPK_MAX, (past + T) // 4)
        parts = lambda c, n: [c.reshape(B, past, -1), n]
    L = -(-(past + Tp) // LANES) * LANES
    tk = _pick(L, (512, 384, 128))
    cat = lambda c, n: jnp.concatenate(
        parts(c, n) + [jnp.zeros((B, L - past - T, n.shape[-1]), n.dtype)], axis=1).astype(BF16)
    ck, cv, cki = cache if cache is not None else (None, None, None)
    keys, vals, kidx = cat(ck, k), cat(cv, v), cat(cki, ki)
    vals_t = jnp.swapaxes(vals.reshape(B, L // tk, tk, dkv), 2, 3)
    rowpad = lambda t: jnp.pad(t, ((0, 0), (0, Tp - T), (0, 0)))
    wi_t = jnp.swapaxes(rowpad(wi), 1, 2)
    o_att = dsa_attention(rowpad(z), 0, (dq + 2 * dkv) // dqi, wi_t, keys, vals_t, kidx, past, n_sel, tk)
    x_new = matmul_ln(o_att[:, :T].reshape(B * T, dq), p['w_out'], x.reshape(B * T, D), ln_g, ln_b)
    new = (k.reshape(B, T, N_KV_C, HEAD_DIM_C), v.reshape(B, T, N_KV_C, HEAD_DIM_C), ki)
    return x_new.reshape(B, T, D), new


def run_trunk(x, even_state, odd_cache, ev, od, norm_w, moe_w):
    ln1_g, ln1_b, ln2_g, ln2_b = norm_w
    router_w, router_b, wg, wu, wd = moe_w
    x, new_even = even_layer(x, *even_state, ev, ln1_g[0], ln1_b[0])
    x = moe_layer(x, router_w, router_b, wg[0], wu[0], wd[0], ln2_g[0], ln2_b[0])
    x, new_odd = odd_layer(x, odd_cache, od, ln1_g[1], ln1_b[1])
    x = moe_layer(x, router_w, router_b, wg[1], wu[1], wd[1], ln2_g[1], ln2_b[1])
    return x, new_even, new_odd


def kernel(x_prompt, x_sample, state_lru_h, state_lru_conv, state_rwkv_wkv, state_rwkv_shift, cache_k, cache_v, cache_kidx, ev_w_in, ev_conv_w, ev_conv_b, ev_gate_r_w, ev_gate_r_b, ev_gate_i_w, ev_gate_i_b, ev_lambda, ev_shift_mu, ev_w0, ev_w2, ev_a0, ev_a2, ev_g2, ev_k_k, ev_k_a, ev_r_k, ev_gn_g, ev_gn_b, ev_w_out, od_w_in, od_w_out, ln1_g, ln1_b, ln2_g, ln2_b, router_w, router_b, moe_w_gate, moe_w_up, moe_w_down):
    ev = dict(
        w_in=ev_w_in[0].astype(BF16), conv_w=ev_conv_w[0], conv_b=ev_conv_b[0][None],
        gate_w=jnp.concatenate([_block_diag(ev_gate_r_w[0]), _block_diag(ev_gate_i_w[0])], axis=1).astype(BF16),
        gate_b=jnp.concatenate([ev_gate_r_b[0], ev_gate_i_b[0]])[None],
        lru_c=(-LRU_C * jax.nn.softplus(-ev_lambda[0]))[None],
        shift_mu=ev_shift_mu[0][None], w0=ev_w0[0][None], a0=ev_a0[0][None],
        w2=ev_w2[0].astype(BF16), a2=ev_a2[0].astype(BF16), g2=ev_g2[0].astype(BF16),
        k_k=ev_k_k[0][None], k_a=ev_k_a[0][None], r_k=ev_r_k[0], gn_g=ev_gn_g[0], gn_b=ev_gn_b[0],
        w_out=ev_w_out[0].astype(BF16))
    n_in = od_w_in.shape[2]
    od = dict(w_in=jnp.pad(od_w_in[0], ((0, 0), (0, (-n_in) % LANES))).astype(BF16),
              w_out=od_w_out[0].astype(BF16))
    norm_w = (ln1_g, ln1_b, ln2_g, ln2_b)
    per_layer = lambda w: [w[l].astype(BF16) for l in range(DEPTH)]
    moe_w = (router_w, router_b, per_layer(moe_w_gate), per_layer(moe_w_up), per_layer(moe_w_down))

    B = x_prompt.shape[0]
    fresh = (jnp.zeros((B,) + state_lru_h.shape[2:], F32),
             jnp.zeros((B,) + state_lru_conv.shape[2:], F32),
             jnp.zeros((B,) + state_rwkv_wkv.shape[2:], F32),
             jnp.zeros((B,) + state_rwkv_shift.shape[2:], F32))
    y_p, pe, po = run_trunk(x_prompt, fresh, None, ev, od, norm_w, moe_w)
    y_s, se, so = run_trunk(
        x_sample, (state_lru_h[0], state_lru_conv[0], state_rwkv_wkv[0], state_rwkv_shift[0]),
        (cache_k[0], cache_v[0], cache_kidx[0]), ev, od, norm_w, moe_w)
    st = lambda t: t[None]
    return (y_p, y_s, *map(st, pe), *map(st, po), *map(st, se), *map(st, so))
```

```python
import functools
import math

import jax
import jax.numpy as jnp
from jax import lax
from jax.experimental import pallas as pl
from jax.experimental.pallas import tpu as pltpu

F32 = jnp.float32
BF16 = jnp.bfloat16

CHUNK = 64
LRU_BLOCKS = 16
CONV_W = 4
LRU_C = 8.0
RWKV_HEAD = 64
DECAY_LORA = 64
AAA_LORA = 64
GATE_LORA = 128
N_HEADS_C = 16
HEAD_DIM_C = 128
N_KV_C = 4
N_IDX_HEADS = 16
D_IDX = 64
TOPK_MAX = 256
N_EXPERTS = 16
N_GROUPS = 4
EXPERTS_PER_GROUP = N_EXPERTS // N_GROUPS
TOP_K = 2
DEPTH = 2
DN_ALPHA = (2 * DEPTH) ** 0.25
LN_EPS = 1e-5
GN_EPS = 64e-5

LANES = 128
SUBLANES = 8
VMEM_LIMIT = 56 * 1024 * 1024
MOE_TM = 256


def _params(sem):
    return pltpu.CompilerParams(dimension_semantics=sem, vmem_limit_bytes=VMEM_LIMIT)


def _mm_kernel(x_ref, w_ref, o_ref, xb_ref):
    @pl.when(pl.program_id(1) == 0)
    def _():
        xb_ref[...] = x_ref[...].astype(BF16)

    o_ref[...] = jnp.dot(xb_ref[...], w_ref[...], preferred_element_type=F32)


def _pick(n, cands):
    for c in cands:
        if n % c == 0:
            return c
    return n


def matmul(x, w):
    M, K = x.shape
    N = w.shape[1]
    tm = _pick(M, (1024, 512, 256, 128, 64, 32, 16, 8))
    tn = _pick(N, (768, 1024, 1408, 512, 384, 256, 128))
    return pl.pallas_call(
        _mm_kernel,
        grid=(M // tm, N // tn),
        in_specs=[pl.BlockSpec((tm, K), lambda i, j: (i, 0)),
                  pl.BlockSpec((K, tn), lambda i, j: (0, j))],
        out_specs=pl.BlockSpec((tm, tn), lambda i, j: (i, j)),
        out_shape=jax.ShapeDtypeStruct((M, N), F32),
        scratch_shapes=[pltpu.VMEM((tm, K), BF16)],
        compiler_params=_params(("arbitrary", "arbitrary")),
    )(x, w)


def _deepnorm(resid, y, g, b):
    h = DN_ALPHA * resid + y
    mu = jnp.mean(h, axis=-1, keepdims=True)
    hc = h - mu
    var = jnp.mean(hc * hc, axis=-1, keepdims=True)
    return hc * lax.rsqrt(var + LN_EPS) * g + b


def _mm_ln_kernel(x_ref, w_ref, r_ref, g_ref, b_ref, o_ref):
    y = jnp.dot(x_ref[...].astype(BF16), w_ref[...], preferred_element_type=F32)
    o_ref[...] = _deepnorm(r_ref[...], y, g_ref[...], b_ref[...])


def _mix_ln_kernel(ya_ref, yb_ref, gate_ref, w_ref, r_ref, g_ref, b_ref, o_ref):
    mix = jnp.concatenate([ya_ref[...], yb_ref[...] * gate_ref[...]], axis=1).astype(BF16)
    y = jnp.dot(mix, w_ref[...], preferred_element_type=F32)
    o_ref[...] = _deepnorm(r_ref[...], y, g_ref[...], b_ref[...])


def matmul_ln(x, w, resid, g, b):
    M, K = x.shape
    N = w.shape[1]
    tm = _pick(M, (256, 128, 64, 32, 16, 8))
    row = pl.BlockSpec((1, N), lambda i: (0, 0))
    return pl.pallas_call(
        _mm_ln_kernel,
        grid=(M // tm,),
        in_specs=[pl.BlockSpec((tm, K), lambda i: (i, 0)), pl.BlockSpec((K, N), lambda i: (0, 0)),
                  pl.BlockSpec((tm, N), lambda i: (i, 0)), row, row],
        out_specs=pl.BlockSpec((tm, N), lambda i: (i, 0)),
        out_shape=jax.ShapeDtypeStruct((M, N), F32),
        compiler_params=_params(("arbitrary",)),
    )(x, w, resid, g.reshape(1, N), b.reshape(1, N))


def mix_ln(ya, yb, gate, w, resid, g, b):
    M, Kh = ya.shape
    N = w.shape[1]
    tm = _pick(M, (256, 128, 64, 32, 16, 8))
    row = pl.BlockSpec((1, N), lambda i: (0, 0))
    half = pl.BlockSpec((tm, Kh), lambda i: (i, 0))
    return pl.pallas_call(
        _mix_ln_kernel,
        grid=(M // tm,),
        in_specs=[half, half, half, pl.BlockSpec((2 * Kh, N), lambda i: (0, 0)),
                  pl.BlockSpec((tm, N), lambda i: (i, 0)), row, row],
        out_specs=pl.BlockSpec((tm, N), lambda i: (i, 0)),
        out_shape=jax.ShapeDtypeStruct((M, N), F32),
        compiler_params=_params(("arbitrary",)),
    )(ya, yb, gate, w, resid, g.reshape(1, N), b.reshape(1, N))


def _expm1(x):
    e = jnp.exp(x)
    safe = jnp.where((e == 1.0) | (e == 0.0), 2.0, e)
    return jnp.where(e == 1.0, x, jnp.where(e == 0.0, -1.0, (e - 1.0) * x / jnp.log(safe)))


def _softplus(x):
    return jnp.maximum(x, 0.0) + jnp.log(1.0 + jnp.exp(-jnp.abs(x)))


HALO = SUBLANES


def _even_pre_kernel(ax_ref, ag_ref, zr_ref, zk_ref, zv_ref, zt_ref,
                     conv_ref, sr_ref, sk_ref, sv_ref, st_ref, h0_ref,
                     cw_ref, cb_ref, gw_ref, gb_ref, lc_ref, mu_ref,
                     w0_ref, a0_ref, w2_ref, a2_ref, g2_ref,
                     ya_ref, r_ref, w_ref, k_ref, v_ref, a_ref, g_ref, hT_ref,
                     xp_scr, pr_scr, pk_scr, pv_scr, pt_scr, a_scr, u_scr, hs_scr, h_scr):
    i = pl.program_id(1)
    tm = ax_ref.shape[1]
    C = ax_ref.shape[2]
    prev = ((xp_scr, None), (pr_scr, sr_ref), (pk_scr, sk_ref), (pv_scr, sv_ref), (pt_scr, st_ref))

    @pl.when(i == 0)
    def _():
        xp_scr[0:HALO, :] = jnp.zeros((HALO, C), F32)
        xp_scr[HALO - (CONV_W - 1):HALO, :] = conv_ref[0]
        for scr, ref in prev[1:]:
            scr[0:HALO, :] = jnp.broadcast_to(ref[0], (HALO, scr.shape[1]))
        h_scr[...] = jnp.broadcast_to(h0_ref[0], h_scr.shape)

    cur = pl.ds(HALO, tm)
    xp_scr[cur, :] = ax_ref[0]
    pr_scr[cur, :] = zr_ref[0]
    pk_scr[cur, :] = zk_ref[0]
    pv_scr[cur, :] = zv_ref[0]
    pt_scr[cur, :] = zt_ref[0]

    xa = cb_ref[...]
    for j in range(CONV_W):
        xa = xa + cw_ref[j:j + 1, :] * xp_scr[pl.ds(HALO - (CONV_W - 1) + j, tm), :]
    gates = jnp.dot(xa.astype(BF16), gw_ref[...], preferred_element_type=F32) + gb_ref[...]
    log_a = lc_ref[...] * jax.nn.sigmoid(gates[:, :C])
    a_scr[...] = jnp.exp(log_a)
    u_scr[...] = jnp.sqrt(-_expm1(2.0 * log_a)) * (jax.nn.sigmoid(gates[:, C:]) * xa)

    def step(t, h):
        h = a_scr[pl.ds(t, 1), :] * h + u_scr[pl.ds(t, 1), :]
        hs_scr[pl.ds(t, 1), :] = h
        return h

    h = lax.fori_loop(0, tm, step, h_scr[0:1, :], unroll=8)
    h_scr[...] = jnp.broadcast_to(h, h_scr.shape)
    hT_ref[0] = h
    ya_ref[0] = hs_scr[...] * jax.nn.gelu(ag_ref[0], approximate=True)

    def shifted(scr, lo, hi):
        zb = scr[cur, :]
        return zb + mu_ref[:, lo:hi] * (scr[pl.ds(HALO - 1, tm), :] - zb)

    r = shifted(pr_scr, 0, C)
    k = shifted(pk_scr, C, 2 * C)
    v = shifted(pv_scr, 2 * C, 3 * C)
    tail = shifted(pt_scr, 3 * C, 3 * C + pt_scr.shape[1])
    wl = jnp.tanh(tail[:, :DECAY_LORA])
    al = tail[:, DECAY_LORA:DECAY_LORA + AAA_LORA]
    gl = jax.nn.sigmoid(tail[:, DECAY_LORA + AAA_LORA:])
    lora = lambda x, w_ref_: jnp.dot(x.astype(BF16), w_ref_[...], preferred_element_type=F32)
    w_log = -_softplus(-(w0_ref[...] + lora(wl, w2_ref))) - 0.5
    a = jax.nn.sigmoid(a0_ref[...] + lora(al, a2_ref))
    r_ref[0] = r
    w_ref[0] = jnp.exp(-jnp.exp(w_log))
    k_ref[0] = k
    v_ref[0] = v
    a_ref[0] = a
    g_ref[0] = lora(gl, g2_ref)

    for scr, _ in prev:
        scr[0:HALO, :] = scr[pl.ds(tm, HALO), :]


def even_pre(z, lru_conv, shift, lru_h, p):
    B, T, _ = z.shape
    C = lru_h.shape[-1]
    tail = DECAY_LORA + AAA_LORA + GATE_LORA
    tm = _pick(T, (128, 64, 32, 16, 8))
    zblk = lambda w, col: pl.BlockSpec((1, tm, w), lambda b, i: (b, i, col))
    sblk = lambda rows, w, col: pl.BlockSpec((1, rows, w), lambda b, i: (b, 0, col))
    full = lambda a: pl.BlockSpec(a.shape, lambda b, i: (0,) * a.ndim)
    weights = [p['conv_w'], p['conv_b'], p['gate_w'], p['gate_b'], p['lru_c'], p['shift_mu'],
               p['w0'], p['a0'], p['w2'], p['a2'], p['g2']]
    tok = jax.ShapeDtypeStruct((B, T, C), F32)
    return pl.pallas_call(
        _even_pre_kernel,
        grid=(B, T // tm),
        in_specs=[zblk(C, 0), zblk(C, 1), zblk(C, 2), zblk(C, 3), zblk(C, 4), zblk(tail, 5 * C // tail),
                  sblk(CONV_W - 1, C, 0), sblk(1, C, 0), sblk(1, C, 1), sblk(1, C, 2),
                  sblk(1, tail, 3 * C // tail), sblk(1, C, 0)] + [full(w) for w in weights],
        out_specs=[zblk(C, 0)] * 7 + [sblk(1, C, 0)],
        out_shape=[tok] * 7 + [jax.ShapeDtypeStruct((B, 1, C), F32)],
        scratch_shapes=[pltpu.VMEM((tm + HALO, C), F32)] * 4 + [pltpu.VMEM((tm + HALO, tail), F32)]
                       + [pltpu.VMEM((tm, C), F32)] * 3 + [pltpu.VMEM((SUBLANES, C), F32)],
        compiler_params=_params(("arbitrary", "arbitrary")),
    )(z, z, z, z, z, z, lru_conv, shift, shift, shift, shift, lru_h[:, None, :], *weights)


def _colsum(x):
    n = x.shape[0] // SUBLANES
    p = x.reshape(n, SUBLANES, x.shape[1]).sum(axis=0)
    p = p + pltpu.roll(p, 4, axis=0)
    p = p + pltpu.roll(p, 2, axis=0)
    p = p + pltpu.roll(p, 1, axis=0)
    return p


def _bcast_rows(p, n):
    return jnp.broadcast_to(p[None], (n // SUBLANES,) + p.shape).reshape(n, p.shape[1])


def _rwkv_kernel(r_ref, w_ref, k_ref, v_ref, a_ref, kk_ref, ka_ref, gng_ref, gnb_ref, rk_ref, s0_ref,
                 y_ref, sT_ref, s_scr):
    tc = pl.program_id(1)
    N = RWKV_HEAD

    @pl.when(tc == 0)
    def _():
        s_scr[...] = s0_ref[...]

    sub = lax.broadcasted_iota(jnp.int32, (SUBLANES, LANES), 0)

    def step(t, carry):
        r = r_ref[t]
        w = w_ref[t]
        k_raw = k_ref[t]
        a = a_ref[t]
        kk = k_raw * kk_ref[...]
        kk = kk / jnp.maximum(_bcast_rows(jnp.sqrt(_colsum(kk * kk)), N), 1e-12)
        nkk = -kk
        kka = kk * a
        k = k_raw * (1.0 + (a - 1.0) * ka_ref[...])

        def vgroup(g, c2):
            yacc = jnp.zeros((SUBLANES, LANES), F32)
            for j in range(SUBLANES):
                vi = g * SUBLANES + j
                S = s_scr[vi]
                sa = _bcast_rows(_colsum(S * nkk), N)
                vb = jnp.broadcast_to(v_ref[t, pl.ds(vi, 1), :], (N, LANES))
                S = S * w + sa * kka + vb * k
                s_scr[vi] = S
                yacc = jnp.where(sub == j, _colsum(S * r), yacc)
            y_ref[t, pl.ds(pl.multiple_of(g * SUBLANES, SUBLANES), SUBLANES), :] = yacc
            return c2

        lax.fori_loop(0, N // SUBLANES, vgroup, 0, unroll=True)

        y = y_ref[t]
        d = y - _bcast_rows(_colsum(y), N) * (1.0 / N)
        var = _bcast_rows(_colsum(d * d), N) * (1.0 / N)
        bonus = _bcast_rows(_colsum(r * k * rk_ref[...]), N)
        y_ref[t] = d * lax.rsqrt(var + GN_EPS) * gng_ref[...] + gnb_ref[...] + bonus * v_ref[t]
        return carry

    lax.fori_loop(0, r_ref.shape[0], step, 0)

    @pl.when(tc == pl.num_programs(1) - 1)
    def _():
        sT_ref[...] = s_scr[...]


def rwkv_scan(r, w, k, v, a, k_k, k_a, gn_g, gn_b, r_k, s0):
    T, N, P = r.shape
    tc = _pick(T, (32, 16, 8))
    blk = pl.BlockSpec((tc, N, LANES), lambda p, t: (t, 0, p))
    par = pl.BlockSpec((N, LANES), lambda p, t: (0, p))
    st = pl.BlockSpec((N, N, LANES), lambda p, t: (0, 0, p))
    return pl.pallas_call(
        _rwkv_kernel,
        grid=(P // LANES, T // tc),
        in_specs=[blk] * 5 + [par] * 5 + [st],
        out_specs=[blk, st],
        out_shape=[jax.ShapeDtypeStruct((T, N, P), F32), jax.ShapeDtypeStruct((N, N, P), F32)],
        scratch_shapes=[pltpu.VMEM((N, N, LANES), F32)],
        compiler_params=_params(("arbitrary", "arbitrary")),
    )(r, w, k, v, a, k_k, k_a, gn_g, gn_b, r_k, s0)


MASKED = -1e30


KEY_NEG_INF = -2139095041
QK_SPLIT = 2


def _tree_sum(parts):
    while len(parts) > 1:
        parts = [parts[i] + parts[i + 1] for i in range(0, len(parts) - 1, 2)] + parts[len(parts) & ~1:]
    return parts[0]


def _dsa_kernel(q_ref, qi_ref, wi_ref, k_ref, vt_ref, ki_ref, o_ref,
                key_scr, m_scr, l_scr, acc_scr, *, past, n_sel, scale, tk):
    qb = pl.program_id(1)
    QB = q_ref.shape[1]
    L = k_ref.shape[1]
    NP = N_HEADS_C // 2
    lane_chunk = lax.broadcasted_iota(jnp.int32, (1, QB), 1) // CHUNK
    limit = past + (qb * (QB // CHUNK) + lane_chunk + 1) * CHUNK
    n_tiles = (past + (qb + 1) * QB + tk - 1) // tk
    u = 2 if (tk < 512 and (L // tk) % 2 == 0) else 1
    n_count = (n_tiles + u - 1) // u
    row = lax.broadcasted_iota(jnp.int32, (tk, QB), 0)
    tile = lambda kt: pl.ds(pl.multiple_of(kt * tk, tk), tk)

    qi = qi_ref[0].astype(BF16)
    qi_pair = [jnp.concatenate([qi[:, (2 * p) * D_IDX:(2 * p + 1) * D_IDX],
                                qi[:, (2 * p + 1) * D_IDX:(2 * p + 2) * D_IDX]], axis=0)
               for p in range(N_IDX_HEADS // 2)]
    wi = wi_ref[0]

    def score_tile(kt, c):
        @pl.when(kt < n_tiles)
        def _():
            kidx = ki_ref[0, tile(kt), :]
            terms = []
            for p in range(N_IDX_HEADS // 2):
                s = lax.dot_general(kidx, qi_pair[p], (((1,), (1,)), ((), ())),
                                    preferred_element_type=F32)
                for e in range(2):
                    h = 2 * p + e
                    terms.append(wi[h:h + 1, :] * jnp.maximum(s[:, e * QB:(e + 1) * QB], 0.0))
            isc = terms[0]
            for t in terms[1:]:
                isc = isc + t
            ok = row + kt * tk < limit
            bits = lax.bitcast_convert_type(jnp.where(ok, isc + 0.0, -jnp.inf), jnp.int32)
            key_scr[tile(kt), :] = bits ^ ((bits >> 31) & jnp.int32(0x7FFFFFFF))

        @pl.when(kt >= n_tiles)
        def _():
            key_scr[tile(kt), :] = jnp.full((tk, QB), KEY_NEG_INF, jnp.int32)

        return c

    lax.fori_loop(0, n_count * u, score_tile, 0)

    int_min = jnp.int32(-2 ** 31)
    ct = tk * u

    def search(i, t):
        cand = t | lax.shift_left(jnp.int32(1), 31 - i)
        cs = cand ^ int_min

        def count_tile(kt, acc):
            keys = key_scr[pl.ds(pl.multiple_of(kt * ct, ct), ct), :]
            hit = jnp.where(keys >= cs, 1.0, 0.0)
            return acc + _tree_sum([hit[r * SUBLANES:(r + 1) * SUBLANES] for r in range(ct // SUBLANES)])

        acc = lax.fori_loop(0, n_count, count_tile, jnp.zeros((SUBLANES, QB), F32))
        cnt = jnp.sum(acc, axis=0, keepdims=True)
        return jnp.where(cnt >= n_sel, cand, t)

    thr = lax.fori_loop(0, 32, search, jnp.zeros((1, QB), jnp.int32)) ^ int_min

    G = N_HEADS_C // N_KV_C
    dh = HEAD_DIM_C
    q = q_ref[0]
    q_pair = [(jnp.concatenate([q[:, (2 * p) * dh:(2 * p + 1) * dh],
                                q[:, (2 * p + 1) * dh:(2 * p + 2) * dh]], axis=0) * scale).astype(BF16)
              for p in range(NP)]
    m_scr[...] = jnp.full(m_scr.shape, MASKED, F32)
    l_scr[...] = jnp.zeros(l_scr.shape, F32)
    acc_scr[...] = jnp.zeros(acc_scr.shape, F32)

    def att_tile(kt, c):
        sel = (key_scr[tile(kt), :] >= thr) & (row + kt * tk < limit)
        b = jnp.where(sel, 0.0, MASKED)
        b2 = jnp.concatenate([b, b], axis=1)
        def logits(p):
            j = (2 * p) // G
            kj = k_ref[0, tile(kt), j * dh:(j + 1) * dh]
            part = tk // QK_SPLIT
            return jnp.concatenate(
                [lax.dot_general(kj[r * part:(r + 1) * part], q_pair[p], (((1,), (1,)), ((), ())),
                                 preferred_element_type=F32) for r in range(QK_SPLIT)], axis=0)

        ahead = 2
        lg = {p: logits(p) for p in range(ahead)}
        for p in range(NP):
            j = (2 * p) // G
            x = lg.pop(p) + b2
            m = m_scr[p:p + 1, :]
            m_new = jnp.maximum(m, jnp.max(x, axis=0, keepdims=True))
            alpha = jnp.exp2(m - m_new)
            pr = jnp.exp2(x - m_new)
            l_scr[p:p + 1, :] = alpha * l_scr[p:p + 1, :] + jnp.sum(pr, axis=0, keepdims=True)
            m_scr[p:p + 1, :] = m_new
            vt = vt_ref[0, kt, j * dh:(j + 1) * dh, :]
            pv = jnp.dot(vt, pr.astype(BF16), preferred_element_type=F32)
            if p + ahead < NP:
                lg[p + ahead] = logits(p + ahead)
            acc_scr[p] = acc_scr[p] * alpha + pv
        return c

    lax.fori_loop(0, n_tiles, att_tile, 0)
    for p in range(NP):
        o2 = acc_scr[p] / l_scr[p:p + 1, :]
        for s in range(2):
            o_ref[0, :, (2 * p + s) * dh:(2 * p + s + 1) * dh] = o2[:, s * QB:(s + 1) * QB].T


def dsa_attention(z, q_col, qi_col, wi_t, keys, vals_t, kidx, past, n_sel, tk):
    B, T, _ = z.shape
    L = keys.shape[1]
    QB = LANES
    dq = N_HEADS_C * HEAD_DIM_C
    dqi = N_IDX_HEADS * D_IDX
    kern = functools.partial(_dsa_kernel, past=past, n_sel=n_sel,
                             scale=HEAD_DIM_C ** -0.5 * math.log2(math.e), tk=tk)
    return pl.pallas_call(
        kern,
        grid=(B, T // QB),
        in_specs=[pl.BlockSpec((1, QB, dq), lambda b, c: (b, c, q_col)),
                  pl.BlockSpec((1, QB, dqi), lambda b, c: (b, c, qi_col)),
                  pl.BlockSpec((1, N_IDX_HEADS, QB), lambda b, c: (b, 0, c)),
                  pl.BlockSpec((1, L, keys.shape[2]), lambda b, c: (b, 0, 0)),
                  pl.BlockSpec((1,) + vals_t.shape[1:], lambda b, c: (b, 0, 0, 0)),
                  pl.BlockSpec((1, L, kidx.shape[2]), lambda b, c: (b, 0, 0))],
        out_specs=pl.BlockSpec((1, QB, dq), lambda b, c: (b, c, 0)),
        out_shape=jax.ShapeDtypeStruct((B, T, dq), F32),
        scratch_shapes=[pltpu.VMEM((L, QB), jnp.int32),
                        pltpu.VMEM((N_HEADS_C // 2, 2 * QB), F32),
                        pltpu.VMEM((N_HEADS_C // 2, 2 * QB), F32),
                        pltpu.VMEM((N_HEADS_C // 2, HEAD_DIM_C, 2 * QB), F32)],
        compiler_params=_params(("arbitrary", "arbitrary")),
    )(z, z, wi_t, keys, vals_t, kidx)


def _router_kernel(x_ref, w_ref, b_ref, e_ref, g_ref):
    logits = lax.dot_general(w_ref[...], x_ref[...], (((1,), (1,)), ((), ())),
                             preferred_element_type=F32, precision=lax.Precision.HIGHEST)
    scores = jax.nn.sigmoid(logits)
    biased = scores + b_ref[...]
    tm = scores.shape[1]
    P = EXPERTS_PER_GROUP

    def ranks(rows):
        out = []
        for i in range(len(rows)):
            rk = jnp.zeros((1, tm), jnp.int32)
            for j in range(len(rows)):
                if j == i:
                    continue
                beat = (rows[j] > rows[i]) | ((rows[j] == rows[i]) & (j < i))
                rk = rk + beat.astype(jnp.int32)
            out.append(rk)
        return out

    b = [biased[i:i + 1] for i in range(N_EXPERTS)]
    s = [scores[i:i + 1] for i in range(N_EXPERTS)]
    grp_rank, grp_score = [], []
    for g in range(N_GROUPS):
        rows = b[g * P:(g + 1) * P]
        rk = ranks(rows)
        grp_rank.append(rk)
        top1 = sum(jnp.where(rk[i] == 0, rows[i], 0.0) for i in range(P))
        top2 = sum(jnp.where(rk[i] == 1, rows[i], 0.0) for i in range(P))
        grp_score.append(top1 + top2)
    gr = ranks(grp_score)
    e1 = jnp.zeros((1, tm), jnp.int32)
    e2 = jnp.zeros((1, tm), jnp.int32)
    w1 = jnp.zeros((1, tm), F32)
    w2 = jnp.zeros((1, tm), F32)
    for g in range(N_GROUPS):
        best = gr[g] == 0
        for i in range(P):
            is1 = best & (grp_rank[g][i] == 0)
            is2 = best & (grp_rank[g][i] == 1)
            e1 = jnp.where(is1, g * P + i, e1)
            e2 = jnp.where(is2, g * P + i, e2)
            w1 = jnp.where(is1, s[g * P + i], w1)
            w2 = jnp.where(is2, s[g * P + i], w2)
    tot = w1 + w2
    e_ref[...] = jnp.concatenate([e1, e2], axis=0)
    g_ref[...] = jnp.concatenate([w1 / tot, w2 / tot], axis=0)


def route(xt, router_w, router_b):
    T, D = xt.shape
    tm = _pick(T, (512, 256, 128))
    return pl.pallas_call(
        _router_kernel,
        grid=(T // tm,),
        in_specs=[pl.BlockSpec((tm, D), lambda i: (i, 0)),
                  pl.BlockSpec((N_EXPERTS, D), lambda i: (0, 0)),
                  pl.BlockSpec((N_EXPERTS, 1), lambda i: (0, 0))],
        out_specs=[pl.BlockSpec((TOP_K, tm), lambda i: (0, i)),
                   pl.BlockSpec((TOP_K, tm), lambda i: (0, i))],
        out_shape=[jax.ShapeDtypeStruct((TOP_K, T), jnp.int32),
                   jax.ShapeDtypeStruct((TOP_K, T), F32)],
        compiler_params=_params(("arbitrary",)),
    )(xt, router_w.T, router_b.reshape(N_EXPERTS, 1))


def _ffn_kernel(be_ref, x_ref, wg_ref, wu_ref, wd_ref, o_ref):
    del be_ref
    x = x_ref[...].astype(BF16)
    a = jnp.dot(x, wg_ref[0, 0], preferred_element_type=F32)
    u = jnp.dot(x, wu_ref[0, 0], preferred_element_type=F32)
    h = (a * jax.nn.sigmoid(a)) * u
    o_ref[...] = jnp.dot(h.astype(BF16), wd_ref[0, 0], preferred_element_type=F32)


def _cast_kernel(x_ref, o_ref):
    o_ref[...] = x_ref[...].astype(o_ref.dtype)


def cast_bf16(w):
    n = w.shape[-1]
    w2 = w.reshape(-1, n)
    rows = w2.shape[0]
    tr = _pick(rows, (1024 * 1024 // n, 512, 256, 128, 64, 32, 16))
    out = pl.pallas_call(
        _cast_kernel,
        grid=(rows // tr,),
        in_specs=[pl.BlockSpec((tr, n), lambda i: (i, 0))],
        out_specs=pl.BlockSpec((tr, n), lambda i: (i, 0)),
        out_shape=jax.ShapeDtypeStruct(w2.shape, BF16),
        compiler_params=_params(("arbitrary",)),
    )(w2)
    return out.reshape(w.shape)


def expert_ffn(xs, block_exp, layer, wg, wu, wd):
    R, D = xs.shape
    F = wg.shape[3]
    nb = R // MOE_TM
    grid_spec = pltpu.PrefetchScalarGridSpec(
        num_scalar_prefetch=1,
        grid=(nb,),
        in_specs=[pl.BlockSpec((MOE_TM, D), lambda i, be: (i, 0)),
                  pl.BlockSpec((1, 1, D, F), lambda i, be: (layer, be[i], 0, 0)),
                  pl.BlockSpec((1, 1, D, F), lambda i, be: (layer, be[i], 0, 0)),
                  pl.BlockSpec((1, 1, F, D), lambda i, be: (layer, be[i], 0, 0))],
        out_specs=pl.BlockSpec((MOE_TM, D), lambda i, be: (i, 0)),
    )
    return pl.pallas_call(
        _ffn_kernel,
        grid_spec=grid_spec,
        out_shape=jax.ShapeDtypeStruct((R, D), F32),
        compiler_params=_params(("arbitrary",)),
    )(block_exp, xs, wg, wu, wd)


def _combine_ln_kernel(ya_ref, yb_ref, ga_ref, gb_ref, r_ref, g_ref, b_ref, o_ref):
    y = ya_ref[...] * ga_ref[...] + yb_ref[...] * gb_ref[...]
    o_ref[...] = _deepnorm(r_ref[...], y, g_ref[...], b_ref[...])


def combine_ln(ya, yb, ga, gb, resid, g, b):
    M, N = ya.shape
    tm = _pick(M, (256, 128, 64, 32, 16, 8))
    blk = pl.BlockSpec((tm, N), lambda i: (i, 0))
    col = pl.BlockSpec((tm, 1), lambda i: (i, 0))
    row = pl.BlockSpec((1, N), lambda i: (0, 0))
    return pl.pallas_call(
        _combine_ln_kernel,
        grid=(M // tm,),
        in_specs=[blk, blk, col, col, blk, row, row],
        out_specs=blk,
        out_shape=jax.ShapeDtypeStruct((M, N), F32),
        compiler_params=_params(("arbitrary",)),
    )(ya, yb, ga, gb, resid, g.reshape(1, N), b.reshape(1, N))


def moe_layer(x, router_w, router_b, layer, wg, wu, wd, ln_g, ln_b):
    shp = x.shape
    xt = x.reshape(-1, shp[-1])
    T = xt.shape[0]
    expert, gate = route(xt, router_w, router_b)
    flat_e = expert.T.reshape(-1)
    n_assign = T * TOP_K
    onehot = (flat_e[:, None] == jnp.arange(N_EXPERTS, dtype=jnp.int32)[None]).astype(jnp.int32)
    csum = jnp.cumsum(onehot, axis=0)
    rank = jnp.take_along_axis(csum, flat_e[:, None], axis=1)[:, 0] - 1
    counts = csum[-1]
    padded = (counts + MOE_TM - 1) // MOE_TM * MOE_TM
    pad_end = jnp.cumsum(padded)
    pad_start = pad_end - padded
    dest = pad_start[flat_e] + rank
    nb = -(-n_assign // MOE_TM) + N_EXPERTS
    rows = nb * MOE_TM
    row_tok = jnp.zeros(rows, jnp.int32).at[dest].set(jnp.arange(n_assign, dtype=jnp.int32) // TOP_K)
    block_exp = jnp.minimum(
        jnp.searchsorted(pad_end, jnp.arange(nb, dtype=jnp.int32) * MOE_TM, side='right'),
        N_EXPERTS - 1).astype(jnp.int32)
    yb = expert_ffn(xt[row_tok], block_exp, layer, wg, wu, wd)
    d2 = dest.reshape(T, TOP_K)
    out = combine_ln(yb[d2[:, 0]], yb[d2[:, 1]], gate[0][:, None], gate[1][:, None], xt, ln_g, ln_b)
    return out.reshape(shp)


def _block_diag(w):
    G, I, O = w.shape
    eye = jnp.eye(G, dtype=w.dtype)
    return (eye[:, None, :, None] * w[:, :, None, :]).reshape(G * I, G * O)


def even_layer(x, lru_h, lru_conv, wkv, shift, p, ln_g, ln_b):
    B, T, D = x.shape
    D_LRU = lru_h.shape[-1]
    D_RWKV = p['w0'].shape[1]
    H = D_RWKV // RWKV_HEAD
    N = RWKV_HEAD
    z = matmul(x.reshape(B * T, D), p['w_in']).reshape(B, T, -1)
    y_a, r, decay, k, v, a, g, h_T = even_pre(z, lru_conv, shift, lru_h, p)

    lanes = lambda t: jnp.transpose(t.reshape(B, T, H, N), (1, 3, 0, 2)).reshape(T, N, B * H)
    per_chain = lambda w: jnp.tile(w.reshape(H, N).T, (1, B))
    s0 = jnp.transpose(wkv, (2, 3, 0, 1)).reshape(N, N, B * H)
    y, s_T = rwkv_scan(lanes(r), lanes(decay), lanes(k), lanes(v), lanes(a),
                       per_chain(p['k_k']), per_chain(p['k_a']), per_chain(p['gn_g']),
                       per_chain(p['gn_b']), per_chain(p['r_k']), s0)
    y = jnp.transpose(y.reshape(T, N, B, H), (2, 0, 3, 1)).reshape(B * T, D_RWKV)
    S_T = jnp.transpose(s_T.reshape(N, N, B, H), (2, 3, 0, 1))

    x_new = mix_ln(y_a.reshape(B * T, D_LRU), y, g.reshape(B * T, D_RWKV), p['w_out'],
                   x.reshape(B * T, D), ln_g, ln_b)
    conv_state = jnp.concatenate([lru_conv, z[..., :D_LRU]], axis=1)[:, T:]
    return x_new.reshape(B, T, D), (h_T[:, 0], conv_state, S_T, z[:, -1:, 2 * D_LRU:])


def odd_layer(x, cache, p, ln_g, ln_b):
    B, T, D = x.shape
    dq = N_HEADS_C * HEAD_DIM_C
    dkv = N_KV_C * HEAD_DIM_C
    dqi = N_IDX_HEADS * D_IDX
    z = matmul(x.reshape(B * T, D), p['w_in']).reshape(B, T, -1)
    k = z[..., dq:dq + dkv]
    v = z[..., dq + dkv:dq + 2 * dkv]
    o = dq + 2 * dkv + dqi
    ki = z[..., o:o + D_IDX]
    wi = z[..., o + D_IDX:o + D_IDX + N_IDX_HEADS] * (dqi ** -0.5)
    Tp = -(-T // LANES) * LANES
    if cache is None:
        past = 0
        n_sel = min(TOPK_MAX, T // 4)
        parts = lambda c, n: [n]
    else:
        past = cache[0].shape[1]
        n_sel = min(TOPK_MAX, (past + T) // 4)
        parts = lambda c, n: [c.reshape(B, past, -1), n]
    L = -(-(past + Tp) // LANES) * LANES
    tk = _pick(L, (512, 384, 128))
    cat = lambda c, n: jnp.concatenate(
        parts(c, n) + [jnp.zeros((B, L - past - T, n.shape[-1]), n.dtype)], axis=1).astype(BF16)
    ck, cv, cki = cache if cache is not None else (None, None, None)
    keys, vals, kidx = cat(ck, k), cat(cv, v), cat(cki, ki)
    vals_t = jnp.swapaxes(vals.reshape(B, L // tk, tk, dkv), 2, 3)
    rowpad = lambda t: jnp.pad(t, ((0, 0), (0, Tp - T), (0, 0)))
    wi_t = jnp.swapaxes(rowpad(wi), 1, 2)
    o_att = dsa_attention(rowpad(z), 0, (dq + 2 * dkv) // dqi, wi_t, keys, vals_t, kidx, past, n_sel, tk)
    x_new = matmul_ln(o_att[:, :T].reshape(B * T, dq), p['w_out'], x.reshape(B * T, D), ln_g, ln_b)
    new = (k.reshape(B, T, N_KV_C, HEAD_DIM_C), v.reshape(B, T, N_KV_C, HEAD_DIM_C), ki)
    return x_new.reshape(B, T, D), new


def run_trunk(x, even_state, odd_cache, ev, od, norm_w, moe_w):
    ln1_g, ln1_b, ln2_g, ln2_b = norm_w
    router_w, router_b, wg, wu, wd = moe_w
    x, new_even = even_layer(x, *even_state, ev, ln1_g[0], ln1_b[0])
    x = moe_layer(x, router_w, router_b, 0, wg, wu, wd, ln2_g[0], ln2_b[0])
    x, new_odd = odd_layer(x, odd_cache, od, ln1_g[1], ln1_b[1])
    x = moe_layer(x, router_w, router_b, 1, wg, wu, wd, ln2_g[1], ln2_b[1])
    return x, new_even, new_odd


def kernel(x_prompt, x_sample, state_lru_h, state_lru_conv, state_rwkv_wkv, state_rwkv_shift, cache_k, cache_v, cache_kidx, ev_w_in, ev_conv_w, ev_conv_b, ev_gate_r_w, ev_gate_r_b, ev_gate_i_w, ev_gate_i_b, ev_lambda, ev_shift_mu, ev_w0, ev_w2, ev_a0, ev_a2, ev_g2, ev_k_k, ev_k_a, ev_r_k, ev_gn_g, ev_gn_b, ev_w_out, od_w_in, od_w_out, ln1_g, ln1_b, ln2_g, ln2_b, router_w, router_b, moe_w_gate, moe_w_up, moe_w_down):
    ev = dict(
        w_in=ev_w_in[0].astype(BF16), conv_w=ev_conv_w[0], conv_b=ev_conv_b[0][None],
        gate_w=jnp.concatenate([_block_diag(ev_gate_r_w[0]), _block_diag(ev_gate_i_w[0])], axis=1).astype(BF16),
        gate_b=jnp.concatenate([ev_gate_r_b[0], ev_gate_i_b[0]])[None],
        lru_c=(-LRU_C * jax.nn.softplus(-ev_lambda[0]))[None],
        shift_mu=ev_shift_mu[0][None], w0=ev_w0[0][None], a0=ev_a0[0][None],
        w2=ev_w2[0].astype(BF16), a2=ev_a2[0].astype(BF16), g2=ev_g2[0].astype(BF16),
        k_k=ev_k_k[0][None], k_a=ev_k_a[0][None], r_k=ev_r_k[0], gn_g=ev_gn_g[0], gn_b=ev_gn_b[0],
        w_out=ev_w_out[0].astype(BF16))
    n_in = od_w_in.shape[2]
    od = dict(w_in=jnp.pad(od_w_in[0], ((0, 0), (0, (-n_in) % LANES))).astype(BF16),
              w_out=od_w_out[0].astype(BF16))
    norm_w = (ln1_g, ln1_b, ln2_g, ln2_b)
    moe_w = (router_w, router_b, cast_bf16(moe_w_gate), cast_bf16(moe_w_up), cast_bf16(moe_w_down))

    B = x_prompt.shape[0]
    fresh = (jnp.zeros((B,) + state_lru_h.shape[2:], F32),
             jnp.zeros((B,) + state_lru_conv.shape[2:], F32),
             jnp.zeros((B,) + state_rwkv_wkv.shape[2:], F32),
             jnp.zeros((B,) + state_rwkv_shift.shape[2:], F32))
    y_p, pe, po = run_trunk(x_prompt, fresh, None, ev, od, norm_w, moe_w)
    y_s, se, so = run_trunk(
        x_sample, (state_lru_h[0], state_lru_conv[0], state_rwkv_wkv[0], state_rwkv_shift[0]),
        (cache_k[0], cache_v[0], cache_kidx[0]), ev, od, norm_w, moe_w)
    st = lambda t: t[None]
    return (y_p, y_s, *map(st, pe), *map(st, po), *map(st, se), *map(st, so))
```

```python
import functools
import math

import jax
import jax.numpy as jnp
from jax import lax
from jax.experimental import pallas as pl
from jax.experimental.pallas import tpu as pltpu

F32 = jnp.float32
BF16 = jnp.bfloat16

CHUNK = 64
LRU_BLOCKS = 16
CONV_W = 4
LRU_C = 8.0
RWKV_HEAD = 64
DECAY_LORA = 64
AAA_LORA = 64
GATE_LORA = 128
N_HEADS_C = 16
HEAD_DIM_C = 128
N_KV_C = 4
N_IDX_HEADS = 16
D_IDX = 64
TOPK_MAX = 256
N_EXPERTS = 16
N_GROUPS = 4
EXPERTS_PER_GROUP = N_EXPERTS // N_GROUPS
TOP_K = 2
DEPTH = 2
DN_ALPHA = (2 * DEPTH) ** 0.25
LN_EPS = 1e-5
GN_EPS = 64e-5

LANES = 128
SUBLANES = 8
VMEM_LIMIT = 56 * 1024 * 1024
MOE_TM = 256


def _params(sem):
    return pltpu.CompilerParams(dimension_semantics=sem, vmem_limit_bytes=VMEM_LIMIT)


def _mm_kernel(x_ref, w_ref, o_ref, xb_ref):
    @pl.when(pl.program_id(1) == 0)
    def _():
        xb_ref[...] = x_ref[...].astype(BF16)

    o_ref[...] = jnp.dot(xb_ref[...], w_ref[...], preferred_element_type=F32)


def _pick(n, cands):
    for c in cands:
        if n % c == 0:
            return c
    return n


def matmul(x, w):
    M, K = x.shape
    N = w.shape[1]
    tm = _pick(M, (1024, 512, 256, 128, 64, 32, 16, 8))
    tn = _pick(N, (768, 1024, 1408, 512, 384, 256, 128))
    return pl.pallas_call(
        _mm_kernel,
        grid=(M // tm, N // tn),
        in_specs=[pl.BlockSpec((tm, K), lambda i, j: (i, 0)),
                  pl.BlockSpec((K, tn), lambda i, j: (0, j))],
        out_specs=pl.BlockSpec((tm, tn), lambda i, j: (i, j)),
        out_shape=jax.ShapeDtypeStruct((M, N), F32),
        scratch_shapes=[pltpu.VMEM((tm, K), BF16)],
        compiler_params=_params(("arbitrary", "arbitrary")),
    )(x, w)


def _deepnorm(resid, y, g, b):
    h = DN_ALPHA * resid + y
    mu = jnp.mean(h, axis=-1, keepdims=True)
    hc = h - mu
    var = jnp.mean(hc * hc, axis=-1, keepdims=True)
    return hc * lax.rsqrt(var + LN_EPS) * g + b


def _mm_ln_kernel(x_ref, w_ref, r_ref, g_ref, b_ref, o_ref):
    y = jnp.dot(x_ref[...].astype(BF16), w_ref[...], preferred_element_type=F32)
    o_ref[...] = _deepnorm(r_ref[...], y, g_ref[...], b_ref[...])


def _mix_ln_kernel(ya_ref, yb_ref, gate_ref, w_ref, r_ref, g_ref, b_ref, o_ref):
    mix = jnp.concatenate([ya_ref[...], yb_ref[...] * gate_ref[...]], axis=1).astype(BF16)
    y = jnp.dot(mix, w_ref[...], preferred_element_type=F32)
    o_ref[...] = _deepnorm(r_ref[...], y, g_ref[...], b_ref[...])


def matmul_ln(x, w, resid, g, b):
    M, K = x.shape
    N = w.shape[1]
    tm = _pick(M, (256, 128, 64, 32, 16, 8))
    row = pl.BlockSpec((1, N), lambda i: (0, 0))
    return pl.pallas_call(
        _mm_ln_kernel,
        grid=(M // tm,),
        in_specs=[pl.BlockSpec((tm, K), lambda i: (i, 0)), pl.BlockSpec((K, N), lambda i: (0, 0)),
                  pl.BlockSpec((tm, N), lambda i: (i, 0)), row, row],
        out_specs=pl.BlockSpec((tm, N), lambda i: (i, 0)),
        out_shape=jax.ShapeDtypeStruct((M, N), F32),
        compiler_params=_params(("arbitrary",)),
    )(x, w, resid, g.reshape(1, N), b.reshape(1, N))


def mix_ln(ya, yb, gate, w, resid, g, b):
    M, Kh = ya.shape
    N = w.shape[1]
    tm = _pick(M, (256, 128, 64, 32, 16, 8))
    row = pl.BlockSpec((1, N), lambda i: (0, 0))
    half = pl.BlockSpec((tm, Kh), lambda i: (i, 0))
    return pl.pallas_call(
        _mix_ln_kernel,
        grid=(M // tm,),
        in_specs=[half, half, half, pl.BlockSpec((2 * Kh, N), lambda i: (0, 0)),
                  pl.BlockSpec((tm, N), lambda i: (i, 0)), row, row],
        out_specs=pl.BlockSpec((tm, N), lambda i: (i, 0)),
        out_shape=jax.ShapeDtypeStruct((M, N), F32),
        compiler_params=_params(("arbitrary",)),
    )(ya, yb, gate, w, resid, g.reshape(1, N), b.reshape(1, N))


def _expm1(x):
    e = jnp.exp(x)
    safe = jnp.where((e == 1.0) | (e == 0.0), 2.0, e)
    return jnp.where(e == 1.0, x, jnp.where(e == 0.0, -1.0, (e - 1.0) * x / jnp.log(safe)))


def _softplus(x):
    return jnp.maximum(x, 0.0) + jnp.log(1.0 + jnp.exp(-jnp.abs(x)))


HALO = SUBLANES


def _even_pre_kernel(ax_ref, ag_ref, zr_ref, zk_ref, zv_ref, zt_ref,
                     conv_ref, sr_ref, sk_ref, sv_ref, st_ref, h0_ref,
                     cw_ref, cb_ref, gw_ref, gb_ref, lc_ref, mu_ref,
                     w0_ref, a0_ref, w2_ref, a2_ref, g2_ref,
                     ya_ref, r_ref, w_ref, k_ref, v_ref, a_ref, g_ref, hT_ref,
                     xp_scr, pr_scr, pk_scr, pv_scr, pt_scr, a_scr, u_scr, hs_scr, h_scr):
    i = pl.program_id(1)
    tm = ax_ref.shape[1]
    C = ax_ref.shape[2]
    prev = ((xp_scr, None), (pr_scr, sr_ref), (pk_scr, sk_ref), (pv_scr, sv_ref), (pt_scr, st_ref))

    @pl.when(i == 0)
    def _():
        xp_scr[0:HALO, :] = jnp.zeros((HALO, C), F32)
        xp_scr[HALO - (CONV_W - 1):HALO, :] = conv_ref[0]
        for scr, ref in prev[1:]:
            scr[0:HALO, :] = jnp.broadcast_to(ref[0], (HALO, scr.shape[1]))
        h_scr[...] = jnp.broadcast_to(h0_ref[0], h_scr.shape)

    cur = pl.ds(HALO, tm)
    xp_scr[cur, :] = ax_ref[0]
    pr_scr[cur, :] = zr_ref[0]
    pk_scr[cur, :] = zk_ref[0]
    pv_scr[cur, :] = zv_ref[0]
    pt_scr[cur, :] = zt_ref[0]

    xa = cb_ref[...]
    for j in range(CONV_W):
        xa = xa + cw_ref[j:j + 1, :] * xp_scr[pl.ds(HALO - (CONV_W - 1) + j, tm), :]
    gates = jnp.dot(xa.astype(BF16), gw_ref[...], preferred_element_type=F32) + gb_ref[...]
    log_a = lc_ref[...] * jax.nn.sigmoid(gates[:, :C])
    a_scr[...] = jnp.exp(log_a)
    u_scr[...] = jnp.sqrt(-_expm1(2.0 * log_a)) * (jax.nn.sigmoid(gates[:, C:]) * xa)

    def step(t, h):
        h = a_scr[pl.ds(t, 1), :] * h + u_scr[pl.ds(t, 1), :]
        hs_scr[pl.ds(t, 1), :] = h
        return h

    h = lax.fori_loop(0, tm, step, h_scr[0:1, :], unroll=8)
    h_scr[...] = jnp.broadcast_to(h, h_scr.shape)
    hT_ref[0] = h
    ya_ref[0] = hs_scr[...] * jax.nn.gelu(ag_ref[0], approximate=True)

    def shifted(scr, lo, hi):
        zb = scr[cur, :]
        return zb + mu_ref[:, lo:hi] * (scr[pl.ds(HALO - 1, tm), :] - zb)

    r = shifted(pr_scr, 0, C)
    k = shifted(pk_scr, C, 2 * C)
    v = shifted(pv_scr, 2 * C, 3 * C)
    tail = shifted(pt_scr, 3 * C, 3 * C + pt_scr.shape[1])
    wl = jnp.tanh(tail[:, :DECAY_LORA])
    al = tail[:, DECAY_LORA:DECAY_LORA + AAA_LORA]
    gl = jax.nn.sigmoid(tail[:, DECAY_LORA + AAA_LORA:])
    lora = lambda x, w_ref_: jnp.dot(x.astype(BF16), w_ref_[...], preferred_element_type=F32)
    w_log = -_softplus(-(w0_ref[...] + lora(wl, w2_ref))) - 0.5
    a = jax.nn.sigmoid(a0_ref[...] + lora(al, a2_ref))
    r_ref[0] = r
    w_ref[0] = jnp.exp(-jnp.exp(w_log))
    k_ref[0] = k
    v_ref[0] = v
    a_ref[0] = a
    g_ref[0] = lora(gl, g2_ref)

    for scr, _ in prev:
        scr[0:HALO, :] = scr[pl.ds(tm, HALO), :]


def even_pre(z, lru_conv, shift, lru_h, p):
    B, T, _ = z.shape
    C = lru_h.shape[-1]
    tail = DECAY_LORA + AAA_LORA + GATE_LORA
    tm = _pick(T, (128, 64, 32, 16, 8))
    zblk = lambda w, col: pl.BlockSpec((1, tm, w), lambda b, i: (b, i, col))
    sblk = lambda rows, w, col: pl.BlockSpec((1, rows, w), lambda b, i: (b, 0, col))
    full = lambda a: pl.BlockSpec(a.shape, lambda b, i: (0,) * a.ndim)
    weights = [p['conv_w'], p['conv_b'], p['gate_w'], p['gate_b'], p['lru_c'], p['shift_mu'],
               p['w0'], p['a0'], p['w2'], p['a2'], p['g2']]
    tok = jax.ShapeDtypeStruct((B, T, C), F32)
    return pl.pallas_call(
        _even_pre_kernel,
        grid=(B, T // tm),
        in_specs=[zblk(C, 0), zblk(C, 1), zblk(C, 2), zblk(C, 3), zblk(C, 4), zblk(tail, 5 * C // tail),
                  sblk(CONV_W - 1, C, 0), sblk(1, C, 0), sblk(1, C, 1), sblk(1, C, 2),
                  sblk(1, tail, 3 * C // tail), sblk(1, C, 0)] + [full(w) for w in weights],
        out_specs=[zblk(C, 0)] * 7 + [sblk(1, C, 0)],
        out_shape=[tok] * 7 + [jax.ShapeDtypeStruct((B, 1, C), F32)],
        scratch_shapes=[pltpu.VMEM((tm + HALO, C), F32)] * 4 + [pltpu.VMEM((tm + HALO, tail), F32)]
                       + [pltpu.VMEM((tm, C), F32)] * 3 + [pltpu.VMEM((SUBLANES, C), F32)],
        compiler_params=_params(("arbitrary", "arbitrary")),
    )(z, z, z, z, z, z, lru_conv, shift, shift, shift, shift, lru_h[:, None, :], *weights)


def _colsum(x):
    n = x.shape[0] // SUBLANES
    p = x.reshape(n, SUBLANES, x.shape[1]).sum(axis=0)
    p = p + pltpu.roll(p, 4, axis=0)
    p = p + pltpu.roll(p, 2, axis=0)
    p = p + pltpu.roll(p, 1, axis=0)
    return p


def _bcast_rows(p, n):
    return jnp.broadcast_to(p[None], (n // SUBLANES,) + p.shape).reshape(n, p.shape[1])


def _rwkv_kernel(r_ref, w_ref, k_ref, v_ref, a_ref, kk_ref, ka_ref, gng_ref, gnb_ref, rk_ref, s0_ref,
                 y_ref, sT_ref, s_scr):
    tc = pl.program_id(1)
    N = RWKV_HEAD

    @pl.when(tc == 0)
    def _():
        s_scr[...] = s0_ref[...]

    sub = lax.broadcasted_iota(jnp.int32, (SUBLANES, LANES), 0)

    def step(t, carry):
        r = r_ref[t]
        w = w_ref[t]
        k_raw = k_ref[t]
        a = a_ref[t]
        kk = k_raw * kk_ref[...]
        kk = kk / jnp.maximum(_bcast_rows(jnp.sqrt(_colsum(kk * kk)), N), 1e-12)
        nkk = -kk
        kka = kk * a
        k = k_raw * (1.0 + (a - 1.0) * ka_ref[...])

        def vgroup(g, c2):
            yacc = jnp.zeros((SUBLANES, LANES), F32)
            for j in range(SUBLANES):
                vi = g * SUBLANES + j
                S = s_scr[vi]
                sa = _bcast_rows(_colsum(S * nkk), N)
                vb = jnp.broadcast_to(v_ref[t, pl.ds(vi, 1), :], (N, LANES))
                S = S * w + sa * kka + vb * k
                s_scr[vi] = S
                yacc = jnp.where(sub == j, _colsum(S * r), yacc)
            y_ref[t, pl.ds(pl.multiple_of(g * SUBLANES, SUBLANES), SUBLANES), :] = yacc
            return c2

        lax.fori_loop(0, N // SUBLANES, vgroup, 0, unroll=True)

        y = y_ref[t]
        d = y - _bcast_rows(_colsum(y), N) * (1.0 / N)
        var = _bcast_rows(_colsum(d * d), N) * (1.0 / N)
        bonus = _bcast_rows(_colsum(r * k * rk_ref[...]), N)
        y_ref[t] = d * lax.rsqrt(var + GN_EPS) * gng_ref[...] + gnb_ref[...] + bonus * v_ref[t]
        return carry

    lax.fori_loop(0, r_ref.shape[0], step, 0)

    @pl.when(tc == pl.num_programs(1) - 1)
    def _():
        sT_ref[...] = s_scr[...]


def rwkv_scan(r, w, k, v, a, k_k, k_a, gn_g, gn_b, r_k, s0):
    T, N, P = r.shape
    tc = _pick(T, (32, 16, 8))
    blk = pl.BlockSpec((tc, N, LANES), lambda p, t: (t, 0, p))
    par = pl.BlockSpec((N, LANES), lambda p, t: (0, p))
    st = pl.BlockSpec((N, N, LANES), lambda p, t: (0, 0, p))
    return pl.pallas_call(
        _rwkv_kernel,
        grid=(P // LANES, T // tc),
        in_specs=[blk] * 5 + [par] * 5 + [st],
        out_specs=[blk, st],
        out_shape=[jax.ShapeDtypeStruct((T, N, P), F32), jax.ShapeDtypeStruct((N, N, P), F32)],
        scratch_shapes=[pltpu.VMEM((N, N, LANES), F32)],
        compiler_params=_params(("arbitrary", "arbitrary")),
    )(r, w, k, v, a, k_k, k_a, gn_g, gn_b, r_k, s0)


MASKED = -1e30


KEY_NEG_INF = -2139095041
QK_SPLIT = 2


def _tree_sum(parts):
    while len(parts) > 1:
        parts = [parts[i] + parts[i + 1] for i in range(0, len(parts) - 1, 2)] + parts[len(parts) & ~1:]
    return parts[0]


def _dsa_kernel(q_ref, qi_ref, wi_ref, k_ref, vt_ref, ki_ref, o_ref,
                key_scr, m_scr, l_scr, acc_scr, *, past, n_sel, scale, tk):
    qb = pl.program_id(1)
    QB = q_ref.shape[1]
    L = k_ref.shape[1]
    NP = N_HEADS_C // 2
    lane_chunk = lax.broadcasted_iota(jnp.int32, (1, QB), 1) // CHUNK
    limit = past + (qb * (QB // CHUNK) + lane_chunk + 1) * CHUNK
    n_tiles = (past + (qb + 1) * QB + tk - 1) // tk
    u = 2 if (tk < 512 and (L // tk) % 2 == 0) else 1
    n_count = (n_tiles + u - 1) // u
    row = lax.broadcasted_iota(jnp.int32, (tk, QB), 0)
    tile = lambda kt: pl.ds(pl.multiple_of(kt * tk, tk), tk)

    qi = qi_ref[0].astype(BF16)
    qi_pair = [jnp.concatenate([qi[:, (2 * p) * D_IDX:(2 * p + 1) * D_IDX],
                                qi[:, (2 * p + 1) * D_IDX:(2 * p + 2) * D_IDX]], axis=0)
               for p in range(N_IDX_HEADS // 2)]
    wi = wi_ref[0]

    def score_tile(kt, c):
        @pl.when(kt < n_tiles)
        def _():
            kidx = ki_ref[0, tile(kt), :]
            terms = []
            for p in range(N_IDX_HEADS // 2):
                s = lax.dot_general(kidx, qi_pair[p], (((1,), (1,)), ((), ())),
                                    preferred_element_type=F32)
                for e in range(2):
                    h = 2 * p + e
                    terms.append(wi[h:h + 1, :] * jnp.maximum(s[:, e * QB:(e + 1) * QB], 0.0))
            isc = terms[0]
            for t in terms[1:]:
                isc = isc + t
            ok = row + kt * tk < limit
            bits = lax.bitcast_convert_type(jnp.where(ok, isc + 0.0, -jnp.inf), jnp.int32)
            key_scr[tile(kt), :] = bits ^ ((bits >> 31) & jnp.int32(0x7FFFFFFF))

        @pl.when(kt >= n_tiles)
        def _():
            key_scr[tile(kt), :] = jnp.full((tk, QB), KEY_NEG_INF, jnp.int32)

        return c

    lax.fori_loop(0, n_count * u, score_tile, 0)

    int_min = jnp.int32(-2 ** 31)
    ct = tk * u

    def search(i, t):
        cand = t | lax.shift_left(jnp.int32(1), 31 - i)
        cs = cand ^ int_min

        def count_tile(kt, acc):
            keys = key_scr[pl.ds(pl.multiple_of(kt * ct, ct), ct), :]
            hit = jnp.where(keys >= cs, 1.0, 0.0)
            return acc + _tree_sum([hit[r * SUBLANES:(r + 1) * SUBLANES] for r in range(ct // SUBLANES)])

        acc = lax.fori_loop(0, n_count, count_tile, jnp.zeros((SUBLANES, QB), F32))
        cnt = jnp.sum(acc, axis=0, keepdims=True)
        return jnp.where(cnt >= n_sel, cand, t)

    thr = lax.fori_loop(0, 32, search, jnp.zeros((1, QB), jnp.int32)) ^ int_min

    G = N_HEADS_C // N_KV_C
    dh = HEAD_DIM_C
    q = q_ref[0]
    q_pair = [(jnp.concatenate([q[:, (2 * p) * dh:(2 * p + 1) * dh],
                                q[:, (2 * p + 1) * dh:(2 * p + 2) * dh]], axis=0) * scale).astype(BF16)
              for p in range(NP)]
    m_scr[...] = jnp.full(m_scr.shape, MASKED, F32)
    l_scr[...] = jnp.zeros(l_scr.shape, F32)
    acc_scr[...] = jnp.zeros(acc_scr.shape, F32)

    def att_tile(kt, c):
        sel = (key_scr[tile(kt), :] >= thr) & (row + kt * tk < limit)
        b = jnp.where(sel, 0.0, MASKED)
        b2 = jnp.concatenate([b, b], axis=1)
        def logits(p):
            j = (2 * p) // G
            kj = k_ref[0, tile(kt), j * dh:(j + 1) * dh]
            part = tk // QK_SPLIT
            return jnp.concatenate(
                [lax.dot_general(kj[r * part:(r + 1) * part], q_pair[p], (((1,), (1,)), ((), ())),
                                 preferred_element_type=F32) for r in range(QK_SPLIT)], axis=0)

        ahead = 2
        lg = {p: logits(p) for p in range(ahead)}
        for p in range(NP):
            j = (2 * p) // G
            x = lg.pop(p) + b2
            m = m_scr[p:p + 1, :]
            m_new = jnp.maximum(m, jnp.max(x, axis=0, keepdims=True))
            alpha = jnp.exp2(m - m_new)
            pr = jnp.exp2(x - m_new)
            l_scr[p:p + 1, :] = alpha * l_scr[p:p + 1, :] + jnp.sum(pr, axis=0, keepdims=True)
            m_scr[p:p + 1, :] = m_new
            vt = vt_ref[0, kt, j * dh:(j + 1) * dh, :]
            pv = jnp.dot(vt, pr.astype(BF16), preferred_element_type=F32)
            if p + ahead < NP:
                lg[p + ahead] = logits(p + ahead)
            acc_scr[p] = acc_scr[p] * alpha + pv
        return c

    lax.fori_loop(0, n_tiles, att_tile, 0)
    for p in range(NP):
        o2 = acc_scr[p] / l_scr[p:p + 1, :]
        for s in range(2):
            o_ref[0, :, (2 * p + s) * dh:(2 * p + s + 1) * dh] = o2[:, s * QB:(s + 1) * QB].T


def dsa_attention(z, q_col, qi_col, wi_t, keys, vals_t, kidx, past, n_sel, tk):
    B, T, _ = z.shape
    L = keys.shape[1]
    QB = LANES
    dq = N_HEADS_C * HEAD_DIM_C
    dqi = N_IDX_HEADS * D_IDX
    kern = functools.partial(_dsa_kernel, past=past, n_sel=n_sel,
                             scale=HEAD_DIM_C ** -0.5 * math.log2(math.e), tk=tk)
    return pl.pallas_call(
        kern,
        grid=(B, T // QB),
        in_specs=[pl.BlockSpec((1, QB, dq), lambda b, c: (b, c, q_col)),
                  pl.BlockSpec((1, QB, dqi), lambda b, c: (b, c, qi_col)),
                  pl.BlockSpec((1, N_IDX_HEADS, QB), lambda b, c: (b, 0, c)),
                  pl.BlockSpec((1, L, keys.shape[2]), lambda b, c: (b, 0, 0)),
                  pl.BlockSpec((1,) + vals_t.shape[1:], lambda b, c: (b, 0, 0, 0)),
                  pl.BlockSpec((1, L, kidx.shape[2]), lambda b, c: (b, 0, 0))],
        out_specs=pl.BlockSpec((1, QB, dq), lambda b, c: (b, c, 0)),
        out_shape=jax.ShapeDtypeStruct((B, T, dq), F32),
        scratch_shapes=[pltpu.VMEM((L, QB), jnp.int32),
                        pltpu.VMEM((N_HEADS_C // 2, 2 * QB), F32),
                        pltpu.VMEM((N_HEADS_C // 2, 2 * QB), F32),
                        pltpu.VMEM((N_HEADS_C // 2, HEAD_DIM_C, 2 * QB), F32)],
        compiler_params=_params(("arbitrary", "arbitrary")),
    )(z, z, wi_t, keys, vals_t, kidx)


def _router_kernel(x_ref, w_ref, b_ref, e_ref, g_ref):
    logits = lax.dot_general(w_ref[...], x_ref[...], (((1,), (1,)), ((), ())),
                             preferred_element_type=F32, precision=lax.Precision.HIGHEST)
    scores = jax.nn.sigmoid(logits)
    biased = scores + b_ref[...]
    tm = scores.shape[1]
    P = EXPERTS_PER_GROUP

    def ranks(rows):
        out = []
        for i in range(len(rows)):
            rk = jnp.zeros((1, tm), jnp.int32)
            for j in range(len(rows)):
                if j == i:
                    continue
                beat = (rows[j] > rows[i]) | ((rows[j] == rows[i]) & (j < i))
                rk = rk + beat.astype(jnp.int32)
            out.append(rk)
        return out

    b = [biased[i:i + 1] for i in range(N_EXPERTS)]
    s = [scores[i:i + 1] for i in range(N_EXPERTS)]
    grp_rank, grp_score = [], []
    for g in range(N_GROUPS):
        rows = b[g * P:(g + 1) * P]
        rk = ranks(rows)
        grp_rank.append(rk)
        top1 = sum(jnp.where(rk[i] == 0, rows[i], 0.0) for i in range(P))
        top2 = sum(jnp.where(rk[i] == 1, rows[i], 0.0) for i in range(P))
        grp_score.append(top1 + top2)
    gr = ranks(grp_score)
    e1 = jnp.zeros((1, tm), jnp.int32)
    e2 = jnp.zeros((1, tm), jnp.int32)
    w1 = jnp.zeros((1, tm), F32)
    w2 = jnp.zeros((1, tm), F32)
    for g in range(N_GROUPS):
        best = gr[g] == 0
        for i in range(P):
            is1 = best & (grp_rank[g][i] == 0)
            is2 = best & (grp_rank[g][i] == 1)
            e1 = jnp.where(is1, g * P + i, e1)
            e2 = jnp.where(is2, g * P + i, e2)
            w1 = jnp.where(is1, s[g * P + i], w1)
            w2 = jnp.where(is2, s[g * P + i], w2)
    tot = w1 + w2
    e_ref[...] = jnp.concatenate([e1, e2], axis=0)
    g_ref[...] = jnp.concatenate([w1 / tot, w2 / tot], axis=0)


def route(xt, router_w, router_b):
    T, D = xt.shape
    tm = _pick(T, (512, 256, 128))
    return pl.pallas_call(
        _router_kernel,
        grid=(T // tm,),
        in_specs=[pl.BlockSpec((tm, D), lambda i: (i, 0)),
                  pl.BlockSpec((N_EXPERTS, D), lambda i: (0, 0)),
                  pl.BlockSpec((N_EXPERTS, 1), lambda i: (0, 0))],
        out_specs=[pl.BlockSpec((TOP_K, tm), lambda i: (0, i)),
                   pl.BlockSpec((TOP_K, tm), lambda i: (0, i))],
        out_shape=[jax.ShapeDtypeStruct((TOP_K, T), jnp.int32),
                   jax.ShapeDtypeStruct((TOP_K, T), F32)],
        compiler_params=_params(("arbitrary",)),
    )(xt, router_w.T, router_b.reshape(N_EXPERTS, 1))


FF_SPLIT = 2


def _ffn_kernel(be_ref, x_ref, part_ref, wg_ref, wu_ref, wd_ref, o_ref, wg_s, wu_s, wd_s):
    j = pl.program_id(0)
    i = pl.program_id(1)

    @pl.when((i == 0) | (be_ref[i] != be_ref[jnp.maximum(i - 1, 0)]))
    def _():
        wg_s[...] = wg_ref[0, 0].astype(BF16)
        wu_s[...] = wu_ref[0, 0].astype(BF16)
        wd_s[...] = wd_ref[0, 0].astype(BF16)

    x = x_ref[...].astype(BF16)
    a = jnp.dot(x, wg_s[...], preferred_element_type=F32)
    u = jnp.dot(x, wu_s[...], preferred_element_type=F32)
    h = (a * jax.nn.sigmoid(a)) * u
    y = jnp.dot(h.astype(BF16), wd_s[...], preferred_element_type=F32)

    @pl.when(j == 0)
    def _():
        o_ref[...] = y

    @pl.when(j != 0)
    def _():
        o_ref[...] = part_ref[...] + y


def _no_init_kernel(o_ref):
    del o_ref


def expert_ffn(xs, block_exp, layer, wg, wu, wd):
    R, D = xs.shape
    F = wg.shape[3]
    Fs = F // FF_SPLIT
    nb = R // MOE_TM
    rows = lambda j, i, be: (i, 0)
    grid_spec = pltpu.PrefetchScalarGridSpec(
        num_scalar_prefetch=1,
        grid=(FF_SPLIT, nb),
        in_specs=[pl.BlockSpec((MOE_TM, D), rows),
                  pl.BlockSpec((MOE_TM, D), lambda j, i, be: (jnp.where(j == 0, nb - 1, i), 0)),
                  pl.BlockSpec((1, 1, D, Fs), lambda j, i, be: (layer, be[i], 0, j)),
                  pl.BlockSpec((1, 1, D, Fs), lambda j, i, be: (layer, be[i], 0, j)),
                  pl.BlockSpec((1, 1, Fs, D), lambda j, i, be: (layer, be[i], j, 0))],
        out_specs=pl.BlockSpec((MOE_TM, D), rows),
        scratch_shapes=[pltpu.VMEM((D, Fs), BF16), pltpu.VMEM((D, Fs), BF16), pltpu.VMEM((Fs, D), BF16)],
    )
    partial = pl.pallas_call(_no_init_kernel, out_shape=jax.ShapeDtypeStruct((R, D), F32),
                             out_specs=pl.BlockSpec(memory_space=pl.ANY))()
    return pl.pallas_call(
        _ffn_kernel,
        grid_spec=grid_spec,
        out_shape=jax.ShapeDtypeStruct((R, D), F32),
        input_output_aliases={2: 0},
        compiler_params=_params(("arbitrary", "arbitrary")),
    )(block_exp, xs, partial, wg, wu, wd)


def _combine_ln_kernel(ya_ref, yb_ref, ga_ref, gb_ref, r_ref, g_ref, b_ref, o_ref):
    y = ya_ref[...] * ga_ref[...] + yb_ref[...] * gb_ref[...]
    o_ref[...] = _deepnorm(r_ref[...], y, g_ref[...], b_ref[...])


def combine_ln(ya, yb, ga, gb, resid, g, b):
    M, N = ya.shape
    tm = _pick(M, (256, 128, 64, 32, 16, 8))
    blk = pl.BlockSpec((tm, N), lambda i: (i, 0))
    col = pl.BlockSpec((tm, 1), lambda i: (i, 0))
    row = pl.BlockSpec((1, N), lambda i: (0, 0))
    return pl.pallas_call(
        _combine_ln_kernel,
        grid=(M // tm,),
        in_specs=[blk, blk, col, col, blk, row, row],
        out_specs=blk,
        out_shape=jax.ShapeDtypeStruct((M, N), F32),
        compiler_params=_params(("arbitrary",)),
    )(ya, yb, ga, gb, resid, g.reshape(1, N), b.reshape(1, N))


def moe_layer(x, router_w, router_b, layer, wg, wu, wd, ln_g, ln_b):
    shp = x.shape
    xt = x.reshape(-1, shp[-1])
    T = xt.shape[0]
    expert, gate = route(xt, router_w, router_b)
    flat_e = expert.T.reshape(-1)
    n_assign = T * TOP_K
    onehot = (flat_e[:, None] == jnp.arange(N_EXPERTS, dtype=jnp.int32)[None]).astype(jnp.int32)
    csum = jnp.cumsum(onehot, axis=0)
    rank = jnp.take_along_axis(csum, flat_e[:, None], axis=1)[:, 0] - 1
    counts = csum[-1]
    padded = (counts + MOE_TM - 1) // MOE_TM * MOE_TM
    pad_end = jnp.cumsum(padded)
    pad_start = pad_end - padded
    dest = pad_start[flat_e] + rank
    nb = -(-n_assign // MOE_TM) + N_EXPERTS
    rows = nb * MOE_TM
    row_tok = jnp.zeros(rows, jnp.int32).at[dest].set(jnp.arange(n_assign, dtype=jnp.int32) // TOP_K)
    block_exp = jnp.minimum(
        jnp.searchsorted(pad_end, jnp.arange(nb, dtype=jnp.int32) * MOE_TM, side='right'),
        N_EXPERTS - 1).astype(jnp.int32)
    yb = expert_ffn(xt[row_tok], block_exp, layer, wg, wu, wd)
    d2 = dest.reshape(T, TOP_K)
    out = combine_ln(yb[d2[:, 0]], yb[d2[:, 1]], gate[0][:, None], gate[1][:, None], xt, ln_g, ln_b)
    return out.reshape(shp)


def _block_diag(w):
    G, I, O = w.shape
    eye = jnp.eye(G, dtype=w.dtype)
    return (eye[:, None, :, None] * w[:, :, None, :]).reshape(G * I, G * O)


def even_layer(x, lru_h, lru_conv, wkv, shift, p, ln_g, ln_b):
    B, T, D = x.shape
    D_LRU = lru_h.shape[-1]
    D_RWKV = p['w0'].shape[1]
    H = D_RWKV // RWKV_HEAD
    N = RWKV_HEAD
    z = matmul(x.reshape(B * T, D), p['w_in']).reshape(B, T, -1)
    y_a, r, decay, k, v, a, g, h_T = even_pre(z, lru_conv, shift, lru_h, p)

    lanes = lambda t: jnp.transpose(t.reshape(B, T, H, N), (1, 3, 0, 2)).reshape(T, N, B * H)
    per_chain = lambda w: jnp.tile(w.reshape(H, N).T, (1, B))
    s0 = jnp.transpose(wkv, (2, 3, 0, 1)).reshape(N, N, B * H)
    y, s_T = rwkv_scan(lanes(r), lanes(decay), lanes(k), lanes(v), lanes(a),
                       per_chain(p['k_k']), per_chain(p['k_a']), per_chain(p['gn_g']),
                       per_chain(p['gn_b']), per_chain(p['r_k']), s0)
    y = jnp.transpose(y.reshape(T, N, B, H), (2, 0, 3, 1)).reshape(B * T, D_RWKV)
    S_T = jnp.transpose(s_T.reshape(N, N, B, H), (2, 3, 0, 1))

    x_new = mix_ln(y_a.reshape(B * T, D_LRU), y, g.reshape(B * T, D_RWKV), p['w_out'],
                   x.reshape(B * T, D), ln_g, ln_b)
    conv_state = jnp.concatenate([lru_conv, z[..., :D_LRU]], axis=1)[:, T:]
    return x_new.reshape(B, T, D), (h_T[:, 0], conv_state, S_T, z[:, -1:, 2 * D_LRU:])


def odd_layer(x, cache, p, ln_g, ln_b):
    B, T, D = x.shape
    dq = N_HEADS_C * HEAD_DIM_C
    dkv = N_KV_C * HEAD_DIM_C
    dqi = N_IDX_HEADS * D_IDX
    z = matmul(x.reshape(B * T, D), p['w_in']).reshape(B, T, -1)
    k = z[..., dq:dq + dkv]
    v = z[..., dq + dkv:dq + 2 * dkv]
    o = dq + 2 * dkv + dqi
    ki = z[..., o:o + D_IDX]
    wi = z[..., o + D_IDX:o + D_IDX + N_IDX_HEADS] * (dqi ** -0.5)
    Tp = -(-T // LANES) * LANES
    if cache is None:
        past = 0
        n_sel = min(TOPK_MAX, T // 4)
        parts = lambda c, n: [n]
    else:
        past = cache[0].shape[1]
        n_sel = min(TOPK_MAX, (past + T) // 4)
        parts = lambda c, n: [c.reshape(B, past, -1), n]
    L = -(-(past + Tp) // LANES) * LANES
    tk = _pick(L, (512, 384, 128))
    cat = lambda c, n: jnp.concatenate(
        parts(c, n) + [jnp.zeros((B, L - past - T, n.shape[-1]), n.dtype)], axis=1).astype(BF16)
    ck, cv, cki = cache if cache is not None else (None, None, None)
    keys, vals, kidx = cat(ck, k), cat(cv, v), cat(cki, ki)
    vals_t = jnp.swapaxes(vals.reshape(B, L // tk, tk, dkv), 2, 3)
    rowpad = lambda t: jnp.pad(t, ((0, 0), (0, Tp - T), (0, 0)))
    wi_t = jnp.swapaxes(rowpad(wi), 1, 2)
    o_att = dsa_attention(rowpad(z), 0, (dq + 2 * dkv) // dqi, wi_t, keys, vals_t, kidx, past, n_sel, tk)
    x_new = matmul_ln(o_att[:, :T].reshape(B * T, dq), p['w_out'], x.reshape(B * T, D), ln_g, ln_b)
    new = (k.reshape(B, T, N_KV_C, HEAD_DIM_C), v.reshape(B, T, N_KV_C, HEAD_DIM_C), ki)
    return x_new.reshape(B, T, D), new


def run_trunk(x, even_state, odd_cache, ev, od, norm_w, moe_w):
    ln1_g, ln1_b, ln2_g, ln2_b = norm_w
    router_w, router_b, wg, wu, wd = moe_w
    x, new_even = even_layer(x, *even_state, ev, ln1_g[0], ln1_b[0])
    x = moe_layer(x, router_w, router_b, 0, wg, wu, wd, ln2_g[0], ln2_b[0])
    x, new_odd = odd_layer(x, odd_cache, od, ln1_g[1], ln1_b[1])
    x = moe_layer(x, router_w, router_b, 1, wg, wu, wd, ln2_g[1], ln2_b[1])
    return x, new_even, new_odd


def kernel(x_prompt, x_sample, state_lru_h, state_lru_conv, state_rwkv_wkv, state_rwkv_shift, cache_k, cache_v, cache_kidx, ev_w_in, ev_conv_w, ev_conv_b, ev_gate_r_w, ev_gate_r_b, ev_gate_i_w, ev_gate_i_b, ev_lambda, ev_shift_mu, ev_w0, ev_w2, ev_a0, ev_a2, ev_g2, ev_k_k, ev_k_a, ev_r_k, ev_gn_g, ev_gn_b, ev_w_out, od_w_in, od_w_out, ln1_g, ln1_b, ln2_g, ln2_b, router_w, router_b, moe_w_gate, moe_w_up, moe_w_down):
    ev = dict(
        w_in=ev_w_in[0].astype(BF16), conv_w=ev_conv_w[0], conv_b=ev_conv_b[0][None],
        gate_w=jnp.concatenate([_block_diag(ev_gate_r_w[0]), _block_diag(ev_gate_i_w[0])], axis=1).astype(BF16),
        gate_b=jnp.concatenate([ev_gate_r_b[0], ev_gate_i_b[0]])[None],
        lru_c=(-LRU_C * jax.nn.softplus(-ev_lambda[0]))[None],
        shift_mu=ev_shift_mu[0][None], w0=ev_w0[0][None], a0=ev_a0[0][None],
        w2=ev_w2[0].astype(BF16), a2=ev_a2[0].astype(BF16), g2=ev_g2[0].astype(BF16),
        k_k=ev_k_k[0][None], k_a=ev_k_a[0][None], r_k=ev_r_k[0], gn_g=ev_gn_g[0], gn_b=ev_gn_b[0],
        w_out=ev_w_out[0].astype(BF16))
    n_in = od_w_in.shape[2]
    od = dict(w_in=jnp.pad(od_w_in[0], ((0, 0), (0, (-n_in) % LANES))).astype(BF16),
              w_out=od_w_out[0].astype(BF16))
    norm_w = (ln1_g, ln1_b, ln2_g, ln2_b)
    moe_w = (router_w, router_b, moe_w_gate, moe_w_up, moe_w_down)

    B = x_prompt.shape[0]
    fresh = (jnp.zeros((B,) + state_lru_h.shape[2:], F32),
             jnp.zeros((B,) + state_lru_conv.shape[2:], F32),
             jnp.zeros((B,) + state_rwkv_wkv.shape[2:], F32),
             jnp.zeros((B,) + state_rwkv_shift.shape[2:], F32))
    y_p, pe, po = run_trunk(x_prompt, fresh, None, ev, od, norm_w, moe_w)
    y_s, se, so = run_trunk(
        x_sample, (state_lru_h[0], state_lru_conv[0], state_rwkv_wkv[0], state_rwkv_shift[0]),
        (cache_k[0], cache_v[0], cache_kidx[0]), ev, od, norm_w, moe_w)
    st = lambda t: t[None]
    return (y_p, y_s, *map(st, pe), *map(st, po), *map(st, se), *map(st, so))
```

```python
import functools
import math

import jax
import jax.numpy as jnp
from jax import lax
from jax.experimental import pallas as pl
from jax.experimental.pallas import tpu as pltpu

F32 = jnp.float32
BF16 = jnp.bfloat16

CHUNK = 64
LRU_BLOCKS = 16
CONV_W = 4
LRU_C = 8.0
RWKV_HEAD = 64
DECAY_LORA = 64
AAA_LORA = 64
GATE_LORA = 128
N_HEADS_C = 16
HEAD_DIM_C = 128
N_KV_C = 4
N_IDX_HEADS = 16
D_IDX = 64
TOPK_MAX = 256
N_EXPERTS = 16
N_GROUPS = 4
EXPERTS_PER_GROUP = N_EXPERTS // N_GROUPS
TOP_K = 2
DEPTH = 2
DN_ALPHA = (2 * DEPTH) ** 0.25
LN_EPS = 1e-5
GN_EPS = 64e-5

LANES = 128
SUBLANES = 8
VMEM_LIMIT = 56 * 1024 * 1024
MOE_TM = 256


def _params(sem):
    return pltpu.CompilerParams(dimension_semantics=sem, vmem_limit_bytes=VMEM_LIMIT)


def _mm_kernel(x_ref, w_ref, o_ref, xb_ref):
    @pl.when(pl.program_id(1) == 0)
    def _():
        xb_ref[...] = x_ref[...].astype(BF16)

    o_ref[...] = jnp.dot(xb_ref[...], w_ref[...], preferred_element_type=F32)


def _pick(n, cands):
    for c in cands:
        if n % c == 0:
            return c
    return n


def matmul(x, w):
    M, K = x.shape
    N = w.shape[1]
    tm = _pick(M, (1024, 512, 256, 128, 64, 32, 16, 8))
    tn = _pick(N, (768, 1024, 1408, 512, 384, 256, 128))
    return pl.pallas_call(
        _mm_kernel,
        grid=(M // tm, N // tn),
        in_specs=[pl.BlockSpec((tm, K), lambda i, j: (i, 0)),
                  pl.BlockSpec((K, tn), lambda i, j: (0, j))],
        out_specs=pl.BlockSpec((tm, tn), lambda i, j: (i, j)),
        out_shape=jax.ShapeDtypeStruct((M, N), F32),
        scratch_shapes=[pltpu.VMEM((tm, K), BF16)],
        compiler_params=_params(("arbitrary", "arbitrary")),
    )(x, w)


def _deepnorm(resid, y, g, b):
    h = DN_ALPHA * resid + y
    mu = jnp.mean(h, axis=-1, keepdims=True)
    hc = h - mu
    var = jnp.mean(hc * hc, axis=-1, keepdims=True)
    return hc * lax.rsqrt(var + LN_EPS) * g + b


def _mm_ln_kernel(x_ref, w_ref, r_ref, g_ref, b_ref, o_ref):
    y = jnp.dot(x_ref[...].astype(BF16), w_ref[...], preferred_element_type=F32)
    o_ref[...] = _deepnorm(r_ref[...], y, g_ref[...], b_ref[...])


def _mix_ln_kernel(ya_ref, yb_ref, gate_ref, w_ref, r_ref, g_ref, b_ref, o_ref):
    mix = jnp.concatenate([ya_ref[...], yb_ref[...] * gate_ref[...]], axis=1).astype(BF16)
    y = jnp.dot(mix, w_ref[...], preferred_element_type=F32)
    o_ref[...] = _deepnorm(r_ref[...], y, g_ref[...], b_ref[...])


def matmul_ln(x, w, resid, g, b):
    M, K = x.shape
    N = w.shape[1]
    tm = _pick(M, (256, 128, 64, 32, 16, 8))
    row = pl.BlockSpec((1, N), lambda i: (0, 0))
    return pl.pallas_call(
        _mm_ln_kernel,
        grid=(M // tm,),
        in_specs=[pl.BlockSpec((tm, K), lambda i: (i, 0)), pl.BlockSpec((K, N), lambda i: (0, 0)),
                  pl.BlockSpec((tm, N), lambda i: (i, 0)), row, row],
        out_specs=pl.BlockSpec((tm, N), lambda i: (i, 0)),
        out_shape=jax.ShapeDtypeStruct((M, N), F32),
        compiler_params=_params(("arbitrary",)),
    )(x, w, resid, g.reshape(1, N), b.reshape(1, N))


def mix_ln(ya, yb, gate, w, resid, g, b):
    M, Kh = ya.shape
    N = w.shape[1]
    tm = _pick(M, (256, 128, 64, 32, 16, 8))
    row = pl.BlockSpec((1, N), lambda i: (0, 0))
    half = pl.BlockSpec((tm, Kh), lambda i: (i, 0))
    return pl.pallas_call(
        _mix_ln_kernel,
        grid=(M // tm,),
        in_specs=[half, half, half, pl.BlockSpec((2 * Kh, N), lambda i: (0, 0)),
                  pl.BlockSpec((tm, N), lambda i: (i, 0)), row, row],
        out_specs=pl.BlockSpec((tm, N), lambda i: (i, 0)),
        out_shape=jax.ShapeDtypeStruct((M, N), F32),
        compiler_params=_params(("arbitrary",)),
    )(ya, yb, gate, w, resid, g.reshape(1, N), b.reshape(1, N))


def _expm1(x):
    e = jnp.exp(x)
    safe = jnp.where((e == 1.0) | (e == 0.0), 2.0, e)
    return jnp.where(e == 1.0, x, jnp.where(e == 0.0, -1.0, (e - 1.0) * x / jnp.log(safe)))


def _softplus(x):
    return jnp.maximum(x, 0.0) + jnp.log(1.0 + jnp.exp(-jnp.abs(x)))


HALO = SUBLANES


def _even_pre_kernel(ax_ref, ag_ref, zr_ref, zk_ref, zv_ref, zt_ref,
                     conv_ref, sr_ref, sk_ref, sv_ref, st_ref, h0_ref,
                     cw_ref, cb_ref, gw_ref, gb_ref, lc_ref, mu_ref,
                     w0_ref, a0_ref, w2_ref, a2_ref, g2_ref,
                     ya_ref, r_ref, w_ref, k_ref, v_ref, a_ref, g_ref, hT_ref,
                     xp_scr, pr_scr, pk_scr, pv_scr, pt_scr, a_scr, u_scr, hs_scr, h_scr):
    i = pl.program_id(1)
    tm = ax_ref.shape[1]
    C = ax_ref.shape[2]
    prev = ((xp_scr, None), (pr_scr, sr_ref), (pk_scr, sk_ref), (pv_scr, sv_ref), (pt_scr, st_ref))

    @pl.when(i == 0)
    def _():
        xp_scr[0:HALO, :] = jnp.zeros((HALO, C), F32)
        xp_scr[HALO - (CONV_W - 1):HALO, :] = conv_ref[0]
        for scr, ref in prev[1:]:
            scr[0:HALO, :] = jnp.broadcast_to(ref[0], (HALO, scr.shape[1]))
        h_scr[...] = jnp.broadcast_to(h0_ref[0], h_scr.shape)

    cur = pl.ds(HALO, tm)
    xp_scr[cur, :] = ax_ref[0]
    pr_scr[cur, :] = zr_ref[0]
    pk_scr[cur, :] = zk_ref[0]
    pv_scr[cur, :] = zv_ref[0]
    pt_scr[cur, :] = zt_ref[0]

    xa = cb_ref[...]
    for j in range(CONV_W):
        xa = xa + cw_ref[j:j + 1, :] * xp_scr[pl.ds(HALO - (CONV_W - 1) + j, tm), :]
    gates = jnp.dot(xa.astype(BF16), gw_ref[...], preferred_element_type=F32) + gb_ref[...]
    log_a = lc_ref[...] * jax.nn.sigmoid(gates[:, :C])
    a_scr[...] = jnp.exp(log_a)
    u_scr[...] = jnp.sqrt(-_expm1(2.0 * log_a)) * (jax.nn.sigmoid(gates[:, C:]) * xa)

    def step(t, h):
        h = a_scr[pl.ds(t, 1), :] * h + u_scr[pl.ds(t, 1), :]
        hs_scr[pl.ds(t, 1), :] = h
        return h

    h = lax.fori_loop(0, tm, step, h_scr[0:1, :], unroll=8)
    h_scr[...] = jnp.broadcast_to(h, h_scr.shape)
    hT_ref[0] = h
    ya_ref[0] = hs_scr[...] * jax.nn.gelu(ag_ref[0], approximate=True)

    def shifted(scr, lo, hi):
        zb = scr[cur, :]
        return zb + mu_ref[:, lo:hi] * (scr[pl.ds(HALO - 1, tm), :] - zb)

    r = shifted(pr_scr, 0, C)
    k = shifted(pk_scr, C, 2 * C)
    v = shifted(pv_scr, 2 * C, 3 * C)
    tail = shifted(pt_scr, 3 * C, 3 * C + pt_scr.shape[1])
    wl = jnp.tanh(tail[:, :DECAY_LORA])
    al = tail[:, DECAY_LORA:DECAY_LORA + AAA_LORA]
    gl = jax.nn.sigmoid(tail[:, DECAY_LORA + AAA_LORA:])
    lora = lambda x, w_ref_: jnp.dot(x.astype(BF16), w_ref_[...], preferred_element_type=F32)
    w_log = -_softplus(-(w0_ref[...] + lora(wl, w2_ref))) - 0.5
    a = jax.nn.sigmoid(a0_ref[...] + lora(al, a2_ref))
    r_ref[0] = r
    w_ref[0] = jnp.exp(-jnp.exp(w_log))
    k_ref[0] = k
    v_ref[0] = v
    a_ref[0] = a
    g_ref[0] = lora(gl, g2_ref)

    for scr, _ in prev:
        scr[0:HALO, :] = scr[pl.ds(tm, HALO), :]


def even_pre(z, lru_conv, shift, lru_h, p):
    B, T, _ = z.shape
    C = lru_h.shape[-1]
    tail = DECAY_LORA + AAA_LORA + GATE_LORA
    tm = _pick(T, (128, 64, 32, 16, 8))
    zblk = lambda w, col: pl.BlockSpec((1, tm, w), lambda b, i: (b, i, col))
    sblk = lambda rows, w, col: pl.BlockSpec((1, rows, w), lambda b, i: (b, 0, col))
    full = lambda a: pl.BlockSpec(a.shape, lambda b, i: (0,) * a.ndim)
    weights = [p['conv_w'], p['conv_b'], p['gate_w'], p['gate_b'], p['lru_c'], p['shift_mu'],
               p['w0'], p['a0'], p['w2'], p['a2'], p['g2']]
    tok = jax.ShapeDtypeStruct((B, T, C), F32)
    return pl.pallas_call(
        _even_pre_kernel,
        grid=(B, T // tm),
        in_specs=[zblk(C, 0), zblk(C, 1), zblk(C, 2), zblk(C, 3), zblk(C, 4), zblk(tail, 5 * C // tail),
                  sblk(CONV_W - 1, C, 0), sblk(1, C, 0), sblk(1, C, 1), sblk(1, C, 2),
                  sblk(1, tail, 3 * C // tail), sblk(1, C, 0)] + [full(w) for w in weights],
        out_specs=[zblk(C, 0)] * 7 + [sblk(1, C, 0)],
        out_shape=[tok] * 7 + [jax.ShapeDtypeStruct((B, 1, C), F32)],
        scratch_shapes=[pltpu.VMEM((tm + HALO, C), F32)] * 4 + [pltpu.VMEM((tm + HALO, tail), F32)]
                       + [pltpu.VMEM((tm, C), F32)] * 3 + [pltpu.VMEM((SUBLANES, C), F32)],
        compiler_params=_params(("arbitrary", "arbitrary")),
    )(z, z, z, z, z, z, lru_conv, shift, shift, shift, shift, lru_h[:, None, :], *weights)


def _colsum(x):
    n = x.shape[0] // SUBLANES
    p = x.reshape(n, SUBLANES, x.shape[1]).sum(axis=0)
    p = p + pltpu.roll(p, 4, axis=0)
    p = p + pltpu.roll(p, 2, axis=0)
    p = p + pltpu.roll(p, 1, axis=0)
    return p


def _bcast_rows(p, n):
    return jnp.broadcast_to(p[None], (n // SUBLANES,) + p.shape).reshape(n, p.shape[1])


def _rwkv_kernel(r_ref, w_ref, k_ref, v_ref, a_ref, kk_ref, ka_ref, gng_ref, gnb_ref, rk_ref, s0_ref,
                 y_ref, sT_ref, s_scr):
    tc = pl.program_id(1)
    N = RWKV_HEAD

    @pl.when(tc == 0)
    def _():
        s_scr[...] = s0_ref[...]

    sub = lax.broadcasted_iota(jnp.int32, (SUBLANES, LANES), 0)

    def step(t, carry):
        r = r_ref[t]
        w = w_ref[t]
        k_raw = k_ref[t]
        a = a_ref[t]
        kk = k_raw * kk_ref[...]
        kk = kk / jnp.maximum(_bcast_rows(jnp.sqrt(_colsum(kk * kk)), N), 1e-12)
        nkk = -kk
        kka = kk * a
        k = k_raw * (1.0 + (a - 1.0) * ka_ref[...])

        def vgroup(g, c2):
            yacc = jnp.zeros((SUBLANES, LANES), F32)
            for j in range(SUBLANES):
                vi = g * SUBLANES + j
                S = s_scr[vi]
                sa = _bcast_rows(_colsum(S * nkk), N)
                vb = jnp.broadcast_to(v_ref[t, pl.ds(vi, 1), :], (N, LANES))
                S = S * w + sa * kka + vb * k
                s_scr[vi] = S
                yacc = jnp.where(sub == j, _colsum(S * r), yacc)
            y_ref[t, pl.ds(pl.multiple_of(g * SUBLANES, SUBLANES), SUBLANES), :] = yacc
            return c2

        lax.fori_loop(0, N // SUBLANES, vgroup, 0, unroll=True)

        y = y_ref[t]
        d = y - _bcast_rows(_colsum(y), N) * (1.0 / N)
        var = _bcast_rows(_colsum(d * d), N) * (1.0 / N)
        bonus = _bcast_rows(_colsum(r * k * rk_ref[...]), N)
        y_ref[t] = d * lax.rsqrt(var + GN_EPS) * gng_ref[...] + gnb_ref[...] + bonus * v_ref[t]
        return carry

    lax.fori_loop(0, r_ref.shape[0], step, 0)

    @pl.when(tc == pl.num_programs(1) - 1)
    def _():
        sT_ref[...] = s_scr[...]


def rwkv_scan(r, w, k, v, a, k_k, k_a, gn_g, gn_b, r_k, s0):
    T, N, P = r.shape
    tc = _pick(T, (32, 16, 8))
    blk = pl.BlockSpec((tc, N, LANES), lambda p, t: (t, 0, p))
    par = pl.BlockSpec((N, LANES), lambda p, t: (0, p))
    st = pl.BlockSpec((N, N, LANES), lambda p, t: (0, 0, p))
    return pl.pallas_call(
        _rwkv_kernel,
        grid=(P // LANES, T // tc),
        in_specs=[blk] * 5 + [par] * 5 + [st],
        out_specs=[blk, st],
        out_shape=[jax.ShapeDtypeStruct((T, N, P), F32), jax.ShapeDtypeStruct((N, N, P), F32)],
        scratch_shapes=[pltpu.VMEM((N, N, LANES), F32)],
        compiler_params=_params(("arbitrary", "arbitrary")),
    )(r, w, k, v, a, k_k, k_a, gn_g, gn_b, r_k, s0)


MASKED = -1e30


KEY_NEG_INF = -2139095041
QK_SPLIT = 2


def _tree_sum(parts):
    while len(parts) > 1:
        parts = [parts[i] + parts[i + 1] for i in range(0, len(parts) - 1, 2)] + parts[len(parts) & ~1:]
    return parts[0]


def _dsa_kernel(q_ref, qi_ref, wi_ref, k_ref, vt_ref, ki_ref, o_ref,
                key_scr, m_scr, l_scr, acc_scr, *, past, n_sel, scale, tk):
    qb = pl.program_id(1)
    QB = q_ref.shape[1]
    L = k_ref.shape[1]
    NP = N_HEADS_C // 2
    lane_chunk = lax.broadcasted_iota(jnp.int32, (1, QB), 1) // CHUNK
    limit = past + (qb * (QB // CHUNK) + lane_chunk + 1) * CHUNK
    n_tiles = (past + (qb + 1) * QB + tk - 1) // tk
    u = 2 if (tk < 512 and (L // tk) % 2 == 0) else 1
    n_count = (n_tiles + u - 1) // u
    row = lax.broadcasted_iota(jnp.int32, (tk, QB), 0)
    tile = lambda kt: pl.ds(pl.multiple_of(kt * tk, tk), tk)

    qi = qi_ref[0].astype(BF16)
    qi_pair = [jnp.concatenate([qi[:, (2 * p) * D_IDX:(2 * p + 1) * D_IDX],
                                qi[:, (2 * p + 1) * D_IDX:(2 * p + 2) * D_IDX]], axis=0)
               for p in range(N_IDX_HEADS // 2)]
    wi = wi_ref[0]

    def score_tile(kt, c):
        @pl.when(kt < n_tiles)
        def _():
            kidx = ki_ref[0, tile(kt), :]
            terms = []
            for p in range(N_IDX_HEADS // 2):
                s = lax.dot_general(kidx, qi_pair[p], (((1,), (1,)), ((), ())),
                                    preferred_element_type=F32)
                for e in range(2):
                    h = 2 * p + e
                    terms.append(wi[h:h + 1, :] * jnp.maximum(s[:, e * QB:(e + 1) * QB], 0.0))
            isc = terms[0]
            for t in terms[1:]:
                isc = isc + t
            ok = row + kt * tk < limit
            bits = lax.bitcast_convert_type(jnp.where(ok, isc + 0.0, -jnp.inf), jnp.int32)
            key_scr[tile(kt), :] = bits ^ ((bits >> 31) & jnp.int32(0x7FFFFFFF))

        @pl.when(kt >= n_tiles)
        def _():
            key_scr[tile(kt), :] = jnp.full((tk, QB), KEY_NEG_INF, jnp.int32)

        return c

    lax.fori_loop(0, n_count * u, score_tile, 0)

    int_min = jnp.int32(-2 ** 31)
    ct = tk * u

    def search(i, t):
        cand = t | lax.shift_left(jnp.int32(1), 31 - i)
        cs = cand ^ int_min

        def count_tile(kt, acc):
            keys = key_scr[pl.ds(pl.multiple_of(kt * ct, ct), ct), :]
            hit = jnp.where(keys >= cs, 1.0, 0.0)
            return acc + _tree_sum([hit[r * SUBLANES:(r + 1) * SUBLANES] for r in range(ct // SUBLANES)])

        acc = lax.fori_loop(0, n_count, count_tile, jnp.zeros((SUBLANES, QB), F32))
        cnt = jnp.sum(acc, axis=0, keepdims=True)
        return jnp.where(cnt >= n_sel, cand, t)

    thr = lax.fori_loop(0, 32, search, jnp.zeros((1, QB), jnp.int32)) ^ int_min

    G = N_HEADS_C // N_KV_C
    dh = HEAD_DIM_C
    q = q_ref[0]
    q_pair = [(jnp.concatenate([q[:, (2 * p) * dh:(2 * p + 1) * dh],
                                q[:, (2 * p + 1) * dh:(2 * p + 2) * dh]], axis=0) * scale).astype(BF16)
              for p in range(NP)]
    m_scr[...] = jnp.full(m_scr.shape, MASKED, F32)
    l_scr[...] = jnp.zeros(l_scr.shape, F32)
    acc_scr[...] = jnp.zeros(acc_scr.shape, F32)

    def att_tile(kt, c):
        sel = (key_scr[tile(kt), :] >= thr) & (row + kt * tk < limit)
        b = jnp.where(sel, 0.0, MASKED)
        b2 = jnp.concatenate([b, b], axis=1)
        def logits(p):
            j = (2 * p) // G
            kj = k_ref[0, tile(kt), j * dh:(j + 1) * dh]
            part = tk // QK_SPLIT
            return jnp.concatenate(
                [lax.dot_general(kj[r * part:(r + 1) * part], q_pair[p], (((1,), (1,)), ((), ())),
                                 preferred_element_type=F32) for r in range(QK_SPLIT)], axis=0)

        ahead = 2
        lg = {p: logits(p) for p in range(ahead)}
        for p in range(NP):
            j = (2 * p) // G
            x = lg.pop(p) + b2
            m = m_scr[p:p + 1, :]
            m_new = jnp.maximum(m, jnp.max(x, axis=0, keepdims=True))
            alpha = jnp.exp2(m - m_new)
            pr = jnp.exp2(x - m_new)
            l_scr[p:p + 1, :] = alpha * l_scr[p:p + 1, :] + jnp.sum(pr, axis=0, keepdims=True)
            m_scr[p:p + 1, :] = m_new
            vt = vt_ref[0, kt, j * dh:(j + 1) * dh, :]
            pv = jnp.dot(vt, pr.astype(BF16), preferred_element_type=F32)
            if p + ahead < NP:
                lg[p + ahead] = logits(p + ahead)
            acc_scr[p] = acc_scr[p] * alpha + pv
        return c

    lax.fori_loop(0, n_tiles, att_tile, 0)
    for p in range(NP):
        o2 = acc_scr[p] / l_scr[p:p + 1, :]
        for s in range(2):
            o_ref[0, :, (2 * p + s) * dh:(2 * p + s + 1) * dh] = o2[:, s * QB:(s + 1) * QB].T


def dsa_attention(z, q_col, qi_col, wi_t, keys, vals_t, kidx, past, n_sel, tk):
    B, T, _ = z.shape
    L = keys.shape[1]
    QB = LANES
    dq = N_HEADS_C * HEAD_DIM_C
    dqi = N_IDX_HEADS * D_IDX
    kern = functools.partial(_dsa_kernel, past=past, n_sel=n_sel,
                             scale=HEAD_DIM_C ** -0.5 * math.log2(math.e), tk=tk)
    return pl.pallas_call(
        kern,
        grid=(B, T // QB),
        in_specs=[pl.BlockSpec((1, QB, dq), lambda b, c: (b, c, q_col)),
                  pl.BlockSpec((1, QB, dqi), lambda b, c: (b, c, qi_col)),
                  pl.BlockSpec((1, N_IDX_HEADS, QB), lambda b, c: (b, 0, c)),
                  pl.BlockSpec((1, L, keys.shape[2]), lambda b, c: (b, 0, 0)),
                  pl.BlockSpec((1,) + vals_t.shape[1:], lambda b, c: (b, 0, 0, 0)),
                  pl.BlockSpec((1, L, kidx.shape[2]), lambda b, c: (b, 0, 0))],
        out_specs=pl.BlockSpec((1, QB, dq), lambda b, c: (b, c, 0)),
        out_shape=jax.ShapeDtypeStruct((B, T, dq), F32),
        scratch_shapes=[pltpu.VMEM((L, QB), jnp.int32),
                        pltpu.VMEM((N_HEADS_C // 2, 2 * QB), F32),
                        pltpu.VMEM((N_HEADS_C // 2, 2 * QB), F32),
                        pltpu.VMEM((N_HEADS_C // 2, HEAD_DIM_C, 2 * QB), F32)],
        compiler_params=_params(("arbitrary", "arbitrary")),
    )(z, z, wi_t, keys, vals_t, kidx)


def _router_kernel(x_ref, w_ref, b_ref, e_ref, g_ref):
    logits = lax.dot_general(w_ref[...], x_ref[...], (((1,), (1,)), ((), ())),
                             preferred_element_type=F32, precision=lax.Precision.HIGHEST)
    scores = jax.nn.sigmoid(logits)
    biased = scores + b_ref[...]
    tm = scores.shape[1]
    P = EXPERTS_PER_GROUP

    def ranks(rows):
        out = []
        for i in range(len(rows)):
            rk = jnp.zeros((1, tm), jnp.int32)
            for j in range(len(rows)):
                if j == i:
                    continue
                beat = (rows[j] > rows[i]) | ((rows[j] == rows[i]) & (j < i))
                rk = rk + beat.astype(jnp.int32)
            out.append(rk)
        return out

    b = [biased[i:i + 1] for i in range(N_EXPERTS)]
    s = [scores[i:i + 1] for i in range(N_EXPERTS)]
    grp_rank, grp_score = [], []
    for g in range(N_GROUPS):
        rows = b[g * P:(g + 1) * P]
        rk = ranks(rows)
        grp_rank.append(rk)
        top1 = sum(jnp.where(rk[i] == 0, rows[i], 0.0) for i in range(P))
        top2 = sum(jnp.where(rk[i] == 1, rows[i], 0.0) for i in range(P))
        grp_score.append(top1 + top2)
    gr = ranks(grp_score)
    e1 = jnp.zeros((1, tm), jnp.int32)
    e2 = jnp.zeros((1, tm), jnp.int32)
    w1 = jnp.zeros((1, tm), F32)
    w2 = jnp.zeros((1, tm), F32)
    for g in range(N_GROUPS):
        best = gr[g] == 0
        for i in range(P):
            is1 = best & (grp_rank[g][i] == 0)
            is2 = best & (grp_rank[g][i] == 1)
            e1 = jnp.where(is1, g * P + i, e1)
            e2 = jnp.where(is2, g * P + i, e2)
            w1 = jnp.where(is1, s[g * P + i], w1)
            w2 = jnp.where(is2, s[g * P + i], w2)
    tot = w1 + w2
    e_ref[...] = jnp.concatenate([e1, e2], axis=0)
    g_ref[...] = jnp.concatenate([w1 / tot, w2 / tot], axis=0)


def route(xt, router_w, router_b):
    T, D = xt.shape
    tm = _pick(T, (512, 256, 128))
    return pl.pallas_call(
        _router_kernel,
        grid=(T // tm,),
        in_specs=[pl.BlockSpec((tm, D), lambda i: (i, 0)),
                  pl.BlockSpec((N_EXPERTS, D), lambda i: (0, 0)),
                  pl.BlockSpec((N_EXPERTS, 1), lambda i: (0, 0))],
        out_specs=[pl.BlockSpec((TOP_K, tm), lambda i: (0, i)),
                   pl.BlockSpec((TOP_K, tm), lambda i: (0, i))],
        out_shape=[jax.ShapeDtypeStruct((TOP_K, T), jnp.int32),
                   jax.ShapeDtypeStruct((TOP_K, T), F32)],
        compiler_params=_params(("arbitrary",)),
    )(xt, router_w.T, router_b.reshape(N_EXPERTS, 1))


FF_SLABS = 4


def _ffn_kernel(be_ref, x_ref, *refs):
    del be_ref
    wg, wu, wd = refs[:FF_SLABS], refs[FF_SLABS:2 * FF_SLABS], refs[2 * FF_SLABS:3 * FF_SLABS]
    o_ref = refs[3 * FF_SLABS]
    x = x_ref[...].astype(BF16)
    y = None
    for s in range(FF_SLABS):
        a = jnp.dot(x, wg[s][0, 0], preferred_element_type=F32)
        u = jnp.dot(x, wu[s][0, 0], preferred_element_type=F32)
        h = (a * jax.nn.sigmoid(a)) * u
        part = jnp.dot(h.astype(BF16), wd[s][0, 0], preferred_element_type=F32)
        y = part if y is None else y + part
    o_ref[...] = y


def _cast_kernel(*refs):
    n = len(refs) // 2
    for x_ref, o_ref in zip(refs[:n], refs[n:]):
        o_ref[...] = x_ref[...].astype(o_ref.dtype)


def cast_slabs(w, axis):
    L, E, A, Bd = w.shape
    if axis == 3:
        blk = (1, 1, A, Bd // FF_SLABS)
        views = [pl.BlockSpec(blk, lambda l, e, s=s: (l, e, 0, s)) for s in range(FF_SLABS)]
    else:
        blk = (1, 1, A // FF_SLABS, Bd)
        views = [pl.BlockSpec(blk, lambda l, e, s=s: (l, e, s, 0)) for s in range(FF_SLABS)]
    out = pl.BlockSpec(blk, lambda l, e: (l, e, 0, 0))
    return pl.pallas_call(
        _cast_kernel,
        grid=(L, E),
        in_specs=views,
        out_specs=[out] * FF_SLABS,
        out_shape=[jax.ShapeDtypeStruct((L, E) + blk[2:], BF16)] * FF_SLABS,
        compiler_params=_params(("arbitrary", "arbitrary")),
    )(*([w] * FF_SLABS))


def expert_ffn(xs, block_exp, layer, wg, wu, wd):
    R, D = xs.shape
    nb = R // MOE_TM
    wspec = lambda w: pl.BlockSpec((1, 1) + w.shape[2:], lambda i, be: (layer, be[i], 0, 0))
    grid_spec = pltpu.PrefetchScalarGridSpec(
        num_scalar_prefetch=1,
        grid=(nb,),
        in_specs=[pl.BlockSpec((MOE_TM, D), lambda i, be: (i, 0))] + [wspec(w) for w in (*wg, *wu, *wd)],
        out_specs=pl.BlockSpec((MOE_TM, D), lambda i, be: (i, 0)),
    )
    return pl.pallas_call(
        _ffn_kernel,
        grid_spec=grid_spec,
        out_shape=jax.ShapeDtypeStruct((R, D), F32),
        compiler_params=_params(("arbitrary",)),
    )(block_exp, xs, *wg, *wu, *wd)


def _combine_ln_kernel(ya_ref, yb_ref, ga_ref, gb_ref, r_ref, g_ref, b_ref, o_ref):
    y = ya_ref[...] * ga_ref[...] + yb_ref[...] * gb_ref[...]
    o_ref[...] = _deepnorm(r_ref[...], y, g_ref[...], b_ref[...])


def combine_ln(ya, yb, ga, gb, resid, g, b):
    M, N = ya.shape
    tm = _pick(M, (256, 128, 64, 32, 16, 8))
    blk = pl.BlockSpec((tm, N), lambda i: (i, 0))
    col = pl.BlockSpec((tm, 1), lambda i: (i, 0))
    row = pl.BlockSpec((1, N), lambda i: (0, 0))
    return pl.pallas_call(
        _combine_ln_kernel,
        grid=(M // tm,),
        in_specs=[blk, blk, col, col, blk, row, row],
        out_specs=blk,
        out_shape=jax.ShapeDtypeStruct((M, N), F32),
        compiler_params=_params(("arbitrary",)),
    )(ya, yb, ga, gb, resid, g.reshape(1, N), b.reshape(1, N))


def moe_layer(x, router_w, router_b, layer, wg, wu, wd, ln_g, ln_b):
    shp = x.shape
    xt = x.reshape(-1, shp[-1])
    T = xt.shape[0]
    expert, gate = route(xt, router_w, router_b)
    flat_e = expert.T.reshape(-1)
    n_assign = T * TOP_K
    onehot = (flat_e[:, None] == jnp.arange(N_EXPERTS, dtype=jnp.int32)[None]).astype(jnp.int32)
    csum = jnp.cumsum(onehot, axis=0)
    rank = jnp.take_along_axis(csum, flat_e[:, None], axis=1)[:, 0] - 1
    counts = csum[-1]
    padded = (counts + MOE_TM - 1) // MOE_TM * MOE_TM
    pad_end = jnp.cumsum(padded)
    pad_start = pad_end - padded
    dest = pad_start[flat_e] + rank
    nb = -(-n_assign // MOE_TM) + N_EXPERTS
    rows = nb * MOE_TM
    row_tok = jnp.zeros(rows, jnp.int32).at[dest].set(jnp.arange(n_assign, dtype=jnp.int32) // TOP_K)
    block_exp = jnp.minimum(
        jnp.searchsorted(pad_end, jnp.arange(nb, dtype=jnp.int32) * MOE_TM, side='right'),
        N_EXPERTS - 1).astype(jnp.int32)
    yb = expert_ffn(xt[row_tok], block_exp, layer, wg, wu, wd)
    d2 = dest.reshape(T, TOP_K)
    out = combine_ln(yb[d2[:, 0]], yb[d2[:, 1]], gate[0][:, None], gate[1][:, None], xt, ln_g, ln_b)
    return out.reshape(shp)


def _block_diag(w):
    G, I, O = w.shape
    eye = jnp.eye(G, dtype=w.dtype)
    return (eye[:, None, :, None] * w[:, :, None, :]).reshape(G * I, G * O)


def even_layer(x, lru_h, lru_conv, wkv, shift, p, ln_g, ln_b):
    B, T, D = x.shape
    D_LRU = lru_h.shape[-1]
    D_RWKV = p['w0'].shape[1]
    H = D_RWKV // RWKV_HEAD
    N = RWKV_HEAD
    z = matmul(x.reshape(B * T, D), p['w_in']).reshape(B, T, -1)
    y_a, r, decay, k, v, a, g, h_T = even_pre(z, lru_conv, shift, lru_h, p)

    lanes = lambda t: jnp.transpose(t.reshape(B, T, H, N), (1, 3, 0, 2)).reshape(T, N, B * H)
    per_chain = lambda w: jnp.tile(w.reshape(H, N).T, (1, B))
    s0 = jnp.transpose(wkv, (2, 3, 0, 1)).reshape(N, N, B * H)
    y, s_T = rwkv_scan(lanes(r), lanes(decay), lanes(k), lanes(v), lanes(a),
                       per_chain(p['k_k']), per_chain(p['k_a']), per_chain(p['gn_g']),
                       per_chain(p['gn_b']), per_chain(p['r_k']), s0)
    y = jnp.transpose(y.reshape(T, N, B, H), (2, 0, 3, 1)).reshape(B * T, D_RWKV)
    S_T = jnp.transpose(s_T.reshape(N, N, B, H), (2, 3, 0, 1))

    x_new = mix_ln(y_a.reshape(B * T, D_LRU), y, g.reshape(B * T, D_RWKV), p['w_out'],
                   x.reshape(B * T, D), ln_g, ln_b)
    conv_state = jnp.concatenate([lru_conv, z[..., :D_LRU]], axis=1)[:, T:]
    return x_new.reshape(B, T, D), (h_T[:, 0], conv_state, S_T, z[:, -1:, 2 * D_LRU:])


def odd_layer(x, cache, p, ln_g, ln_b):
    B, T, D = x.shape
    dq = N_HEADS_C * HEAD_DIM_C
    dkv = N_KV_C * HEAD_DIM_C
    dqi = N_IDX_HEADS * D_IDX
    z = matmul(x.reshape(B * T, D), p['w_in']).reshape(B, T, -1)
    k = z[..., dq:dq + dkv]
    v = z[..., dq + dkv:dq + 2 * dkv]
    o = dq + 2 * dkv + dqi
    ki = z[..., o:o + D_IDX]
    wi = z[..., o + D_IDX:o + D_IDX + N_IDX_HEADS] * (dqi ** -0.5)
    Tp = -(-T // LANES) * LANES
    if cache is None:
        past = 0
        n_sel = min(TOPK_MAX, T // 4)
        parts = lambda c, n: [n]
    else:
        past = cache[0].shape[1]
        n_sel = min(TOPK_MAX, (past + T) // 4)
        parts = lambda c, n: [c.reshape(B, past, -1), n]
    L = -(-(past + Tp) // LANES) * LANES
    tk = _pick(L, (512, 384, 128))
    cat = lambda c, n: jnp.concatenate(
        parts(c, n) + [jnp.zeros((B, L - past - T, n.shape[-1]), n.dtype)], axis=1).astype(BF16)
    ck, cv, cki = cache if cache is not None else (None, None, None)
    keys, vals, kidx = cat(ck, k), cat(cv, v), cat(cki, ki)
    vals_t = jnp.swapaxes(vals.reshape(B, L // tk, tk, dkv), 2, 3)
    rowpad = lambda t: jnp.pad(t, ((0, 0), (0, Tp - T), (0, 0)))
    wi_t = jnp.swapaxes(rowpad(wi), 1, 2)
    o_att = dsa_attention(rowpad(z), 0, (dq + 2 * dkv) // dqi, wi_t, keys, vals_t, kidx, past, n_sel, tk)
    x_new = matmul_ln(o_att[:, :T].reshape(B * T, dq), p['w_out'], x.reshape(B * T, D), ln_g, ln_b)
    new = (k.reshape(B, T, N_KV_C, HEAD_DIM_C), v.reshape(B, T, N_KV_C, HEAD_DIM_C), ki)
    return x_new.reshape(B, T, D), new


def run_trunk(x, even_state, odd_cache, ev, od, norm_w, moe_w):
    ln1_g, ln1_b, ln2_g, ln2_b = norm_w
    router_w, router_b, wg, wu, wd = moe_w
    x, new_even = even_layer(x, *even_state, ev, ln1_g[0], ln1_b[0])
    x = moe_layer(x, router_w, router_b, 0, wg, wu, wd, ln2_g[0], ln2_b[0])
    x, new_odd = odd_layer(x, odd_cache, od, ln1_g[1], ln1_b[1])
    x = moe_layer(x, router_w, router_b, 1, wg, wu, wd, ln2_g[1], ln2_b[1])
    return x, new_even, new_odd


def kernel(x_prompt, x_sample, state_lru_h, state_lru_conv, state_rwkv_wkv, state_rwkv_shift, cache_k, cache_v, cache_kidx, ev_w_in, ev_conv_w, ev_conv_b, ev_gate_r_w, ev_gate_r_b, ev_gate_i_w, ev_gate_i_b, ev_lambda, ev_shift_mu, ev_w0, ev_w2, ev_a0, ev_a2, ev_g2, ev_k_k, ev_k_a, ev_r_k, ev_gn_g, ev_gn_b, ev_w_out, od_w_in, od_w_out, ln1_g, ln1_b, ln2_g, ln2_b, router_w, router_b, moe_w_gate, moe_w_up, moe_w_down):
    ev = dict(
        w_in=ev_w_in[0].astype(BF16), conv_w=ev_conv_w[0], conv_b=ev_conv_b[0][None],
        gate_w=jnp.concatenate([_block_diag(ev_gate_r_w[0]), _block_diag(ev_gate_i_w[0])], axis=1).astype(BF16),
        gate_b=jnp.concatenate([ev_gate_r_b[0], ev_gate_i_b[0]])[None],
        lru_c=(-LRU_C * jax.nn.softplus(-ev_lambda[0]))[None],
        shift_mu=ev_shift_mu[0][None], w0=ev_w0[0][None], a0=ev_a0[0][None],
        w2=ev_w2[0].astype(BF16), a2=ev_a2[0].astype(BF16), g2=ev_g2[0].astype(BF16),
        k_k=ev_k_k[0][None], k_a=ev_k_a[0][None], r_k=ev_r_k[0], gn_g=ev_gn_g[0], gn_b=ev_gn_b[0],
        w_out=ev_w_out[0].astype(BF16))
    n_in = od_w_in.shape[2]
    od = dict(w_in=jnp.pad(od_w_in[0], ((0, 0), (0, (-n_in) % LANES))).astype(BF16),
              w_out=od_w_out[0].astype(BF16))
    norm_w = (ln1_g, ln1_b, ln2_g, ln2_b)
    moe_w = (router_w, router_b, cast_slabs(moe_w_gate, 3), cast_slabs(moe_w_up, 3), cast_slabs(moe_w_down, 2))

    B = x_prompt.shape[0]
    fresh = (jnp.zeros((B,) + state_lru_h.shape[2:], F32),
             jnp.zeros((B,) + state_lru_conv.shape[2:], F32),
             jnp.zeros((B,) + state_rwkv_wkv.shape[2:], F32),
             jnp.zeros((B,) + state_rwkv_shift.shape[2:], F32))
    y_p, pe, po = run_trunk(x_prompt, fresh, None, ev, od, norm_w, moe_w)
    y_s, se, so = run_trunk(
        x_sample, (state_lru_h[0], state_lru_conv[0], state_rwkv_wkv[0], state_rwkv_shift[0]),
        (cache_k[0], cache_v[0], cache_kidx[0]), ev, od, norm_w, moe_w)
    st = lambda t: t[None]
    return (y_p, y_s, *map(st, pe), *map(st, po), *map(st, se), *map(st, so))
```

```python
import functools
import math

import jax
import jax.numpy as jnp
from jax import lax
from jax.experimental import pallas as pl
from jax.experimental.pallas import tpu as pltpu

F32 = jnp.float32
BF16 = jnp.bfloat16

CHUNK = 64
LRU_BLOCKS = 16
CONV_W = 4
LRU_C = 8.0
RWKV_HEAD = 64
DECAY_LORA = 64
AAA_LORA = 64
GATE_LORA = 128
N_HEADS_C = 16
HEAD_DIM_C = 128
N_KV_C = 4
N_IDX_HEADS = 16
D_IDX = 64
TOPK_MAX = 256
N_EXPERTS = 16
N_GROUPS = 4
EXPERTS_PER_GROUP = N_EXPERTS // N_GROUPS
TOP_K = 2
DEPTH = 2
DN_ALPHA = (2 * DEPTH) ** 0.25
LN_EPS = 1e-5
GN_EPS = 64e-5

LANES = 128
SUBLANES = 8
VMEM_LIMIT = 56 * 1024 * 1024
MOE_TM = 256


def _params(sem):
    return pltpu.CompilerParams(dimension_semantics=sem, vmem_limit_bytes=VMEM_LIMIT)


def _mm_kernel(x_ref, w_ref, o_ref, xb_ref):
    @pl.when(pl.program_id(1) == 0)
    def _():
        xb_ref[...] = x_ref[...].astype(BF16)

    o_ref[...] = jnp.dot(xb_ref[...], w_ref[...], preferred_element_type=F32)


def _pick(n, cands):
    for c in cands:
        if n % c == 0:
            return c
    return n


def matmul(x, w):
    M, K = x.shape
    N = w.shape[1]
    tm = _pick(M, (1024, 512, 256, 128, 64, 32, 16, 8))
    tn = _pick(N, (768, 1024, 1408, 512, 384, 256, 128))
    return pl.pallas_call(
        _mm_kernel,
        grid=(M // tm, N // tn),
        in_specs=[pl.BlockSpec((tm, K), lambda i, j: (i, 0)),
                  pl.BlockSpec((K, tn), lambda i, j: (0, j))],
        out_specs=pl.BlockSpec((tm, tn), lambda i, j: (i, j)),
        out_shape=jax.ShapeDtypeStruct((M, N), F32),
        scratch_shapes=[pltpu.VMEM((tm, K), BF16)],
        compiler_params=_params(("arbitrary", "arbitrary")),
    )(x, w)


def _deepnorm(resid, y, g, b):
    h = DN_ALPHA * resid + y
    mu = jnp.mean(h, axis=-1, keepdims=True)
    hc = h - mu
    var = jnp.mean(hc * hc, axis=-1, keepdims=True)
    return hc * lax.rsqrt(var + LN_EPS) * g + b


def _ln_route_tail(y, r_ref, g_ref, b_ref, rw_ref, rb_ref, o_ref, e_ref, gt_ref):
    o = _deepnorm(r_ref[...], y, g_ref[...], b_ref[...])
    o_ref[...] = o
    e_ref[...], gt_ref[...] = _route_tokens(o, rw_ref[...], rb_ref[...])


def _mm_ln_kernel(x_ref, w_ref, *tail):
    y = jnp.dot(x_ref[...].astype(BF16), w_ref[...], preferred_element_type=F32)
    _ln_route_tail(y, *tail)


def _mix_ln_kernel(ya_ref, yb_ref, gate_ref, w_ref, *tail):
    mix = jnp.concatenate([ya_ref[...], yb_ref[...] * gate_ref[...]], axis=1).astype(BF16)
    y = jnp.dot(mix, w_ref[...], preferred_element_type=F32)
    _ln_route_tail(y, *tail)


def _ln_route_call(kern, lhs, w, resid, g, b, router_w, router_b):
    M, N = resid.shape
    tm = _pick(M, (256, 128))
    full = lambda a: pl.BlockSpec(a.shape, lambda i: (0,) * a.ndim)
    rows = lambda a: pl.BlockSpec((tm, a.shape[1]), lambda i: (i, 0))
    consts = [g.reshape(1, N), b.reshape(1, N), router_w.T, router_b.reshape(-1, 1)]
    pair = pl.BlockSpec((TOP_K, tm), lambda i: (0, i))
    return pl.pallas_call(
        kern,
        grid=(M // tm,),
        in_specs=[rows(a) for a in lhs] + [full(w), rows(resid)] + [full(c) for c in consts],
        out_specs=[rows(resid), pair, pair],
        out_shape=[jax.ShapeDtypeStruct((M, N), F32), jax.ShapeDtypeStruct((TOP_K, M), jnp.int32),
                   jax.ShapeDtypeStruct((TOP_K, M), F32)],
        compiler_params=_params(("arbitrary",)),
    )(*lhs, w, resid, *consts)


def matmul_ln(x, w, resid, g, b, router_w, router_b):
    return _ln_route_call(_mm_ln_kernel, [x], w, resid, g, b, router_w, router_b)


def mix_ln(ya, yb, gate, w, resid, g, b, router_w, router_b):
    return _ln_route_call(_mix_ln_kernel, [ya, yb, gate], w, resid, g, b, router_w, router_b)


def _expm1(x):
    e = jnp.exp(x)
    safe = jnp.where((e == 1.0) | (e == 0.0), 2.0, e)
    return jnp.where(e == 1.0, x, jnp.where(e == 0.0, -1.0, (e - 1.0) * x / jnp.log(safe)))


def _softplus(x):
    return jnp.maximum(x, 0.0) + jnp.log(1.0 + jnp.exp(-jnp.abs(x)))


HALO = SUBLANES


def _even_pre_kernel(ax_ref, ag_ref, zr_ref, zk_ref, zv_ref, zt_ref,
                     conv_ref, sr_ref, sk_ref, sv_ref, st_ref, h0_ref,
                     cw_ref, cb_ref, gw_ref, gb_ref, lc_ref, mu_ref,
                     w0_ref, a0_ref, w2_ref, a2_ref, g2_ref,
                     ya_ref, r_ref, w_ref, k_ref, v_ref, a_ref, g_ref, hT_ref,
                     xp_scr, pr_scr, pk_scr, pv_scr, pt_scr, a_scr, u_scr, hs_scr, h_scr):
    i = pl.program_id(1)
    tm = ax_ref.shape[1]
    C = ax_ref.shape[2]
    prev = ((xp_scr, None), (pr_scr, sr_ref), (pk_scr, sk_ref), (pv_scr, sv_ref), (pt_scr, st_ref))

    @pl.when(i == 0)
    def _():
        xp_scr[...] = jnp.zeros((HALO, C), F32)
        xp_scr[HALO - (CONV_W - 1):HALO, :] = conv_ref[0]
        for scr, ref in prev[1:]:
            scr[...] = jnp.broadcast_to(ref[0], scr.shape)
        h_scr[...] = jnp.broadcast_to(h0_ref[0], h_scr.shape)

    def earlier(x, scr, d):
        rot = pltpu.roll(x, d, axis=0)
        halo = pltpu.roll(scr[...], d, axis=0)
        sub = lax.broadcasted_iota(jnp.int32, halo.shape, 0)
        return jnp.concatenate([jnp.where(sub < d, halo, rot[:HALO]), rot[HALO:]], axis=0)

    ax = ax_ref[0]
    xa = cb_ref[...] + cw_ref[CONV_W - 1:CONV_W, :] * ax
    for d in range(1, CONV_W):
        xa = xa + cw_ref[CONV_W - 1 - d:CONV_W - d, :] * earlier(ax, xp_scr, d)
    gates = jnp.dot(xa.astype(BF16), gw_ref[...], preferred_element_type=F32) + gb_ref[...]
    log_a = lc_ref[...] * jax.nn.sigmoid(gates[:, :C])
    a_scr[...] = jnp.exp(log_a)
    u_scr[...] = jnp.sqrt(-_expm1(2.0 * log_a)) * (jax.nn.sigmoid(gates[:, C:]) * xa)

    def step(t, h):
        h = a_scr[pl.ds(t, 1), :] * h + u_scr[pl.ds(t, 1), :]
        hs_scr[pl.ds(t, 1), :] = h
        return h

    h = lax.fori_loop(0, tm, step, h_scr[0:1, :], unroll=8)
    h_scr[...] = jnp.broadcast_to(h, h_scr.shape)
    hT_ref[0] = h
    ya_ref[0] = hs_scr[...] * jax.nn.gelu(ag_ref[0], approximate=True)

    zr, zk, zv, zt = zr_ref[0], zk_ref[0], zv_ref[0], zt_ref[0]

    def shifted(zb, scr, lo, hi):
        return zb + mu_ref[:, lo:hi] * (earlier(zb, scr, 1) - zb)

    r = shifted(zr, pr_scr, 0, C)
    k = shifted(zk, pk_scr, C, 2 * C)
    v = shifted(zv, pv_scr, 2 * C, 3 * C)
    tail = shifted(zt, pt_scr, 3 * C, 3 * C + pt_scr.shape[1])
    wl = jnp.tanh(tail[:, :DECAY_LORA])
    al = tail[:, DECAY_LORA:DECAY_LORA + AAA_LORA]
    gl = jax.nn.sigmoid(tail[:, DECAY_LORA + AAA_LORA:])
    lora = lambda x, w_ref_: jnp.dot(x.astype(BF16), w_ref_[...], preferred_element_type=F32)
    w_log = -_softplus(-(w0_ref[...] + lora(wl, w2_ref))) - 0.5
    a = jax.nn.sigmoid(a0_ref[...] + lora(al, a2_ref))
    r_ref[0] = r
    w_ref[0] = jnp.exp(-jnp.exp(w_log))
    k_ref[0] = k
    v_ref[0] = v
    a_ref[0] = a
    g_ref[0] = lora(gl, g2_ref)

    for (scr, _), x in zip(prev, (ax, zr, zk, zv, zt)):
        scr[...] = x[tm - HALO:]


def even_pre(z, lru_conv, shift, lru_h, p):
    B, T, _ = z.shape
    C = lru_h.shape[-1]
    tail = DECAY_LORA + AAA_LORA + GATE_LORA
    tm = _pick(T, (128, 64, 32, 16, 8))
    zblk = lambda w, col: pl.BlockSpec((1, tm, w), lambda b, i: (b, i, col))
    sblk = lambda rows, w, col: pl.BlockSpec((1, rows, w), lambda b, i: (b, 0, col))
    full = lambda a: pl.BlockSpec(a.shape, lambda b, i: (0,) * a.ndim)
    weights = [p['conv_w'], p['conv_b'], p['gate_w'], p['gate_b'], p['lru_c'], p['shift_mu'],
               p['w0'], p['a0'], p['w2'], p['a2'], p['g2']]
    tok = jax.ShapeDtypeStruct((B, T, C), F32)
    return pl.pallas_call(
        _even_pre_kernel,
        grid=(B, T // tm),
        in_specs=[zblk(C, 0), zblk(C, 1), zblk(C, 2), zblk(C, 3), zblk(C, 4), zblk(tail, 5 * C // tail),
                  sblk(CONV_W - 1, C, 0), sblk(1, C, 0), sblk(1, C, 1), sblk(1, C, 2),
                  sblk(1, tail, 3 * C // tail), sblk(1, C, 0)] + [full(w) for w in weights],
        out_specs=[zblk(C, 0)] * 7 + [sblk(1, C, 0)],
        out_shape=[tok] * 7 + [jax.ShapeDtypeStruct((B, 1, C), F32)],
        scratch_shapes=[pltpu.VMEM((HALO, C), F32)] * 4 + [pltpu.VMEM((HALO, tail), F32)]
                       + [pltpu.VMEM((tm, C), F32)] * 3 + [pltpu.VMEM((SUBLANES, C), F32)],
        compiler_params=_params(("arbitrary", "arbitrary")),
    )(z, z, z, z, z, z, lru_conv, shift, shift, shift, shift, lru_h[:, None, :], *weights)


def _colsum(x):
    n = x.shape[0] // SUBLANES
    p = x.reshape(n, SUBLANES, x.shape[1]).sum(axis=0)
    p = p + pltpu.roll(p, 4, axis=0)
    p = p + pltpu.roll(p, 2, axis=0)
    p = p + pltpu.roll(p, 1, axis=0)
    return p


def _bcast_rows(p, n):
    return jnp.broadcast_to(p[None], (n // SUBLANES,) + p.shape).reshape(n, p.shape[1])


def _rwkv_kernel(r_ref, w_ref, k_ref, v_ref, a_ref, kk_ref, ka_ref, gng_ref, gnb_ref, rk_ref, s0_ref,
                 y_ref, sT_ref, s_scr):
    tc = pl.program_id(1)
    N = RWKV_HEAD

    @pl.when(tc == 0)
    def _():
        s_scr[...] = s0_ref[...]

    sub = lax.broadcasted_iota(jnp.int32, (SUBLANES, LANES), 0)

    def step(t, carry):
        r = r_ref[t]
        w = w_ref[t]
        k_raw = k_ref[t]
        a = a_ref[t]
        kk = k_raw * kk_ref[...]
        kk = kk / jnp.maximum(_bcast_rows(jnp.sqrt(_colsum(kk * kk)), N), 1e-12)
        nkk = -kk
        kka = kk * a
        k = k_raw * (1.0 + (a - 1.0) * ka_ref[...])

        def vgroup(g, c2):
            yacc = jnp.zeros((SUBLANES, LANES), F32)
            for j in range(SUBLANES):
                vi = g * SUBLANES + j
                S = s_scr[vi]
                sa = _bcast_rows(_colsum(S * nkk), N)
                vb = jnp.broadcast_to(v_ref[t, pl.ds(vi, 1), :], (N, LANES))
                S = S * w + sa * kka + vb * k
                s_scr[vi] = S
                yacc = jnp.where(sub == j, _colsum(S * r), yacc)
            y_ref[t, pl.ds(pl.multiple_of(g * SUBLANES, SUBLANES), SUBLANES), :] = yacc
            return c2

        lax.fori_loop(0, N // SUBLANES, vgroup, 0, unroll=True)

        y = y_ref[t]
        d = y - _bcast_rows(_colsum(y), N) * (1.0 / N)
        var = _bcast_rows(_colsum(d * d), N) * (1.0 / N)
        bonus = _bcast_rows(_colsum(r * k * rk_ref[...]), N)
        y_ref[t] = d * lax.rsqrt(var + GN_EPS) * gng_ref[...] + gnb_ref[...] + bonus * v_ref[t]
        return carry

    lax.fori_loop(0, r_ref.shape[0], step, 0)

    @pl.when(tc == pl.num_programs(1) - 1)
    def _():
        sT_ref[...] = s_scr[...]


def rwkv_scan(r, w, k, v, a, k_k, k_a, gn_g, gn_b, r_k, s0):
    T, N, P = r.shape
    tc = _pick(T, (32, 16, 8))
    blk = pl.BlockSpec((tc, N, LANES), lambda p, t: (t, 0, p))
    par = pl.BlockSpec((N, LANES), lambda p, t: (0, p))
    st = pl.BlockSpec((N, N, LANES), lambda p, t: (0, 0, p))
    return pl.pallas_call(
        _rwkv_kernel,
        grid=(P // LANES, T // tc),
        in_specs=[blk] * 5 + [par] * 5 + [st],
        out_specs=[blk, st],
        out_shape=[jax.ShapeDtypeStruct((T, N, P), F32), jax.ShapeDtypeStruct((N, N, P), F32)],
        scratch_shapes=[pltpu.VMEM((N, N, LANES), F32)],
        compiler_params=_params(("arbitrary", "arbitrary")),
    )(r, w, k, v, a, k_k, k_a, gn_g, gn_b, r_k, s0)


MASKED = -1e30


KEY_NEG_INF = -2139095041
QK_SPLIT = 2


def _tree_sum(parts):
    while len(parts) > 1:
        parts = [parts[i] + parts[i + 1] for i in range(0, len(parts) - 1, 2)] + parts[len(parts) & ~1:]
    return parts[0]


def _dsa_kernel(q_ref, qi_ref, wi_ref, k_ref, vt_ref, ki_ref, o_ref,
                key_scr, m_scr, l_scr, acc_scr, *, past, n_sel, scale, tk):
    qb = pl.program_id(1)
    QB = q_ref.shape[1]
    L = k_ref.shape[1]
    NP = N_HEADS_C // 2
    lane_chunk = lax.broadcasted_iota(jnp.int32, (1, QB), 1) // CHUNK
    limit = past + (qb * (QB // CHUNK) + lane_chunk + 1) * CHUNK
    n_tiles = (past + (qb + 1) * QB + tk - 1) // tk
    u = 2 if (tk < 512 and (L // tk) % 2 == 0) else 1
    n_count = (n_tiles + u - 1) // u
    row = lax.broadcasted_iota(jnp.int32, (tk, QB), 0)
    tile = lambda kt: pl.ds(pl.multiple_of(kt * tk, tk), tk)

    qi = qi_ref[0].astype(BF16)
    qi_pair = [jnp.concatenate([qi[:, (2 * p) * D_IDX:(2 * p + 1) * D_IDX],
                                qi[:, (2 * p + 1) * D_IDX:(2 * p + 2) * D_IDX]], axis=0)
               for p in range(N_IDX_HEADS // 2)]
    wi = wi_ref[0]

    def score_tile(kt, c):
        @pl.when(kt < n_tiles)
        def _():
            kidx = ki_ref[0, tile(kt), :]
            terms = []
            for p in range(N_IDX_HEADS // 2):
                s = lax.dot_general(kidx, qi_pair[p], (((1,), (1,)), ((), ())),
                                    preferred_element_type=F32)
                for e in range(2):
                    h = 2 * p + e
                    terms.append(wi[h:h + 1, :] * jnp.maximum(s[:, e * QB:(e + 1) * QB], 0.0))
            isc = terms[0]
            for t in terms[1:]:
                isc = isc + t
            ok = row + kt * tk < limit
            bits = lax.bitcast_convert_type(jnp.where(ok, isc + 0.0, -jnp.inf), jnp.int32)
            key_scr[tile(kt), :] = bits ^ ((bits >> 31) & jnp.int32(0x7FFFFFFF))

        @pl.when(kt >= n_tiles)
        def _():
            key_scr[tile(kt), :] = jnp.full((tk, QB), KEY_NEG_INF, jnp.int32)

        return c

    lax.fori_loop(0, n_count * u, score_tile, 0)

    int_min = jnp.int32(-2 ** 31)
    ct = tk * u

    def search(i, t):
        cand = t | lax.shift_left(jnp.int32(1), 31 - i)
        cs = cand ^ int_min

        def count_tile(kt, acc):
            keys = key_scr[pl.ds(pl.multiple_of(kt * ct, ct), ct), :]
            hit = jnp.where(keys >= cs, 1.0, 0.0)
            return acc + _tree_sum([hit[r * SUBLANES:(r + 1) * SUBLANES] for r in range(ct // SUBLANES)])

        acc = lax.fori_loop(0, n_count, count_tile, jnp.zeros((SUBLANES, QB), F32))
        cnt = jnp.sum(acc, axis=0, keepdims=True)
        return jnp.where(cnt >= n_sel, cand, t)

    thr = lax.fori_loop(0, 32, search, jnp.zeros((1, QB), jnp.int32)) ^ int_min

    G = N_HEADS_C // N_KV_C
    dh = HEAD_DIM_C
    q = q_ref[0]
    q_pair = [(jnp.concatenate([q[:, (2 * p) * dh:(2 * p + 1) * dh],
                                q[:, (2 * p + 1) * dh:(2 * p + 2) * dh]], axis=0) * scale).astype(BF16)
              for p in range(NP)]
    m_scr[...] = jnp.full(m_scr.shape, MASKED, F32)
    l_scr[...] = jnp.zeros(l_scr.shape, F32)
    acc_scr[...] = jnp.zeros(acc_scr.shape, F32)

    def att_tile(kt, c):
        sel = (key_scr[tile(kt), :] >= thr) & (row + kt * tk < limit)
        b = jnp.where(sel, 0.0, MASKED)
        b2 = jnp.concatenate([b, b], axis=1)
        def logits(p):
            j = (2 * p) // G
            kj = k_ref[0, tile(kt), j * dh:(j + 1) * dh]
            part = tk // QK_SPLIT
            return jnp.concatenate(
                [lax.dot_general(kj[r * part:(r + 1) * part], q_pair[p], (((1,), (1,)), ((), ())),
                                 preferred_element_type=F32) for r in range(QK_SPLIT)], axis=0)

        ahead = 2
        lg = {p: logits(p) for p in range(ahead)}
        for p in range(NP):
            j = (2 * p) // G
            x = lg.pop(p) + b2
            m = m_scr[p:p + 1, :]
            m_new = jnp.maximum(m, jnp.max(x, axis=0, keepdims=True))
            alpha = jnp.exp2(m - m_new)
            pr = jnp.exp2(x - m_new)
            l_scr[p:p + 1, :] = alpha * l_scr[p:p + 1, :] + jnp.sum(pr, axis=0, keepdims=True)
            m_scr[p:p + 1, :] = m_new
            vt = vt_ref[0, kt, j * dh:(j + 1) * dh, :]
            pv = jnp.dot(vt, pr.astype(BF16), preferred_element_type=F32)
            if p + ahead < NP:
                lg[p + ahead] = logits(p + ahead)
            acc_scr[p] = acc_scr[p] * alpha + pv
        return c

    lax.fori_loop(0, n_tiles, att_tile, 0)
    for p in range(NP):
        o2 = acc_scr[p] / l_scr[p:p + 1, :]
        for s in range(2):
            o_ref[0, :, (2 * p + s) * dh:(2 * p + s + 1) * dh] = o2[:, s * QB:(s + 1) * QB].T


def dsa_attention(z, q_col, qi_col, wi_t, keys, vals_t, kidx, past, n_sel, tk):
    B, T, _ = z.shape
    L = keys.shape[1]
    QB = LANES
    dq = N_HEADS_C * HEAD_DIM_C
    dqi = N_IDX_HEADS * D_IDX
    kern = functools.partial(_dsa_kernel, past=past, n_sel=n_sel,
                             scale=HEAD_DIM_C ** -0.5 * math.log2(math.e), tk=tk)
    return pl.pallas_call(
        kern,
        grid=(B, T // QB),
        in_specs=[pl.BlockSpec((1, QB, dq), lambda b, c: (b, c, q_col)),
                  pl.BlockSpec((1, QB, dqi), lambda b, c: (b, c, qi_col)),
                  pl.BlockSpec((1, N_IDX_HEADS, QB), lambda b, c: (b, 0, c)),
                  pl.BlockSpec((1, L, keys.shape[2]), lambda b, c: (b, 0, 0)),
                  pl.BlockSpec((1,) + vals_t.shape[1:], lambda b, c: (b, 0, 0, 0)),
                  pl.BlockSpec((1, L, kidx.shape[2]), lambda b, c: (b, 0, 0))],
        out_specs=pl.BlockSpec((1, QB, dq), lambda b, c: (b, c, 0)),
        out_shape=jax.ShapeDtypeStruct((B, T, dq), F32),
        scratch_shapes=[pltpu.VMEM((L, QB), jnp.int32),
                        pltpu.VMEM((N_HEADS_C // 2, 2 * QB), F32),
                        pltpu.VMEM((N_HEADS_C // 2, 2 * QB), F32),
                        pltpu.VMEM((N_HEADS_C // 2, HEAD_DIM_C, 2 * QB), F32)],
        compiler_params=_params(("arbitrary", "arbitrary")),
    )(z, z, wi_t, keys, vals_t, kidx)


def _route_tokens(x, w, bias):
    logits = lax.dot_general(w, x, (((1,), (1,)), ((), ())),
                             preferred_element_type=F32, precision=lax.Precision.HIGHEST)
    scores = jax.nn.sigmoid(logits)
    biased = scores + bias
    tm = scores.shape[1]
    P = EXPERTS_PER_GROUP

    def ranks(rows):
        out = []
        for i in range(len(rows)):
            rk = jnp.zeros((1, tm), jnp.int32)
            for j in range(len(rows)):
                if j == i:
                    continue
                beat = (rows[j] > rows[i]) | ((rows[j] == rows[i]) & (j < i))
                rk = rk + beat.astype(jnp.int32)
            out.append(rk)
        return out

    b = [biased[i:i + 1] for i in range(N_EXPERTS)]
    s = [scores[i:i + 1] for i in range(N_EXPERTS)]
    grp_rank, grp_score = [], []
    for g in range(N_GROUPS):
        rows = b[g * P:(g + 1) * P]
        rk = ranks(rows)
        grp_rank.append(rk)
        top1 = sum(jnp.where(rk[i] == 0, rows[i], 0.0) for i in range(P))
        top2 = sum(jnp.where(rk[i] == 1, rows[i], 0.0) for i in range(P))
        grp_score.append(top1 + top2)
    gr = ranks(grp_score)
    e1 = jnp.zeros((1, tm), jnp.int32)
    e2 = jnp.zeros((1, tm), jnp.int32)
    w1 = jnp.zeros((1, tm), F32)
    w2 = jnp.zeros((1, tm), F32)
    for g in range(N_GROUPS):
        best = gr[g] == 0
        for i in range(P):
            is1 = best & (grp_rank[g][i] == 0)
            is2 = best & (grp_rank[g][i] == 1)
            e1 = jnp.where(is1, g * P + i, e1)
            e2 = jnp.where(is2, g * P + i, e2)
            w1 = jnp.where(is1, s[g * P + i], w1)
            w2 = jnp.where(is2, s[g * P + i], w2)
    tot = w1 + w2
    return jnp.concatenate([e1, e2], axis=0), jnp.concatenate([w1 / tot, w2 / tot], axis=0)


def _ffn_kernel(be_ref, x_ref, wg_ref, wu_ref, wd_ref, o_ref):
    i = pl.program_id(0)
    used = be_ref[pl.num_programs(0)]

    @pl.when(i < used)
    def _():
        x = x_ref[...].astype(BF16)
        a = jnp.dot(x, wg_ref[0, 0], preferred_element_type=F32)
        u = jnp.dot(x, wu_ref[0, 0], preferred_element_type=F32)
        h = (a * jax.nn.sigmoid(a)) * u
        o_ref[...] = jnp.dot(h.astype(BF16), wd_ref[0, 0], preferred_element_type=F32)

    @pl.when(i >= used)
    def _():
        o_ref[...] = jnp.zeros(o_ref.shape, F32)


def _cast_kernel(x_ref, o_ref):
    o_ref[...] = x_ref[...].astype(o_ref.dtype)


def cast_bf16(w):
    n = w.shape[-1]
    w2 = w.reshape(-1, n)
    rows = w2.shape[0]
    tr = _pick(rows, (1024 * 1024 // n, 512, 256, 128, 64, 32, 16))
    out = pl.pallas_call(
        _cast_kernel,
        grid=(rows // tr,),
        in_specs=[pl.BlockSpec((tr, n), lambda i: (i, 0))],
        out_specs=pl.BlockSpec((tr, n), lambda i: (i, 0)),
        out_shape=jax.ShapeDtypeStruct(w2.shape, BF16),
        compiler_params=_params(("arbitrary",)),
    )(w2)
    return out.reshape(w.shape)


def expert_ffn(xs, block_exp, layer, wg, wu, wd):
    R, D = xs.shape
    F = wg.shape[3]
    nb = R // MOE_TM
    grid_spec = pltpu.PrefetchScalarGridSpec(
        num_scalar_prefetch=1,
        grid=(nb,),
        in_specs=[pl.BlockSpec((MOE_TM, D), lambda i, be: (i, 0)),
                  pl.BlockSpec((1, 1, D, F), lambda i, be: (layer, be[i], 0, 0)),
                  pl.BlockSpec((1, 1, D, F), lambda i, be: (layer, be[i], 0, 0)),
                  pl.BlockSpec((1, 1, F, D), lambda i, be: (layer, be[i], 0, 0))],
        out_specs=pl.BlockSpec((MOE_TM, D), lambda i, be: (i, 0)),
    )
    return pl.pallas_call(
        _ffn_kernel,
        grid_spec=grid_spec,
        out_shape=jax.ShapeDtypeStruct((R, D), F32),
        compiler_params=_params(("arbitrary",)),
    )(block_exp, xs, wg, wu, wd)


def _combine_ln_kernel(ya_ref, yb_ref, ga_ref, gb_ref, r_ref, g_ref, b_ref, o_ref):
    y = ya_ref[...] * ga_ref[...] + yb_ref[...] * gb_ref[...]
    o_ref[...] = _deepnorm(r_ref[...], y, g_ref[...], b_ref[...])


def combine_ln(ya, yb, ga, gb, resid, g, b):
    M, N = ya.shape
    tm = _pick(M, (256, 128, 64, 32, 16, 8))
    blk = pl.BlockSpec((tm, N), lambda i: (i, 0))
    col = pl.BlockSpec((tm, 1), lambda i: (i, 0))
    row = pl.BlockSpec((1, N), lambda i: (0, 0))
    return pl.pallas_call(
        _combine_ln_kernel,
        grid=(M // tm,),
        in_specs=[blk, blk, col, col, blk, row, row],
        out_specs=blk,
        out_shape=jax.ShapeDtypeStruct((M, N), F32),
        compiler_params=_params(("arbitrary",)),
    )(ya, yb, ga, gb, resid, g.reshape(1, N), b.reshape(1, N))


def moe_layer(x, expert, gate, layer, wg, wu, wd, ln_g, ln_b):
    shp = x.shape
    xt = x.reshape(-1, shp[-1])
    T = xt.shape[0]
    flat_e = expert.T.reshape(-1)
    n_assign = T * TOP_K
    onehot = (flat_e[:, None] == jnp.arange(N_EXPERTS, dtype=jnp.int32)[None]).astype(jnp.int32)
    csum = jnp.cumsum(onehot, axis=0)
    rank = jnp.take_along_axis(csum, flat_e[:, None], axis=1)[:, 0] - 1
    counts = csum[-1]
    padded = (counts + MOE_TM - 1) // MOE_TM * MOE_TM
    pad_end = jnp.cumsum(padded)
    pad_start = pad_end - padded
    dest = pad_start[flat_e] + rank
    nb = -(-n_assign // MOE_TM) + N_EXPERTS
    rows = nb * MOE_TM
    row_tok = jnp.zeros(rows, jnp.int32).at[dest].set(jnp.arange(n_assign, dtype=jnp.int32) // TOP_K)
    first_row = jnp.arange(nb, dtype=jnp.int32) * MOE_TM
    block_exp = jnp.minimum(jnp.sum((pad_end[None, :] <= first_row[:, None]).astype(jnp.int32), axis=1),
                            N_EXPERTS - 1)
    block_exp = jnp.concatenate([block_exp, pad_end[-1:] // MOE_TM])
    yb = expert_ffn(xt[row_tok], block_exp, layer, wg, wu, wd)
    d2 = dest.reshape(T, TOP_K)
    out = combine_ln(yb[d2[:, 0]], yb[d2[:, 1]], gate[0][:, None], gate[1][:, None], xt, ln_g, ln_b)
    return out.reshape(shp)


def _block_diag(w):
    G, I, O = w.shape
    eye = jnp.eye(G, dtype=w.dtype)
    return (eye[:, None, :, None] * w[:, :, None, :]).reshape(G * I, G * O)


def even_layer(x, lru_h, lru_conv, wkv, shift, p, ln_g, ln_b, router):
    B, T, D = x.shape
    D_LRU = lru_h.shape[-1]
    D_RWKV = p['w0'].shape[1]
    H = D_RWKV // RWKV_HEAD
    N = RWKV_HEAD
    z = matmul(x.reshape(B * T, D), p['w_in']).reshape(B, T, -1)
    y_a, r, decay, k, v, a, g, h_T = even_pre(z, lru_conv, shift, lru_h, p)

    lanes = lambda t: jnp.transpose(t.reshape(B, T, H, N), (1, 3, 0, 2)).reshape(T, N, B * H)
    per_chain = lambda w: jnp.tile(w.reshape(H, N).T, (1, B))
    s0 = jnp.transpose(wkv, (2, 3, 0, 1)).reshape(N, N, B * H)
    y, s_T = rwkv_scan(lanes(r), lanes(decay), lanes(k), lanes(v), lanes(a),
                       per_chain(p['k_k']), per_chain(p['k_a']), per_chain(p['gn_g']),
                       per_chain(p['gn_b']), per_chain(p['r_k']), s0)
    y = jnp.transpose(y.reshape(T, N, B, H), (2, 0, 3, 1)).reshape(B * T, D_RWKV)
    S_T = jnp.transpose(s_T.reshape(N, N, B, H), (2, 3, 0, 1))

    x_new, expert, gate = mix_ln(y_a.reshape(B * T, D_LRU), y, g.reshape(B * T, D_RWKV), p['w_out'],
                                 x.reshape(B * T, D), ln_g, ln_b, *router)
    conv_state = jnp.concatenate([lru_conv, z[..., :D_LRU]], axis=1)[:, T:]
    return (x_new.reshape(B, T, D), expert, gate), (h_T[:, 0], conv_state, S_T, z[:, -1:, 2 * D_LRU:])


def odd_layer(x, cache, p, ln_g, ln_b, router):
    B, T, D = x.shape
    dq = N_HEADS_C * HEAD_DIM_C
    dkv = N_KV_C * HEAD_DIM_C
    dqi = N_IDX_HEADS * D_IDX
    z = matmul(x.reshape(B * T, D), p['w_in']).reshape(B, T, -1)
    k = z[..., dq:dq + dkv]
    v = z[..., dq + dkv:dq + 2 * dkv]
    o = dq + 2 * dkv + dqi
    ki = z[..., o:o + D_IDX]
    wi = z[..., o + D_IDX:o + D_IDX + N_IDX_HEADS] * (dqi ** -0.5)
    Tp = -(-T // LANES) * LANES
    if cache is None:
        past = 0
        n_sel = min(TOPK_MAX, T // 4)
        parts = lambda c, n: [n]
    else:
        past = cache[0].shape[1]
        n_sel = min(TOPK_MAX, (past + T) // 4)
        parts = lambda c, n: [c.reshape(B, past, -1), n]
    L = -(-(past + Tp) // LANES) * LANES
    tk = _pick(L, (512, 384, 128))
    cat = lambda c, n: jnp.concatenate(
        parts(c, n) + [jnp.zeros((B, L - past - T, n.shape[-1]), n.dtype)], axis=1).astype(BF16)
    ck, cv, cki = cache if cache is not None else (None, None, None)
    keys, vals, kidx = cat(ck, k), cat(cv, v), cat(cki, ki)
    vals_t = jnp.swapaxes(vals.reshape(B, L // tk, tk, dkv), 2, 3)
    rowpad = lambda t: jnp.pad(t, ((0, 0), (0, Tp - T), (0, 0)))
    wi_t = jnp.swapaxes(rowpad(wi), 1, 2)
    o_att = dsa_attention(rowpad(z), 0, (dq + 2 * dkv) // dqi, wi_t, keys, vals_t, kidx, past, n_sel, tk)
    x_new, expert, gate = matmul_ln(o_att[:, :T].reshape(B * T, dq), p['w_out'], x.reshape(B * T, D),
                                    ln_g, ln_b, *router)
    new = (k.reshape(B, T, N_KV_C, HEAD_DIM_C), v.reshape(B, T, N_KV_C, HEAD_DIM_C), ki)
    return (x_new.reshape(B, T, D), expert, gate), new


def run_trunk(x, even_state, odd_cache, ev, od, norm_w, moe_w):
    ln1_g, ln1_b, ln2_g, ln2_b = norm_w
    router_w, router_b, wg, wu, wd = moe_w
    router = (router_w, router_b)
    (x, expert, gate), new_even = even_layer(x, *even_state, ev, ln1_g[0], ln1_b[0], router)
    x = moe_layer(x, expert, gate, 0, wg, wu, wd, ln2_g[0], ln2_b[0])
    (x, expert, gate), new_odd = odd_layer(x, odd_cache, od, ln1_g[1], ln1_b[1], router)
    x = moe_layer(x, expert, gate, 1, wg, wu, wd, ln2_g[1], ln2_b[1])
    return x, new_even, new_odd


def kernel(x_prompt, x_sample, state_lru_h, state_lru_conv, state_rwkv_wkv, state_rwkv_shift, cache_k, cache_v, cache_kidx, ev_w_in, ev_conv_w, ev_conv_b, ev_gate_r_w, ev_gate_r_b, ev_gate_i_w, ev_gate_i_b, ev_lambda, ev_shift_mu, ev_w0, ev_w2, ev_a0, ev_a2, ev_g2, ev_k_k, ev_k_a, ev_r_k, ev_gn_g, ev_gn_b, ev_w_out, od_w_in, od_w_out, ln1_g, ln1_b, ln2_g, ln2_b, router_w, router_b, moe_w_gate, moe_w_up, moe_w_down):
    ev = dict(
        w_in=ev_w_in[0].astype(BF16), conv_w=ev_conv_w[0], conv_b=ev_conv_b[0][None],
        gate_w=jnp.concatenate([_block_diag(ev_gate_r_w[0]), _block_diag(ev_gate_i_w[0])], axis=1).astype(BF16),
        gate_b=jnp.concatenate([ev_gate_r_b[0], ev_gate_i_b[0]])[None],
        lru_c=(-LRU_C * jax.nn.softplus(-ev_lambda[0]))[None],
        shift_mu=ev_shift_mu[0][None], w0=ev_w0[0][None], a0=ev_a0[0][None],
        w2=ev_w2[0].astype(BF16), a2=ev_a2[0].astype(BF16), g2=ev_g2[0].astype(BF16),
        k_k=ev_k_k[0][None], k_a=ev_k_a[0][None], r_k=ev_r_k[0], gn_g=ev_gn_g[0], gn_b=ev_gn_b[0],
        w_out=ev_w_out[0].astype(BF16))
    n_in = od_w_in.shape[2]
    od = dict(w_in=jnp.pad(od_w_in[0], ((0, 0), (0, (-n_in) % LANES))).astype(BF16),
              w_out=od_w_out[0].astype(BF16))
    norm_w = (ln1_g, ln1_b, ln2_g, ln2_b)
    moe_w = (router_w, router_b, cast_bf16(moe_w_gate), cast_bf16(moe_w_up), cast_bf16(moe_w_down))

    B = x_prompt.shape[0]
    fresh = (jnp.zeros((B,) + state_lru_h.shape[2:], F32),
             jnp.zeros((B,) + state_lru_conv.shape[2:], F32),
             jnp.zeros((B,) + state_rwkv_wkv.shape[2:], F32),
             jnp.zeros((B,) + state_rwkv_shift.shape[2:], F32))
    y_p, pe, po = run_trunk(x_prompt, fresh, None, ev, od, norm_w, moe_w)
    y_s, se, so = run_trunk(
        x_sample, (state_lru_h[0], state_lru_conv[0], state_rwkv_wkv[0], state_rwkv_shift[0]),
        (cache_k[0], cache_v[0], cache_kidx[0]), ev, od, norm_w, moe_w)
    st = lambda t: t[None]
    return (y_p, y_s, *map(st, pe), *map(st, po), *map(st, se), *map(st, so))
```

```python
import functools
import math

import jax
import jax.numpy as jnp
from jax import lax
from jax.experimental import pallas as pl
from jax.experimental.pallas import tpu as pltpu

F32 = jnp.float32
BF16 = jnp.bfloat16

CHUNK = 64
LRU_BLOCKS = 16
CONV_W = 4
LRU_C = 8.0
RWKV_HEAD = 64
DECAY_LORA = 64
AAA_LORA = 64
GATE_LORA = 128
N_HEADS_C = 16
HEAD_DIM_C = 128
N_KV_C = 4
N_IDX_HEADS = 16
D_IDX = 64
TOPK_MAX = 256
N_EXPERTS = 16
N_GROUPS = 4
EXPERTS_PER_GROUP = N_EXPERTS // N_GROUPS
TOP_K = 2
DEPTH = 2
DN_ALPHA = (2 * DEPTH) ** 0.25
LN_EPS = 1e-5
GN_EPS = 64e-5

LANES = 128
SUBLANES = 8
VMEM_LIMIT = 56 * 1024 * 1024
MOE_TM = 256


def _params(sem):
    return pltpu.CompilerParams(dimension_semantics=sem, vmem_limit_bytes=VMEM_LIMIT)


def _mm_kernel(x_ref, w_ref, o_ref, xb_ref):
    @pl.when(pl.program_id(1) == 0)
    def _():
        xb_ref[...] = x_ref[...].astype(BF16)

    o_ref[...] = jnp.dot(xb_ref[...], w_ref[...], preferred_element_type=F32)


def _pick(n, cands):
    for c in cands:
        if n % c == 0:
            return c
    return n


def matmul(x, w):
    M, K = x.shape
    N = w.shape[1]
    tm = _pick(M, (1024, 512, 256, 128, 64, 32, 16, 8))
    tn = _pick(N, (768, 1024, 1408, 512, 384, 256, 128))
    return pl.pallas_call(
        _mm_kernel,
        grid=(M // tm, N // tn),
        in_specs=[pl.BlockSpec((tm, K), lambda i, j: (i, 0)),
                  pl.BlockSpec((K, tn), lambda i, j: (0, j))],
        out_specs=pl.BlockSpec((tm, tn), lambda i, j: (i, j)),
        out_shape=jax.ShapeDtypeStruct((M, N), F32),
        scratch_shapes=[pltpu.VMEM((tm, K), BF16)],
        compiler_params=_params(("arbitrary", "arbitrary")),
    )(x, w)


def _deepnorm(resid, y, g, b):
    h = DN_ALPHA * resid + y
    mu = jnp.mean(h, axis=-1, keepdims=True)
    hc = h - mu
    var = jnp.mean(hc * hc, axis=-1, keepdims=True)
    return hc * lax.rsqrt(var + LN_EPS) * g + b


def _mm_ln_kernel(x_ref, w_ref, r_ref, g_ref, b_ref, o_ref):
    y = jnp.dot(x_ref[...].astype(BF16), w_ref[...], preferred_element_type=F32)
    o_ref[...] = _deepnorm(r_ref[...], y, g_ref[...], b_ref[...])


def _mix_ln_kernel(ya_ref, yb_ref, gate_ref, w_ref, r_ref, g_ref, b_ref, o_ref):
    mix = jnp.concatenate([ya_ref[...], yb_ref[...] * gate_ref[...]], axis=1).astype(BF16)
    y = jnp.dot(mix, w_ref[...], preferred_element_type=F32)
    o_ref[...] = _deepnorm(r_ref[...], y, g_ref[...], b_ref[...])


def _ln_call(kern, lhs, w, resid, g, b):
    M, N = resid.shape
    tm = _pick(M, (256, 128, 64, 32, 16, 8))
    full = lambda a: pl.BlockSpec(a.shape, lambda i: (0,) * a.ndim)
    rows = lambda a: pl.BlockSpec((tm, a.shape[1]), lambda i: (i, 0))
    consts = [g.reshape(1, N), b.reshape(1, N)]
    return pl.pallas_call(
        kern,
        grid=(M // tm,),
        in_specs=[rows(a) for a in lhs] + [full(w), rows(resid)] + [full(c) for c in consts],
        out_specs=rows(resid),
        out_shape=jax.ShapeDtypeStruct((M, N), F32),
        compiler_params=_params(("arbitrary",)),
    )(*lhs, w, resid, *consts)


def matmul_ln(x, w, resid, g, b):
    return _ln_call(_mm_ln_kernel, [x], w, resid, g, b)


def mix_ln(ya, yb, gate, w, resid, g, b):
    return _ln_call(_mix_ln_kernel, [ya, yb, gate], w, resid, g, b)


def _expm1(x):
    e = jnp.exp(x)
    safe = jnp.where((e == 1.0) | (e == 0.0), 2.0, e)
    return jnp.where(e == 1.0, x, jnp.where(e == 0.0, -1.0, (e - 1.0) * x / jnp.log(safe)))


def _softplus(x):
    return jnp.maximum(x, 0.0) + jnp.log(1.0 + jnp.exp(-jnp.abs(x)))


HALO = SUBLANES


def _even_pre_kernel(ax_ref, ag_ref, zr_ref, zk_ref, zv_ref, zt_ref,
                     conv_ref, sr_ref, sk_ref, sv_ref, st_ref, h0_ref,
                     cw_ref, cb_ref, gw_ref, gb_ref, lc_ref, mu_ref,
                     w0_ref, a0_ref, w2_ref, a2_ref, g2_ref,
                     ya_ref, r_ref, w_ref, k_ref, v_ref, a_ref, g_ref, hT_ref,
                     xp_scr, pr_scr, pk_scr, pv_scr, pt_scr, a_scr, u_scr, hs_scr, h_scr):
    i = pl.program_id(1)
    tm = ax_ref.shape[1]
    C = ax_ref.shape[2]
    prev = ((xp_scr, None), (pr_scr, sr_ref), (pk_scr, sk_ref), (pv_scr, sv_ref), (pt_scr, st_ref))

    @pl.when(i == 0)
    def _():
        xp_scr[...] = jnp.zeros((HALO, C), F32)
        xp_scr[HALO - (CONV_W - 1):HALO, :] = conv_ref[0]
        for scr, ref in prev[1:]:
            scr[...] = jnp.broadcast_to(ref[0], scr.shape)
        h_scr[...] = jnp.broadcast_to(h0_ref[0], h_scr.shape)

    def earlier(x, scr, d):
        rot = pltpu.roll(x, d, axis=0)
        halo = pltpu.roll(scr[...], d, axis=0)
        sub = lax.broadcasted_iota(jnp.int32, halo.shape, 0)
        return jnp.concatenate([jnp.where(sub < d, halo, rot[:HALO]), rot[HALO:]], axis=0)

    ax = ax_ref[0]
    xa = cb_ref[...] + cw_ref[CONV_W - 1:CONV_W, :] * ax
    for d in range(1, CONV_W):
        xa = xa + cw_ref[CONV_W - 1 - d:CONV_W - d, :] * earlier(ax, xp_scr, d)
    gates = jnp.dot(xa.astype(BF16), gw_ref[...], preferred_element_type=F32) + gb_ref[...]
    log_a = lc_ref[...] * jax.nn.sigmoid(gates[:, :C])
    a_scr[...] = jnp.exp(log_a)
    u_scr[...] = jnp.sqrt(-_expm1(2.0 * log_a)) * (jax.nn.sigmoid(gates[:, C:]) * xa)

    def step(t, h):
        h = a_scr[pl.ds(t, 1), :] * h + u_scr[pl.ds(t, 1), :]
        hs_scr[pl.ds(t, 1), :] = h
        return h

    h = lax.fori_loop(0, tm, step, h_scr[0:1, :], unroll=8)
    h_scr[...] = jnp.broadcast_to(h, h_scr.shape)
    hT_ref[0] = h
    ya_ref[0] = hs_scr[...] * jax.nn.gelu(ag_ref[0], approximate=True)

    zr, zk, zv, zt = zr_ref[0], zk_ref[0], zv_ref[0], zt_ref[0]

    def shifted(zb, scr, lo, hi):
        return zb + mu_ref[:, lo:hi] * (earlier(zb, scr, 1) - zb)

    r = shifted(zr, pr_scr, 0, C)
    k = shifted(zk, pk_scr, C, 2 * C)
    v = shifted(zv, pv_scr, 2 * C, 3 * C)
    tail = shifted(zt, pt_scr, 3 * C, 3 * C + pt_scr.shape[1])
    wl = jnp.tanh(tail[:, :DECAY_LORA])
    al = tail[:, DECAY_LORA:DECAY_LORA + AAA_LORA]
    gl = jax.nn.sigmoid(tail[:, DECAY_LORA + AAA_LORA:])
    lora = lambda x, w_ref_: jnp.dot(x.astype(BF16), w_ref_[...], preferred_element_type=F32)
    w_log = -_softplus(-(w0_ref[...] + lora(wl, w2_ref))) - 0.5
    a = jax.nn.sigmoid(a0_ref[...] + lora(al, a2_ref))
    r_ref[0] = r
    w_ref[0] = jnp.exp(-jnp.exp(w_log))
    k_ref[0] = k
    v_ref[0] = v
    a_ref[0] = a
    g_ref[0] = lora(gl, g2_ref)

    for (scr, _), x in zip(prev, (ax, zr, zk, zv, zt)):
        scr[...] = x[tm - HALO:]


def even_pre(z, lru_conv, shift, lru_h, p):
    B, T, _ = z.shape
    C = lru_h.shape[-1]
    tail = DECAY_LORA + AAA_LORA + GATE_LORA
    tm = _pick(T, (128, 64, 32, 16, 8))
    zblk = lambda w, col: pl.BlockSpec((1, tm, w), lambda b, i: (b, i, col))
    sblk = lambda rows, w, col: pl.BlockSpec((1, rows, w), lambda b, i: (b, 0, col))
    full = lambda a: pl.BlockSpec(a.shape, lambda b, i: (0,) * a.ndim)
    weights = [p['conv_w'], p['conv_b'], p['gate_w'], p['gate_b'], p['lru_c'], p['shift_mu'],
               p['w0'], p['a0'], p['w2'], p['a2'], p['g2']]
    tok = jax.ShapeDtypeStruct((B, T, C), F32)
    return pl.pallas_call(
        _even_pre_kernel,
        grid=(B, T // tm),
        in_specs=[zblk(C, 0), zblk(C, 1), zblk(C, 2), zblk(C, 3), zblk(C, 4), zblk(tail, 5 * C // tail),
                  sblk(CONV_W - 1, C, 0), sblk(1, C, 0), sblk(1, C, 1), sblk(1, C, 2),
                  sblk(1, tail, 3 * C // tail), sblk(1, C, 0)] + [full(w) for w in weights],
        out_specs=[zblk(C, 0)] * 7 + [sblk(1, C, 0)],
        out_shape=[tok] * 7 + [jax.ShapeDtypeStruct((B, 1, C), F32)],
        scratch_shapes=[pltpu.VMEM((HALO, C), F32)] * 4 + [pltpu.VMEM((HALO, tail), F32)]
                       + [pltpu.VMEM((tm, C), F32)] * 3 + [pltpu.VMEM((SUBLANES, C), F32)],
        compiler_params=_params(("arbitrary", "arbitrary")),
    )(z, z, z, z, z, z, lru_conv, shift, shift, shift, shift, lru_h[:, None, :], *weights)


def _colsum(x):
    n = x.shape[0] // SUBLANES
    p = x.reshape(n, SUBLANES, x.shape[1]).sum(axis=0)
    p = p + pltpu.roll(p, 4, axis=0)
    p = p + pltpu.roll(p, 2, axis=0)
    p = p + pltpu.roll(p, 1, axis=0)
    return p


def _bcast_rows(p, n):
    return jnp.broadcast_to(p[None], (n // SUBLANES,) + p.shape).reshape(n, p.shape[1])


def _rwkv_kernel(r_ref, w_ref, k_ref, v_ref, a_ref, kk_ref, ka_ref, gng_ref, gnb_ref, rk_ref, s0_ref,
                 y_ref, sT_ref, s_scr):
    tc = pl.program_id(1)
    N = RWKV_HEAD

    @pl.when(tc == 0)
    def _():
        s_scr[...] = s0_ref[...]

    sub = lax.broadcasted_iota(jnp.int32, (SUBLANES, LANES), 0)

    def step(t, carry):
        r = r_ref[t]
        w = w_ref[t]
        k_raw = k_ref[t]
        a = a_ref[t]
        kk = k_raw * kk_ref[...]
        kk = kk / jnp.maximum(_bcast_rows(jnp.sqrt(_colsum(kk * kk)), N), 1e-12)
        nkk = -kk
        kka = kk * a
        k = k_raw * (1.0 + (a - 1.0) * ka_ref[...])

        def vgroup(g, c2):
            yacc = jnp.zeros((SUBLANES, LANES), F32)
            for j in range(SUBLANES):
                vi = g * SUBLANES + j
                S = s_scr[vi]
                sa = _bcast_rows(_colsum(S * nkk), N)
                vb = jnp.broadcast_to(v_ref[t, pl.ds(vi, 1), :], (N, LANES))
                S = S * w + sa * kka + vb * k
                s_scr[vi] = S
                yacc = jnp.where(sub == j, _colsum(S * r), yacc)
            y_ref[t, pl.ds(pl.multiple_of(g * SUBLANES, SUBLANES), SUBLANES), :] = yacc
            return c2

        lax.fori_loop(0, N // SUBLANES, vgroup, 0, unroll=True)

        y = y_ref[t]
        d = y - _bcast_rows(_colsum(y), N) * (1.0 / N)
        var = _bcast_rows(_colsum(d * d), N) * (1.0 / N)
        bonus = _bcast_rows(_colsum(r * k * rk_ref[...]), N)
        y_ref[t] = d * lax.rsqrt(var + GN_EPS) * gng_ref[...] + gnb_ref[...] + bonus * v_ref[t]
        return carry

    lax.fori_loop(0, r_ref.shape[0], step, 0)

    @pl.when(tc == pl.num_programs(1) - 1)
    def _():
        sT_ref[...] = s_scr[...]


def rwkv_scan(r, w, k, v, a, k_k, k_a, gn_g, gn_b, r_k, s0):
    T, N, P = r.shape
    tc = _pick(T, (32, 16, 8))
    blk = pl.BlockSpec((tc, N, LANES), lambda p, t: (t, 0, p))
    par = pl.BlockSpec((N, LANES), lambda p, t: (0, p))
    st = pl.BlockSpec((N, N, LANES), lambda p, t: (0, 0, p))
    return pl.pallas_call(
        _rwkv_kernel,
        grid=(P // LANES, T // tc),
        in_specs=[blk] * 5 + [par] * 5 + [st],
        out_specs=[blk, st],
        out_shape=[jax.ShapeDtypeStruct((T, N, P), F32), jax.ShapeDtypeStruct((N, N, P), F32)],
        scratch_shapes=[pltpu.VMEM((N, N, LANES), F32)],
        compiler_params=_params(("arbitrary", "arbitrary")),
    )(r, w, k, v, a, k_k, k_a, gn_g, gn_b, r_k, s0)


MASKED = -1e30


KEY_NEG_INF = -2139095041
QK_SPLIT = 2


def _tree_sum(parts):
    while len(parts) > 1:
        parts = [parts[i] + parts[i + 1] for i in range(0, len(parts) - 1, 2)] + parts[len(parts) & ~1:]
    return parts[0]


def _dsa_kernel(q_ref, qi_ref, wi_ref, k_ref, vt_ref, ki_ref, o_ref,
                key_scr, m_scr, l_scr, acc_scr, *, past, n_sel, scale, tk):
    qb = pl.program_id(1)
    QB = q_ref.shape[1]
    L = k_ref.shape[1]
    NP = N_HEADS_C // 2
    lane_chunk = lax.broadcasted_iota(jnp.int32, (1, QB), 1) // CHUNK
    limit = past + (qb * (QB // CHUNK) + lane_chunk + 1) * CHUNK
    n_tiles = (past + (qb + 1) * QB + tk - 1) // tk
    u = 2 if (tk < 512 and (L // tk) % 2 == 0) else 1
    n_count = (n_tiles + u - 1) // u
    row = lax.broadcasted_iota(jnp.int32, (tk, QB), 0)
    tile = lambda kt: pl.ds(pl.multiple_of(kt * tk, tk), tk)

    qi = qi_ref[0].astype(BF16)
    qi_pair = [jnp.concatenate([qi[:, (2 * p) * D_IDX:(2 * p + 1) * D_IDX],
                                qi[:, (2 * p + 1) * D_IDX:(2 * p + 2) * D_IDX]], axis=0)
               for p in range(N_IDX_HEADS // 2)]
    wi = wi_ref[0]

    def score_tile(kt, c):
        @pl.when(kt < n_tiles)
        def _():
            kidx = ki_ref[0, tile(kt), :]
            terms = []
            for p in range(N_IDX_HEADS // 2):
                s = lax.dot_general(kidx, qi_pair[p], (((1,), (1,)), ((), ())),
                                    preferred_element_type=F32)
                for e in range(2):
                    h = 2 * p + e
                    terms.append(wi[h:h + 1, :] * jnp.maximum(s[:, e * QB:(e + 1) * QB], 0.0))
            isc = terms[0]
            for t in terms[1:]:
                isc = isc + t
            ok = row + kt * tk < limit
            bits = lax.bitcast_convert_type(jnp.where(ok, isc + 0.0, -jnp.inf), jnp.int32)
            key_scr[tile(kt), :] = bits ^ ((bits >> 31) & jnp.int32(0x7FFFFFFF))

        @pl.when(kt >= n_tiles)
        def _():
            key_scr[tile(kt), :] = jnp.full((tk, QB), KEY_NEG_INF, jnp.int32)

        return c

    lax.fori_loop(0, n_count * u, score_tile, 0)

    int_min = jnp.int32(-2 ** 31)
    ct = tk * u

    def search(i, t):
        cand = t | lax.shift_left(jnp.int32(1), 31 - i)
        cs = cand ^ int_min

        def count_tile(kt, acc):
            keys = key_scr[pl.ds(pl.multiple_of(kt * ct, ct), ct), :]
            hit = jnp.where(keys >= cs, 1.0, 0.0)
            return acc + _tree_sum([hit[r * SUBLANES:(r + 1) * SUBLANES] for r in range(ct // SUBLANES)])

        acc = lax.fori_loop(0, n_count, count_tile, jnp.zeros((SUBLANES, QB), F32))
        cnt = jnp.sum(acc, axis=0, keepdims=True)
        return jnp.where(cnt >= n_sel, cand, t)

    thr = lax.fori_loop(0, 32, search, jnp.zeros((1, QB), jnp.int32)) ^ int_min

    G = N_HEADS_C // N_KV_C
    dh = HEAD_DIM_C
    q = q_ref[0]
    q_pair = [(jnp.concatenate([q[:, (2 * p) * dh:(2 * p + 1) * dh],
                                q[:, (2 * p + 1) * dh:(2 * p + 2) * dh]], axis=0) * scale).astype(BF16)
              for p in range(NP)]
    m_scr[...] = jnp.full(m_scr.shape, MASKED, F32)
    l_scr[...] = jnp.zeros(l_scr.shape, F32)
    acc_scr[...] = jnp.zeros(acc_scr.shape, F32)

    def att_tile(kt, c):
        sel = (key_scr[tile(kt), :] >= thr) & (row + kt * tk < limit)
        b = jnp.where(sel, 0.0, MASKED)
        b2 = jnp.concatenate([b, b], axis=1)
        def logits(p):
            j = (2 * p) // G
            kj = k_ref[0, tile(kt), j * dh:(j + 1) * dh]
            part = tk // QK_SPLIT
            return jnp.concatenate(
                [lax.dot_general(kj[r * part:(r + 1) * part], q_pair[p], (((1,), (1,)), ((), ())),
                                 preferred_element_type=F32) for r in range(QK_SPLIT)], axis=0)

        ahead = 2
        lg = {p: logits(p) for p in range(ahead)}
        for p in range(NP):
            j = (2 * p) // G
            x = lg.pop(p) + b2
            m = m_scr[p:p + 1, :]
            m_new = jnp.maximum(m, jnp.max(x, axis=0, keepdims=True))
            alpha = jnp.exp2(m - m_new)
            pr = jnp.exp2(x - m_new)
            l_scr[p:p + 1, :] = alpha * l_scr[p:p + 1, :] + jnp.sum(pr, axis=0, keepdims=True)
            m_scr[p:p + 1, :] = m_new
            vt = vt_ref[0, kt, j * dh:(j + 1) * dh, :]
            pv = jnp.dot(vt, pr.astype(BF16), preferred_element_type=F32)
            if p + ahead < NP:
                lg[p + ahead] = logits(p + ahead)
            acc_scr[p] = acc_scr[p] * alpha + pv
        return c

    lax.fori_loop(0, n_tiles, att_tile, 0)
    for p in range(NP):
        o2 = acc_scr[p] / l_scr[p:p + 1, :]
        for s in range(2):
            o_ref[0, :, (2 * p + s) * dh:(2 * p + s + 1) * dh] = o2[:, s * QB:(s + 1) * QB].T


def dsa_attention(z, q_col, qi_col, wi_t, keys, vals_t, kidx, past, n_sel, tk):
    B, T, _ = z.shape
    L = keys.shape[1]
    QB = LANES
    dq = N_HEADS_C * HEAD_DIM_C
    dqi = N_IDX_HEADS * D_IDX
    kern = functools.partial(_dsa_kernel, past=past, n_sel=n_sel,
                             scale=HEAD_DIM_C ** -0.5 * math.log2(math.e), tk=tk)
    return pl.pallas_call(
        kern,
        grid=(B, T // QB),
        in_specs=[pl.BlockSpec((1, QB, dq), lambda b, c: (b, c, q_col)),
                  pl.BlockSpec((1, QB, dqi), lambda b, c: (b, c, qi_col)),
                  pl.BlockSpec((1, N_IDX_HEADS, QB), lambda b, c: (b, 0, c)),
                  pl.BlockSpec((1, L, keys.shape[2]), lambda b, c: (b, 0, 0)),
                  pl.BlockSpec((1,) + vals_t.shape[1:], lambda b, c: (b, 0, 0, 0)),
                  pl.BlockSpec((1, L, kidx.shape[2]), lambda b, c: (b, 0, 0))],
        out_specs=pl.BlockSpec((1, QB, dq), lambda b, c: (b, c, 0)),
        out_shape=jax.ShapeDtypeStruct((B, T, dq), F32),
        scratch_shapes=[pltpu.VMEM((L, QB), jnp.int32),
                        pltpu.VMEM((N_HEADS_C // 2, 2 * QB), F32),
                        pltpu.VMEM((N_HEADS_C // 2, 2 * QB), F32),
                        pltpu.VMEM((N_HEADS_C // 2, HEAD_DIM_C, 2 * QB), F32)],
        compiler_params=_params(("arbitrary", "arbitrary")),
    )(z, z, wi_t, keys, vals_t, kidx)


def _route_tokens(x, w, bias):
    logits = lax.dot_general(w, x, (((1,), (1,)), ((), ())),
                             preferred_element_type=F32, precision=lax.Precision.HIGHEST)
    scores = jax.nn.sigmoid(logits)
    biased = scores + bias
    tm = scores.shape[1]
    P = EXPERTS_PER_GROUP

    def ranks(rows):
        out = []
        for i in range(len(rows)):
            rk = jnp.zeros((1, tm), jnp.int32)
            for j in range(len(rows)):
                if j == i:
                    continue
                beat = (rows[j] > rows[i]) | ((rows[j] == rows[i]) & (j < i))
                rk = rk + beat.astype(jnp.int32)
            out.append(rk)
        return out

    b = [biased[i:i + 1] for i in range(N_EXPERTS)]
    s = [scores[i:i + 1] for i in range(N_EXPERTS)]
    grp_rank, grp_score = [], []
    for g in range(N_GROUPS):
        rows = b[g * P:(g + 1) * P]
        rk = ranks(rows)
        grp_rank.append(rk)
        top1 = sum(jnp.where(rk[i] == 0, rows[i], 0.0) for i in range(P))
        top2 = sum(jnp.where(rk[i] == 1, rows[i], 0.0) for i in range(P))
        grp_score.append(top1 + top2)
    gr = ranks(grp_score)
    e1 = jnp.zeros((1, tm), jnp.int32)
    e2 = jnp.zeros((1, tm), jnp.int32)
    w1 = jnp.zeros((1, tm), F32)
    w2 = jnp.zeros((1, tm), F32)
    for g in range(N_GROUPS):
        best = gr[g] == 0
        for i in range(P):
            is1 = best & (grp_rank[g][i] == 0)
            is2 = best & (grp_rank[g][i] == 1)
            e1 = jnp.where(is1, g * P + i, e1)
            e2 = jnp.where(is2, g * P + i, e2)
            w1 = jnp.where(is1, s[g * P + i], w1)
            w2 = jnp.where(is2, s[g * P + i], w2)
    tot = w1 + w2
    return jnp.concatenate([e1, e2], axis=0), jnp.concatenate([w1 / tot, w2 / tot], axis=0)


def _router_kernel(x_ref, w_ref, b_ref, e_ref, g_ref):
    e_ref[...], g_ref[...] = _route_tokens(x_ref[...], w_ref[...], b_ref[...])


def route(xt, router_w, router_b):
    T, D = xt.shape
    tm = _pick(T, (512, 256, 128))
    return pl.pallas_call(
        _router_kernel,
        grid=(T // tm,),
        in_specs=[pl.BlockSpec((tm, D), lambda i: (i, 0)),
                  pl.BlockSpec((N_EXPERTS, D), lambda i: (0, 0)),
                  pl.BlockSpec((N_EXPERTS, 1), lambda i: (0, 0))],
        out_specs=[pl.BlockSpec((TOP_K, tm), lambda i: (0, i)),
                   pl.BlockSpec((TOP_K, tm), lambda i: (0, i))],
        out_shape=[jax.ShapeDtypeStruct((TOP_K, T), jnp.int32),
                   jax.ShapeDtypeStruct((TOP_K, T), F32)],
        compiler_params=_params(("arbitrary",)),
    )(xt, router_w.T, router_b.reshape(N_EXPERTS, 1))


def _ffn_kernel(be_ref, x_ref, wg_ref, wu_ref, wd_ref, o_ref):
    i = pl.program_id(0)
    used = be_ref[pl.num_programs(0)]

    @pl.when(i < used)
    def _():
        x = x_ref[...].astype(BF16)
        a = jnp.dot(x, wg_ref[0, 0], preferred_element_type=F32)
        u = jnp.dot(x, wu_ref[0, 0], preferred_element_type=F32)
        h = (a * jax.nn.sigmoid(a)) * u
        o_ref[...] = jnp.dot(h.astype(BF16), wd_ref[0, 0], preferred_element_type=F32)

    @pl.when(i >= used)
    def _():
        o_ref[...] = jnp.zeros(o_ref.shape, F32)


def _cast_kernel(x_ref, o_ref):
    o_ref[...] = x_ref[...].astype(o_ref.dtype)


def cast_bf16(w):
    n = w.shape[-1]
    w2 = w.reshape(-1, n)
    rows = w2.shape[0]
    tr = _pick(rows, (1024 * 1024 // n, 512, 256, 128, 64, 32, 16))
    out = pl.pallas_call(
        _cast_kernel,
        grid=(rows // tr,),
        in_specs=[pl.BlockSpec((tr, n), lambda i: (i, 0))],
        out_specs=pl.BlockSpec((tr, n), lambda i: (i, 0)),
        out_shape=jax.ShapeDtypeStruct(w2.shape, BF16),
        compiler_params=_params(("arbitrary",)),
    )(w2)
    return out.reshape(w.shape)


def expert_ffn(xs, block_exp, layer, wg, wu, wd):
    R, D = xs.shape
    F = wg.shape[3]
    nb = R // MOE_TM
    grid_spec = pltpu.PrefetchScalarGridSpec(
        num_scalar_prefetch=1,
        grid=(nb,),
        in_specs=[pl.BlockSpec((MOE_TM, D), lambda i, be: (i, 0)),
                  pl.BlockSpec((1, 1, D, F), lambda i, be: (layer, be[i], 0, 0)),
                  pl.BlockSpec((1, 1, D, F), lambda i, be: (layer, be[i], 0, 0)),
                  pl.BlockSpec((1, 1, F, D), lambda i, be: (layer, be[i], 0, 0))],
        out_specs=pl.BlockSpec((MOE_TM, D), lambda i, be: (i, 0)),
    )
    return pl.pallas_call(
        _ffn_kernel,
        grid_spec=grid_spec,
        out_shape=jax.ShapeDtypeStruct((R, D), F32),
        compiler_params=_params(("arbitrary",)),
    )(block_exp, xs, wg, wu, wd)


def _combine_ln_kernel(ya_ref, yb_ref, ga_ref, gb_ref, r_ref, g_ref, b_ref, o_ref):
    y = ya_ref[...] * ga_ref[...] + yb_ref[...] * gb_ref[...]
    o_ref[...] = _deepnorm(r_ref[...], y, g_ref[...], b_ref[...])


def combine_ln(ya, yb, ga, gb, resid, g, b):
    M, N = ya.shape
    tm = _pick(M, (256, 128, 64, 32, 16, 8))
    blk = pl.BlockSpec((tm, N), lambda i: (i, 0))
    col = pl.BlockSpec((tm, 1), lambda i: (i, 0))
    row = pl.BlockSpec((1, N), lambda i: (0, 0))
    return pl.pallas_call(
        _combine_ln_kernel,
        grid=(M // tm,),
        in_specs=[blk, blk, col, col, blk, row, row],
        out_specs=blk,
        out_shape=jax.ShapeDtypeStruct((M, N), F32),
        compiler_params=_params(("arbitrary",)),
    )(ya, yb, ga, gb, resid, g.reshape(1, N), b.reshape(1, N))


def moe_layer(x, router_w, router_b, layer, wg, wu, wd, ln_g, ln_b):
    shp = x.shape
    xt = x.reshape(-1, shp[-1])
    T = xt.shape[0]
    expert, gate = route(xt, router_w, router_b)
    flat_e = expert.T.reshape(-1)
    n_assign = T * TOP_K
    onehot = (flat_e[:, None] == jnp.arange(N_EXPERTS, dtype=jnp.int32)[None]).astype(jnp.int32)
    csum = jnp.cumsum(onehot, axis=0)
    rank = jnp.take_along_axis(csum, flat_e[:, None], axis=1)[:, 0] - 1
    counts = csum[-1]
    padded = (counts + MOE_TM - 1) // MOE_TM * MOE_TM
    pad_end = jnp.cumsum(padded)
    pad_start = pad_end - padded
    dest = pad_start[flat_e] + rank
    nb = -(-n_assign // MOE_TM) + N_EXPERTS
    rows = nb * MOE_TM
    row_tok = jnp.zeros(rows, jnp.int32).at[dest].set(jnp.arange(n_assign, dtype=jnp.int32) // TOP_K)
    first_row = jnp.arange(nb, dtype=jnp.int32) * MOE_TM
    block_exp = jnp.minimum(jnp.sum((pad_end[None, :] <= first_row[:, None]).astype(jnp.int32), axis=1),
                            N_EXPERTS - 1)
    block_exp = jnp.concatenate([block_exp, pad_end[-1:] // MOE_TM])
    yb = expert_ffn(xt[row_tok], block_exp, layer, wg, wu, wd)
    d2 = dest.reshape(T, TOP_K)
    out = combine_ln(yb[d2[:, 0]], yb[d2[:, 1]], gate[0][:, None], gate[1][:, None], xt, ln_g, ln_b)
    return out.reshape(shp)


def _block_diag(w):
    G, I, O = w.shape
    eye = jnp.eye(G, dtype=w.dtype)
    return (eye[:, None, :, None] * w[:, :, None, :]).reshape(G * I, G * O)


def even_layer(x, lru_h, lru_conv, wkv, shift, p, ln_g, ln_b):
    B, T, D = x.shape
    D_LRU = lru_h.shape[-1]
    D_RWKV = p['w0'].shape[1]
    H = D_RWKV // RWKV_HEAD
    N = RWKV_HEAD
    z = matmul(x.reshape(B * T, D), p['w_in']).reshape(B, T, -1)
    y_a, r, decay, k, v, a, g, h_T = even_pre(z, lru_conv, shift, lru_h, p)

    lanes = lambda t: jnp.transpose(t.reshape(B, T, H, N), (1, 3, 0, 2)).reshape(T, N, B * H)
    per_chain = lambda w: jnp.tile(w.reshape(H, N).T, (1, B))
    s0 = jnp.transpose(wkv, (2, 3, 0, 1)).reshape(N, N, B * H)
    y, s_T = rwkv_scan(lanes(r), lanes(decay), lanes(k), lanes(v), lanes(a),
                       per_chain(p['k_k']), per_chain(p['k_a']), per_chain(p['gn_g']),
                       per_chain(p['gn_b']), per_chain(p['r_k']), s0)
    y = jnp.transpose(y.reshape(T, N, B, H), (2, 0, 3, 1)).reshape(B * T, D_RWKV)
    S_T = jnp.transpose(s_T.reshape(N, N, B, H), (2, 3, 0, 1))

    x_new = mix_ln(y_a.reshape(B * T, D_LRU), y, g.reshape(B * T, D_RWKV), p['w_out'],
                   x.reshape(B * T, D), ln_g, ln_b)
    conv_state = jnp.concatenate([lru_conv, z[..., :D_LRU]], axis=1)[:, T:]
    return x_new.reshape(B, T, D), (h_T[:, 0], conv_state, S_T, z[:, -1:, 2 * D_LRU:])


def odd_layer(x, cache, p, ln_g, ln_b):
    B, T, D = x.shape
    dq = N_HEADS_C * HEAD_DIM_C
    dkv = N_KV_C * HEAD_DIM_C
    dqi = N_IDX_HEADS * D_IDX
    z = matmul(x.reshape(B * T, D), p['w_in']).reshape(B, T, -1)
    k = z[..., dq:dq + dkv]
    v = z[..., dq + dkv:dq + 2 * dkv]
    o = dq + 2 * dkv + dqi
    ki = z[..., o:o + D_IDX]
    wi = z[..., o + D_IDX:o + D_IDX + N_IDX_HEADS] * (dqi ** -0.5)
    Tp = -(-T // LANES) * LANES
    if cache is None:
        past = 0
        n_sel = min(TOPK_MAX, T // 4)
        parts = lambda c, n: [n]
    else:
        past = cache[0].shape[1]
        n_sel = min(TOPK_MAX, (past + T) // 4)
        parts = lambda c, n: [c.reshape(B, past, -1), n]
    L = -(-(past + Tp) // LANES) * LANES
    tk = _pick(L, (512, 384, 128))
    cat = lambda c, n: jnp.concatenate(
        parts(c, n) + [jnp.zeros((B, L - past - T, n.shape[-1]), n.dtype)], axis=1).astype(BF16)
    ck, cv, cki = cache if cache is not None else (None, None, None)
    keys, vals, kidx = cat(ck, k), cat(cv, v), cat(cki, ki)
    vals_t = jnp.swapaxes(vals.reshape(B, L // tk, tk, dkv), 2, 3)
    rowpad = lambda t: jnp.pad(t, ((0, 0), (0, Tp - T), (0, 0)))
    wi_t = jnp.swapaxes(rowpad(wi), 1, 2)
    o_att = dsa_attention(rowpad(z), 0, (dq + 2 * dkv) // dqi, wi_t, keys, vals_t, kidx, past, n_sel, tk)
    x_new = matmul_ln(o_att[:, :T].reshape(B * T, dq), p['w_out'], x.reshape(B * T, D), ln_g, ln_b)
    new = (k.reshape(B, T, N_KV_C, HEAD_DIM_C), v.reshape(B, T, N_KV_C, HEAD_DIM_C), ki)
    return x_new.reshape(B, T, D), new


def run_trunk(x, even_state, odd_cache, ev, od, norm_w, moe_w):
    ln1_g, ln1_b, ln2_g, ln2_b = norm_w
    router_w, router_b, wg, wu, wd = moe_w
    x, new_even = even_layer(x, *even_state, ev, ln1_g[0], ln1_b[0])
    x = moe_layer(x, router_w, router_b, 0, wg, wu, wd, ln2_g[0], ln2_b[0])
    x, new_odd = odd_layer(x, odd_cache, od, ln1_g[1], ln1_b[1])
    x = moe_layer(x, router_w, router_b, 1, wg, wu, wd, ln2_g[1], ln2_b[1])
    return x, new_even, new_odd


def kernel(x_prompt, x_sample, state_lru_h, state_lru_conv, state_rwkv_wkv, state_rwkv_shift, cache_k, cache_v, cache_kidx, ev_w_in, ev_conv_w, ev_conv_b, ev_gate_r_w, ev_gate_r_b, ev_gate_i_w, ev_gate_i_b, ev_lambda, ev_shift_mu, ev_w0, ev_w2, ev_a0, ev_a2, ev_g2, ev_k_k, ev_k_a, ev_r_k, ev_gn_g, ev_gn_b, ev_w_out, od_w_in, od_w_out, ln1_g, ln1_b, ln2_g, ln2_b, router_w, router_b, moe_w_gate, moe_w_up, moe_w_down):
    ev = dict(
        w_in=ev_w_in[0].astype(BF16), conv_w=ev_conv_w[0], conv_b=ev_conv_b[0][None],
        gate_w=jnp.concatenate([_block_diag(ev_gate_r_w[0]), _block_diag(ev_gate_i_w[0])], axis=1).astype(BF16),
        gate_b=jnp.concatenate([ev_gate_r_b[0], ev_gate_i_b[0]])[None],
        lru_c=(-LRU_C * jax.nn.softplus(-ev_lambda[0]))[None],
        shift_mu=ev_shift_mu[0][None], w0=ev_w0[0][None], a0=ev_a0[0][None],
        w2=ev_w2[0].astype(BF16), a2=ev_a2[0].astype(BF16), g2=ev_g2[0].astype(BF16),
        k_k=ev_k_k[0][None], k_a=ev_k_a[0][None], r_k=ev_r_k[0], gn_g=ev_gn_g[0], gn_b=ev_gn_b[0],
        w_out=ev_w_out[0].astype(BF16))
    n_in = od_w_in.shape[2]
    od = dict(w_in=jnp.pad(od_w_in[0], ((0, 0), (0, (-n_in) % LANES))).astype(BF16),
              w_out=od_w_out[0].astype(BF16))
    norm_w = (ln1_g, ln1_b, ln2_g, ln2_b)
    moe_w = (router_w, router_b, cast_bf16(moe_w_gate), cast_bf16(moe_w_up), cast_bf16(moe_w_down))

    B = x_prompt.shape[0]
    fresh = (jnp.zeros((B,) + state_lru_h.shape[2:], F32),
             jnp.zeros((B,) + state_lru_conv.shape[2:], F32),
             jnp.zeros((B,) + state_rwkv_wkv.shape[2:], F32),
             jnp.zeros((B,) + state_rwkv_shift.shape[2:], F32))
    y_p, pe, po = run_trunk(x_prompt, fresh, None, ev, od, norm_w, moe_w)
    y_s, se, so = run_trunk(
        x_sample, (state_lru_h[0], state_lru_conv[0], state_rwkv_wkv[0], state_rwkv_shift[0]),
        (cache_k[0], cache_v[0], cache_kidx[0]), ev, od, norm_w, moe_w)
    st = lambda t: t[None]
    return (y_p, y_s, *map(st, pe), *map(st, po), *map(st, se), *map(st, so))
```

```python
import functools
import math

import jax
import jax.numpy as jnp
from jax import lax
from jax.experimental import pallas as pl
from jax.experimental.pallas import tpu as pltpu

F32 = jnp.float32
BF16 = jnp.bfloat16

CHUNK = 64
LRU_BLOCKS = 16
CONV_W = 4
LRU_C = 8.0
RWKV_HEAD = 64
DECAY_LORA = 64
AAA_LORA = 64
GATE_LORA = 128
N_HEADS_C = 16
HEAD_DIM_C = 128
N_KV_C = 4
N_IDX_HEADS = 16
D_IDX = 64
TOPK_MAX = 256
N_EXPERTS = 16
N_GROUPS = 4
EXPERTS_PER_GROUP = N_EXPERTS // N_GROUPS
TOP_K = 2
DEPTH = 2
DN_ALPHA = (2 * DEPTH) ** 0.25
LN_EPS = 1e-5
GN_EPS = 64e-5

LANES = 128
SUBLANES = 8
VMEM_LIMIT = 56 * 1024 * 1024
MOE_TM = 256


def _params(sem):
    return pltpu.CompilerParams(dimension_semantics=sem, vmem_limit_bytes=VMEM_LIMIT)


def _mm_kernel(x_ref, w_ref, o_ref, xb_ref):
    @pl.when(pl.program_id(1) == 0)
    def _():
        xb_ref[...] = x_ref[...].astype(BF16)

    o_ref[...] = jnp.dot(xb_ref[...], w_ref[...], preferred_element_type=F32)


def _pick(n, cands):
    for c in cands:
        if n % c == 0:
            return c
    return n


def matmul(x, w):
    M, K = x.shape
    N = w.shape[1]
    tm = _pick(M, (1024, 512, 256, 128, 64, 32, 16, 8))
    tn = _pick(N, (1792, 1408, 1024, 768, 512, 384, 256, 128))
    return pl.pallas_call(
        _mm_kernel,
        grid=(M // tm, N // tn),
        in_specs=[pl.BlockSpec((tm, K), lambda i, j: (i, 0)),
                  pl.BlockSpec((K, tn), lambda i, j: (0, j))],
        out_specs=pl.BlockSpec((tm, tn), lambda i, j: (i, j)),
        out_shape=jax.ShapeDtypeStruct((M, N), F32),
        scratch_shapes=[pltpu.VMEM((tm, K), BF16)],
        compiler_params=_params(("arbitrary", "arbitrary")),
    )(x, w)


def _deepnorm(resid, y, g, b):
    h = DN_ALPHA * resid + y
    mu = jnp.mean(h, axis=-1, keepdims=True)
    hc = h - mu
    var = jnp.mean(hc * hc, axis=-1, keepdims=True)
    return hc * lax.rsqrt(var + LN_EPS) * g + b


def _mm_ln_kernel(x_ref, w_ref, r_ref, g_ref, b_ref, o_ref):
    y = jnp.dot(x_ref[...].astype(BF16), w_ref[...], preferred_element_type=F32)
    o_ref[...] = _deepnorm(r_ref[...], y, g_ref[...], b_ref[...])


def _mix_ln_kernel(ya_ref, yb_ref, gate_ref, w_ref, r_ref, g_ref, b_ref, o_ref):
    mix = jnp.concatenate([ya_ref[...], yb_ref[...] * gate_ref[...]], axis=1).astype(BF16)
    y = jnp.dot(mix, w_ref[...], preferred_element_type=F32)
    o_ref[...] = _deepnorm(r_ref[...], y, g_ref[...], b_ref[...])


def _ln_call(kern, lhs, w, resid, g, b):
    M, N = resid.shape
    tm = _pick(M, (512, 256, 128, 64, 32, 16, 8))
    full = lambda a: pl.BlockSpec(a.shape, lambda i: (0,) * a.ndim)
    rows = lambda a: pl.BlockSpec((tm, a.shape[1]), lambda i: (i, 0))
    consts = [g.reshape(1, N), b.reshape(1, N)]
    return pl.pallas_call(
        kern,
        grid=(M // tm,),
        in_specs=[rows(a) for a in lhs] + [full(w), rows(resid)] + [full(c) for c in consts],
        out_specs=rows(resid),
        out_shape=jax.ShapeDtypeStruct((M, N), F32),
        compiler_params=_params(("arbitrary",)),
    )(*lhs, w, resid, *consts)


def matmul_ln(x, w, resid, g, b):
    return _ln_call(_mm_ln_kernel, [x], w, resid, g, b)


def mix_ln(ya, yb, gate, w, resid, g, b):
    return _ln_call(_mix_ln_kernel, [ya, yb, gate], w, resid, g, b)


def _expm1(x):
    e = jnp.exp(x)
    safe = jnp.where((e == 1.0) | (e == 0.0), 2.0, e)
    return jnp.where(e == 1.0, x, jnp.where(e == 0.0, -1.0, (e - 1.0) * x / jnp.log(safe)))


def _softplus(x):
    return jnp.maximum(x, 0.0) + jnp.log(1.0 + jnp.exp(-jnp.abs(x)))


HALO = SUBLANES


def _even_pre_kernel(ax_ref, ag_ref, zr_ref, zk_ref, zv_ref, zt_ref,
                     conv_ref, sr_ref, sk_ref, sv_ref, st_ref, h0_ref,
                     cw_ref, cb_ref, gw_ref, gb_ref, lc_ref, mu_ref,
                     w0_ref, a0_ref, w2_ref, a2_ref, g2_ref,
                     ya_ref, r_ref, w_ref, k_ref, v_ref, a_ref, g_ref, hT_ref,
                     xp_scr, pr_scr, pk_scr, pv_scr, pt_scr, a_scr, u_scr, hs_scr, h_scr):
    i = pl.program_id(1)
    tm = ax_ref.shape[1]
    C = ax_ref.shape[2]
    prev = ((xp_scr, None), (pr_scr, sr_ref), (pk_scr, sk_ref), (pv_scr, sv_ref), (pt_scr, st_ref))

    @pl.when(i == 0)
    def _():
        xp_scr[...] = jnp.zeros((HALO, C), F32)
        xp_scr[HALO - (CONV_W - 1):HALO, :] = conv_ref[0]
        for scr, ref in prev[1:]:
            scr[...] = jnp.broadcast_to(ref[0], scr.shape)
        h_scr[...] = jnp.broadcast_to(h0_ref[0], h_scr.shape)

    def earlier(x, scr, d):
        rot = pltpu.roll(x, d, axis=0)
        halo = pltpu.roll(scr[...], d, axis=0)
        sub = lax.broadcasted_iota(jnp.int32, halo.shape, 0)
        return jnp.concatenate([jnp.where(sub < d, halo, rot[:HALO]), rot[HALO:]], axis=0)

    ax = ax_ref[0]
    xa = cb_ref[...] + cw_ref[CONV_W - 1:CONV_W, :] * ax
    for d in range(1, CONV_W):
        xa = xa + cw_ref[CONV_W - 1 - d:CONV_W - d, :] * earlier(ax, xp_scr, d)
    gates = jnp.dot(xa.astype(BF16), gw_ref[...], preferred_element_type=F32) + gb_ref[...]
    log_a = lc_ref[...] * jax.nn.sigmoid(gates[:, :C])
    a_scr[...] = jnp.exp(log_a)
    u_scr[...] = jnp.sqrt(-_expm1(2.0 * log_a)) * (jax.nn.sigmoid(gates[:, C:]) * xa)

    def step(t, h):
        h = a_scr[pl.ds(t, 1), :] * h + u_scr[pl.ds(t, 1), :]
        hs_scr[pl.ds(t, 1), :] = h
        return h

    h = lax.fori_loop(0, tm, step, h_scr[0:1, :], unroll=8)
    h_scr[...] = jnp.broadcast_to(h, h_scr.shape)
    hT_ref[0] = h
    ya_ref[0] = hs_scr[...] * jax.nn.gelu(ag_ref[0], approximate=True)

    zr, zk, zv, zt = zr_ref[0], zk_ref[0], zv_ref[0], zt_ref[0]

    def shifted(zb, scr, lo, hi):
        return zb + mu_ref[:, lo:hi] * (earlier(zb, scr, 1) - zb)

    r = shifted(zr, pr_scr, 0, C)
    k = shifted(zk, pk_scr, C, 2 * C)
    v = shifted(zv, pv_scr, 2 * C, 3 * C)
    tail = shifted(zt, pt_scr, 3 * C, 3 * C + pt_scr.shape[1])
    wl = jnp.tanh(tail[:, :DECAY_LORA])
    al = tail[:, DECAY_LORA:DECAY_LORA + AAA_LORA]
    gl = jax.nn.sigmoid(tail[:, DECAY_LORA + AAA_LORA:])
    lora = lambda x, w_ref_: jnp.dot(x.astype(BF16), w_ref_[...], preferred_element_type=F32)
    w_log = -_softplus(-(w0_ref[...] + lora(wl, w2_ref))) - 0.5
    a = jax.nn.sigmoid(a0_ref[...] + lora(al, a2_ref))
    r_ref[0] = r
    w_ref[0] = jnp.exp(-jnp.exp(w_log))
    k_ref[0] = k
    v_ref[0] = v
    a_ref[0] = a
    g_ref[0] = lora(gl, g2_ref)

    for (scr, _), x in zip(prev, (ax, zr, zk, zv, zt)):
        scr[...] = x[tm - HALO:]


def even_pre(z, lru_conv, shift, lru_h, p):
    B, T, _ = z.shape
    C = lru_h.shape[-1]
    tail = DECAY_LORA + AAA_LORA + GATE_LORA
    tm = _pick(T, (128, 64, 32, 16, 8))
    zblk = lambda w, col: pl.BlockSpec((1, tm, w), lambda b, i: (b, i, col))
    sblk = lambda rows, w, col: pl.BlockSpec((1, rows, w), lambda b, i: (b, 0, col))
    full = lambda a: pl.BlockSpec(a.shape, lambda b, i: (0,) * a.ndim)
    weights = [p['conv_w'], p['conv_b'], p['gate_w'], p['gate_b'], p['lru_c'], p['shift_mu'],
               p['w0'], p['a0'], p['w2'], p['a2'], p['g2']]
    tok = jax.ShapeDtypeStruct((B, T, C), F32)
    return pl.pallas_call(
        _even_pre_kernel,
        grid=(B, T // tm),
        in_specs=[zblk(C, 0), zblk(C, 1), zblk(C, 2), zblk(C, 3), zblk(C, 4), zblk(tail, 5 * C // tail),
                  sblk(CONV_W - 1, C, 0), sblk(1, C, 0), sblk(1, C, 1), sblk(1, C, 2),
                  sblk(1, tail, 3 * C // tail), sblk(1, C, 0)] + [full(w) for w in weights],
        out_specs=[zblk(C, 0)] * 7 + [sblk(1, C, 0)],
        out_shape=[tok] * 7 + [jax.ShapeDtypeStruct((B, 1, C), F32)],
        scratch_shapes=[pltpu.VMEM((HALO, C), F32)] * 4 + [pltpu.VMEM((HALO, tail), F32)]
                       + [pltpu.VMEM((tm, C), F32)] * 3 + [pltpu.VMEM((SUBLANES, C), F32)],
        compiler_params=_params(("arbitrary", "arbitrary")),
    )(z, z, z, z, z, z, lru_conv, shift, shift, shift, shift, lru_h[:, None, :], *weights)


def _colsum(x):
    n = x.shape[0] // SUBLANES
    p = x.reshape(n, SUBLANES, x.shape[1]).sum(axis=0)
    p = p + pltpu.roll(p, 4, axis=0)
    p = p + pltpu.roll(p, 2, axis=0)
    p = p + pltpu.roll(p, 1, axis=0)
    return p


def _bcast_rows(p, n):
    return jnp.broadcast_to(p[None], (n // SUBLANES,) + p.shape).reshape(n, p.shape[1])


def _rwkv_kernel(r_ref, w_ref, k_ref, v_ref, a_ref, kk_ref, ka_ref, gng_ref, gnb_ref, rk_ref, s0_ref,
                 y_ref, sT_ref, s_scr):
    tc = pl.program_id(1)
    N = RWKV_HEAD

    @pl.when(tc == 0)
    def _():
        s_scr[...] = s0_ref[...]

    sub = lax.broadcasted_iota(jnp.int32, (SUBLANES, LANES), 0)

    def step(t, carry):
        r = r_ref[t]
        w = w_ref[t]
        k_raw = k_ref[t]
        a = a_ref[t]
        kk = k_raw * kk_ref[...]
        kk = kk / jnp.maximum(_bcast_rows(jnp.sqrt(_colsum(kk * kk)), N), 1e-12)
        nkk = -kk
        kka = kk * a
        k = k_raw * (1.0 + (a - 1.0) * ka_ref[...])

        def vgroup(g, c2):
            yacc = jnp.zeros((SUBLANES, LANES), F32)
            for j in range(SUBLANES):
                vi = g * SUBLANES + j
                S = s_scr[vi]
                sa = _bcast_rows(_colsum(S * nkk), N)
                vb = jnp.broadcast_to(v_ref[t, pl.ds(vi, 1), :], (N, LANES))
                S = S * w + sa * kka + vb * k
                s_scr[vi] = S
                yacc = jnp.where(sub == j, _colsum(S * r), yacc)
            y_ref[t, pl.ds(pl.multiple_of(g * SUBLANES, SUBLANES), SUBLANES), :] = yacc
            return c2

        lax.fori_loop(0, N // SUBLANES, vgroup, 0, unroll=True)

        y = y_ref[t]
        d = y - _bcast_rows(_colsum(y), N) * (1.0 / N)
        var = _bcast_rows(_colsum(d * d), N) * (1.0 / N)
        bonus = _bcast_rows(_colsum(r * k * rk_ref[...]), N)
        y_ref[t] = d * lax.rsqrt(var + GN_EPS) * gng_ref[...] + gnb_ref[...] + bonus * v_ref[t]
        return carry

    lax.fori_loop(0, r_ref.shape[0], step, 0)

    @pl.when(tc == pl.num_programs(1) - 1)
    def _():
        sT_ref[...] = s_scr[...]


def rwkv_scan(r, w, k, v, a, k_k, k_a, gn_g, gn_b, r_k, s0):
    T, N, P = r.shape
    tc = _pick(T, (32, 16, 8))
    blk = pl.BlockSpec((tc, N, LANES), lambda p, t: (t, 0, p))
    par = pl.BlockSpec((N, LANES), lambda p, t: (0, p))
    st = pl.BlockSpec((N, N, LANES), lambda p, t: (0, 0, p))
    return pl.pallas_call(
        _rwkv_kernel,
        grid=(P // LANES, T // tc),
        in_specs=[blk] * 5 + [par] * 5 + [st],
        out_specs=[blk, st],
        out_shape=[jax.ShapeDtypeStruct((T, N, P), F32), jax.ShapeDtypeStruct((N, N, P), F32)],
        scratch_shapes=[pltpu.VMEM((N, N, LANES), F32)],
        compiler_params=_params(("arbitrary", "arbitrary")),
    )(r, w, k, v, a, k_k, k_a, gn_g, gn_b, r_k, s0)


MASKED = -1e30


KEY_NEG_INF = -2139095041
QK_SPLIT = 2


def _tree_sum(parts):
    while len(parts) > 1:
        parts = [parts[i] + parts[i + 1] for i in range(0, len(parts) - 1, 2)] + parts[len(parts) & ~1:]
    return parts[0]


def _dsa_kernel(q_ref, qi_ref, wi_ref, k_ref, vt_ref, ki_ref, o_ref,
                key_scr, m_scr, l_scr, acc_scr, *, past, n_sel, scale, tk):
    qb = pl.program_id(1)
    QB = q_ref.shape[1]
    L = k_ref.shape[1]
    NP = N_HEADS_C // 2
    lane_chunk = lax.broadcasted_iota(jnp.int32, (1, QB), 1) // CHUNK
    limit = past + (qb * (QB // CHUNK) + lane_chunk + 1) * CHUNK
    n_tiles = (past + (qb + 1) * QB + tk - 1) // tk
    u = 2 if (tk < 512 and (L // tk) % 2 == 0) else 1
    n_count = (n_tiles + u - 1) // u
    row = lax.broadcasted_iota(jnp.int32, (tk, QB), 0)
    tile = lambda kt: pl.ds(pl.multiple_of(kt * tk, tk), tk)

    qi = qi_ref[0].astype(BF16)
    qi_pair = [jnp.concatenate([qi[:, (2 * p) * D_IDX:(2 * p + 1) * D_IDX],
                                qi[:, (2 * p + 1) * D_IDX:(2 * p + 2) * D_IDX]], axis=0)
               for p in range(N_IDX_HEADS // 2)]
    wi = wi_ref[0]

    def score_tile(kt, c):
        @pl.when(kt < n_tiles)
        def _():
            kidx = ki_ref[0, tile(kt), :]
            terms = []
            for p in range(N_IDX_HEADS // 2):
                s = lax.dot_general(kidx, qi_pair[p], (((1,), (1,)), ((), ())),
                                    preferred_element_type=F32)
                for e in range(2):
                    h = 2 * p + e
                    terms.append(wi[h:h + 1, :] * jnp.maximum(s[:, e * QB:(e + 1) * QB], 0.0))
            isc = terms[0]
            for t in terms[1:]:
                isc = isc + t
            ok = row + kt * tk < limit
            bits = lax.bitcast_convert_type(jnp.where(ok, isc + 0.0, -jnp.inf), jnp.int32)
            key_scr[tile(kt), :] = bits ^ ((bits >> 31) & jnp.int32(0x7FFFFFFF))

        @pl.when(kt >= n_tiles)
        def _():
            key_scr[tile(kt), :] = jnp.full((tk, QB), KEY_NEG_INF, jnp.int32)

        return c

    lax.fori_loop(0, n_count * u, score_tile, 0)

    int_min = jnp.int32(-2 ** 31)
    ct = tk * u

    def search(i, t):
        cand = t | lax.shift_left(jnp.int32(1), 31 - i)
        cs = cand ^ int_min

        def count_tile(kt, acc):
            keys = key_scr[pl.ds(pl.multiple_of(kt * ct, ct), ct), :]
            hit = jnp.where(keys >= cs, 1.0, 0.0)
            return acc + _tree_sum([hit[r * SUBLANES:(r + 1) * SUBLANES] for r in range(ct // SUBLANES)])

        acc = lax.fori_loop(0, n_count, count_tile, jnp.zeros((SUBLANES, QB), F32))
        cnt = jnp.sum(acc, axis=0, keepdims=True)
        return jnp.where(cnt >= n_sel, cand, t)

    thr = lax.fori_loop(0, 32, search, jnp.zeros((1, QB), jnp.int32)) ^ int_min

    G = N_HEADS_C // N_KV_C
    dh = HEAD_DIM_C
    q = q_ref[0]
    q_pair = [(jnp.concatenate([q[:, (2 * p) * dh:(2 * p + 1) * dh],
                                q[:, (2 * p + 1) * dh:(2 * p + 2) * dh]], axis=0) * scale).astype(BF16)
              for p in range(NP)]
    m_scr[...] = jnp.full(m_scr.shape, MASKED, F32)
    l_scr[...] = jnp.zeros(l_scr.shape, F32)
    acc_scr[...] = jnp.zeros(acc_scr.shape, F32)

    def att_tile(kt, c):
        sel = (key_scr[tile(kt), :] >= thr) & (row + kt * tk < limit)
        b = jnp.where(sel, 0.0, MASKED)
        b2 = jnp.concatenate([b, b], axis=1)
        def logits(p):
            j = (2 * p) // G
            kj = k_ref[0, tile(kt), j * dh:(j + 1) * dh]
            part = tk // QK_SPLIT
            return jnp.concatenate(
                [lax.dot_general(kj[r * part:(r + 1) * part], q_pair[p], (((1,), (1,)), ((), ())),
                                 preferred_element_type=F32) for r in range(QK_SPLIT)], axis=0)

        ahead = 2
        lg = {p: logits(p) for p in range(ahead)}
        for p in range(NP):
            j = (2 * p) // G
            x = lg.pop(p) + b2
            m = m_scr[p:p + 1, :]
            m_new = jnp.maximum(m, jnp.max(x, axis=0, keepdims=True))
            alpha = jnp.exp2(m - m_new)
            pr = jnp.exp2(x - m_new)
            l_scr[p:p + 1, :] = alpha * l_scr[p:p + 1, :] + jnp.sum(pr, axis=0, keepdims=True)
            m_scr[p:p + 1, :] = m_new
            vt = vt_ref[0, kt, j * dh:(j + 1) * dh, :]
            pv = jnp.dot(vt, pr.astype(BF16), preferred_element_type=F32)
            if p + ahead < NP:
                lg[p + ahead] = logits(p + ahead)
            acc_scr[p] = acc_scr[p] * alpha + pv
        return c

    lax.fori_loop(0, n_tiles, att_tile, 0)
    for p in range(NP):
        o2 = acc_scr[p] / l_scr[p:p + 1, :]
        for s in range(2):
            o_ref[0, :, (2 * p + s) * dh:(2 * p + s + 1) * dh] = o2[:, s * QB:(s + 1) * QB].T


def dsa_attention(z, q_col, qi_col, wi_t, keys, vals_t, kidx, past, n_sel, tk):
    B, T, _ = z.shape
    L = keys.shape[1]
    QB = LANES
    dq = N_HEADS_C * HEAD_DIM_C
    dqi = N_IDX_HEADS * D_IDX
    kern = functools.partial(_dsa_kernel, past=past, n_sel=n_sel,
                             scale=HEAD_DIM_C ** -0.5 * math.log2(math.e), tk=tk)
    return pl.pallas_call(
        kern,
        grid=(B, T // QB),
        in_specs=[pl.BlockSpec((1, QB, dq), lambda b, c: (b, c, q_col)),
                  pl.BlockSpec((1, QB, dqi), lambda b, c: (b, c, qi_col)),
                  pl.BlockSpec((1, N_IDX_HEADS, QB), lambda b, c: (b, 0, c)),
                  pl.BlockSpec((1, L, keys.shape[2]), lambda b, c: (b, 0, 0)),
                  pl.BlockSpec((1,) + vals_t.shape[1:], lambda b, c: (b, 0, 0, 0)),
                  pl.BlockSpec((1, L, kidx.shape[2]), lambda b, c: (b, 0, 0))],
        out_specs=pl.BlockSpec((1, QB, dq), lambda b, c: (b, c, 0)),
        out_shape=jax.ShapeDtypeStruct((B, T, dq), F32),
        scratch_shapes=[pltpu.VMEM((L, QB), jnp.int32),
                        pltpu.VMEM((N_HEADS_C // 2, 2 * QB), F32),
                        pltpu.VMEM((N_HEADS_C // 2, 2 * QB), F32),
                        pltpu.VMEM((N_HEADS_C // 2, HEAD_DIM_C, 2 * QB), F32)],
        compiler_params=_params(("arbitrary", "arbitrary")),
    )(z, z, wi_t, keys, vals_t, kidx)


def _route_tokens(x, w, bias):
    logits = lax.dot_general(w, x, (((1,), (1,)), ((), ())),
                             preferred_element_type=F32, precision=lax.Precision.HIGHEST)
    scores = jax.nn.sigmoid(logits)
    biased = scores + bias
    tm = scores.shape[1]
    P = EXPERTS_PER_GROUP

    def ranks(rows):
        out = []
        for i in range(len(rows)):
            rk = jnp.zeros((1, tm), jnp.int32)
            for j in range(len(rows)):
                if j == i:
                    continue
                beat = (rows[j] > rows[i]) | ((rows[j] == rows[i]) & (j < i))
                rk = rk + beat.astype(jnp.int32)
            out.append(rk)
        return out

    b = [biased[i:i + 1] for i in range(N_EXPERTS)]
    s = [scores[i:i + 1] for i in range(N_EXPERTS)]
    grp_rank, grp_score = [], []
    for g in range(N_GROUPS):
        rows = b[g * P:(g + 1) * P]
        rk = ranks(rows)
        grp_rank.append(rk)
        top1 = sum(jnp.where(rk[i] == 0, rows[i], 0.0) for i in range(P))
        top2 = sum(jnp.where(rk[i] == 1, rows[i], 0.0) for i in range(P))
        grp_score.append(top1 + top2)
    gr = ranks(grp_score)
    e1 = jnp.zeros((1, tm), jnp.int32)
    e2 = jnp.zeros((1, tm), jnp.int32)
    w1 = jnp.zeros((1, tm), F32)
    w2 = jnp.zeros((1, tm), F32)
    for g in range(N_GROUPS):
        best = gr[g] == 0
        for i in range(P):
            is1 = best & (grp_rank[g][i] == 0)
            is2 = best & (grp_rank[g][i] == 1)
            e1 = jnp.where(is1, g * P + i, e1)
            e2 = jnp.where(is2, g * P + i, e2)
            w1 = jnp.where(is1, s[g * P + i], w1)
            w2 = jnp.where(is2, s[g * P + i], w2)
    tot = w1 + w2
    return jnp.concatenate([e1, e2], axis=0), jnp.concatenate([w1 / tot, w2 / tot], axis=0)


def _router_kernel(x_ref, w_ref, b_ref, e_ref, g_ref):
    e_ref[...], g_ref[...] = _route_tokens(x_ref[...], w_ref[...], b_ref[...])


def route(xt, router_w, router_b):
    T, D = xt.shape
    tm = _pick(T, (512, 256, 128))
    return pl.pallas_call(
        _router_kernel,
        grid=(T // tm,),
        in_specs=[pl.BlockSpec((tm, D), lambda i: (i, 0)),
                  pl.BlockSpec((N_EXPERTS, D), lambda i: (0, 0)),
                  pl.BlockSpec((N_EXPERTS, 1), lambda i: (0, 0))],
        out_specs=[pl.BlockSpec((TOP_K, tm), lambda i: (0, i)),
                   pl.BlockSpec((TOP_K, tm), lambda i: (0, i))],
        out_shape=[jax.ShapeDtypeStruct((TOP_K, T), jnp.int32),
                   jax.ShapeDtypeStruct((TOP_K, T), F32)],
        compiler_params=_params(("arbitrary",)),
    )(xt, router_w.T, router_b.reshape(N_EXPERTS, 1))


def _ffn_kernel(be_ref, x_ref, wg_ref, wu_ref, wd_ref, o_ref):
    i = pl.program_id(0)
    used = be_ref[pl.num_programs(0)]

    @pl.when(i < used)
    def _():
        x = x_ref[...].astype(BF16)
        a = jnp.dot(x, wg_ref[0, 0], preferred_element_type=F32)
        u = jnp.dot(x, wu_ref[0, 0], preferred_element_type=F32)
        h = (a * jax.nn.sigmoid(a)) * u
        o_ref[...] = jnp.dot(h.astype(BF16), wd_ref[0, 0], preferred_element_type=F32)

    @pl.when(i >= used)
    def _():
        o_ref[...] = jnp.zeros(o_ref.shape, F32)


def _cast_kernel(x_ref, o_ref):
    o_ref[...] = x_ref[...].astype(o_ref.dtype)


def cast_bf16(w):
    n = w.shape[-1]
    w2 = w.reshape(-1, n)
    rows = w2.shape[0]
    tr = _pick(rows, (1024 * 1024 // n, 512, 256, 128, 64, 32, 16))
    out = pl.pallas_call(
        _cast_kernel,
        grid=(rows // tr,),
        in_specs=[pl.BlockSpec((tr, n), lambda i: (i, 0))],
        out_specs=pl.BlockSpec((tr, n), lambda i: (i, 0)),
        out_shape=jax.ShapeDtypeStruct(w2.shape, BF16),
        compiler_params=_params(("arbitrary",)),
    )(w2)
    return out.reshape(w.shape)


def expert_ffn(xs, block_exp, layer, wg, wu, wd):
    R, D = xs.shape
    F = wg.shape[3]
    nb = R // MOE_TM
    grid_spec = pltpu.PrefetchScalarGridSpec(
        num_scalar_prefetch=1,
        grid=(nb,),
        in_specs=[pl.BlockSpec((MOE_TM, D), lambda i, be: (i, 0)),
                  pl.BlockSpec((1, 1, D, F), lambda i, be: (layer, be[i], 0, 0)),
                  pl.BlockSpec((1, 1, D, F), lambda i, be: (layer, be[i], 0, 0)),
                  pl.BlockSpec((1, 1, F, D), lambda i, be: (layer, be[i], 0, 0))],
        out_specs=pl.BlockSpec((MOE_TM, D), lambda i, be: (i, 0)),
    )
    return pl.pallas_call(
        _ffn_kernel,
        grid_spec=grid_spec,
        out_shape=jax.ShapeDtypeStruct((R, D), F32),
        compiler_params=_params(("arbitrary",)),
    )(block_exp, xs, wg, wu, wd)


def _combine_ln_kernel(ya_ref, yb_ref, ga_ref, gb_ref, r_ref, g_ref, b_ref, o_ref):
    y = ya_ref[...] * ga_ref[...] + yb_ref[...] * gb_ref[...]
    o_ref[...] = _deepnorm(r_ref[...], y, g_ref[...], b_ref[...])


def combine_ln(ya, yb, ga, gb, resid, g, b):
    M, N = ya.shape
    tm = _pick(M, (256, 128, 64, 32, 16, 8))
    blk = pl.BlockSpec((tm, N), lambda i: (i, 0))
    col = pl.BlockSpec((tm, 1), lambda i: (i, 0))
    row = pl.BlockSpec((1, N), lambda i: (0, 0))
    return pl.pallas_call(
        _combine_ln_kernel,
        grid=(M // tm,),
        in_specs=[blk, blk, col, col, blk, row, row],
        out_specs=blk,
        out_shape=jax.ShapeDtypeStruct((M, N), F32),
        compiler_params=_params(("arbitrary",)),
    )(ya, yb, ga, gb, resid, g.reshape(1, N), b.reshape(1, N))


def moe_layer(x, router_w, router_b, layer, wg, wu, wd, ln_g, ln_b):
    shp = x.shape
    xt = x.reshape(-1, shp[-1])
    T = xt.shape[0]
    expert, gate = route(xt, router_w, router_b)
    flat_e = expert.T.reshape(-1)
    n_assign = T * TOP_K
    onehot = (flat_e[:, None] == jnp.arange(N_EXPERTS, dtype=jnp.int32)[None]).astype(jnp.int32)
    csum = jnp.cumsum(onehot, axis=0)
    rank = jnp.take_along_axis(csum, flat_e[:, None], axis=1)[:, 0] - 1
    counts = csum[-1]
    padded = (counts + MOE_TM - 1) // MOE_TM * MOE_TM
    pad_end = jnp.cumsum(padded)
    pad_start = pad_end - padded
    dest = pad_start[flat_e] + rank
    nb = -(-n_assign // MOE_TM) + N_EXPERTS
    rows = nb * MOE_TM
    row_tok = jnp.zeros(rows, jnp.int32).at[dest].set(jnp.arange(n_assign, dtype=jnp.int32) // TOP_K)
    first_row = jnp.arange(nb, dtype=jnp.int32) * MOE_TM
    block_exp = jnp.minimum(jnp.sum((pad_end[None, :] <= first_row[:, None]).astype(jnp.int32), axis=1),
                            N_EXPERTS - 1)
    block_exp = jnp.concatenate([block_exp, pad_end[-1:] // MOE_TM])
    yb = expert_ffn(xt[row_tok], block_exp, layer, wg, wu, wd)
    d2 = dest.reshape(T, TOP_K)
    out = combine_ln(yb[d2[:, 0]], yb[d2[:, 1]], gate[0][:, None], gate[1][:, None], xt, ln_g, ln_b)
    return out.reshape(shp)


def _block_diag(w):
    G, I, O = w.shape
    eye = jnp.eye(G, dtype=w.dtype)
    return (eye[:, None, :, None] * w[:, :, None, :]).reshape(G * I, G * O)


def even_layer(x, lru_h, lru_conv, wkv, shift, p, ln_g, ln_b):
    B, T, D = x.shape
    D_LRU = lru_h.shape[-1]
    D_RWKV = p['w0'].shape[1]
    H = D_RWKV // RWKV_HEAD
    N = RWKV_HEAD
    z = matmul(x.reshape(B * T, D), p['w_in']).reshape(B, T, -1)
    y_a, r, decay, k, v, a, g, h_T = even_pre(z, lru_conv, shift, lru_h, p)

    lanes = lambda t: jnp.transpose(t.reshape(B, T, H, N), (1, 3, 0, 2)).reshape(T, N, B * H)
    per_chain = lambda w: jnp.tile(w.reshape(H, N).T, (1, B))
    s0 = jnp.transpose(wkv, (2, 3, 0, 1)).reshape(N, N, B * H)
    y, s_T = rwkv_scan(lanes(r), lanes(decay), lanes(k), lanes(v), lanes(a),
                       per_chain(p['k_k']), per_chain(p['k_a']), per_chain(p['gn_g']),
                       per_chain(p['gn_b']), per_chain(p['r_k']), s0)
    y = jnp.transpose(y.reshape(T, N, B, H), (2, 0, 3, 1)).reshape(B * T, D_RWKV)
    S_T = jnp.transpose(s_T.reshape(N, N, B, H), (2, 3, 0, 1))

    x_new = mix_ln(y_a.reshape(B * T, D_LRU), y, g.reshape(B * T, D_RWKV), p['w_out'],
                   x.reshape(B * T, D), ln_g, ln_b)
    conv_state = jnp.concatenate([lru_conv, z[..., :D_LRU]], axis=1)[:, T:]
    return x_new.reshape(B, T, D), (h_T[:, 0], conv_state, S_T, z[:, -1:, 2 * D_LRU:])


def odd_layer(x, cache, p, ln_g, ln_b):
    B, T, D = x.shape
    dq = N_HEADS_C * HEAD_DIM_C
    dkv = N_KV_C * HEAD_DIM_C
    dqi = N_IDX_HEADS * D_IDX
    z = matmul(x.reshape(B * T, D), p['w_in']).reshape(B, T, -1)
    k = z[..., dq:dq + dkv]
    v = z[..., dq + dkv:dq + 2 * dkv]
    o = dq + 2 * dkv + dqi
    ki = z[..., o:o + D_IDX]
    wi = z[..., o + D_IDX:o + D_IDX + N_IDX_HEADS] * (dqi ** -0.5)
    Tp = -(-T // LANES) * LANES
    if cache is None:
        past = 0
        n_sel = min(TOPK_MAX, T // 4)
        parts = lambda c, n: [n]
    else:
        past = cache[0].shape[1]
        n_sel = min(TOPK_MAX, (past + T) // 4)
        parts = lambda c, n: [c.reshape(B, past, -1), n]
    L = -(-(past + Tp) // LANES) * LANES
    tk = _pick(L, (512, 384, 128))
    cat = lambda c, n: jnp.concatenate(
        parts(c, n) + [jnp.zeros((B, L - past - T, n.shape[-1]), n.dtype)], axis=1).astype(BF16)
    ck, cv, cki = cache if cache is not None else (None, None, None)
    keys, vals, kidx = cat(ck, k), cat(cv, v), cat(cki, ki)
    vals_t = jnp.swapaxes(vals.reshape(B, L // tk, tk, dkv), 2, 3)
    rowpad = lambda t: jnp.pad(t, ((0, 0), (0, Tp - T), (0, 0)))
    wi_t = jnp.swapaxes(rowpad(wi), 1, 2)
    o_att = dsa_attention(rowpad(z), 0, (dq + 2 * dkv) // dqi, wi_t, keys, vals_t, kidx, past, n_sel, tk)
    x_new = matmul_ln(o_att[:, :T].reshape(B * T, dq), p['w_out'], x.reshape(B * T, D), ln_g, ln_b)
    new = (k.reshape(B, T, N_KV_C, HEAD_DIM_C), v.reshape(B, T, N_KV_C, HEAD_DIM_C), ki)
    return x_new.reshape(B, T, D), new


def run_trunk(x, even_state, odd_cache, ev, od, norm_w, moe_w):
    ln1_g, ln1_b, ln2_g, ln2_b = norm_w
    router_w, router_b, wg, wu, wd = moe_w
    x, new_even = even_layer(x, *even_state, ev, ln1_g[0], ln1_b[0])
    x = moe_layer(x, router_w, router_b, 0, wg, wu, wd, ln2_g[0], ln2_b[0])
    x, new_odd = odd_layer(x, odd_cache, od, ln1_g[1], ln1_b[1])
    x = moe_layer(x, router_w, router_b, 1, wg, wu, wd, ln2_g[1], ln2_b[1])
    return x, new_even, new_odd


def kernel(x_prompt, x_sample, state_lru_h, state_lru_conv, state_rwkv_wkv, state_rwkv_shift, cache_k, cache_v, cache_kidx, ev_w_in, ev_conv_w, ev_conv_b, ev_gate_r_w, ev_gate_r_b, ev_gate_i_w, ev_gate_i_b, ev_lambda, ev_shift_mu, ev_w0, ev_w2, ev_a0, ev_a2, ev_g2, ev_k_k, ev_k_a, ev_r_k, ev_gn_g, ev_gn_b, ev_w_out, od_w_in, od_w_out, ln1_g, ln1_b, ln2_g, ln2_b, router_w, router_b, moe_w_gate, moe_w_up, moe_w_down):
    ev = dict(
        w_in=ev_w_in[0].astype(BF16), conv_w=ev_conv_w[0], conv_b=ev_conv_b[0][None],
        gate_w=jnp.concatenate([_block_diag(ev_gate_r_w[0]), _block_diag(ev_gate_i_w[0])], axis=1).astype(BF16),
        gate_b=jnp.concatenate([ev_gate_r_b[0], ev_gate_i_b[0]])[None],
        lru_c=(-LRU_C * jax.nn.softplus(-ev_lambda[0]))[None],
        shift_mu=ev_shift_mu[0][None], w0=ev_w0[0][None], a0=ev_a0[0][None],
        w2=ev_w2[0].astype(BF16), a2=ev_a2[0].astype(BF16), g2=ev_g2[0].astype(BF16),
        k_k=ev_k_k[0][None], k_a=ev_k_a[0][None], r_k=ev_r_k[0], gn_g=ev_gn_g[0], gn_b=ev_gn_b[0],
        w_out=ev_w_out[0].astype(BF16))
    n_in = od_w_in.shape[2]
    od = dict(w_in=jnp.pad(od_w_in[0], ((0, 0), (0, (-n_in) % LANES))).astype(BF16),
              w_out=od_w_out[0].astype(BF16))
    norm_w = (ln1_g, ln1_b, ln2_g, ln2_b)
    moe_w = (router_w, router_b, cast_bf16(moe_w_gate), cast_bf16(moe_w_up), cast_bf16(moe_w_down))

    B = x_prompt.shape[0]
    fresh = (jnp.zeros((B,) + state_lru_h.shape[2:], F32),
             jnp.zeros((B,) + state_lru_conv.shape[2:], F32),
             jnp.zeros((B,) + state_rwkv_wkv.shape[2:], F32),
             jnp.zeros((B,) + state_rwkv_shift.shape[2:], F32))
    y_p, pe, po = run_trunk(x_prompt, fresh, None, ev, od, norm_w, moe_w)
    y_s, se, so = run_trunk(
        x_sample, (state_lru_h[0], state_lru_conv[0], state_rwkv_wkv[0], state_rwkv_shift[0]),
        (cache_k[0], cache_v[0], cache_kidx[0]), ev, od, norm_w, moe_w)
    st = lambda t: t[None]
    return (y_p, y_s, *map(st, pe), *map(st, po), *map(st, se), *map(st, so))
```

```python
import functools
import math

import jax
import jax.numpy as jnp
from jax import lax
from jax.experimental import pallas as pl
from jax.experimental.pallas import tpu as pltpu

F32 = jnp.float32
BF16 = jnp.bfloat16

CHUNK = 64
LRU_BLOCKS = 16
CONV_W = 4
LRU_C = 8.0
RWKV_HEAD = 64
DECAY_LORA = 64
AAA_LORA = 64
GATE_LORA = 128
N_HEADS_C = 16
HEAD_DIM_C = 128
N_KV_C = 4
N_IDX_HEADS = 16
D_IDX = 64
TOPK_MAX = 256
N_EXPERTS = 16
N_GROUPS = 4
EXPERTS_PER_GROUP = N_EXPERTS // N_GROUPS
TOP_K = 2
DEPTH = 2
DN_ALPHA = (2 * DEPTH) ** 0.25
LN_EPS = 1e-5
GN_EPS = 64e-5

LANES = 128
SUBLANES = 8
VMEM_LIMIT = 56 * 1024 * 1024
MOE_TM = 256


def _params(sem):
    return pltpu.CompilerParams(dimension_semantics=sem, vmem_limit_bytes=VMEM_LIMIT)


def _mm_kernel(x_ref, w_ref, o_ref, xb_ref):
    @pl.when(pl.program_id(1) == 0)
    def _():
        xb_ref[...] = x_ref[...].astype(BF16)

    o_ref[...] = jnp.dot(xb_ref[...], w_ref[...], preferred_element_type=F32)


def _pick(n, cands):
    for c in cands:
        if n % c == 0:
            return c
    return n


def matmul(x, w):
    M, K = x.shape
    N = w.shape[1]
    tm = _pick(M, (1024, 512, 256, 128, 64, 32, 16, 8))
    tn = _pick(N, (1792, 1408, 1024, 768, 512, 384, 256, 128))
    return pl.pallas_call(
        _mm_kernel,
        grid=(M // tm, N // tn),
        in_specs=[pl.BlockSpec((tm, K), lambda i, j: (i, 0)),
                  pl.BlockSpec((K, tn), lambda i, j: (0, j))],
        out_specs=pl.BlockSpec((tm, tn), lambda i, j: (i, j)),
        out_shape=jax.ShapeDtypeStruct((M, N), F32),
        scratch_shapes=[pltpu.VMEM((tm, K), BF16)],
        compiler_params=_params(("arbitrary", "arbitrary")),
    )(x, w)


def _deepnorm(resid, y, g, b):
    h = DN_ALPHA * resid + y
    mu = jnp.mean(h, axis=-1, keepdims=True)
    hc = h - mu
    var = jnp.mean(hc * hc, axis=-1, keepdims=True)
    return hc * lax.rsqrt(var + LN_EPS) * g + b


def _mm_ln_kernel(x_ref, w_ref, r_ref, g_ref, b_ref, o_ref):
    y = jnp.dot(x_ref[...].astype(BF16), w_ref[...], preferred_element_type=F32)
    o_ref[...] = _deepnorm(r_ref[...], y, g_ref[...], b_ref[...])


def _mix_ln_kernel(ya_ref, yb_ref, gate_ref, w_ref, r_ref, g_ref, b_ref, o_ref):
    mix = jnp.concatenate([ya_ref[...], yb_ref[...] * gate_ref[...]], axis=1).astype(BF16)
    y = jnp.dot(mix, w_ref[...], preferred_element_type=F32)
    o_ref[...] = _deepnorm(r_ref[...], y, g_ref[...], b_ref[...])


def _ln_call(kern, lhs, w, resid, g, b):
    M, N = resid.shape
    tm = _pick(M, (512, 256, 128, 64, 32, 16, 8))
    full = lambda a: pl.BlockSpec(a.shape, lambda i: (0,) * a.ndim)
    rows = lambda a: pl.BlockSpec((tm, a.shape[1]), lambda i: (i, 0))
    consts = [g.reshape(1, N), b.reshape(1, N)]
    return pl.pallas_call(
        kern,
        grid=(M // tm,),
        in_specs=[rows(a) for a in lhs] + [full(w), rows(resid)] + [full(c) for c in consts],
        out_specs=rows(resid),
        out_shape=jax.ShapeDtypeStruct((M, N), F32),
        compiler_params=_params(("arbitrary",)),
    )(*lhs, w, resid, *consts)


def matmul_ln(x, w, resid, g, b):
    return _ln_call(_mm_ln_kernel, [x], w, resid, g, b)


def mix_ln(ya, yb, gate, w, resid, g, b):
    return _ln_call(_mix_ln_kernel, [ya, yb, gate], w, resid, g, b)


def _expm1(x):
    e = jnp.exp(x)
    safe = jnp.where((e == 1.0) | (e == 0.0), 2.0, e)
    return jnp.where(e == 1.0, x, jnp.where(e == 0.0, -1.0, (e - 1.0) * x / jnp.log(safe)))


def _softplus(x):
    return jnp.maximum(x, 0.0) + jnp.log(1.0 + jnp.exp(-jnp.abs(x)))


HALO = SUBLANES


def _even_pre_kernel(ax_ref, ag_ref, zr_ref, zk_ref, zv_ref, zt_ref,
                     conv_ref, sr_ref, sk_ref, sv_ref, st_ref, h0_ref,
                     cw_ref, cb_ref, gw_ref, gb_ref, lc_ref, mu_ref,
                     w0_ref, a0_ref, w2_ref, a2_ref, g2_ref,
                     ya_ref, r_ref, w_ref, k_ref, v_ref, a_ref, g_ref, hT_ref,
                     xp_scr, pr_scr, pk_scr, pv_scr, pt_scr, a_scr, u_scr, hs_scr, h_scr):
    i = pl.program_id(1)
    tm = ax_ref.shape[1]
    C = ax_ref.shape[2]
    prev = ((xp_scr, None), (pr_scr, sr_ref), (pk_scr, sk_ref), (pv_scr, sv_ref), (pt_scr, st_ref))

    @pl.when(i == 0)
    def _():
        xp_scr[...] = jnp.zeros((HALO, C), F32)
        xp_scr[HALO - (CONV_W - 1):HALO, :] = conv_ref[0]
        for scr, ref in prev[1:]:
            scr[...] = jnp.broadcast_to(ref[0], scr.shape)
        h_scr[...] = jnp.broadcast_to(h0_ref[0], h_scr.shape)

    def earlier(x, scr, d):
        rot = pltpu.roll(x, d, axis=0)
        halo = pltpu.roll(scr[...], d, axis=0)
        sub = lax.broadcasted_iota(jnp.int32, halo.shape, 0)
        return jnp.concatenate([jnp.where(sub < d, halo, rot[:HALO]), rot[HALO:]], axis=0)

    ax = ax_ref[0]
    xa = cb_ref[...] + cw_ref[CONV_W - 1:CONV_W, :] * ax
    for d in range(1, CONV_W):
        xa = xa + cw_ref[CONV_W - 1 - d:CONV_W - d, :] * earlier(ax, xp_scr, d)
    gates = jnp.dot(xa.astype(BF16), gw_ref[...], preferred_element_type=F32) + gb_ref[...]
    log_a = lc_ref[...] * jax.nn.sigmoid(gates[:, :C])
    a_scr[...] = jnp.exp(log_a)
    u_scr[...] = jnp.sqrt(-_expm1(2.0 * log_a)) * (jax.nn.sigmoid(gates[:, C:]) * xa)

    def step(t, h):
        h = a_scr[pl.ds(t, 1), :] * h + u_scr[pl.ds(t, 1), :]
        hs_scr[pl.ds(t, 1), :] = h
        return h

    h = lax.fori_loop(0, tm, step, h_scr[0:1, :], unroll=8)
    h_scr[...] = jnp.broadcast_to(h, h_scr.shape)
    hT_ref[0] = h
    ya_ref[0] = hs_scr[...] * jax.nn.gelu(ag_ref[0], approximate=True)

    zr, zk, zv, zt = zr_ref[0], zk_ref[0], zv_ref[0], zt_ref[0]

    def shifted(zb, scr, lo, hi):
        return zb + mu_ref[:, lo:hi] * (earlier(zb, scr, 1) - zb)

    r = shifted(zr, pr_scr, 0, C)
    k = shifted(zk, pk_scr, C, 2 * C)
    v = shifted(zv, pv_scr, 2 * C, 3 * C)
    tail = shifted(zt, pt_scr, 3 * C, 3 * C + pt_scr.shape[1])
    wl = jnp.tanh(tail[:, :DECAY_LORA])
    al = tail[:, DECAY_LORA:DECAY_LORA + AAA_LORA]
    gl = jax.nn.sigmoid(tail[:, DECAY_LORA + AAA_LORA:])
    lora = lambda x, w_ref_: jnp.dot(x.astype(BF16), w_ref_[...], preferred_element_type=F32)
    w_log = -_softplus(-(w0_ref[...] + lora(wl, w2_ref))) - 0.5
    a = jax.nn.sigmoid(a0_ref[...] + lora(al, a2_ref))
    r_ref[0] = r
    w_ref[0] = jnp.exp(-jnp.exp(w_log))
    k_ref[0] = k
    v_ref[0] = v
    a_ref[0] = a
    g_ref[0] = lora(gl, g2_ref)

    for (scr, _), x in zip(prev, (ax, zr, zk, zv, zt)):
        scr[...] = x[tm - HALO:]


def even_pre(z, lru_conv, shift, lru_h, p):
    B, T, _ = z.shape
    C = lru_h.shape[-1]
    tail = DECAY_LORA + AAA_LORA + GATE_LORA
    tm = _pick(T, (256, 128, 64, 32, 16, 8))
    zblk = lambda w, col: pl.BlockSpec((1, tm, w), lambda b, i: (b, i, col))
    sblk = lambda rows, w, col: pl.BlockSpec((1, rows, w), lambda b, i: (b, 0, col))
    full = lambda a: pl.BlockSpec(a.shape, lambda b, i: (0,) * a.ndim)
    weights = [p['conv_w'], p['conv_b'], p['gate_w'], p['gate_b'], p['lru_c'], p['shift_mu'],
               p['w0'], p['a0'], p['w2'], p['a2'], p['g2']]
    tok = jax.ShapeDtypeStruct((B, T, C), F32)
    return pl.pallas_call(
        _even_pre_kernel,
        grid=(B, T // tm),
        in_specs=[zblk(C, 0), zblk(C, 1), zblk(C, 2), zblk(C, 3), zblk(C, 4), zblk(tail, 5 * C // tail),
                  sblk(CONV_W - 1, C, 0), sblk(1, C, 0), sblk(1, C, 1), sblk(1, C, 2),
                  sblk(1, tail, 3 * C // tail), sblk(1, C, 0)] + [full(w) for w in weights],
        out_specs=[zblk(C, 0)] * 7 + [sblk(1, C, 0)],
        out_shape=[tok] * 7 + [jax.ShapeDtypeStruct((B, 1, C), F32)],
        scratch_shapes=[pltpu.VMEM((HALO, C), F32)] * 4 + [pltpu.VMEM((HALO, tail), F32)]
                       + [pltpu.VMEM((tm, C), F32)] * 3 + [pltpu.VMEM((SUBLANES, C), F32)],
        compiler_params=_params(("arbitrary", "arbitrary")),
    )(z, z, z, z, z, z, lru_conv, shift, shift, shift, shift, lru_h[:, None, :], *weights)


def _colsum(x):
    n = x.shape[0] // SUBLANES
    p = x.reshape(n, SUBLANES, x.shape[1]).sum(axis=0)
    p = p + pltpu.roll(p, 4, axis=0)
    p = p + pltpu.roll(p, 2, axis=0)
    p = p + pltpu.roll(p, 1, axis=0)
    return p


def _bcast_rows(p, n):
    return jnp.broadcast_to(p[None], (n // SUBLANES,) + p.shape).reshape(n, p.shape[1])


def _rwkv_kernel(r_ref, w_ref, k_ref, v_ref, a_ref, kk_ref, ka_ref, gng_ref, gnb_ref, rk_ref, s0_ref,
                 y_ref, sT_ref, s_scr):
    tc = pl.program_id(1)
    N = RWKV_HEAD

    @pl.when(tc == 0)
    def _():
        s_scr[...] = s0_ref[...]

    sub = lax.broadcasted_iota(jnp.int32, (SUBLANES, LANES), 0)

    def step(t, carry):
        r = r_ref[t]
        w = w_ref[t]
        k_raw = k_ref[t]
        a = a_ref[t]
        kk = k_raw * kk_ref[...]
        kk = kk / jnp.maximum(_bcast_rows(jnp.sqrt(_colsum(kk * kk)), N), 1e-12)
        nkk = -kk
        kka = kk * a
        k = k_raw * (1.0 + (a - 1.0) * ka_ref[...])

        def vgroup(g, c2):
            yacc = jnp.zeros((SUBLANES, LANES), F32)
            for j in range(SUBLANES):
                vi = g * SUBLANES + j
                S = s_scr[vi]
                sa = _bcast_rows(_colsum(S * nkk), N)
                vb = jnp.broadcast_to(v_ref[t, pl.ds(vi, 1), :], (N, LANES))
                S = S * w + sa * kka + vb * k
                s_scr[vi] = S
                yacc = jnp.where(sub == j, _colsum(S * r), yacc)
            y_ref[t, pl.ds(pl.multiple_of(g * SUBLANES, SUBLANES), SUBLANES), :] = yacc
            return c2

        lax.fori_loop(0, N // SUBLANES, vgroup, 0, unroll=True)

        y = y_ref[t]
        d = y - _bcast_rows(_colsum(y), N) * (1.0 / N)
        var = _bcast_rows(_colsum(d * d), N) * (1.0 / N)
        bonus = _bcast_rows(_colsum(r * k * rk_ref[...]), N)
        y_ref[t] = d * lax.rsqrt(var + GN_EPS) * gng_ref[...] + gnb_ref[...] + bonus * v_ref[t]
        return carry

    lax.fori_loop(0, r_ref.shape[0], step, 0)

    @pl.when(tc == pl.num_programs(1) - 1)
    def _():
        sT_ref[...] = s_scr[...]


def rwkv_scan(r, w, k, v, a, k_k, k_a, gn_g, gn_b, r_k, s0):
    T, N, P = r.shape
    tc = _pick(T, (64, 32, 16, 8))
    blk = pl.BlockSpec((tc, N, LANES), lambda p, t: (t, 0, p))
    par = pl.BlockSpec((N, LANES), lambda p, t: (0, p))
    st = pl.BlockSpec((N, N, LANES), lambda p, t: (0, 0, p))
    return pl.pallas_call(
        _rwkv_kernel,
        grid=(P // LANES, T // tc),
        in_specs=[blk] * 5 + [par] * 5 + [st],
        out_specs=[blk, st],
        out_shape=[jax.ShapeDtypeStruct((T, N, P), F32), jax.ShapeDtypeStruct((N, N, P), F32)],
        scratch_shapes=[pltpu.VMEM((N, N, LANES), F32)],
        compiler_params=_params(("arbitrary", "arbitrary")),
    )(r, w, k, v, a, k_k, k_a, gn_g, gn_b, r_k, s0)


MASKED = -1e30


KEY_NEG_INF = -2139095041
QK_SPLIT = 2


def _tree_sum(parts):
    while len(parts) > 1:
        parts = [parts[i] + parts[i + 1] for i in range(0, len(parts) - 1, 2)] + parts[len(parts) & ~1:]
    return parts[0]


def _dsa_kernel(q_ref, qi_ref, wi_ref, k_ref, vt_ref, ki_ref, o_ref,
                key_scr, m_scr, l_scr, acc_scr, *, past, n_sel, scale, tk):
    qb = pl.program_id(1)
    QB = q_ref.shape[1]
    L = k_ref.shape[1]
    NP = N_HEADS_C // 2
    lane_chunk = lax.broadcasted_iota(jnp.int32, (1, QB), 1) // CHUNK
    limit = past + (qb * (QB // CHUNK) + lane_chunk + 1) * CHUNK
    n_tiles = (past + (qb + 1) * QB + tk - 1) // tk
    u = 2 if (tk < 512 and (L // tk) % 2 == 0) else 1
    n_count = (n_tiles + u - 1) // u
    row = lax.broadcasted_iota(jnp.int32, (tk, QB), 0)
    tile = lambda kt: pl.ds(pl.multiple_of(kt * tk, tk), tk)

    qi = qi_ref[0].astype(BF16)
    qi_pair = [jnp.concatenate([qi[:, (2 * p) * D_IDX:(2 * p + 1) * D_IDX],
                                qi[:, (2 * p + 1) * D_IDX:(2 * p + 2) * D_IDX]], axis=0)
               for p in range(N_IDX_HEADS // 2)]
    wi = wi_ref[0]

    def score_tile(kt, c):
        @pl.when(kt < n_tiles)
        def _():
            kidx = ki_ref[0, tile(kt), :]
            terms = []
            for p in range(N_IDX_HEADS // 2):
                s = lax.dot_general(kidx, qi_pair[p], (((1,), (1,)), ((), ())),
                                    preferred_element_type=F32)
                for e in range(2):
                    h = 2 * p + e
                    terms.append(wi[h:h + 1, :] * jnp.maximum(s[:, e * QB:(e + 1) * QB], 0.0))
            isc = terms[0]
            for t in terms[1:]:
                isc = isc + t
            ok = row + kt * tk < limit
            bits = lax.bitcast_convert_type(jnp.where(ok, isc + 0.0, -jnp.inf), jnp.int32)
            key_scr[tile(kt), :] = bits ^ ((bits >> 31) & jnp.int32(0x7FFFFFFF))

        @pl.when(kt >= n_tiles)
        def _():
            key_scr[tile(kt), :] = jnp.full((tk, QB), KEY_NEG_INF, jnp.int32)

        return c

    lax.fori_loop(0, n_count * u, score_tile, 0)

    int_min = jnp.int32(-2 ** 31)
    ct = tk * u

    def search(i, t):
        cand = t | lax.shift_left(jnp.int32(1), 31 - i)
        cs = cand ^ int_min

        def count_tile(kt, acc):
            keys = key_scr[pl.ds(pl.multiple_of(kt * ct, ct), ct), :]
            hit = jnp.where(keys >= cs, 1.0, 0.0)
            return acc + _tree_sum([hit[r * SUBLANES:(r + 1) * SUBLANES] for r in range(ct // SUBLANES)])

        acc = lax.fori_loop(0, n_count, count_tile, jnp.zeros((SUBLANES, QB), F32))
        cnt = jnp.sum(acc, axis=0, keepdims=True)
        return jnp.where(cnt >= n_sel, cand, t)

    thr = lax.fori_loop(0, 32, search, jnp.zeros((1, QB), jnp.int32)) ^ int_min

    G = N_HEADS_C // N_KV_C
    dh = HEAD_DIM_C
    q = q_ref[0]
    q_pair = [(jnp.concatenate([q[:, (2 * p) * dh:(2 * p + 1) * dh],
                                q[:, (2 * p + 1) * dh:(2 * p + 2) * dh]], axis=0) * scale).astype(BF16)
              for p in range(NP)]
    m_scr[...] = jnp.full(m_scr.shape, MASKED, F32)
    l_scr[...] = jnp.zeros(l_scr.shape, F32)
    acc_scr[...] = jnp.zeros(acc_scr.shape, F32)

    def att_tile(kt, c):
        sel = (key_scr[tile(kt), :] >= thr) & (row + kt * tk < limit)
        b = jnp.where(sel, 0.0, MASKED)
        b2 = jnp.concatenate([b, b], axis=1)
        def logits(p):
            j = (2 * p) // G
            kj = k_ref[0, tile(kt), j * dh:(j + 1) * dh]
            part = tk // QK_SPLIT
            return jnp.concatenate(
                [lax.dot_general(kj[r * part:(r + 1) * part], q_pair[p], (((1,), (1,)), ((), ())),
                                 preferred_element_type=F32) for r in range(QK_SPLIT)], axis=0)

        ahead = 2
        lg = {p: logits(p) for p in range(ahead)}
        for p in range(NP):
            j = (2 * p) // G
            x = lg.pop(p) + b2
            m = m_scr[p:p + 1, :]
            m_new = jnp.maximum(m, jnp.max(x, axis=0, keepdims=True))
            alpha = jnp.exp2(m - m_new)
            pr = jnp.exp2(x - m_new)
            l_scr[p:p + 1, :] = alpha * l_scr[p:p + 1, :] + jnp.sum(pr, axis=0, keepdims=True)
            m_scr[p:p + 1, :] = m_new
            vt = vt_ref[0, kt, j * dh:(j + 1) * dh, :]
            pv = jnp.dot(vt, pr.astype(BF16), preferred_element_type=F32)
            if p + ahead < NP:
                lg[p + ahead] = logits(p + ahead)
            acc_scr[p] = acc_scr[p] * alpha + pv
        return c

    lax.fori_loop(0, n_tiles, att_tile, 0)
    for p in range(NP):
        o2 = acc_scr[p] / l_scr[p:p + 1, :]
        for s in range(2):
            o_ref[0, :, (2 * p + s) * dh:(2 * p + s + 1) * dh] = o2[:, s * QB:(s + 1) * QB].T


def dsa_attention(z, q_col, qi_col, wi_t, keys, vals_t, kidx, past, n_sel, tk):
    B, T, _ = z.shape
    L = keys.shape[1]
    QB = LANES
    dq = N_HEADS_C * HEAD_DIM_C
    dqi = N_IDX_HEADS * D_IDX
    kern = functools.partial(_dsa_kernel, past=past, n_sel=n_sel,
                             scale=HEAD_DIM_C ** -0.5 * math.log2(math.e), tk=tk)
    return pl.pallas_call(
        kern,
        grid=(B, T // QB),
        in_specs=[pl.BlockSpec((1, QB, dq), lambda b, c: (b, c, q_col)),
                  pl.BlockSpec((1, QB, dqi), lambda b, c: (b, c, qi_col)),
                  pl.BlockSpec((1, N_IDX_HEADS, QB), lambda b, c: (b, 0, c)),
                  pl.BlockSpec((1, L, keys.shape[2]), lambda b, c: (b, 0, 0)),
                  pl.BlockSpec((1,) + vals_t.shape[1:], lambda b, c: (b, 0, 0, 0)),
                  pl.BlockSpec((1, L, kidx.shape[2]), lambda b, c: (b, 0, 0))],
        out_specs=pl.BlockSpec((1, QB, dq), lambda b, c: (b, c, 0)),
        out_shape=jax.ShapeDtypeStruct((B, T, dq), F32),
        scratch_shapes=[pltpu.VMEM((L, QB), jnp.int32),
                        pltpu.VMEM((N_HEADS_C // 2, 2 * QB), F32),
                        pltpu.VMEM((N_HEADS_C // 2, 2 * QB), F32),
                        pltpu.VMEM((N_HEADS_C // 2, HEAD_DIM_C, 2 * QB), F32)],
        compiler_params=_params(("arbitrary", "arbitrary")),
    )(z, z, wi_t, keys, vals_t, kidx)


def _route_tokens(x, w, bias):
    def split(t):
        hi = t.astype(BF16)
        return hi, (t - hi.astype(F32)).astype(BF16)

    nt = lambda a, c: lax.dot_general(a, c, (((1,), (1,)), ((), ())), preferred_element_type=F32)
    (w_hi, w_lo), (x_hi, x_lo) = split(w), split(x)
    logits = nt(w_hi, x_hi) + (nt(w_hi, x_lo) + nt(w_lo, x_hi))
    scores = jax.nn.sigmoid(logits)
    biased = scores + bias
    tm = scores.shape[1]
    P = EXPERTS_PER_GROUP

    def ranks(rows):
        out = []
        for i in range(len(rows)):
            rk = jnp.zeros((1, tm), jnp.int32)
            for j in range(len(rows)):
                if j == i:
                    continue
                beat = (rows[j] > rows[i]) | ((rows[j] == rows[i]) & (j < i))
                rk = rk + beat.astype(jnp.int32)
            out.append(rk)
        return out

    b = [biased[i:i + 1] for i in range(N_EXPERTS)]
    s = [scores[i:i + 1] for i in range(N_EXPERTS)]
    grp_rank, grp_score = [], []
    for g in range(N_GROUPS):
        rows = b[g * P:(g + 1) * P]
        rk = ranks(rows)
        grp_rank.append(rk)
        top1 = sum(jnp.where(rk[i] == 0, rows[i], 0.0) for i in range(P))
        top2 = sum(jnp.where(rk[i] == 1, rows[i], 0.0) for i in range(P))
        grp_score.append(top1 + top2)
    gr = ranks(grp_score)
    e1 = jnp.zeros((1, tm), jnp.int32)
    e2 = jnp.zeros((1, tm), jnp.int32)
    w1 = jnp.zeros((1, tm), F32)
    w2 = jnp.zeros((1, tm), F32)
    for g in range(N_GROUPS):
        best = gr[g] == 0
        for i in range(P):
            is1 = best & (grp_rank[g][i] == 0)
            is2 = best & (grp_rank[g][i] == 1)
            e1 = jnp.where(is1, g * P + i, e1)
            e2 = jnp.where(is2, g * P + i, e2)
            w1 = jnp.where(is1, s[g * P + i], w1)
            w2 = jnp.where(is2, s[g * P + i], w2)
    tot = w1 + w2
    return jnp.concatenate([e1, e2], axis=0), jnp.concatenate([w1 / tot, w2 / tot], axis=0)


def _router_kernel(x_ref, w_ref, b_ref, e_ref, g_ref):
    e_ref[...], g_ref[...] = _route_tokens(x_ref[...], w_ref[...], b_ref[...])


def route(xt, router_w, router_b):
    T, D = xt.shape
    tm = _pick(T, (512, 256, 128))
    return pl.pallas_call(
        _router_kernel,
        grid=(T // tm,),
        in_specs=[pl.BlockSpec((tm, D), lambda i: (i, 0)),
                  pl.BlockSpec((N_EXPERTS, D), lambda i: (0, 0)),
                  pl.BlockSpec((N_EXPERTS, 1), lambda i: (0, 0))],
        out_specs=[pl.BlockSpec((TOP_K, tm), lambda i: (0, i)),
                   pl.BlockSpec((TOP_K, tm), lambda i: (0, i))],
        out_shape=[jax.ShapeDtypeStruct((TOP_K, T), jnp.int32),
                   jax.ShapeDtypeStruct((TOP_K, T), F32)],
        compiler_params=_params(("arbitrary",)),
    )(xt, router_w.T, router_b.reshape(N_EXPERTS, 1))


def _ffn_kernel(be_ref, x_ref, wg_ref, wu_ref, wd_ref, o_ref):
    i = pl.program_id(0)
    used = be_ref[pl.num_programs(0)]

    @pl.when(i < used)
    def _():
        x = x_ref[...].astype(BF16)
        a = jnp.dot(x, wg_ref[0, 0], preferred_element_type=F32)
        u = jnp.dot(x, wu_ref[0, 0], preferred_element_type=F32)
        h = (a * jax.nn.sigmoid(a)) * u
        o_ref[...] = jnp.dot(h.astype(BF16), wd_ref[0, 0], preferred_element_type=F32)

    @pl.when(i >= used)
    def _():
        o_ref[...] = jnp.zeros(o_ref.shape, F32)


def _cast_kernel(x_ref, o_ref):
    o_ref[...] = x_ref[...].astype(o_ref.dtype)


def cast_bf16(w):
    n = w.shape[-1]
    w2 = w.reshape(-1, n)
    rows = w2.shape[0]
    tr = _pick(rows, (1024 * 1024 // n, 512, 256, 128, 64, 32, 16))
    out = pl.pallas_call(
        _cast_kernel,
        grid=(rows // tr,),
        in_specs=[pl.BlockSpec((tr, n), lambda i: (i, 0))],
        out_specs=pl.BlockSpec((tr, n), lambda i: (i, 0)),
        out_shape=jax.ShapeDtypeStruct(w2.shape, BF16),
        compiler_params=_params(("arbitrary",)),
    )(w2)
    return out.reshape(w.shape)


def expert_ffn(xs, block_exp, layer, wg, wu, wd):
    R, D = xs.shape
    F = wg.shape[3]
    nb = R // MOE_TM
    grid_spec = pltpu.PrefetchScalarGridSpec(
        num_scalar_prefetch=1,
        grid=(nb,),
        in_specs=[pl.BlockSpec((MOE_TM, D), lambda i, be: (i, 0)),
                  pl.BlockSpec((1, 1, D, F), lambda i, be: (layer, be[i], 0, 0)),
                  pl.BlockSpec((1, 1, D, F), lambda i, be: (layer, be[i], 0, 0)),
                  pl.BlockSpec((1, 1, F, D), lambda i, be: (layer, be[i], 0, 0))],
        out_specs=pl.BlockSpec((MOE_TM, D), lambda i, be: (i, 0)),
    )
    return pl.pallas_call(
        _ffn_kernel,
        grid_spec=grid_spec,
        out_shape=jax.ShapeDtypeStruct((R, D), F32),
        compiler_params=_params(("arbitrary",)),
    )(block_exp, xs, wg, wu, wd)


def _combine_ln_kernel(ya_ref, yb_ref, ga_ref, gb_ref, r_ref, g_ref, b_ref, o_ref):
    y = ya_ref[...] * ga_ref[...] + yb_ref[...] * gb_ref[...]
    o_ref[...] = _deepnorm(r_ref[...], y, g_ref[...], b_ref[...])


def combine_ln(ya, yb, ga, gb, resid, g, b):
    M, N = ya.shape
    tm = _pick(M, (256, 128, 64, 32, 16, 8))
    blk = pl.BlockSpec((tm, N), lambda i: (i, 0))
    col = pl.BlockSpec((tm, 1), lambda i: (i, 0))
    row = pl.BlockSpec((1, N), lambda i: (0, 0))
    return pl.pallas_call(
        _combine_ln_kernel,
        grid=(M // tm,),
        in_specs=[blk, blk, col, col, blk, row, row],
        out_specs=blk,
        out_shape=jax.ShapeDtypeStruct((M, N), F32),
        compiler_params=_params(("arbitrary",)),
    )(ya, yb, ga, gb, resid, g.reshape(1, N), b.reshape(1, N))


def moe_layer(x, router_w, router_b, layer, wg, wu, wd, ln_g, ln_b):
    shp = x.shape
    xt = x.reshape(-1, shp[-1])
    T = xt.shape[0]
    expert, gate = route(xt, router_w, router_b)
    flat_e = expert.T.reshape(-1)
    n_assign = T * TOP_K
    onehot = (flat_e[:, None] == jnp.arange(N_EXPERTS, dtype=jnp.int32)[None]).astype(jnp.int32)
    csum = jnp.cumsum(onehot, axis=0)
    rank = jnp.take_along_axis(csum, flat_e[:, None], axis=1)[:, 0] - 1
    counts = csum[-1]
    padded = (counts + MOE_TM - 1) // MOE_TM * MOE_TM
    pad_end = jnp.cumsum(padded)
    pad_start = pad_end - padded
    dest = pad_start[flat_e] + rank
    nb = -(-n_assign // MOE_TM) + N_EXPERTS
    rows = nb * MOE_TM
    row_tok = jnp.zeros(rows, jnp.int32).at[dest].set(jnp.arange(n_assign, dtype=jnp.int32) // TOP_K)
    first_row = jnp.arange(nb, dtype=jnp.int32) * MOE_TM
    block_exp = jnp.minimum(jnp.sum((pad_end[None, :] <= first_row[:, None]).astype(jnp.int32), axis=1),
                            N_EXPERTS - 1)
    block_exp = jnp.concatenate([block_exp, pad_end[-1:] // MOE_TM])
    yb = expert_ffn(xt[row_tok], block_exp, layer, wg, wu, wd)
    d2 = dest.reshape(T, TOP_K)
    out = combine_ln(yb[d2[:, 0]], yb[d2[:, 1]], gate[0][:, None], gate[1][:, None], xt, ln_g, ln_b)
    return out.reshape(shp)


def _block_diag(w):
    G, I, O = w.shape
    eye = jnp.eye(G, dtype=w.dtype)
    return (eye[:, None, :, None] * w[:, :, None, :]).reshape(G * I, G * O)


def even_layer(x, lru_h, lru_conv, wkv, shift, p, ln_g, ln_b):
    B, T, D = x.shape
    D_LRU = lru_h.shape[-1]
    D_RWKV = p['w0'].shape[1]
    H = D_RWKV // RWKV_HEAD
    N = RWKV_HEAD
    z = matmul(x.reshape(B * T, D), p['w_in']).reshape(B, T, -1)
    y_a, r, decay, k, v, a, g, h_T = even_pre(z, lru_conv, shift, lru_h, p)

    lanes = lambda t: jnp.transpose(t.reshape(B, T, H, N), (1, 3, 0, 2)).reshape(T, N, B * H)
    per_chain = lambda w: jnp.tile(w.reshape(H, N).T, (1, B))
    s0 = jnp.transpose(wkv, (2, 3, 0, 1)).reshape(N, N, B * H)
    y, s_T = rwkv_scan(lanes(r), lanes(decay), lanes(k), lanes(v), lanes(a),
                       per_chain(p['k_k']), per_chain(p['k_a']), per_chain(p['gn_g']),
                       per_chain(p['gn_b']), per_chain(p['r_k']), s0)
    y = jnp.transpose(y.reshape(T, N, B, H), (2, 0, 3, 1)).reshape(B * T, D_RWKV)
    S_T = jnp.transpose(s_T.reshape(N, N, B, H), (2, 3, 0, 1))

    x_new = mix_ln(y_a.reshape(B * T, D_LRU), y, g.reshape(B * T, D_RWKV), p['w_out'],
                   x.reshape(B * T, D), ln_g, ln_b)
    conv_state = jnp.concatenate([lru_conv, z[..., :D_LRU]], axis=1)[:, T:]
    return x_new.reshape(B, T, D), (h_T[:, 0], conv_state, S_T, z[:, -1:, 2 * D_LRU:])


def odd_layer(x, cache, p, ln_g, ln_b):
    B, T, D = x.shape
    dq = N_HEADS_C * HEAD_DIM_C
    dkv = N_KV_C * HEAD_DIM_C
    dqi = N_IDX_HEADS * D_IDX
    z = matmul(x.reshape(B * T, D), p['w_in']).reshape(B, T, -1)
    k = z[..., dq:dq + dkv]
    v = z[..., dq + dkv:dq + 2 * dkv]
    o = dq + 2 * dkv + dqi
    ki = z[..., o:o + D_IDX]
    wi = z[..., o + D_IDX:o + D_IDX + N_IDX_HEADS] * (dqi ** -0.5)
    Tp = -(-T // LANES) * LANES
    if cache is None:
        past = 0
        n_sel = min(TOPK_MAX, T // 4)
        parts = lambda c, n: [n]
    else:
        past = cache[0].shape[1]
        n_sel = min(TOPK_MAX, (past + T) // 4)
        parts = lambda c, n: [c.reshape(B, past, -1), n]
    L = -(-(past + Tp) // LANES) * LANES
    tk = _pick(L, (512, 384, 128))
    cat = lambda c, n: jnp.concatenate(
        parts(c, n) + [jnp.zeros((B, L - past - T, n.shape[-1]), n.dtype)], axis=1).astype(BF16)
    ck, cv, cki = cache if cache is not None else (None, None, None)
    keys, vals, kidx = cat(ck, k), cat(cv, v), cat(cki, ki)
    vals_t = jnp.swapaxes(vals.reshape(B, L // tk, tk, dkv), 2, 3)
    rowpad = lambda t: jnp.pad(t, ((0, 0), (0, Tp - T), (0, 0)))
    wi_t = jnp.swapaxes(rowpad(wi), 1, 2)
    o_att = dsa_attention(rowpad(z), 0, (dq + 2 * dkv) // dqi, wi_t, keys, vals_t, kidx, past, n_sel, tk)
    x_new = matmul_ln(o_att[:, :T].reshape(B * T, dq), p['w_out'], x.reshape(B * T, D), ln_g, ln_b)
    new = (k.reshape(B, T, N_KV_C, HEAD_DIM_C), v.reshape(B, T, N_KV_C, HEAD_DIM_C), ki)
    return x_new.reshape(B, T, D), new


def run_trunk(x, even_state, odd_cache, ev, od, norm_w, moe_w):
    ln1_g, ln1_b, ln2_g, ln2_b = norm_w
    router_w, router_b, wg, wu, wd = moe_w
    x, new_even = even_layer(x, *even_state, ev, ln1_g[0], ln1_b[0])
    x = moe_layer(x, router_w, router_b, 0, wg, wu, wd, ln2_g[0], ln2_b[0])
    x, new_odd = odd_layer(x, odd_cache, od, ln1_g[1], ln1_b[1])
    x = moe_layer(x, router_w, router_b, 1, wg, wu, wd, ln2_g[1], ln2_b[1])
    return x, new_even, new_odd


def kernel(x_prompt, x_sample, state_lru_h, state_lru_conv, state_rwkv_wkv, state_rwkv_shift, cache_k, cache_v, cache_kidx, ev_w_in, ev_conv_w, ev_conv_b, ev_gate_r_w, ev_gate_r_b, ev_gate_i_w, ev_gate_i_b, ev_lambda, ev_shift_mu, ev_w0, ev_w2, ev_a0, ev_a2, ev_g2, ev_k_k, ev_k_a, ev_r_k, ev_gn_g, ev_gn_b, ev_w_out, od_w_in, od_w_out, ln1_g, ln1_b, ln2_g, ln2_b, router_w, router_b, moe_w_gate, moe_w_up, moe_w_down):
    ev = dict(
        w_in=ev_w_in[0].astype(BF16), conv_w=ev_conv_w[0], conv_b=ev_conv_b[0][None],
        gate_w=jnp.concatenate([_block_diag(ev_gate_r_w[0]), _block_diag(ev_gate_i_w[0])], axis=1).astype(BF16),
        gate_b=jnp.concatenate([ev_gate_r_b[0], ev_gate_i_b[0]])[None],
        lru_c=(-LRU_C * jax.nn.softplus(-ev_lambda[0]))[None],
        shift_mu=ev_shift_mu[0][None], w0=ev_w0[0][None], a0=ev_a0[0][None],
        w2=ev_w2[0].astype(BF16), a2=ev_a2[0].astype(BF16), g2=ev_g2[0].astype(BF16),
        k_k=ev_k_k[0][None], k_a=ev_k_a[0][None], r_k=ev_r_k[0], gn_g=ev_gn_g[0], gn_b=ev_gn_b[0],
        w_out=ev_w_out[0].astype(BF16))
    n_in = od_w_in.shape[2]
    od = dict(w_in=jnp.pad(od_w_in[0], ((0, 0), (0, (-n_in) % LANES))).astype(BF16),
              w_out=od_w_out[0].astype(BF16))
    norm_w = (ln1_g, ln1_b, ln2_g, ln2_b)
    moe_w = (router_w, router_b, cast_bf16(moe_w_gate), cast_bf16(moe_w_up), cast_bf16(moe_w_down))

    B = x_prompt.shape[0]
    fresh = (jnp.zeros((B,) + state_lru_h.shape[2:], F32),
             jnp.zeros((B,) + state_lru_conv.shape[2:], F32),
             jnp.zeros((B,) + state_rwkv_wkv.shape[2:], F32),
             jnp.zeros((B,) + state_rwkv_shift.shape[2:], F32))
    y_p, pe, po = run_trunk(x_prompt, fresh, None, ev, od, norm_w, moe_w)
    y_s, se, so = run_trunk(
        x_sample, (state_lru_h[0], state_lru_conv[0], state_rwkv_wkv[0], state_rwkv_shift[0]),
        (cache_k[0], cache_v[0], cache_kidx[0]), ev, od, norm_w, moe_w)
    st = lambda t: t[None]
    return (y_p, y_s, *map(st, pe), *map(st, po), *map(st, se), *map(st, so))
```

```python
import functools
import math

import jax
import jax.numpy as jnp
from jax import lax
from jax.experimental import pallas as pl
from jax.experimental.pallas import tpu as pltpu

F32 = jnp.float32
BF16 = jnp.bfloat16

CHUNK = 64
LRU_BLOCKS = 16
CONV_W = 4
LRU_C = 8.0
RWKV_HEAD = 64
DECAY_LORA = 64
AAA_LORA = 64
GATE_LORA = 128
N_HEADS_C = 16
HEAD_DIM_C = 128
N_KV_C = 4
N_IDX_HEADS = 16
D_IDX = 64
TOPK_MAX = 256
N_EXPERTS = 16
N_GROUPS = 4
EXPERTS_PER_GROUP = N_EXPERTS // N_GROUPS
TOP_K = 2
DEPTH = 2
DN_ALPHA = (2 * DEPTH) ** 0.25
LN_EPS = 1e-5
GN_EPS = 64e-5

LANES = 128
SUBLANES = 8
VMEM_LIMIT = 56 * 1024 * 1024
MOE_TM = 256


def _params(sem):
    return pltpu.CompilerParams(dimension_semantics=sem, vmem_limit_bytes=VMEM_LIMIT)


def _mm_kernel(x_ref, w_ref, o_ref, xb_ref):
    @pl.when(pl.program_id(1) == 0)
    def _():
        xb_ref[...] = x_ref[...].astype(BF16)

    o_ref[...] = jnp.dot(xb_ref[...], w_ref[...], preferred_element_type=F32)


def _pick(n, cands):
    for c in cands:
        if n % c == 0:
            return c
    return n


def matmul(x, w):
    M, K = x.shape
    N = w.shape[1]
    tm = _pick(M, (1024, 512, 256, 128, 64, 32, 16, 8))
    tn = _pick(N, (1792, 1408, 1024, 768, 512, 384, 256, 128))
    return pl.pallas_call(
        _mm_kernel,
        grid=(M // tm, N // tn),
        in_specs=[pl.BlockSpec((tm, K), lambda i, j: (i, 0)),
                  pl.BlockSpec((K, tn), lambda i, j: (0, j))],
        out_specs=pl.BlockSpec((tm, tn), lambda i, j: (i, j)),
        out_shape=jax.ShapeDtypeStruct((M, N), F32),
        scratch_shapes=[pltpu.VMEM((tm, K), BF16)],
        compiler_params=_params(("arbitrary", "arbitrary")),
    )(x, w)


def _deepnorm(resid, y, g, b):
    h = DN_ALPHA * resid + y
    mu = jnp.mean(h, axis=-1, keepdims=True)
    hc = h - mu
    var = jnp.mean(hc * hc, axis=-1, keepdims=True)
    return hc * lax.rsqrt(var + LN_EPS) * g + b


def _mm_ln_kernel(x_ref, w_ref, r_ref, g_ref, b_ref, o_ref):
    y = jnp.dot(x_ref[...].astype(BF16), w_ref[...], preferred_element_type=F32)
    o_ref[...] = _deepnorm(r_ref[...], y, g_ref[...], b_ref[...])


def _mix_ln_kernel(ya_ref, yb_ref, gate_ref, w_ref, r_ref, g_ref, b_ref, o_ref):
    mix = jnp.concatenate([ya_ref[...], yb_ref[...] * gate_ref[...]], axis=1).astype(BF16)
    y = jnp.dot(mix, w_ref[...], preferred_element_type=F32)
    o_ref[...] = _deepnorm(r_ref[...], y, g_ref[...], b_ref[...])


def _ln_call(kern, lhs, w, resid, g, b):
    M, N = resid.shape
    tm = _pick(M, (512, 256, 128, 64, 32, 16, 8))
    full = lambda a: pl.BlockSpec(a.shape, lambda i: (0,) * a.ndim)
    rows = lambda a: pl.BlockSpec((tm, a.shape[1]), lambda i: (i, 0))
    consts = [g.reshape(1, N), b.reshape(1, N)]
    return pl.pallas_call(
        kern,
        grid=(M // tm,),
        in_specs=[rows(a) for a in lhs] + [full(w), rows(resid)] + [full(c) for c in consts],
        out_specs=rows(resid),
        out_shape=jax.ShapeDtypeStruct((M, N), F32),
        compiler_params=_params(("arbitrary",)),
    )(*lhs, w, resid, *consts)


def matmul_ln(x, w, resid, g, b):
    return _ln_call(_mm_ln_kernel, [x], w, resid, g, b)


def mix_ln(ya, yb, gate, w, resid, g, b):
    return _ln_call(_mix_ln_kernel, [ya, yb, gate], w, resid, g, b)


def _expm1(x):
    e = jnp.exp(x)
    safe = jnp.where((e == 1.0) | (e == 0.0), 2.0, e)
    return jnp.where(e == 1.0, x, jnp.where(e == 0.0, -1.0, (e - 1.0) * x / jnp.log(safe)))


def _softplus(x):
    return jnp.maximum(x, 0.0) + jnp.log(1.0 + jnp.exp(-jnp.abs(x)))


HALO = SUBLANES


def _even_pre_kernel(ax_ref, ag_ref, zr_ref, zk_ref, zv_ref, zt_ref,
                     conv_ref, sr_ref, sk_ref, sv_ref, st_ref, h0_ref,
                     cw_ref, cb_ref, gw_ref, gb_ref, lc_ref, mu_ref,
                     w0_ref, a0_ref, w2_ref, a2_ref, g2_ref,
                     ya_ref, r_ref, w_ref, k_ref, v_ref, a_ref, g_ref, hT_ref,
                     xp_scr, pr_scr, pk_scr, pv_scr, pt_scr, a_scr, u_scr, hs_scr, h_scr):
    i = pl.program_id(1)
    tm = ax_ref.shape[1]
    C = ax_ref.shape[2]
    prev = ((xp_scr, None), (pr_scr, sr_ref), (pk_scr, sk_ref), (pv_scr, sv_ref), (pt_scr, st_ref))

    @pl.when(i == 0)
    def _():
        xp_scr[...] = jnp.zeros((HALO, C), F32)
        xp_scr[HALO - (CONV_W - 1):HALO, :] = conv_ref[0]
        for scr, ref in prev[1:]:
            scr[...] = jnp.broadcast_to(ref[0], scr.shape)
        h_scr[...] = jnp.broadcast_to(h0_ref[0], h_scr.shape)

    def earlier(x, scr, d):
        rot = pltpu.roll(x, d, axis=0)
        halo = pltpu.roll(scr[...], d, axis=0)
        sub = lax.broadcasted_iota(jnp.int32, halo.shape, 0)
        return jnp.concatenate([jnp.where(sub < d, halo, rot[:HALO]), rot[HALO:]], axis=0)

    ax = ax_ref[0]
    xa = cb_ref[...] + cw_ref[CONV_W - 1:CONV_W, :] * ax
    for d in range(1, CONV_W):
        xa = xa + cw_ref[CONV_W - 1 - d:CONV_W - d, :] * earlier(ax, xp_scr, d)
    gates = jnp.dot(xa.astype(BF16), gw_ref[...], preferred_element_type=F32) + gb_ref[...]
    log_a = lc_ref[...] * jax.nn.sigmoid(gates[:, :C])
    a_scr[...] = jnp.exp(log_a)
    u_scr[...] = jnp.sqrt(-_expm1(2.0 * log_a)) * (jax.nn.sigmoid(gates[:, C:]) * xa)

    def step(t, h):
        h = a_scr[pl.ds(t, 1), :] * h + u_scr[pl.ds(t, 1), :]
        hs_scr[pl.ds(t, 1), :] = h
        return h

    h = lax.fori_loop(0, tm, step, h_scr[0:1, :], unroll=8)
    h_scr[...] = jnp.broadcast_to(h, h_scr.shape)
    hT_ref[0] = h
    ya_ref[0] = hs_scr[...] * jax.nn.gelu(ag_ref[0], approximate=True)

    zr, zk, zv, zt = zr_ref[0], zk_ref[0], zv_ref[0], zt_ref[0]

    def shifted(zb, scr, lo, hi):
        return zb + mu_ref[:, lo:hi] * (earlier(zb, scr, 1) - zb)

    r = shifted(zr, pr_scr, 0, C)
    k = shifted(zk, pk_scr, C, 2 * C)
    v = shifted(zv, pv_scr, 2 * C, 3 * C)
    tail = shifted(zt, pt_scr, 3 * C, 3 * C + pt_scr.shape[1])
    wl = jnp.tanh(tail[:, :DECAY_LORA])
    al = tail[:, DECAY_LORA:DECAY_LORA + AAA_LORA]
    gl = jax.nn.sigmoid(tail[:, DECAY_LORA + AAA_LORA:])
    lora = lambda x, w_ref_: jnp.dot(x.astype(BF16), w_ref_[...], preferred_element_type=F32)
    w_log = -_softplus(-(w0_ref[...] + lora(wl, w2_ref))) - 0.5
    a = jax.nn.sigmoid(a0_ref[...] + lora(al, a2_ref))
    r_ref[0] = r
    w_ref[0] = jnp.exp(-jnp.exp(w_log))
    k_ref[0] = k
    v_ref[0] = v
    a_ref[0] = a
    g_ref[0] = lora(gl, g2_ref)

    for (scr, _), x in zip(prev, (ax, zr, zk, zv, zt)):
        scr[...] = x[tm - HALO:]


def even_pre(z, lru_conv, shift, lru_h, p):
    B, T, _ = z.shape
    C = lru_h.shape[-1]
    tail = DECAY_LORA + AAA_LORA + GATE_LORA
    tm = _pick(T, (256, 128, 64, 32, 16, 8))
    zblk = lambda w, col: pl.BlockSpec((1, tm, w), lambda b, i: (b, i, col))
    sblk = lambda rows, w, col: pl.BlockSpec((1, rows, w), lambda b, i: (b, 0, col))
    full = lambda a: pl.BlockSpec(a.shape, lambda b, i: (0,) * a.ndim)
    weights = [p['conv_w'], p['conv_b'], p['gate_w'], p['gate_b'], p['lru_c'], p['shift_mu'],
               p['w0'], p['a0'], p['w2'], p['a2'], p['g2']]
    tok = jax.ShapeDtypeStruct((B, T, C), F32)
    return pl.pallas_call(
        _even_pre_kernel,
        grid=(B, T // tm),
        in_specs=[zblk(C, 0), zblk(C, 1), zblk(C, 2), zblk(C, 3), zblk(C, 4), zblk(tail, 5 * C // tail),
                  sblk(CONV_W - 1, C, 0), sblk(1, C, 0), sblk(1, C, 1), sblk(1, C, 2),
                  sblk(1, tail, 3 * C // tail), sblk(1, C, 0)] + [full(w) for w in weights],
        out_specs=[zblk(C, 0)] * 7 + [sblk(1, C, 0)],
        out_shape=[tok] * 7 + [jax.ShapeDtypeStruct((B, 1, C), F32)],
        scratch_shapes=[pltpu.VMEM((HALO, C), F32)] * 4 + [pltpu.VMEM((HALO, tail), F32)]
                       + [pltpu.VMEM((tm, C), F32)] * 3 + [pltpu.VMEM((SUBLANES, C), F32)],
        compiler_params=_params(("arbitrary", "arbitrary")),
    )(z, z, z, z, z, z, lru_conv, shift, shift, shift, shift, lru_h[:, None, :], *weights)


def _colsum(x):
    n = x.shape[0] // SUBLANES
    p = x.reshape(n, SUBLANES, x.shape[1]).sum(axis=0)
    p = p + pltpu.roll(p, 4, axis=0)
    p = p + pltpu.roll(p, 2, axis=0)
    p = p + pltpu.roll(p, 1, axis=0)
    return p


def _bcast_rows(p, n):
    return jnp.broadcast_to(p[None], (n // SUBLANES,) + p.shape).reshape(n, p.shape[1])


def _rwkv_kernel(r_ref, w_ref, k_ref, v_ref, a_ref, kk_ref, ka_ref, gng_ref, gnb_ref, rk_ref, s0_ref,
                 y_ref, sT_ref, s_scr):
    tc = pl.program_id(1)
    N = RWKV_HEAD

    @pl.when(tc == 0)
    def _():
        s_scr[...] = s0_ref[...]

    sub = lax.broadcasted_iota(jnp.int32, (SUBLANES, LANES), 0)

    def step(t, carry):
        r = r_ref[t]
        w = w_ref[t]
        k_raw = k_ref[t]
        a = a_ref[t]
        kk = k_raw * kk_ref[...]
        kk = kk / jnp.maximum(_bcast_rows(jnp.sqrt(_colsum(kk * kk)), N), 1e-12)
        nkk = -kk
        kka = kk * a
        k = k_raw * (1.0 + (a - 1.0) * ka_ref[...])

        def vgroup(g, c2):
            yacc = jnp.zeros((SUBLANES, LANES), F32)
            for j in range(SUBLANES):
                vi = g * SUBLANES + j
                S = s_scr[vi]
                sa = _bcast_rows(_colsum(S * nkk), N)
                vb = jnp.broadcast_to(v_ref[t, pl.ds(vi, 1), :], (N, LANES))
                S = S * w + sa * kka + vb * k
                s_scr[vi] = S
                yacc = jnp.where(sub == j, _colsum(S * r), yacc)
            y_ref[t, pl.ds(pl.multiple_of(g * SUBLANES, SUBLANES), SUBLANES), :] = yacc
            return c2

        lax.fori_loop(0, N // SUBLANES, vgroup, 0, unroll=True)

        y = y_ref[t]
        d = y - _bcast_rows(_colsum(y), N) * (1.0 / N)
        var = _bcast_rows(_colsum(d * d), N) * (1.0 / N)
        bonus = _bcast_rows(_colsum(r * k * rk_ref[...]), N)
        y_ref[t] = d * lax.rsqrt(var + GN_EPS) * gng_ref[...] + gnb_ref[...] + bonus * v_ref[t]
        return carry

    lax.fori_loop(0, r_ref.shape[0], step, 0)

    @pl.when(tc == pl.num_programs(1) - 1)
    def _():
        sT_ref[...] = s_scr[...]


def rwkv_scan(r, w, k, v, a, k_k, k_a, gn_g, gn_b, r_k, s0):
    T, N, P = r.shape
    tc = _pick(T, (64, 32, 16, 8))
    blk = pl.BlockSpec((tc, N, LANES), lambda p, t: (t, 0, p))
    par = pl.BlockSpec((N, LANES), lambda p, t: (0, p))
    st = pl.BlockSpec((N, N, LANES), lambda p, t: (0, 0, p))
    return pl.pallas_call(
        _rwkv_kernel,
        grid=(P // LANES, T // tc),
        in_specs=[blk] * 5 + [par] * 5 + [st],
        out_specs=[blk, st],
        out_shape=[jax.ShapeDtypeStruct((T, N, P), F32), jax.ShapeDtypeStruct((N, N, P), F32)],
        scratch_shapes=[pltpu.VMEM((N, N, LANES), F32)],
        compiler_params=_params(("arbitrary", "arbitrary")),
    )(r, w, k, v, a, k_k, k_a, gn_g, gn_b, r_k, s0)


MASKED = -1e30


KEY_NEG_INF = -2139095041
QK_SPLIT = 4


def _tree_sum(parts):
    while len(parts) > 1:
        parts = [parts[i] + parts[i + 1] for i in range(0, len(parts) - 1, 2)] + parts[len(parts) & ~1:]
    return parts[0]


def _dsa_kernel(q_ref, qi_ref, wi_ref, k_ref, vt_ref, ki_ref, o_ref,
                key_scr, m_scr, l_scr, acc_scr, *, past, n_sel, scale, tk):
    qb = pl.program_id(1)
    QB = q_ref.shape[1]
    L = k_ref.shape[1]
    NP = N_HEADS_C // 2
    lane_chunk = lax.broadcasted_iota(jnp.int32, (1, QB), 1) // CHUNK
    limit = past + (qb * (QB // CHUNK) + lane_chunk + 1) * CHUNK
    n_tiles = (past + (qb + 1) * QB + tk - 1) // tk
    u = 2 if (tk < 512 and (L // tk) % 2 == 0) else 1
    n_count = (n_tiles + u - 1) // u
    row = lax.broadcasted_iota(jnp.int32, (tk, QB), 0)
    tile = lambda kt: pl.ds(pl.multiple_of(kt * tk, tk), tk)

    qi = qi_ref[0].astype(BF16)
    qi_pair = [jnp.concatenate([qi[:, (2 * p) * D_IDX:(2 * p + 1) * D_IDX],
                                qi[:, (2 * p + 1) * D_IDX:(2 * p + 2) * D_IDX]], axis=0)
               for p in range(N_IDX_HEADS // 2)]
    wi = wi_ref[0]

    def score_tile(kt, c):
        @pl.when(kt < n_tiles)
        def _():
            kidx = ki_ref[0, tile(kt), :]
            terms = []
            for p in range(N_IDX_HEADS // 2):
                s = lax.dot_general(kidx, qi_pair[p], (((1,), (1,)), ((), ())),
                                    preferred_element_type=F32)
                for e in range(2):
                    h = 2 * p + e
                    terms.append(wi[h:h + 1, :] * jnp.maximum(s[:, e * QB:(e + 1) * QB], 0.0))
            isc = terms[0]
            for t in terms[1:]:
                isc = isc + t
            ok = row + kt * tk < limit
            bits = lax.bitcast_convert_type(jnp.where(ok, isc + 0.0, -jnp.inf), jnp.int32)
            key_scr[tile(kt), :] = bits ^ ((bits >> 31) & jnp.int32(0x7FFFFFFF))

        @pl.when(kt >= n_tiles)
        def _():
            key_scr[tile(kt), :] = jnp.full((tk, QB), KEY_NEG_INF, jnp.int32)

        return c

    lax.fori_loop(0, n_count * u, score_tile, 0)

    int_min = jnp.int32(-2 ** 31)
    ct = tk * u

    def search(i, t):
        cand = t | lax.shift_left(jnp.int32(1), 31 - i)
        cs = cand ^ int_min

        def count_tile(kt, acc):
            keys = key_scr[pl.ds(pl.multiple_of(kt * ct, ct), ct), :]
            hit = jnp.where(keys >= cs, 1.0, 0.0)
            return acc + _tree_sum([hit[r * SUBLANES:(r + 1) * SUBLANES] for r in range(ct // SUBLANES)])

        acc = lax.fori_loop(0, n_count, count_tile, jnp.zeros((SUBLANES, QB), F32))
        cnt = jnp.sum(acc, axis=0, keepdims=True)
        return jnp.where(cnt >= n_sel, cand, t)

    thr = lax.fori_loop(0, 32, search, jnp.zeros((1, QB), jnp.int32)) ^ int_min

    G = N_HEADS_C // N_KV_C
    dh = HEAD_DIM_C
    q = q_ref[0]
    q_pair = [(jnp.concatenate([q[:, (2 * p) * dh:(2 * p + 1) * dh],
                                q[:, (2 * p + 1) * dh:(2 * p + 2) * dh]], axis=0) * scale).astype(BF16)
              for p in range(NP)]
    m_scr[...] = jnp.full(m_scr.shape, MASKED, F32)
    l_scr[...] = jnp.zeros(l_scr.shape, F32)
    acc_scr[...] = jnp.zeros(acc_scr.shape, F32)

    def att_tile(kt, c):
        sel = (key_scr[tile(kt), :] >= thr) & (row + kt * tk < limit)
        b = jnp.where(sel, 0.0, MASKED)
        b2 = jnp.concatenate([b, b], axis=1)
        def logits(p):
            j = (2 * p) // G
            kj = k_ref[0, tile(kt), j * dh:(j + 1) * dh]
            part = tk // QK_SPLIT
            return jnp.concatenate(
                [lax.dot_general(kj[r * part:(r + 1) * part], q_pair[p], (((1,), (1,)), ((), ())),
                                 preferred_element_type=F32) for r in range(QK_SPLIT)], axis=0)

        ahead = 4
        lg = {p: logits(p) for p in range(ahead)}
        for p in range(NP):
            j = (2 * p) // G
            x = lg.pop(p) + b2
            m = m_scr[p:p + 1, :]
            m_new = jnp.maximum(m, jnp.max(x, axis=0, keepdims=True))
            alpha = jnp.exp2(m - m_new)
            pr = jnp.exp2(x - m_new)
            l_scr[p:p + 1, :] = alpha * l_scr[p:p + 1, :] + jnp.sum(pr, axis=0, keepdims=True)
            m_scr[p:p + 1, :] = m_new
            vt = vt_ref[0, kt, j * dh:(j + 1) * dh, :]
            pv = jnp.dot(vt, pr.astype(BF16), preferred_element_type=F32)
            if p + ahead < NP:
                lg[p + ahead] = logits(p + ahead)
            acc_scr[p] = acc_scr[p] * alpha + pv
        return c

    lax.fori_loop(0, n_tiles, att_tile, 0)
    for p in range(NP):
        o2 = acc_scr[p] / l_scr[p:p + 1, :]
        for s in range(2):
            o_ref[0, :, (2 * p + s) * dh:(2 * p + s + 1) * dh] = o2[:, s * QB:(s + 1) * QB].T


def dsa_attention(z, q_col, qi_col, wi_t, keys, vals_t, kidx, past, n_sel, tk):
    B, T, _ = z.shape
    L = keys.shape[1]
    QB = LANES
    dq = N_HEADS_C * HEAD_DIM_C
    dqi = N_IDX_HEADS * D_IDX
    kern = functools.partial(_dsa_kernel, past=past, n_sel=n_sel,
                             scale=HEAD_DIM_C ** -0.5 * math.log2(math.e), tk=tk)
    return pl.pallas_call(
        kern,
        grid=(B, T // QB),
        in_specs=[pl.BlockSpec((1, QB, dq), lambda b, c: (b, c, q_col)),
                  pl.BlockSpec((1, QB, dqi), lambda b, c: (b, c, qi_col)),
                  pl.BlockSpec((1, N_IDX_HEADS, QB), lambda b, c: (b, 0, c)),
                  pl.BlockSpec((1, L, keys.shape[2]), lambda b, c: (b, 0, 0)),
                  pl.BlockSpec((1,) + vals_t.shape[1:], lambda b, c: (b, 0, 0, 0)),
                  pl.BlockSpec((1, L, kidx.shape[2]), lambda b, c: (b, 0, 0))],
        out_specs=pl.BlockSpec((1, QB, dq), lambda b, c: (b, c, 0)),
        out_shape=jax.ShapeDtypeStruct((B, T, dq), F32),
        scratch_shapes=[pltpu.VMEM((L, QB), jnp.int32),
                        pltpu.VMEM((N_HEADS_C // 2, 2 * QB), F32),
                        pltpu.VMEM((N_HEADS_C // 2, 2 * QB), F32),
                        pltpu.VMEM((N_HEADS_C // 2, HEAD_DIM_C, 2 * QB), F32)],
        compiler_params=_params(("arbitrary", "arbitrary")),
    )(z, z, wi_t, keys, vals_t, kidx)


def _route_tokens(x, w, bias):
    def split(t):
        hi = t.astype(BF16)
        return hi, (t - hi.astype(F32)).astype(BF16)

    nt = lambda a, c: lax.dot_general(a, c, (((1,), (1,)), ((), ())), preferred_element_type=F32)
    (w_hi, w_lo), (x_hi, x_lo) = split(w), split(x)
    logits = nt(w_hi, x_hi) + (nt(w_hi, x_lo) + nt(w_lo, x_hi))
    scores = jax.nn.sigmoid(logits)
    biased = scores + bias
    tm = scores.shape[1]
    P = EXPERTS_PER_GROUP

    def ranks(rows):
        out = []
        for i in range(len(rows)):
            rk = jnp.zeros((1, tm), jnp.int32)
            for j in range(len(rows)):
                if j == i:
                    continue
                beat = (rows[j] > rows[i]) | ((rows[j] == rows[i]) & (j < i))
                rk = rk + beat.astype(jnp.int32)
            out.append(rk)
        return out

    b = [biased[i:i + 1] for i in range(N_EXPERTS)]
    s = [scores[i:i + 1] for i in range(N_EXPERTS)]
    grp_rank, grp_score = [], []
    for g in range(N_GROUPS):
        rows = b[g * P:(g + 1) * P]
        rk = ranks(rows)
        grp_rank.append(rk)
        top1 = sum(jnp.where(rk[i] == 0, rows[i], 0.0) for i in range(P))
        top2 = sum(jnp.where(rk[i] == 1, rows[i], 0.0) for i in range(P))
        grp_score.append(top1 + top2)
    gr = ranks(grp_score)
    e1 = jnp.zeros((1, tm), jnp.int32)
    e2 = jnp.zeros((1, tm), jnp.int32)
    w1 = jnp.zeros((1, tm), F32)
    w2 = jnp.zeros((1, tm), F32)
    for g in range(N_GROUPS):
        best = gr[g] == 0
        for i in range(P):
            is1 = best & (grp_rank[g][i] == 0)
            is2 = best & (grp_rank[g][i] == 1)
            e1 = jnp.where(is1, g * P + i, e1)
            e2 = jnp.where(is2, g * P + i, e2)
            w1 = jnp.where(is1, s[g * P + i], w1)
            w2 = jnp.where(is2, s[g * P + i], w2)
    tot = w1 + w2
    return jnp.concatenate([e1, e2], axis=0), jnp.concatenate([w1 / tot, w2 / tot], axis=0)


def _router_kernel(x_ref, w_ref, b_ref, e_ref, g_ref):
    e_ref[...], g_ref[...] = _route_tokens(x_ref[...], w_ref[...], b_ref[...])


def route(xt, router_w, router_b):
    T, D = xt.shape
    tm = _pick(T, (512, 256, 128))
    return pl.pallas_call(
        _router_kernel,
        grid=(T // tm,),
        in_specs=[pl.BlockSpec((tm, D), lambda i: (i, 0)),
                  pl.BlockSpec((N_EXPERTS, D), lambda i: (0, 0)),
                  pl.BlockSpec((N_EXPERTS, 1), lambda i: (0, 0))],
        out_specs=[pl.BlockSpec((TOP_K, tm), lambda i: (0, i)),
                   pl.BlockSpec((TOP_K, tm), lambda i: (0, i))],
        out_shape=[jax.ShapeDtypeStruct((TOP_K, T), jnp.int32),
                   jax.ShapeDtypeStruct((TOP_K, T), F32)],
        compiler_params=_params(("arbitrary",)),
    )(xt, router_w.T, router_b.reshape(N_EXPERTS, 1))


def _ffn_kernel(be_ref, x_ref, wg_ref, wu_ref, wd_ref, o_ref):
    i = pl.program_id(0)
    used = be_ref[pl.num_programs(0)]

    @pl.when(i < used)
    def _():
        x = x_ref[...].astype(BF16)
        a = jnp.dot(x, wg_ref[0, 0], preferred_element_type=F32)
        u = jnp.dot(x, wu_ref[0, 0], preferred_element_type=F32)
        h = (a * jax.nn.sigmoid(a)) * u
        o_ref[...] = jnp.dot(h.astype(BF16), wd_ref[0, 0], preferred_element_type=F32)

    @pl.when(i >= used)
    def _():
        o_ref[...] = jnp.zeros(o_ref.shape, F32)


def _cast_kernel(x_ref, o_ref):
    o_ref[...] = x_ref[...].astype(o_ref.dtype)


def cast_bf16(w):
    n = w.shape[-1]
    w2 = w.reshape(-1, n)
    rows = w2.shape[0]
    tr = _pick(rows, (1024 * 1024 // n, 512, 256, 128, 64, 32, 16))
    out = pl.pallas_call(
        _cast_kernel,
        grid=(rows // tr,),
        in_specs=[pl.BlockSpec((tr, n), lambda i: (i, 0))],
        out_specs=pl.BlockSpec((tr, n), lambda i: (i, 0)),
        out_shape=jax.ShapeDtypeStruct(w2.shape, BF16),
        compiler_params=_params(("arbitrary",)),
    )(w2)
    return out.reshape(w.shape)


def expert_ffn(xs, block_exp, layer, wg, wu, wd):
    R, D = xs.shape
    F = wg.shape[3]
    nb = R // MOE_TM
    grid_spec = pltpu.PrefetchScalarGridSpec(
        num_scalar_prefetch=1,
        grid=(nb,),
        in_specs=[pl.BlockSpec((MOE_TM, D), lambda i, be: (i, 0)),
                  pl.BlockSpec((1, 1, D, F), lambda i, be: (layer, be[i], 0, 0)),
                  pl.BlockSpec((1, 1, D, F), lambda i, be: (layer, be[i], 0, 0)),
                  pl.BlockSpec((1, 1, F, D), lambda i, be: (layer, be[i], 0, 0))],
        out_specs=pl.BlockSpec((MOE_TM, D), lambda i, be: (i, 0)),
    )
    return pl.pallas_call(
        _ffn_kernel,
        grid_spec=grid_spec,
        out_shape=jax.ShapeDtypeStruct((R, D), F32),
        compiler_params=_params(("arbitrary",)),
    )(block_exp, xs, wg, wu, wd)


def _combine_ln_kernel(ya_ref, yb_ref, ga_ref, gb_ref, r_ref, g_ref, b_ref, o_ref):
    y = ya_ref[...] * ga_ref[...] + yb_ref[...] * gb_ref[...]
    o_ref[...] = _deepnorm(r_ref[...], y, g_ref[...], b_ref[...])


def combine_ln(ya, yb, ga, gb, resid, g, b):
    M, N = ya.shape
    tm = _pick(M, (256, 128, 64, 32, 16, 8))
    blk = pl.BlockSpec((tm, N), lambda i: (i, 0))
    col = pl.BlockSpec((tm, 1), lambda i: (i, 0))
    row = pl.BlockSpec((1, N), lambda i: (0, 0))
    return pl.pallas_call(
        _combine_ln_kernel,
        grid=(M // tm,),
        in_specs=[blk, blk, col, col, blk, row, row],
        out_specs=blk,
        out_shape=jax.ShapeDtypeStruct((M, N), F32),
        compiler_params=_params(("arbitrary",)),
    )(ya, yb, ga, gb, resid, g.reshape(1, N), b.reshape(1, N))


def moe_layer(x, router_w, router_b, layer, wg, wu, wd, ln_g, ln_b):
    shp = x.shape
    xt = x.reshape(-1, shp[-1])
    T = xt.shape[0]
    expert, gate = route(xt, router_w, router_b)
    flat_e = expert.T.reshape(-1)
    n_assign = T * TOP_K
    onehot = (flat_e[:, None] == jnp.arange(N_EXPERTS, dtype=jnp.int32)[None]).astype(jnp.int32)
    csum = jnp.cumsum(onehot, axis=0)
    rank = jnp.take_along_axis(csum, flat_e[:, None], axis=1)[:, 0] - 1
    counts = csum[-1]
    padded = (counts + MOE_TM - 1) // MOE_TM * MOE_TM
    pad_end = jnp.cumsum(padded)
    pad_start = pad_end - padded
    dest = pad_start[flat_e] + rank
    nb = -(-n_assign // MOE_TM) + N_EXPERTS
    rows = nb * MOE_TM
    row_tok = jnp.zeros(rows, jnp.int32).at[dest].set(jnp.arange(n_assign, dtype=jnp.int32) // TOP_K)
    first_row = jnp.arange(nb, dtype=jnp.int32) * MOE_TM
    block_exp = jnp.minimum(jnp.sum((pad_end[None, :] <= first_row[:, None]).astype(jnp.int32), axis=1),
                            N_EXPERTS - 1)
    block_exp = jnp.concatenate([block_exp, pad_end[-1:] // MOE_TM])
    yb = expert_ffn(xt[row_tok], block_exp, layer, wg, wu, wd)
    d2 = dest.reshape(T, TOP_K)
    out = combine_ln(yb[d2[:, 0]], yb[d2[:, 1]], gate[0][:, None], gate[1][:, None], xt, ln_g, ln_b)
    return out.reshape(shp)


def _block_diag(w):
    G, I, O = w.shape
    eye = jnp.eye(G, dtype=w.dtype)
    return (eye[:, None, :, None] * w[:, :, None, :]).reshape(G * I, G * O)


def even_layer(x, lru_h, lru_conv, wkv, shift, p, ln_g, ln_b):
    B, T, D = x.shape
    D_LRU = lru_h.shape[-1]
    D_RWKV = p['w0'].shape[1]
    H = D_RWKV // RWKV_HEAD
    N = RWKV_HEAD
    z = matmul(x.reshape(B * T, D), p['w_in']).reshape(B, T, -1)
    y_a, r, decay, k, v, a, g, h_T = even_pre(z, lru_conv, shift, lru_h, p)

    lanes = lambda t: jnp.transpose(t.reshape(B, T, H, N), (1, 3, 0, 2)).reshape(T, N, B * H)
    per_chain = lambda w: jnp.tile(w.reshape(H, N).T, (1, B))
    s0 = jnp.transpose(wkv, (2, 3, 0, 1)).reshape(N, N, B * H)
    y, s_T = rwkv_scan(lanes(r), lanes(decay), lanes(k), lanes(v), lanes(a),
                       per_chain(p['k_k']), per_chain(p['k_a']), per_chain(p['gn_g']),
                       per_chain(p['gn_b']), per_chain(p['r_k']), s0)
    y = jnp.transpose(y.reshape(T, N, B, H), (2, 0, 3, 1)).reshape(B * T, D_RWKV)
    S_T = jnp.transpose(s_T.reshape(N, N, B, H), (2, 3, 0, 1))

    x_new = mix_ln(y_a.reshape(B * T, D_LRU), y, g.reshape(B * T, D_RWKV), p['w_out'],
                   x.reshape(B * T, D), ln_g, ln_b)
    conv_state = jnp.concatenate([lru_conv, z[..., :D_LRU]], axis=1)[:, T:]
    return x_new.reshape(B, T, D), (h_T[:, 0], conv_state, S_T, z[:, -1:, 2 * D_LRU:])


def odd_layer(x, cache, p, ln_g, ln_b):
    B, T, D = x.shape
    dq = N_HEADS_C * HEAD_DIM_C
    dkv = N_KV_C * HEAD_DIM_C
    dqi = N_IDX_HEADS * D_IDX
    z = matmul(x.reshape(B * T, D), p['w_in']).reshape(B, T, -1)
    k = z[..., dq:dq + dkv]
    v = z[..., dq + dkv:dq + 2 * dkv]
    o = dq + 2 * dkv + dqi
    ki = z[..., o:o + D_IDX]
    wi = z[..., o + D_IDX:o + D_IDX + N_IDX_HEADS] * (dqi ** -0.5)
    Tp = -(-T // LANES) * LANES
    if cache is None:
        past = 0
        n_sel = min(TOPK_MAX, T // 4)
        parts = lambda c, n: [n]
    else:
        past = cache[0].shape[1]
        n_sel = min(TOPK_MAX, (past + T) // 4)
        parts = lambda c, n: [c.reshape(B, past, -1), n]
    L = -(-(past + Tp) // LANES) * LANES
    tk = _pick(L, (512, 384, 128))
    cat = lambda c, n: jnp.concatenate(
        parts(c, n) + [jnp.zeros((B, L - past - T, n.shape[-1]), n.dtype)], axis=1).astype(BF16)
    ck, cv, cki = cache if cache is not None else (None, None, None)
    keys, vals, kidx = cat(ck, k), cat(cv, v), cat(cki, ki)
    vals_t = jnp.swapaxes(vals.reshape(B, L // tk, tk, dkv), 2, 3)
    rowpad = lambda t: jnp.pad(t, ((0, 0), (0, Tp - T), (0, 0)))
    wi_t = jnp.swapaxes(rowpad(wi), 1, 2)
    o_att = dsa_attention(rowpad(z), 0, (dq + 2 * dkv) // dqi, wi_t, keys, vals_t, kidx, past, n_sel, tk)
    x_new = matmul_ln(o_att[:, :T].reshape(B * T, dq), p['w_out'], x.reshape(B * T, D), ln_g, ln_b)
    new = (k.reshape(B, T, N_KV_C, HEAD_DIM_C), v.reshape(B, T, N_KV_C, HEAD_DIM_C), ki)
    return x_new.reshape(B, T, D), new


def run_trunk(x, even_state, odd_cache, ev, od, norm_w, moe_w):
    ln1_g, ln1_b, ln2_g, ln2_b = norm_w
    router_w, router_b, wg, wu, wd = moe_w
    x, new_even = even_layer(x, *even_state, ev, ln1_g[0], ln1_b[0])
    x = moe_layer(x, router_w, router_b, 0, wg, wu, wd, ln2_g[0], ln2_b[0])
    x, new_odd = odd_layer(x, odd_cache, od, ln1_g[1], ln1_b[1])
    x = moe_layer(x, router_w, router_b, 1, wg, wu, wd, ln2_g[1], ln2_b[1])
    return x, new_even, new_odd


def kernel(x_prompt, x_sample, state_lru_h, state_lru_conv, state_rwkv_wkv, state_rwkv_shift, cache_k, cache_v, cache_kidx, ev_w_in, ev_conv_w, ev_conv_b, ev_gate_r_w, ev_gate_r_b, ev_gate_i_w, ev_gate_i_b, ev_lambda, ev_shift_mu, ev_w0, ev_w2, ev_a0, ev_a2, ev_g2, ev_k_k, ev_k_a, ev_r_k, ev_gn_g, ev_gn_b, ev_w_out, od_w_in, od_w_out, ln1_g, ln1_b, ln2_g, ln2_b, router_w, router_b, moe_w_gate, moe_w_up, moe_w_down):
    ev = dict(
        w_in=ev_w_in[0].astype(BF16), conv_w=ev_conv_w[0], conv_b=ev_conv_b[0][None],
        gate_w=jnp.concatenate([_block_diag(ev_gate_r_w[0]), _block_diag(ev_gate_i_w[0])], axis=1).astype(BF16),
        gate_b=jnp.concatenate([ev_gate_r_b[0], ev_gate_i_b[0]])[None],
        lru_c=(-LRU_C * jax.nn.softplus(-ev_lambda[0]))[None],
        shift_mu=ev_shift_mu[0][None], w0=ev_w0[0][None], a0=ev_a0[0][None],
        w2=ev_w2[0].astype(BF16), a2=ev_a2[0].astype(BF16), g2=ev_g2[0].astype(BF16),
        k_k=ev_k_k[0][None], k_a=ev_k_a[0][None], r_k=ev_r_k[0], gn_g=ev_gn_g[0], gn_b=ev_gn_b[0],
        w_out=ev_w_out[0].astype(BF16))
    n_in = od_w_in.shape[2]
    od = dict(w_in=jnp.pad(od_w_in[0], ((0, 0), (0, (-n_in) % LANES))).astype(BF16),
              w_out=od_w_out[0].astype(BF16))
    norm_w = (ln1_g, ln1_b, ln2_g, ln2_b)
    moe_w = (router_w, router_b, cast_bf16(moe_w_gate), cast_bf16(moe_w_up), cast_bf16(moe_w_down))

    B = x_prompt.shape[0]
    fresh = (jnp.zeros((B,) + state_lru_h.shape[2:], F32),
             jnp.zeros((B,) + state_lru_conv.shape[2:], F32),
             jnp.zeros((B,) + state_rwkv_wkv.shape[2:], F32),
             jnp.zeros((B,) + state_rwkv_shift.shape[2:], F32))
    y_p, pe, po = run_trunk(x_prompt, fresh, None, ev, od, norm_w, moe_w)
    y_s, se, so = run_trunk(
        x_sample, (state_lru_h[0], state_lru_conv[0], state_rwkv_wkv[0], state_rwkv_shift[0]),
        (cache_k[0], cache_v[0], cache_kidx[0]), ev, od, norm_w, moe_w)
    st = lambda t: t[None]
    return (y_p, y_s, *map(st, pe), *map(st, po), *map(st, se), *map(st, so))
```

```python
import functools
import math

import jax
import jax.numpy as jnp
from jax import lax
from jax.experimental import pallas as pl
from jax.experimental.pallas import tpu as pltpu

F32 = jnp.float32
BF16 = jnp.bfloat16

CHUNK = 64
CONV_W = 4
LRU_C = 8.0
RWKV_HEAD = 64
DECAY_LORA = 64
AAA_LORA = 64
GATE_LORA = 128
N_HEADS_C = 16
HEAD_DIM_C = 128
N_KV_C = 4
N_IDX_HEADS = 16
D_IDX = 64
TOPK_MAX = 256
N_EXPERTS = 16
N_GROUPS = 4
EXPERTS_PER_GROUP = N_EXPERTS // N_GROUPS
TOP_K = 2
DEPTH = 2
DN_ALPHA = (2 * DEPTH) ** 0.25
LN_EPS = 1e-5
GN_EPS = 64e-5

LANES = 128
SUBLANES = 8
VMEM_LIMIT = 56 * 1024 * 1024
MOE_TM = 256


def _params(sem):
    return pltpu.CompilerParams(dimension_semantics=sem, vmem_limit_bytes=VMEM_LIMIT)


def _mm_kernel(x_ref, w_ref, o_ref, xb_ref):
    @pl.when(pl.program_id(1) == 0)
    def _():
        xb_ref[...] = x_ref[...].astype(BF16)

    o_ref[...] = jnp.dot(xb_ref[...], w_ref[...], preferred_element_type=F32)


def _pick(n, cands):
    for c in cands:
        if n % c == 0:
            return c
    return n


def matmul(x, w):
    M, K = x.shape
    N = w.shape[1]
    tm = _pick(M, (1024, 512, 256, 128, 64, 32, 16, 8))
    tn = _pick(N, (1792, 1408, 1024, 768, 512, 384, 256, 128))
    return pl.pallas_call(
        _mm_kernel,
        grid=(M // tm, N // tn),
        in_specs=[pl.BlockSpec((tm, K), lambda i, j: (i, 0)),
                  pl.BlockSpec((K, tn), lambda i, j: (0, j))],
        out_specs=pl.BlockSpec((tm, tn), lambda i, j: (i, j)),
        out_shape=jax.ShapeDtypeStruct((M, N), F32),
        scratch_shapes=[pltpu.VMEM((tm, K), BF16)],
        compiler_params=_params(("arbitrary", "arbitrary")),
    )(x, w)


def _deepnorm(resid, y, g, b):
    h = DN_ALPHA * resid + y
    mu = jnp.mean(h, axis=-1, keepdims=True)
    hc = h - mu
    var = jnp.mean(hc * hc, axis=-1, keepdims=True)
    return hc * lax.rsqrt(var + LN_EPS) * g + b


def _mm_ln_kernel(x_ref, w_ref, r_ref, g_ref, b_ref, o_ref):
    y = jnp.dot(x_ref[...].astype(BF16), w_ref[...], preferred_element_type=F32)
    o_ref[...] = _deepnorm(r_ref[...], y, g_ref[...], b_ref[...])


def _mix_ln_kernel(ya_ref, yb_ref, gate_ref, w_ref, r_ref, g_ref, b_ref, o_ref):
    mix = jnp.concatenate([ya_ref[...], yb_ref[...] * gate_ref[...]], axis=1).astype(BF16)
    y = jnp.dot(mix, w_ref[...], preferred_element_type=F32)
    o_ref[...] = _deepnorm(r_ref[...], y, g_ref[...], b_ref[...])


def _ln_call(kern, lhs, w, resid, g, b):
    M, N = resid.shape
    tm = _pick(M, (512, 256, 128, 64, 32, 16, 8))
    full = lambda a: pl.BlockSpec(a.shape, lambda i: (0,) * a.ndim)
    rows = lambda a: pl.BlockSpec((tm, a.shape[1]), lambda i: (i, 0))
    consts = [g.reshape(1, N), b.reshape(1, N)]
    return pl.pallas_call(
        kern,
        grid=(M // tm,),
        in_specs=[rows(a) for a in lhs] + [full(w), rows(resid)] + [full(c) for c in consts],
        out_specs=rows(resid),
        out_shape=jax.ShapeDtypeStruct((M, N), F32),
        compiler_params=_params(("arbitrary",)),
    )(*lhs, w, resid, *consts)


def matmul_ln(x, w, resid, g, b):
    return _ln_call(_mm_ln_kernel, [x], w, resid, g, b)


def mix_ln(ya, yb, gate, w, resid, g, b):
    return _ln_call(_mix_ln_kernel, [ya, yb, gate], w, resid, g, b)


def _expm1(x):
    e = jnp.exp(x)
    safe = jnp.where((e == 1.0) | (e == 0.0), 2.0, e)
    return jnp.where(e == 1.0, x, jnp.where(e == 0.0, -1.0, (e - 1.0) * x / jnp.log(safe)))


def _softplus(x):
    return jnp.maximum(x, 0.0) + jnp.log(1.0 + jnp.exp(-jnp.abs(x)))


HALO = SUBLANES


def _even_pre_kernel(ax_ref, ag_ref, zr_ref, zk_ref, zv_ref, zt_ref,
                     conv_ref, sr_ref, sk_ref, sv_ref, st_ref, h0_ref,
                     cw_ref, cb_ref, gw_ref, gb_ref, lc_ref, mu_ref,
                     w0_ref, a0_ref, w2_ref, a2_ref, g2_ref,
                     ya_ref, r_ref, w_ref, k_ref, v_ref, a_ref, g_ref, hT_ref,
                     xp_scr, pr_scr, pk_scr, pv_scr, pt_scr, a_scr, u_scr, hs_scr, h_scr):
    i = pl.program_id(1)
    tm = ax_ref.shape[1]
    C = ax_ref.shape[2]
    prev = ((xp_scr, None), (pr_scr, sr_ref), (pk_scr, sk_ref), (pv_scr, sv_ref), (pt_scr, st_ref))

    @pl.when(i == 0)
    def _():
        xp_scr[...] = jnp.zeros((HALO, C), F32)
        xp_scr[HALO - (CONV_W - 1):HALO, :] = conv_ref[0]
        for scr, ref in prev[1:]:
            scr[...] = jnp.broadcast_to(ref[0], scr.shape)
        h_scr[...] = jnp.broadcast_to(h0_ref[0], h_scr.shape)

    def earlier(x, scr, d):
        rot = pltpu.roll(x, d, axis=0)
        halo = pltpu.roll(scr[...], d, axis=0)
        sub = lax.broadcasted_iota(jnp.int32, halo.shape, 0)
        return jnp.concatenate([jnp.where(sub < d, halo, rot[:HALO]), rot[HALO:]], axis=0)

    ax = ax_ref[0]
    xa = cb_ref[...] + cw_ref[CONV_W - 1:CONV_W, :] * ax
    for d in range(1, CONV_W):
        xa = xa + cw_ref[CONV_W - 1 - d:CONV_W - d, :] * earlier(ax, xp_scr, d)
    gates = jnp.dot(xa.astype(BF16), gw_ref[...], preferred_element_type=F32) + gb_ref[...]
    log_a = lc_ref[...] * jax.nn.sigmoid(gates[:, :C])
    a_scr[...] = jnp.exp(log_a)
    u_scr[...] = jnp.sqrt(-_expm1(2.0 * log_a)) * (jax.nn.sigmoid(gates[:, C:]) * xa)

    def step(t, h):
        h = a_scr[pl.ds(t, 1), :] * h + u_scr[pl.ds(t, 1), :]
        hs_scr[pl.ds(t, 1), :] = h
        return h

    h = lax.fori_loop(0, tm, step, h_scr[0:1, :], unroll=8)
    h_scr[...] = jnp.broadcast_to(h, h_scr.shape)
    hT_ref[0] = h
    ya_ref[0] = hs_scr[...] * jax.nn.gelu(ag_ref[0], approximate=True)

    zr, zk, zv, zt = zr_ref[0], zk_ref[0], zv_ref[0], zt_ref[0]

    def shifted(zb, scr, lo, hi):
        return zb + mu_ref[:, lo:hi] * (earlier(zb, scr, 1) - zb)

    r = shifted(zr, pr_scr, 0, C)
    k = shifted(zk, pk_scr, C, 2 * C)
    v = shifted(zv, pv_scr, 2 * C, 3 * C)
    tail = shifted(zt, pt_scr, 3 * C, 3 * C + pt_scr.shape[1])
    wl = jnp.tanh(tail[:, :DECAY_LORA])
    al = tail[:, DECAY_LORA:DECAY_LORA + AAA_LORA]
    gl = jax.nn.sigmoid(tail[:, DECAY_LORA + AAA_LORA:])
    lora = lambda x, w_ref_: jnp.dot(x.astype(BF16), w_ref_[...], preferred_element_type=F32)
    w_log = -_softplus(-(w0_ref[...] + lora(wl, w2_ref))) - 0.5
    a = jax.nn.sigmoid(a0_ref[...] + lora(al, a2_ref))
    r_ref[0] = r
    w_ref[0] = jnp.exp(-jnp.exp(w_log))
    k_ref[0] = k
    v_ref[0] = v
    a_ref[0] = a
    g_ref[0] = lora(gl, g2_ref)

    for (scr, _), x in zip(prev, (ax, zr, zk, zv, zt)):
        scr[...] = x[tm - HALO:]


def even_pre(z, lru_conv, shift, lru_h, p):
    B, T, _ = z.shape
    C = lru_h.shape[-1]
    tail = DECAY_LORA + AAA_LORA + GATE_LORA
    tm = _pick(T, (256, 128, 64, 32, 16, 8))
    zblk = lambda w, col: pl.BlockSpec((1, tm, w), lambda b, i: (b, i, col))
    sblk = lambda rows, w, col: pl.BlockSpec((1, rows, w), lambda b, i: (b, 0, col))
    full = lambda a: pl.BlockSpec(a.shape, lambda b, i: (0,) * a.ndim)
    weights = [p['conv_w'], p['conv_b'], p['gate_w'], p['gate_b'], p['lru_c'], p['shift_mu'],
               p['w0'], p['a0'], p['w2'], p['a2'], p['g2']]
    tok = jax.ShapeDtypeStruct((B, T, C), F32)
    return pl.pallas_call(
        _even_pre_kernel,
        grid=(B, T // tm),
        in_specs=[zblk(C, 0), zblk(C, 1), zblk(C, 2), zblk(C, 3), zblk(C, 4), zblk(tail, 5 * C // tail),
                  sblk(CONV_W - 1, C, 0), sblk(1, C, 0), sblk(1, C, 1), sblk(1, C, 2),
                  sblk(1, tail, 3 * C // tail), sblk(1, C, 0)] + [full(w) for w in weights],
        out_specs=[zblk(C, 0)] * 7 + [sblk(1, C, 0)],
        out_shape=[tok] * 7 + [jax.ShapeDtypeStruct((B, 1, C), F32)],
        scratch_shapes=[pltpu.VMEM((HALO, C), F32)] * 4 + [pltpu.VMEM((HALO, tail), F32)]
                       + [pltpu.VMEM((tm, C), F32)] * 3 + [pltpu.VMEM((SUBLANES, C), F32)],
        compiler_params=_params(("arbitrary", "arbitrary")),
    )(z, z, z, z, z, z, lru_conv, shift, shift, shift, shift, lru_h[:, None, :], *weights)


def _colsum(x):
    n = x.shape[0] // SUBLANES
    p = x.reshape(n, SUBLANES, x.shape[1]).sum(axis=0)
    p = p + pltpu.roll(p, 4, axis=0)
    p = p + pltpu.roll(p, 2, axis=0)
    p = p + pltpu.roll(p, 1, axis=0)
    return p


def _bcast_rows(p, n):
    return jnp.broadcast_to(p[None], (n // SUBLANES,) + p.shape).reshape(n, p.shape[1])


def _rwkv_kernel(r_ref, w_ref, k_ref, v_ref, a_ref, kk_ref, ka_ref, gng_ref, gnb_ref, rk_ref, s0_ref,
                 y_ref, sT_ref, s_scr):
    tc = pl.program_id(1)
    N = RWKV_HEAD

    @pl.when(tc == 0)
    def _():
        s_scr[...] = s0_ref[...]

    sub = lax.broadcasted_iota(jnp.int32, (SUBLANES, LANES), 0)

    def step(t, carry):
        r = r_ref[t]
        w = w_ref[t]
        k_raw = k_ref[t]
        a = a_ref[t]
        kk = k_raw * kk_ref[...]
        kk = kk / jnp.maximum(_bcast_rows(jnp.sqrt(_colsum(kk * kk)), N), 1e-12)
        nkk = -kk
        kka = kk * a
        k = k_raw * (1.0 + (a - 1.0) * ka_ref[...])

        def vgroup(g, c2):
            yacc = jnp.zeros((SUBLANES, LANES), F32)
            for j in range(SUBLANES):
                vi = g * SUBLANES + j
                S = s_scr[vi]
                sa = _bcast_rows(_colsum(S * nkk), N)
                vb = jnp.broadcast_to(v_ref[t, pl.ds(vi, 1), :], (N, LANES))
                S = S * w + sa * kka + vb * k
                s_scr[vi] = S
                yacc = jnp.where(sub == j, _colsum(S * r), yacc)
            y_ref[t, pl.ds(pl.multiple_of(g * SUBLANES, SUBLANES), SUBLANES), :] = yacc
            return c2

        lax.fori_loop(0, N // SUBLANES, vgroup, 0, unroll=True)

        y = y_ref[t]
        d = y - _bcast_rows(_colsum(y), N) * (1.0 / N)
        var = _bcast_rows(_colsum(d * d), N) * (1.0 / N)
        bonus = _bcast_rows(_colsum(r * k * rk_ref[...]), N)
        y_ref[t] = d * lax.rsqrt(var + GN_EPS) * gng_ref[...] + gnb_ref[...] + bonus * v_ref[t]
        return carry

    lax.fori_loop(0, r_ref.shape[0], step, 0)

    @pl.when(tc == pl.num_programs(1) - 1)
    def _():
        sT_ref[...] = s_scr[...]


def rwkv_scan(r, w, k, v, a, k_k, k_a, gn_g, gn_b, r_k, s0):
    T, N, P = r.shape
    tc = _pick(T, (64, 32, 16, 8))
    blk = pl.BlockSpec((tc, N, LANES), lambda p, t: (t, 0, p))
    par = pl.BlockSpec((N, LANES), lambda p, t: (0, p))
    st = pl.BlockSpec((N, N, LANES), lambda p, t: (0, 0, p))
    return pl.pallas_call(
        _rwkv_kernel,
        grid=(P // LANES, T // tc),
        in_specs=[blk] * 5 + [par] * 5 + [st],
        out_specs=[blk, st],
        out_shape=[jax.ShapeDtypeStruct((T, N, P), F32), jax.ShapeDtypeStruct((N, N, P), F32)],
        scratch_shapes=[pltpu.VMEM((N, N, LANES), F32)],
        compiler_params=_params(("arbitrary", "arbitrary")),
    )(r, w, k, v, a, k_k, k_a, gn_g, gn_b, r_k, s0)


MASKED = -1e30


KEY_NEG_INF = -2139095041
QK_SPLIT = 4


def _tree_sum(parts):
    while len(parts) > 1:
        parts = [parts[i] + parts[i + 1] for i in range(0, len(parts) - 1, 2)] + parts[len(parts) & ~1:]
    return parts[0]


def _dsa_kernel(q_ref, qi_ref, wi_ref, k_ref, vt_ref, ki_ref, o_ref,
                key_scr, m_scr, l_scr, acc_scr, *, past, n_sel, scale, tk):
    qb = pl.program_id(1)
    QB = q_ref.shape[1]
    L = k_ref.shape[1]
    NP = N_HEADS_C // 2
    lane_chunk = lax.broadcasted_iota(jnp.int32, (1, QB), 1) // CHUNK
    limit = past + (qb * (QB // CHUNK) + lane_chunk + 1) * CHUNK
    n_tiles = (past + (qb + 1) * QB + tk - 1) // tk
    u = 2 if (tk < 512 and (L // tk) % 2 == 0) else 1
    n_count = (n_tiles + u - 1) // u
    row = lax.broadcasted_iota(jnp.int32, (tk, QB), 0)
    tile = lambda kt: pl.ds(pl.multiple_of(kt * tk, tk), tk)

    qi = qi_ref[0].astype(BF16)
    qi_pair = [jnp.concatenate([qi[:, (2 * p) * D_IDX:(2 * p + 1) * D_IDX],
                                qi[:, (2 * p + 1) * D_IDX:(2 * p + 2) * D_IDX]], axis=0)
               for p in range(N_IDX_HEADS // 2)]
    wi = wi_ref[0]

    def score_tile(kt, c):
        @pl.when(kt < n_tiles)
        def _():
            kidx = ki_ref[0, tile(kt), :]
            terms = []
            for p in range(N_IDX_HEADS // 2):
                s = lax.dot_general(kidx, qi_pair[p], (((1,), (1,)), ((), ())),
                                    preferred_element_type=F32)
                for e in range(2):
                    h = 2 * p + e
                    terms.append(wi[h:h + 1, :] * jnp.maximum(s[:, e * QB:(e + 1) * QB], 0.0))
            isc = terms[0]
            for t in terms[1:]:
                isc = isc + t
            ok = row + kt * tk < limit
            bits = lax.bitcast_convert_type(jnp.where(ok, isc + 0.0, -jnp.inf), jnp.int32)
            key_scr[tile(kt), :] = bits ^ ((bits >> 31) & jnp.int32(0x7FFFFFFF))

        @pl.when(kt >= n_tiles)
        def _():
            key_scr[tile(kt), :] = jnp.full((tk, QB), KEY_NEG_INF, jnp.int32)

        return c

    lax.fori_loop(0, n_count * u, score_tile, 0)

    int_min = jnp.int32(-2 ** 31)
    ct = tk * u

    def search(i, t):
        cand = t | lax.shift_left(jnp.int32(1), 31 - i)
        cs = cand ^ int_min

        def count_tile(kt, acc):
            keys = key_scr[pl.ds(pl.multiple_of(kt * ct, ct), ct), :]
            hit = jnp.where(keys >= cs, 1.0, 0.0)
            return acc + _tree_sum([hit[r * SUBLANES:(r + 1) * SUBLANES] for r in range(ct // SUBLANES)])

        acc = lax.fori_loop(0, n_count, count_tile, jnp.zeros((SUBLANES, QB), F32))
        cnt = jnp.sum(acc, axis=0, keepdims=True)
        return jnp.where(cnt >= n_sel, cand, t)

    thr = lax.fori_loop(0, 32, search, jnp.zeros((1, QB), jnp.int32)) ^ int_min

    G = N_HEADS_C // N_KV_C
    dh = HEAD_DIM_C
    q = q_ref[0]
    q_pair = [(jnp.concatenate([q[:, (2 * p) * dh:(2 * p + 1) * dh],
                                q[:, (2 * p + 1) * dh:(2 * p + 2) * dh]], axis=0) * scale).astype(BF16)
              for p in range(NP)]
    m_scr[...] = jnp.full(m_scr.shape, MASKED, F32)
    l_scr[...] = jnp.zeros(l_scr.shape, F32)
    acc_scr[...] = jnp.zeros(acc_scr.shape, F32)

    def att_tile(kt, c):
        sel = (key_scr[tile(kt), :] >= thr) & (row + kt * tk < limit)
        b = jnp.where(sel, 0.0, MASKED)
        b2 = jnp.concatenate([b, b], axis=1)
        def logits(p):
            j = (2 * p) // G
            kj = k_ref[0, tile(kt), j * dh:(j + 1) * dh]
            part = tk // QK_SPLIT
            return jnp.concatenate(
                [lax.dot_general(kj[r * part:(r + 1) * part], q_pair[p], (((1,), (1,)), ((), ())),
                                 preferred_element_type=F32) for r in range(QK_SPLIT)], axis=0)

        ahead = 4
        lg = {p: logits(p) for p in range(ahead)}
        for p in range(NP):
            j = (2 * p) // G
            x = lg.pop(p) + b2
            m = m_scr[p:p + 1, :]
            m_new = jnp.maximum(m, jnp.max(x, axis=0, keepdims=True))
            alpha = jnp.exp2(m - m_new)
            pr = jnp.exp2(x - m_new)
            l_scr[p:p + 1, :] = alpha * l_scr[p:p + 1, :] + jnp.sum(pr, axis=0, keepdims=True)
            m_scr[p:p + 1, :] = m_new
            vt = vt_ref[0, kt, j * dh:(j + 1) * dh, :]
            pv = jnp.dot(vt, pr.astype(BF16), preferred_element_type=F32)
            if p + ahead < NP:
                lg[p + ahead] = logits(p + ahead)
            acc_scr[p] = acc_scr[p] * alpha + pv
        return c

    lax.fori_loop(0, n_tiles, att_tile, 0)
    for p in range(NP):
        o2 = acc_scr[p] / l_scr[p:p + 1, :]
        for s in range(2):
            o_ref[0, :, (2 * p + s) * dh:(2 * p + s + 1) * dh] = o2[:, s * QB:(s + 1) * QB].T


def dsa_attention(z, q_col, qi_col, wi_t, keys, vals_t, kidx, past, n_sel, tk):
    B, T, _ = z.shape
    L = keys.shape[1]
    QB = LANES
    dq = N_HEADS_C * HEAD_DIM_C
    dqi = N_IDX_HEADS * D_IDX
    kern = functools.partial(_dsa_kernel, past=past, n_sel=n_sel,
                             scale=HEAD_DIM_C ** -0.5 * math.log2(math.e), tk=tk)
    return pl.pallas_call(
        kern,
        grid=(B, T // QB),
        in_specs=[pl.BlockSpec((1, QB, dq), lambda b, c: (b, c, q_col)),
                  pl.BlockSpec((1, QB, dqi), lambda b, c: (b, c, qi_col)),
                  pl.BlockSpec((1, N_IDX_HEADS, QB), lambda b, c: (b, 0, c)),
                  pl.BlockSpec((1, L, keys.shape[2]), lambda b, c: (b, 0, 0)),
                  pl.BlockSpec((1,) + vals_t.shape[1:], lambda b, c: (b, 0, 0, 0)),
                  pl.BlockSpec((1, L, kidx.shape[2]), lambda b, c: (b, 0, 0))],
        out_specs=pl.BlockSpec((1, QB, dq), lambda b, c: (b, c, 0)),
        out_shape=jax.ShapeDtypeStruct((B, T, dq), F32),
        scratch_shapes=[pltpu.VMEM((L, QB), jnp.int32),
                        pltpu.VMEM((N_HEADS_C // 2, 2 * QB), F32),
                        pltpu.VMEM((N_HEADS_C // 2, 2 * QB), F32),
                        pltpu.VMEM((N_HEADS_C // 2, HEAD_DIM_C, 2 * QB), F32)],
        compiler_params=_params(("arbitrary", "arbitrary")),
    )(z, z, wi_t, keys, vals_t, kidx)


def _route_tokens(x, w, bias):
    def split(t):
        hi = t.astype(BF16)
        return hi, (t - hi.astype(F32)).astype(BF16)

    nt = lambda a, c: lax.dot_general(a, c, (((1,), (1,)), ((), ())), preferred_element_type=F32)
    (w_hi, w_lo), (x_hi, x_lo) = split(w), split(x)
    logits = nt(w_hi, x_hi) + (nt(w_hi, x_lo) + nt(w_lo, x_hi))
    scores = jax.nn.sigmoid(logits)
    biased = scores + bias
    tm = scores.shape[1]
    P = EXPERTS_PER_GROUP

    def ranks(rows):
        out = []
        for i in range(len(rows)):
            rk = jnp.zeros((1, tm), jnp.int32)
            for j in range(len(rows)):
                if j == i:
                    continue
                beat = (rows[j] > rows[i]) | ((rows[j] == rows[i]) & (j < i))
                rk = rk + beat.astype(jnp.int32)
            out.append(rk)
        return out

    b = [biased[i:i + 1] for i in range(N_EXPERTS)]
    s = [scores[i:i + 1] for i in range(N_EXPERTS)]
    grp_rank, grp_score = [], []
    for g in range(N_GROUPS):
        rows = b[g * P:(g + 1) * P]
        rk = ranks(rows)
        grp_rank.append(rk)
        top1 = sum(jnp.where(rk[i] == 0, rows[i], 0.0) for i in range(P))
        top2 = sum(jnp.where(rk[i] == 1, rows[i], 0.0) for i in range(P))
        grp_score.append(top1 + top2)
    gr = ranks(grp_score)
    e1 = jnp.zeros((1, tm), jnp.int32)
    e2 = jnp.zeros((1, tm), jnp.int32)
    w1 = jnp.zeros((1, tm), F32)
    w2 = jnp.zeros((1, tm), F32)
    for g in range(N_GROUPS):
        best = gr[g] == 0
        for i in range(P):
            is1 = best & (grp_rank[g][i] == 0)
            is2 = best & (grp_rank[g][i] == 1)
            e1 = jnp.where(is1, g * P + i, e1)
            e2 = jnp.where(is2, g * P + i, e2)
            w1 = jnp.where(is1, s[g * P + i], w1)
            w2 = jnp.where(is2, s[g * P + i], w2)
    tot = w1 + w2
    return jnp.concatenate([e1, e2], axis=0), jnp.concatenate([w1 / tot, w2 / tot], axis=0)


def _router_kernel(x_ref, w_ref, b_ref, e_ref, g_ref):
    e_ref[...], g_ref[...] = _route_tokens(x_ref[...], w_ref[...], b_ref[...])


def route(xt, router_w, router_b):
    T, D = xt.shape
    tm = _pick(T, (512, 256, 128))
    return pl.pallas_call(
        _router_kernel,
        grid=(T // tm,),
        in_specs=[pl.BlockSpec((tm, D), lambda i: (i, 0)),
                  pl.BlockSpec((N_EXPERTS, D), lambda i: (0, 0)),
                  pl.BlockSpec((N_EXPERTS, 1), lambda i: (0, 0))],
        out_specs=[pl.BlockSpec((TOP_K, tm), lambda i: (0, i)),
                   pl.BlockSpec((TOP_K, tm), lambda i: (0, i))],
        out_shape=[jax.ShapeDtypeStruct((TOP_K, T), jnp.int32),
                   jax.ShapeDtypeStruct((TOP_K, T), F32)],
        compiler_params=_params(("arbitrary",)),
    )(xt, router_w.T, router_b.reshape(N_EXPERTS, 1))


def _ffn_kernel(be_ref, x_ref, wg_ref, wu_ref, wd_ref, o_ref):
    i = pl.program_id(0)
    used = be_ref[pl.num_programs(0)]

    @pl.when(i < used)
    def _():
        x = x_ref[...].astype(BF16)
        a = jnp.dot(x, wg_ref[0, 0], preferred_element_type=F32)
        u = jnp.dot(x, wu_ref[0, 0], preferred_element_type=F32)
        h = (a * jax.nn.sigmoid(a)) * u
        o_ref[...] = jnp.dot(h.astype(BF16), wd_ref[0, 0], preferred_element_type=F32)

    @pl.when(i >= used)
    def _():
        o_ref[...] = jnp.zeros(o_ref.shape, F32)


def _cast_kernel(x_ref, o_ref):
    o_ref[...] = x_ref[...].astype(o_ref.dtype)


def cast_bf16(w):
    n = w.shape[-1]
    w2 = w.reshape(-1, n)
    rows = w2.shape[0]
    tr = _pick(rows, (1024 * 1024 // n, 512, 256, 128, 64, 32, 16))
    out = pl.pallas_call(
        _cast_kernel,
        grid=(rows // tr,),
        in_specs=[pl.BlockSpec((tr, n), lambda i: (i, 0))],
        out_specs=pl.BlockSpec((tr, n), lambda i: (i, 0)),
        out_shape=jax.ShapeDtypeStruct(w2.shape, BF16),
        compiler_params=_params(("arbitrary",)),
    )(w2)
    return out.reshape(w.shape)


def expert_ffn(xs, block_exp, layer, wg, wu, wd):
    R, D = xs.shape
    F = wg.shape[3]
    nb = R // MOE_TM
    grid_spec = pltpu.PrefetchScalarGridSpec(
        num_scalar_prefetch=1,
        grid=(nb,),
        in_specs=[pl.BlockSpec((MOE_TM, D), lambda i, be: (i, 0)),
                  pl.BlockSpec((1, 1, D, F), lambda i, be: (layer, be[i], 0, 0)),
                  pl.BlockSpec((1, 1, D, F), lambda i, be: (layer, be[i], 0, 0)),
                  pl.BlockSpec((1, 1, F, D), lambda i, be: (layer, be[i], 0, 0))],
        out_specs=pl.BlockSpec((MOE_TM, D), lambda i, be: (i, 0)),
    )
    return pl.pallas_call(
        _ffn_kernel,
        grid_spec=grid_spec,
        out_shape=jax.ShapeDtypeStruct((R, D), F32),
        compiler_params=_params(("arbitrary",)),
    )(block_exp, xs, wg, wu, wd)


def _combine_ln_kernel(ya_ref, yb_ref, ga_ref, gb_ref, r_ref, g_ref, b_ref, o_ref):
    y = ya_ref[...] * ga_ref[...] + yb_ref[...] * gb_ref[...]
    o_ref[...] = _deepnorm(r_ref[...], y, g_ref[...], b_ref[...])


def combine_ln(ya, yb, ga, gb, resid, g, b):
    M, N = ya.shape
    tm = _pick(M, (256, 128, 64, 32, 16, 8))
    blk = pl.BlockSpec((tm, N), lambda i: (i, 0))
    col = pl.BlockSpec((tm, 1), lambda i: (i, 0))
    row = pl.BlockSpec((1, N), lambda i: (0, 0))
    return pl.pallas_call(
        _combine_ln_kernel,
        grid=(M // tm,),
        in_specs=[blk, blk, col, col, blk, row, row],
        out_specs=blk,
        out_shape=jax.ShapeDtypeStruct((M, N), F32),
        compiler_params=_params(("arbitrary",)),
    )(ya, yb, ga, gb, resid, g.reshape(1, N), b.reshape(1, N))


def moe_layer(x, router_w, router_b, layer, wg, wu, wd, ln_g, ln_b):
    shp = x.shape
    xt = x.reshape(-1, shp[-1])
    T = xt.shape[0]
    expert, gate = route(xt, router_w, router_b)
    flat_e = expert.T.reshape(-1)
    n_assign = T * TOP_K
    onehot = (flat_e[:, None] == jnp.arange(N_EXPERTS, dtype=jnp.int32)[None]).astype(jnp.int32)
    csum = jnp.cumsum(onehot, axis=0)
    rank = jnp.take_along_axis(csum, flat_e[:, None], axis=1)[:, 0] - 1
    counts = csum[-1]
    padded = (counts + MOE_TM - 1) // MOE_TM * MOE_TM
    pad_end = jnp.cumsum(padded)
    pad_start = pad_end - padded
    dest = pad_start[flat_e] + rank
    nb = -(-n_assign // MOE_TM) + N_EXPERTS
    rows = nb * MOE_TM
    row_tok = jnp.zeros(rows, jnp.int32).at[dest].set(jnp.arange(n_assign, dtype=jnp.int32) // TOP_K)
    first_row = jnp.arange(nb, dtype=jnp.int32) * MOE_TM
    block_exp = jnp.minimum(jnp.sum((pad_end[None, :] <= first_row[:, None]).astype(jnp.int32), axis=1),
                            N_EXPERTS - 1)
    block_exp = jnp.concatenate([block_exp, pad_end[-1:] // MOE_TM])
    yb = expert_ffn(xt[row_tok], block_exp, layer, wg, wu, wd)
    d2 = dest.reshape(T, TOP_K)
    out = combine_ln(yb[d2[:, 0]], yb[d2[:, 1]], gate[0][:, None], gate[1][:, None], xt, ln_g, ln_b)
    return out.reshape(shp)


def _block_diag(w):
    G, I, O = w.shape
    eye = jnp.eye(G, dtype=w.dtype)
    return (eye[:, None, :, None] * w[:, :, None, :]).reshape(G * I, G * O)


def even_layer(x, lru_h, lru_conv, wkv, shift, p, ln_g, ln_b):
    B, T, D = x.shape
    D_LRU = lru_h.shape[-1]
    D_RWKV = p['w0'].shape[1]
    H = D_RWKV // RWKV_HEAD
    N = RWKV_HEAD
    z = matmul(x.reshape(B * T, D), p['w_in']).reshape(B, T, -1)
    y_a, r, decay, k, v, a, g, h_T = even_pre(z, lru_conv, shift, lru_h, p)

    lanes = lambda t: jnp.transpose(t.reshape(B, T, H, N), (1, 3, 0, 2)).reshape(T, N, B * H)
    per_chain = lambda w: jnp.tile(w.reshape(H, N).T, (1, B))
    s0 = jnp.transpose(wkv, (2, 3, 0, 1)).reshape(N, N, B * H)
    y, s_T = rwkv_scan(lanes(r), lanes(decay), lanes(k), lanes(v), lanes(a),
                       per_chain(p['k_k']), per_chain(p['k_a']), per_chain(p['gn_g']),
                       per_chain(p['gn_b']), per_chain(p['r_k']), s0)
    y = jnp.transpose(y.reshape(T, N, B, H), (2, 0, 3, 1)).reshape(B * T, D_RWKV)
    S_T = jnp.transpose(s_T.reshape(N, N, B, H), (2, 3, 0, 1))

    x_new = mix_ln(y_a.reshape(B * T, D_LRU), y, g.reshape(B * T, D_RWKV), p['w_out'],
                   x.reshape(B * T, D), ln_g, ln_b)
    conv_state = jnp.concatenate([lru_conv, z[..., :D_LRU]], axis=1)[:, T:]
    return x_new.reshape(B, T, D), (h_T[:, 0], conv_state, S_T, z[:, -1:, 2 * D_LRU:])


def odd_layer(x, cache, p, ln_g, ln_b):
    B, T, D = x.shape
    dq = N_HEADS_C * HEAD_DIM_C
    dkv = N_KV_C * HEAD_DIM_C
    dqi = N_IDX_HEADS * D_IDX
    z = matmul(x.reshape(B * T, D), p['w_in']).reshape(B, T, -1)
    k = z[..., dq:dq + dkv]
    v = z[..., dq + dkv:dq + 2 * dkv]
    o = dq + 2 * dkv + dqi
    ki = z[..., o:o + D_IDX]
    wi = z[..., o + D_IDX:o + D_IDX + N_IDX_HEADS] * (dqi ** -0.5)
    Tp = -(-T // LANES) * LANES
    if cache is None:
        past = 0
        n_sel = min(TOPK_MAX, T // 4)
        parts = lambda c, n: [n]
    else:
        past = cache[0].shape[1]
        n_sel = min(TOPK_MAX, (past + T) // 4)
        parts = lambda c, n: [c.reshape(B, past, -1), n]
    L = -(-(past + Tp) // LANES) * LANES
    tk = _pick(L, (512, 384, 128))
    cat = lambda c, n: jnp.concatenate(
        parts(c, n) + [jnp.zeros((B, L - past - T, n.shape[-1]), n.dtype)], axis=1).astype(BF16)
    ck, cv, cki = cache if cache is not None else (None, None, None)
    keys, vals, kidx = cat(ck, k), cat(cv, v), cat(cki, ki)
    vals_t = jnp.swapaxes(vals.reshape(B, L // tk, tk, dkv), 2, 3)
    rowpad = lambda t: jnp.pad(t, ((0, 0), (0, Tp - T), (0, 0)))
    wi_t = jnp.swapaxes(rowpad(wi), 1, 2)
    o_att = dsa_attention(rowpad(z), 0, (dq + 2 * dkv) // dqi, wi_t, keys, vals_t, kidx, past, n_sel, tk)
    x_new = matmul_ln(o_att[:, :T].reshape(B * T, dq), p['w_out'], x.reshape(B * T, D), ln_g, ln_b)
    new = (k.reshape(B, T, N_KV_C, HEAD_DIM_C), v.reshape(B, T, N_KV_C, HEAD_DIM_C), ki)
    return x_new.reshape(B, T, D), new


def run_trunk(x, even_state, odd_cache, ev, od, norm_w, moe_w):
    ln1_g, ln1_b, ln2_g, ln2_b = norm_w
    router_w, router_b, wg, wu, wd = moe_w
    x, new_even = even_layer(x, *even_state, ev, ln1_g[0], ln1_b[0])
    x = moe_layer(x, router_w, router_b, 0, wg, wu, wd, ln2_g[0], ln2_b[0])
    x, new_odd = odd_layer(x, odd_cache, od, ln1_g[1], ln1_b[1])
    x = moe_layer(x, router_w, router_b, 1, wg, wu, wd, ln2_g[1], ln2_b[1])
    return x, new_even, new_odd


def kernel(x_prompt, x_sample, state_lru_h, state_lru_conv, state_rwkv_wkv, state_rwkv_shift, cache_k, cache_v, cache_kidx, ev_w_in, ev_conv_w, ev_conv_b, ev_gate_r_w, ev_gate_r_b, ev_gate_i_w, ev_gate_i_b, ev_lambda, ev_shift_mu, ev_w0, ev_w2, ev_a0, ev_a2, ev_g2, ev_k_k, ev_k_a, ev_r_k, ev_gn_g, ev_gn_b, ev_w_out, od_w_in, od_w_out, ln1_g, ln1_b, ln2_g, ln2_b, router_w, router_b, moe_w_gate, moe_w_up, moe_w_down):
    ev = dict(
        w_in=ev_w_in[0].astype(BF16), conv_w=ev_conv_w[0], conv_b=ev_conv_b[0][None],
        gate_w=jnp.concatenate([_block_diag(ev_gate_r_w[0]), _block_diag(ev_gate_i_w[0])], axis=1).astype(BF16),
        gate_b=jnp.concatenate([ev_gate_r_b[0], ev_gate_i_b[0]])[None],
        lru_c=(-LRU_C * jax.nn.softplus(-ev_lambda[0]))[None],
        shift_mu=ev_shift_mu[0][None], w0=ev_w0[0][None], a0=ev_a0[0][None],
        w2=ev_w2[0].astype(BF16), a2=ev_a2[0].astype(BF16), g2=ev_g2[0].astype(BF16),
        k_k=ev_k_k[0][None], k_a=ev_k_a[0][None], r_k=ev_r_k[0], gn_g=ev_gn_g[0], gn_b=ev_gn_b[0],
        w_out=ev_w_out[0].astype(BF16))
    n_in = od_w_in.shape[2]
    od = dict(w_in=jnp.pad(od_w_in[0], ((0, 0), (0, (-n_in) % LANES))).astype(BF16),
              w_out=od_w_out[0].astype(BF16))
    norm_w = (ln1_g, ln1_b, ln2_g, ln2_b)
    moe_w = (router_w, router_b, cast_bf16(moe_w_gate), cast_bf16(moe_w_up), cast_bf16(moe_w_down))

    B = x_prompt.shape[0]
    fresh = (jnp.zeros((B,) + state_lru_h.shape[2:], F32),
             jnp.zeros((B,) + state_lru_conv.shape[2:], F32),
             jnp.zeros((B,) + state_rwkv_wkv.shape[2:], F32),
             jnp.zeros((B,) + state_rwkv_shift.shape[2:], F32))
    y_p, pe, po = run_trunk(x_prompt, fresh, None, ev, od, norm_w, moe_w)
    y_s, se, so = run_trunk(
        x_sample, (state_lru_h[0], state_lru_conv[0], state_rwkv_wkv[0], state_rwkv_shift[0]),
        (cache_k[0], cache_v[0], cache_kidx[0]), ev, od, norm_w, moe_w)
    st = lambda t: t[None]
    return (y_p, y_s, *map(st, pe), *map(st, po), *map(st, se), *map(st, so))
```

```python
import functools
import math

import jax
import jax.numpy as jnp
from jax import lax
from jax.experimental import pallas as pl
from jax.experimental.pallas import tpu as pltpu

F32 = jnp.float32
BF16 = jnp.bfloat16

CHUNK = 64
CONV_W = 4
LRU_C = 8.0
RWKV_HEAD = 64
DECAY_LORA = 64
AAA_LORA = 64
GATE_LORA = 128
N_HEADS_C = 16
HEAD_DIM_C = 128
N_KV_C = 4
N_IDX_HEADS = 16
D_IDX = 64
TOPK_MAX = 256
N_EXPERTS = 16
N_GROUPS = 4
EXPERTS_PER_GROUP = N_EXPERTS // N_GROUPS
TOP_K = 2
DEPTH = 2
DN_ALPHA = (2 * DEPTH) ** 0.25
LN_EPS = 1e-5
GN_EPS = 64e-5

LANES = 128
SUBLANES = 8
VMEM_LIMIT = 56 * 1024 * 1024
MOE_TM = 256


def _params(sem):
    return pltpu.CompilerParams(dimension_semantics=sem, vmem_limit_bytes=VMEM_LIMIT)


def _mm_kernel(x_ref, w_ref, o_ref, xb_ref):
    @pl.when(pl.program_id(1) == 0)
    def _():
        xb_ref[...] = x_ref[...].astype(BF16)

    o_ref[...] = jnp.dot(xb_ref[...], w_ref[...], preferred_element_type=F32)


def _pick(n, cands):
    for c in cands:
        if n % c == 0:
            return c
    return n


def matmul(x, w):
    M, K = x.shape
    N = w.shape[1]
    tm = _pick(M, (1024, 512, 256, 128, 64, 32, 16, 8))
    tn = _pick(N, (1792, 1408, 1024, 768, 512, 384, 256, 128))
    return pl.pallas_call(
        _mm_kernel,
        grid=(M // tm, N // tn),
        in_specs=[pl.BlockSpec((tm, K), lambda i, j: (i, 0)),
                  pl.BlockSpec((K, tn), lambda i, j: (0, j))],
        out_specs=pl.BlockSpec((tm, tn), lambda i, j: (i, j)),
        out_shape=jax.ShapeDtypeStruct((M, N), F32),
        scratch_shapes=[pltpu.VMEM((tm, K), BF16)],
        compiler_params=_params(("arbitrary", "arbitrary")),
    )(x, w)


def _deepnorm(resid, y, g, b):
    h = DN_ALPHA * resid + y
    mu = jnp.mean(h, axis=-1, keepdims=True)
    hc = h - mu
    var = jnp.mean(hc * hc, axis=-1, keepdims=True)
    return hc * lax.rsqrt(var + LN_EPS) * g + b


def _mm_ln_kernel(x_ref, w_ref, r_ref, g_ref, b_ref, o_ref):
    y = jnp.dot(x_ref[...].astype(BF16), w_ref[...], preferred_element_type=F32)
    o_ref[...] = _deepnorm(r_ref[...], y, g_ref[...], b_ref[...])


def _mix_ln_kernel(ya_ref, yb_ref, gate_ref, w_ref, r_ref, g_ref, b_ref, o_ref):
    mix = jnp.concatenate([ya_ref[...], yb_ref[...] * gate_ref[...]], axis=1).astype(BF16)
    y = jnp.dot(mix, w_ref[...], preferred_element_type=F32)
    o_ref[...] = _deepnorm(r_ref[...], y, g_ref[...], b_ref[...])


def _ln_call(kern, lhs, w, resid, g, b):
    M, N = resid.shape
    tm = _pick(M, (512, 256, 128, 64, 32, 16, 8))
    full = lambda a: pl.BlockSpec(a.shape, lambda i: (0,) * a.ndim)
    rows = lambda a: pl.BlockSpec((tm, a.shape[1]), lambda i: (i, 0))
    consts = [g.reshape(1, N), b.reshape(1, N)]
    return pl.pallas_call(
        kern,
        grid=(M // tm,),
        in_specs=[rows(a) for a in lhs] + [full(w), rows(resid)] + [full(c) for c in consts],
        out_specs=rows(resid),
        out_shape=jax.ShapeDtypeStruct((M, N), F32),
        compiler_params=_params(("arbitrary",)),
    )(*lhs, w, resid, *consts)


def matmul_ln(x, w, resid, g, b):
    return _ln_call(_mm_ln_kernel, [x], w, resid, g, b)


def mix_ln(ya, yb, gate, w, resid, g, b):
    return _ln_call(_mix_ln_kernel, [ya, yb, gate], w, resid, g, b)


def _expm1(x):
    e = jnp.exp(x)
    safe = jnp.where((e == 1.0) | (e == 0.0), 2.0, e)
    return jnp.where(e == 1.0, x, jnp.where(e == 0.0, -1.0, (e - 1.0) * x / jnp.log(safe)))


def _softplus(x):
    return jnp.maximum(x, 0.0) + jnp.log(1.0 + jnp.exp(-jnp.abs(x)))


HALO = SUBLANES
SCAN_FEATS = 5


def _even_pre_kernel(ax_ref, ag_ref, zr_ref, zk_ref, zv_ref, zt_ref,
                     conv_ref, sr_ref, sk_ref, sv_ref, st_ref, h0_ref,
                     cw_ref, cb_ref, gw_ref, gb_ref, lc_ref, mu_ref,
                     w0_ref, a0_ref, w2_ref, a2_ref, g2_ref,
                     ya_ref, feat_ref, g_ref, hT_ref,
                     xp_scr, pr_scr, pk_scr, pv_scr, pt_scr, a_scr, u_scr, hs_scr, h_scr):
    i = pl.program_id(1)
    tm = ax_ref.shape[1]
    C = ax_ref.shape[2]
    prev = ((xp_scr, None), (pr_scr, sr_ref), (pk_scr, sk_ref), (pv_scr, sv_ref), (pt_scr, st_ref))

    @pl.when(i == 0)
    def _():
        xp_scr[...] = jnp.zeros((HALO, C), F32)
        xp_scr[HALO - (CONV_W - 1):HALO, :] = conv_ref[0]
        for scr, ref in prev[1:]:
            scr[...] = jnp.broadcast_to(ref[0], scr.shape)
        h_scr[...] = jnp.broadcast_to(h0_ref[0], h_scr.shape)

    def earlier(x, scr, d):
        rot = pltpu.roll(x, d, axis=0)
        halo = pltpu.roll(scr[...], d, axis=0)
        sub = lax.broadcasted_iota(jnp.int32, halo.shape, 0)
        return jnp.concatenate([jnp.where(sub < d, halo, rot[:HALO]), rot[HALO:]], axis=0)

    ax = ax_ref[0]
    xa = cb_ref[...] + cw_ref[CONV_W - 1:CONV_W, :] * ax
    for d in range(1, CONV_W):
        xa = xa + cw_ref[CONV_W - 1 - d:CONV_W - d, :] * earlier(ax, xp_scr, d)
    gates = jnp.dot(xa.astype(BF16), gw_ref[...], preferred_element_type=F32) + gb_ref[...]
    log_a = lc_ref[...] * jax.nn.sigmoid(gates[:, :C])
    a_scr[...] = jnp.exp(log_a)
    u_scr[...] = jnp.sqrt(-_expm1(2.0 * log_a)) * (jax.nn.sigmoid(gates[:, C:]) * xa)

    def step(t, h):
        h = a_scr[pl.ds(t, 1), :] * h + u_scr[pl.ds(t, 1), :]
        hs_scr[pl.ds(t, 1), :] = h
        return h

    h = lax.fori_loop(0, tm, step, h_scr[0:1, :], unroll=8)
    h_scr[...] = jnp.broadcast_to(h, h_scr.shape)
    hT_ref[0] = h
    ya_ref[0] = hs_scr[...] * jax.nn.gelu(ag_ref[0], approximate=True)

    zr, zk, zv, zt = zr_ref[0], zk_ref[0], zv_ref[0], zt_ref[0]

    def shifted(zb, scr, lo, hi):
        return zb + mu_ref[:, lo:hi] * (earlier(zb, scr, 1) - zb)

    r = shifted(zr, pr_scr, 0, C)
    k = shifted(zk, pk_scr, C, 2 * C)
    v = shifted(zv, pv_scr, 2 * C, 3 * C)
    tail = shifted(zt, pt_scr, 3 * C, 3 * C + pt_scr.shape[1])
    wl = jnp.tanh(tail[:, :DECAY_LORA])
    al = tail[:, DECAY_LORA:DECAY_LORA + AAA_LORA]
    gl = jax.nn.sigmoid(tail[:, DECAY_LORA + AAA_LORA:])
    lora = lambda x, w_ref_: jnp.dot(x.astype(BF16), w_ref_[...], preferred_element_type=F32)
    w_log = -_softplus(-(w0_ref[...] + lora(wl, w2_ref))) - 0.5
    a = jax.nn.sigmoid(a0_ref[...] + lora(al, a2_ref))
    feat_ref[0, 0] = r
    feat_ref[0, 1] = jnp.exp(-jnp.exp(w_log))
    feat_ref[0, 2] = k
    feat_ref[0, 3] = v
    feat_ref[0, 4] = a
    g_ref[0] = lora(gl, g2_ref)

    for (scr, _), x in zip(prev, (ax, zr, zk, zv, zt)):
        scr[...] = x[tm - HALO:]


def even_pre(z, lru_conv, shift, lru_h, p):
    B, T, _ = z.shape
    C = lru_h.shape[-1]
    tail = DECAY_LORA + AAA_LORA + GATE_LORA
    tm = _pick(T, (256, 128, 64, 32, 16, 8))
    zblk = lambda w, col: pl.BlockSpec((1, tm, w), lambda b, i: (b, i, col))
    sblk = lambda rows, w, col: pl.BlockSpec((1, rows, w), lambda b, i: (b, 0, col))
    full = lambda a: pl.BlockSpec(a.shape, lambda b, i: (0,) * a.ndim)
    weights = [p['conv_w'], p['conv_b'], p['gate_w'], p['gate_b'], p['lru_c'], p['shift_mu'],
               p['w0'], p['a0'], p['w2'], p['a2'], p['g2']]
    tok = jax.ShapeDtypeStruct((B, T, C), F32)
    return pl.pallas_call(
        _even_pre_kernel,
        grid=(B, T // tm),
        in_specs=[zblk(C, 0), zblk(C, 1), zblk(C, 2), zblk(C, 3), zblk(C, 4), zblk(tail, 5 * C // tail),
                  sblk(CONV_W - 1, C, 0), sblk(1, C, 0), sblk(1, C, 1), sblk(1, C, 2),
                  sblk(1, tail, 3 * C // tail), sblk(1, C, 0)] + [full(w) for w in weights],
        out_specs=[zblk(C, 0), pl.BlockSpec((1, SCAN_FEATS, tm, C), lambda b, i: (b, 0, i, 0)),
                   zblk(C, 0), sblk(1, C, 0)],
        out_shape=[tok, jax.ShapeDtypeStruct((B, SCAN_FEATS, T, C), F32), tok,
                   jax.ShapeDtypeStruct((B, 1, C), F32)],
        scratch_shapes=[pltpu.VMEM((HALO, C), F32)] * 4 + [pltpu.VMEM((HALO, tail), F32)]
                       + [pltpu.VMEM((tm, C), F32)] * 3 + [pltpu.VMEM((SUBLANES, C), F32)],
        compiler_params=_params(("arbitrary", "arbitrary")),
    )(z, z, z, z, z, z, lru_conv, shift, shift, shift, shift, lru_h[:, None, :], *weights)


def _colsum(x):
    n = x.shape[0] // SUBLANES
    p = x.reshape(n, SUBLANES, x.shape[1]).sum(axis=0)
    p = p + pltpu.roll(p, 4, axis=0)
    p = p + pltpu.roll(p, 2, axis=0)
    p = p + pltpu.roll(p, 1, axis=0)
    return p


def _bcast_rows(p, n):
    return jnp.broadcast_to(p[None], (n // SUBLANES,) + p.shape).reshape(n, p.shape[1])


def _rwkv_kernel(r_ref, w_ref, k_ref, v_ref, a_ref, kk_ref, ka_ref, gng_ref, gnb_ref, rk_ref, s0_ref,
                 y_ref, sT_ref, s_scr):
    tc = pl.program_id(1)
    N = RWKV_HEAD

    @pl.when(tc == 0)
    def _():
        s_scr[...] = s0_ref[...]

    sub = lax.broadcasted_iota(jnp.int32, (SUBLANES, LANES), 0)

    def step(t, carry):
        r = r_ref[t]
        w = w_ref[t]
        k_raw = k_ref[t]
        a = a_ref[t]
        kk = k_raw * kk_ref[...]
        kk = kk / jnp.maximum(_bcast_rows(jnp.sqrt(_colsum(kk * kk)), N), 1e-12)
        nkk = -kk
        kka = kk * a
        k = k_raw * (1.0 + (a - 1.0) * ka_ref[...])

        def vgroup(g, c2):
            yacc = jnp.zeros((SUBLANES, LANES), F32)
            for j in range(SUBLANES):
                vi = g * SUBLANES + j
                S = s_scr[vi]
                sa = _bcast_rows(_colsum(S * nkk), N)
                vb = jnp.broadcast_to(v_ref[t, pl.ds(vi, 1), :], (N, LANES))
                S = S * w + sa * kka + vb * k
                s_scr[vi] = S
                yacc = jnp.where(sub == j, _colsum(S * r), yacc)
            y_ref[t, pl.ds(pl.multiple_of(g * SUBLANES, SUBLANES), SUBLANES), :] = yacc
            return c2

        lax.fori_loop(0, N // SUBLANES, vgroup, 0, unroll=True)

        y = y_ref[t]
        d = y - _bcast_rows(_colsum(y), N) * (1.0 / N)
        var = _bcast_rows(_colsum(d * d), N) * (1.0 / N)
        bonus = _bcast_rows(_colsum(r * k * rk_ref[...]), N)
        y_ref[t] = d * lax.rsqrt(var + GN_EPS) * gng_ref[...] + gnb_ref[...] + bonus * v_ref[t]
        return carry

    lax.fori_loop(0, r_ref.shape[0], step, 0)

    @pl.when(tc == pl.num_programs(1) - 1)
    def _():
        sT_ref[...] = s_scr[...]


def rwkv_scan(feats, k_k, k_a, gn_g, gn_b, r_k, s0):
    _, T, N, P = feats.shape
    tc = _pick(T, (64, 32, 16, 8))
    blk = pl.BlockSpec((tc, N, LANES), lambda p, t: (t, 0, p))
    feat = lambda s: pl.BlockSpec((None, tc, N, LANES), lambda p, t: (s, t, 0, p))
    par = pl.BlockSpec((N, LANES), lambda p, t: (0, p))
    st = pl.BlockSpec((N, N, LANES), lambda p, t: (0, 0, p))
    return pl.pallas_call(
        _rwkv_kernel,
        grid=(P // LANES, T // tc),
        in_specs=[feat(s) for s in range(SCAN_FEATS)] + [par] * 5 + [st],
        out_specs=[blk, st],
        out_shape=[jax.ShapeDtypeStruct((T, N, P), F32), jax.ShapeDtypeStruct((N, N, P), F32)],
        scratch_shapes=[pltpu.VMEM((N, N, LANES), F32)],
        compiler_params=_params(("arbitrary", "arbitrary")),
    )(*([feats] * SCAN_FEATS), k_k, k_a, gn_g, gn_b, r_k, s0)


MASKED = -1e30


KEY_NEG_INF = -2139095041
QK_SPLIT = 4


def _tree_sum(parts):
    while len(parts) > 1:
        parts = [parts[i] + parts[i + 1] for i in range(0, len(parts) - 1, 2)] + parts[len(parts) & ~1:]
    return parts[0]


def _dsa_kernel(q_ref, qi_ref, wi_ref, k_ref, vt_ref, ki_ref, o_ref,
                key_scr, m_scr, l_scr, acc_scr, *, past, n_sel, scale, tk):
    qb = pl.program_id(1)
    QB = q_ref.shape[1]
    L = k_ref.shape[1]
    NP = N_HEADS_C // 2
    lane_chunk = lax.broadcasted_iota(jnp.int32, (1, QB), 1) // CHUNK
    limit = past + (qb * (QB // CHUNK) + lane_chunk + 1) * CHUNK
    n_tiles = (past + (qb + 1) * QB + tk - 1) // tk
    u = 2 if (tk < 512 and (L // tk) % 2 == 0) else 1
    n_count = (n_tiles + u - 1) // u
    row = lax.broadcasted_iota(jnp.int32, (tk, QB), 0)
    tile = lambda kt: pl.ds(pl.multiple_of(kt * tk, tk), tk)

    qi = qi_ref[0].astype(BF16)
    qi_pair = [jnp.concatenate([qi[:, (2 * p) * D_IDX:(2 * p + 1) * D_IDX],
                                qi[:, (2 * p + 1) * D_IDX:(2 * p + 2) * D_IDX]], axis=0)
               for p in range(N_IDX_HEADS // 2)]
    wi = wi_ref[0]

    def score_tile(kt, c):
        @pl.when(kt < n_tiles)
        def _():
            kidx = ki_ref[0, tile(kt), :]
            terms = []
            for p in range(N_IDX_HEADS // 2):
                s = lax.dot_general(kidx, qi_pair[p], (((1,), (1,)), ((), ())),
                                    preferred_element_type=F32)
                for e in range(2):
                    h = 2 * p + e
                    terms.append(wi[h:h + 1, :] * jnp.maximum(s[:, e * QB:(e + 1) * QB], 0.0))
            isc = terms[0]
            for t in terms[1:]:
                isc = isc + t
            ok = row + kt * tk < limit
            bits = lax.bitcast_convert_type(jnp.where(ok, isc + 0.0, -jnp.inf), jnp.int32)
            key_scr[tile(kt), :] = bits ^ ((bits >> 31) & jnp.int32(0x7FFFFFFF))

        @pl.when(kt >= n_tiles)
        def _():
            key_scr[tile(kt), :] = jnp.full((tk, QB), KEY_NEG_INF, jnp.int32)

        return c

    lax.fori_loop(0, n_count * u, score_tile, 0)

    int_min = jnp.int32(-2 ** 31)
    ct = tk * u

    def search(i, t):
        cand = t | lax.shift_left(jnp.int32(1), 31 - i)
        cs = cand ^ int_min

        def count_tile(kt, acc):
            keys = key_scr[pl.ds(pl.multiple_of(kt * ct, ct), ct), :]
            hit = jnp.where(keys >= cs, 1.0, 0.0)
            return acc + _tree_sum([hit[r * SUBLANES:(r + 1) * SUBLANES] for r in range(ct // SUBLANES)])

        acc = lax.fori_loop(0, n_count, count_tile, jnp.zeros((SUBLANES, QB), F32))
        cnt = jnp.sum(acc, axis=0, keepdims=True)
        return jnp.where(cnt >= n_sel, cand, t)

    thr = lax.fori_loop(0, 32, search, jnp.zeros((1, QB), jnp.int32)) ^ int_min

    G = N_HEADS_C // N_KV_C
    dh = HEAD_DIM_C
    q = q_ref[0]
    q_pair = [(jnp.concatenate([q[:, (2 * p) * dh:(2 * p + 1) * dh],
                                q[:, (2 * p + 1) * dh:(2 * p + 2) * dh]], axis=0) * scale).astype(BF16)
              for p in range(NP)]
    m_scr[...] = jnp.full(m_scr.shape, MASKED, F32)
    l_scr[...] = jnp.zeros(l_scr.shape, F32)
    acc_scr[...] = jnp.zeros(acc_scr.shape, F32)

    def att_tile(kt, c):
        sel = (key_scr[tile(kt), :] >= thr) & (row + kt * tk < limit)
        b = jnp.where(sel, 0.0, MASKED)
        b2 = jnp.concatenate([b, b], axis=1)
        def logits(p):
            j = (2 * p) // G
            kj = k_ref[0, tile(kt), j * dh:(j + 1) * dh]
            part = tk // QK_SPLIT
            return jnp.concatenate(
                [lax.dot_general(kj[r * part:(r + 1) * part], q_pair[p], (((1,), (1,)), ((), ())),
                                 preferred_element_type=F32) for r in range(QK_SPLIT)], axis=0)

        ahead = 4
        lg = {p: logits(p) for p in range(ahead)}
        for p in range(NP):
            j = (2 * p) // G
            x = lg.pop(p) + b2
            m = m_scr[p:p + 1, :]
            m_new = jnp.maximum(m, jnp.max(x, axis=0, keepdims=True))
            alpha = jnp.exp2(m - m_new)
            pr = jnp.exp2(x - m_new)
            l_scr[p:p + 1, :] = alpha * l_scr[p:p + 1, :] + jnp.sum(pr, axis=0, keepdims=True)
            m_scr[p:p + 1, :] = m_new
            vt = vt_ref[0, kt, j * dh:(j + 1) * dh, :]
            pv = jnp.dot(vt, pr.astype(BF16), preferred_element_type=F32)
            if p + ahead < NP:
                lg[p + ahead] = logits(p + ahead)
            acc_scr[p] = acc_scr[p] * alpha + pv
        return c

    lax.fori_loop(0, n_tiles, att_tile, 0)
    for p in range(NP):
        o2 = acc_scr[p] / l_scr[p:p + 1, :]
        for s in range(2):
            o_ref[0, :, (2 * p + s) * dh:(2 * p + s + 1) * dh] = o2[:, s * QB:(s + 1) * QB].T


def dsa_attention(z, q_col, qi_col, wi_t, keys, vals_t, kidx, past, n_sel, tk):
    B, T, _ = z.shape
    L = keys.shape[1]
    QB = LANES
    dq = N_HEADS_C * HEAD_DIM_C
    dqi = N_IDX_HEADS * D_IDX
    kern = functools.partial(_dsa_kernel, past=past, n_sel=n_sel,
                             scale=HEAD_DIM_C ** -0.5 * math.log2(math.e), tk=tk)
    return pl.pallas_call(
        kern,
        grid=(B, T // QB),
        in_specs=[pl.BlockSpec((1, QB, dq), lambda b, c: (b, c, q_col)),
                  pl.BlockSpec((1, QB, dqi), lambda b, c: (b, c, qi_col)),
                  pl.BlockSpec((1, N_IDX_HEADS, QB), lambda b, c: (b, 0, c)),
                  pl.BlockSpec((1, L, keys.shape[2]), lambda b, c: (b, 0, 0)),
                  pl.BlockSpec((1,) + vals_t.shape[1:], lambda b, c: (b, 0, 0, 0)),
                  pl.BlockSpec((1, L, kidx.shape[2]), lambda b, c: (b, 0, 0))],
        out_specs=pl.BlockSpec((1, QB, dq), lambda b, c: (b, c, 0)),
        out_shape=jax.ShapeDtypeStruct((B, T, dq), F32),
        scratch_shapes=[pltpu.VMEM((L, QB), jnp.int32),
                        pltpu.VMEM((N_HEADS_C // 2, 2 * QB), F32),
                        pltpu.VMEM((N_HEADS_C // 2, 2 * QB), F32),
                        pltpu.VMEM((N_HEADS_C // 2, HEAD_DIM_C, 2 * QB), F32)],
        compiler_params=_params(("arbitrary", "arbitrary")),
    )(z, z, wi_t, keys, vals_t, kidx)


def _route_tokens(x, w, bias):
    def split(t):
        hi = t.astype(BF16)
        return hi, (t - hi.astype(F32)).astype(BF16)

    nt = lambda a, c: lax.dot_general(a, c, (((1,), (1,)), ((), ())), preferred_element_type=F32)
    (w_hi, w_lo), (x_hi, x_lo) = split(w), split(x)
    logits = nt(w_hi, x_hi) + (nt(w_hi, x_lo) + nt(w_lo, x_hi))
    scores = jax.nn.sigmoid(logits)
    biased = scores + bias
    tm = scores.shape[1]
    P = EXPERTS_PER_GROUP

    def ranks(rows):
        out = []
        for i in range(len(rows)):
            rk = jnp.zeros((1, tm), jnp.int32)
            for j in range(len(rows)):
                if j == i:
                    continue
                beat = (rows[j] > rows[i]) | ((rows[j] == rows[i]) & (j < i))
                rk = rk + beat.astype(jnp.int32)
            out.append(rk)
        return out

    b = [biased[i:i + 1] for i in range(N_EXPERTS)]
    s = [scores[i:i + 1] for i in range(N_EXPERTS)]
    grp_rank, grp_score = [], []
    for g in range(N_GROUPS):
        rows = b[g * P:(g + 1) * P]
        rk = ranks(rows)
        grp_rank.append(rk)
        top1 = sum(jnp.where(rk[i] == 0, rows[i], 0.0) for i in range(P))
        top2 = sum(jnp.where(rk[i] == 1, rows[i], 0.0) for i in range(P))
        grp_score.append(top1 + top2)
    gr = ranks(grp_score)
    e1 = jnp.zeros((1, tm), jnp.int32)
    e2 = jnp.zeros((1, tm), jnp.int32)
    w1 = jnp.zeros((1, tm), F32)
    w2 = jnp.zeros((1, tm), F32)
    for g in range(N_GROUPS):
        best = gr[g] == 0
        for i in range(P):
            is1 = best & (grp_rank[g][i] == 0)
            is2 = best & (grp_rank[g][i] == 1)
            e1 = jnp.where(is1, g * P + i, e1)
            e2 = jnp.where(is2, g * P + i, e2)
            w1 = jnp.where(is1, s[g * P + i], w1)
            w2 = jnp.where(is2, s[g * P + i], w2)
    tot = w1 + w2
    return jnp.concatenate([e1, e2], axis=0), jnp.concatenate([w1 / tot, w2 / tot], axis=0)


def _router_kernel(x_ref, w_ref, b_ref, e_ref, g_ref):
    e_ref[...], g_ref[...] = _route_tokens(x_ref[...], w_ref[...], b_ref[...])


def route(xt, router_w, router_b):
    T, D = xt.shape
    tm = _pick(T, (512, 256, 128))
    return pl.pallas_call(
        _router_kernel,
        grid=(T // tm,),
        in_specs=[pl.BlockSpec((tm, D), lambda i: (i, 0)),
                  pl.BlockSpec((N_EXPERTS, D), lambda i: (0, 0)),
                  pl.BlockSpec((N_EXPERTS, 1), lambda i: (0, 0))],
        out_specs=[pl.BlockSpec((TOP_K, tm), lambda i: (0, i)),
                   pl.BlockSpec((TOP_K, tm), lambda i: (0, i))],
        out_shape=[jax.ShapeDtypeStruct((TOP_K, T), jnp.int32),
                   jax.ShapeDtypeStruct((TOP_K, T), F32)],
        compiler_params=_params(("arbitrary",)),
    )(xt, router_w.T, router_b.reshape(N_EXPERTS, 1))


def _ffn_kernel(be_ref, x_ref, wg_ref, wu_ref, wd_ref, o_ref):
    i = pl.program_id(0)
    used = be_ref[pl.num_programs(0)]

    @pl.when(i < used)
    def _():
        x = x_ref[...].astype(BF16)
        a = jnp.dot(x, wg_ref[0, 0], preferred_element_type=F32)
        u = jnp.dot(x, wu_ref[0, 0], preferred_element_type=F32)
        h = (a * jax.nn.sigmoid(a)) * u
        o_ref[...] = jnp.dot(h.astype(BF16), wd_ref[0, 0], preferred_element_type=F32)

    @pl.when(i >= used)
    def _():
        o_ref[...] = jnp.zeros(o_ref.shape, F32)


def _cast_kernel(x_ref, o_ref):
    o_ref[...] = x_ref[...].astype(o_ref.dtype)


def cast_bf16(w):
    n = w.shape[-1]
    w2 = w.reshape(-1, n)
    rows = w2.shape[0]
    tr = _pick(rows, (1024 * 1024 // n, 512, 256, 128, 64, 32, 16))
    out = pl.pallas_call(
        _cast_kernel,
        grid=(rows // tr,),
        in_specs=[pl.BlockSpec((tr, n), lambda i: (i, 0))],
        out_specs=pl.BlockSpec((tr, n), lambda i: (i, 0)),
        out_shape=jax.ShapeDtypeStruct(w2.shape, BF16),
        compiler_params=_params(("arbitrary",)),
    )(w2)
    return out.reshape(w.shape)


def expert_ffn(xs, block_exp, layer, wg, wu, wd):
    R, D = xs.shape
    F = wg.shape[3]
    nb = R // MOE_TM
    grid_spec = pltpu.PrefetchScalarGridSpec(
        num_scalar_prefetch=1,
        grid=(nb,),
        in_specs=[pl.BlockSpec((MOE_TM, D), lambda i, be: (i, 0)),
                  pl.BlockSpec((1, 1, D, F), lambda i, be: (layer, be[i], 0, 0)),
                  pl.BlockSpec((1, 1, D, F), lambda i, be: (layer, be[i], 0, 0)),
                  pl.BlockSpec((1, 1, F, D), lambda i, be: (layer, be[i], 0, 0))],
        out_specs=pl.BlockSpec((MOE_TM, D), lambda i, be: (i, 0)),
    )
    return pl.pallas_call(
        _ffn_kernel,
        grid_spec=grid_spec,
        out_shape=jax.ShapeDtypeStruct((R, D), F32),
        compiler_params=_params(("arbitrary",)),
    )(block_exp, xs, wg, wu, wd)


def _combine_ln_kernel(ya_ref, yb_ref, ga_ref, gb_ref, r_ref, g_ref, b_ref, o_ref):
    y = ya_ref[...] * ga_ref[...] + yb_ref[...] * gb_ref[...]
    o_ref[...] = _deepnorm(r_ref[...], y, g_ref[...], b_ref[...])


def combine_ln(ya, yb, ga, gb, resid, g, b):
    M, N = ya.shape
    tm = _pick(M, (256, 128, 64, 32, 16, 8))
    blk = pl.BlockSpec((tm, N), lambda i: (i, 0))
    col = pl.BlockSpec((tm, 1), lambda i: (i, 0))
    row = pl.BlockSpec((1, N), lambda i: (0, 0))
    return pl.pallas_call(
        _combine_ln_kernel,
        grid=(M // tm,),
        in_specs=[blk, blk, col, col, blk, row, row],
        out_specs=blk,
        out_shape=jax.ShapeDtypeStruct((M, N), F32),
        compiler_params=_params(("arbitrary",)),
    )(ya, yb, ga, gb, resid, g.reshape(1, N), b.reshape(1, N))


def moe_layer(x, router_w, router_b, layer, wg, wu, wd, ln_g, ln_b):
    shp = x.shape
    xt = x.reshape(-1, shp[-1])
    T = xt.shape[0]
    expert, gate = route(xt, router_w, router_b)
    flat_e = expert.T.reshape(-1)
    n_assign = T * TOP_K
    onehot = (flat_e[:, None] == jnp.arange(N_EXPERTS, dtype=jnp.int32)[None]).astype(jnp.int32)
    csum = jnp.cumsum(onehot, axis=0)
    rank = jnp.take_along_axis(csum, flat_e[:, None], axis=1)[:, 0] - 1
    counts = csum[-1]
    padded = (counts + MOE_TM - 1) // MOE_TM * MOE_TM
    pad_end = jnp.cumsum(padded)
    pad_start = pad_end - padded
    dest = pad_start[flat_e] + rank
    nb = -(-n_assign // MOE_TM) + N_EXPERTS
    rows = nb * MOE_TM
    row_tok = jnp.zeros(rows, jnp.int32).at[dest].set(jnp.arange(n_assign, dtype=jnp.int32) // TOP_K)
    first_row = jnp.arange(nb, dtype=jnp.int32) * MOE_TM
    block_exp = jnp.minimum(jnp.sum((pad_end[None, :] <= first_row[:, None]).astype(jnp.int32), axis=1),
                            N_EXPERTS - 1)
    block_exp = jnp.concatenate([block_exp, pad_end[-1:] // MOE_TM])
    yb = expert_ffn(xt[row_tok], block_exp, layer, wg, wu, wd)
    d2 = dest.reshape(T, TOP_K)
    out = combine_ln(yb[d2[:, 0]], yb[d2[:, 1]], gate[0][:, None], gate[1][:, None], xt, ln_g, ln_b)
    return out.reshape(shp)


def _block_diag(w):
    G, I, O = w.shape
    eye = jnp.eye(G, dtype=w.dtype)
    return (eye[:, None, :, None] * w[:, :, None, :]).reshape(G * I, G * O)


def even_layer(x, lru_h, lru_conv, wkv, shift, p, ln_g, ln_b):
    B, T, D = x.shape
    D_LRU = lru_h.shape[-1]
    D_RWKV = p['w0'].shape[1]
    H = D_RWKV // RWKV_HEAD
    N = RWKV_HEAD
    z = matmul(x.reshape(B * T, D), p['w_in']).reshape(B, T, -1)
    y_a, feats, g, h_T = even_pre(z, lru_conv, shift, lru_h, p)

    feats = jnp.transpose(feats.reshape(B, SCAN_FEATS, T, H, N), (1, 2, 4, 0, 3))
    feats = feats.reshape(SCAN_FEATS, T, N, B * H)
    per_chain = lambda w: jnp.tile(w.reshape(H, N).T, (1, B))
    s0 = jnp.transpose(wkv, (2, 3, 0, 1)).reshape(N, N, B * H)
    y, s_T = rwkv_scan(feats, per_chain(p['k_k']), per_chain(p['k_a']), per_chain(p['gn_g']),
                       per_chain(p['gn_b']), per_chain(p['r_k']), s0)
    y = jnp.transpose(y.reshape(T, N, B, H), (2, 0, 3, 1)).reshape(B * T, D_RWKV)
    S_T = jnp.transpose(s_T.reshape(N, N, B, H), (2, 3, 0, 1))

    x_new = mix_ln(y_a.reshape(B * T, D_LRU), y, g.reshape(B * T, D_RWKV), p['w_out'],
                   x.reshape(B * T, D), ln_g, ln_b)
    conv_state = jnp.concatenate([lru_conv, z[..., :D_LRU]], axis=1)[:, T:]
    return x_new.reshape(B, T, D), (h_T[:, 0], conv_state, S_T, z[:, -1:, 2 * D_LRU:])


def odd_layer(x, cache, p, ln_g, ln_b):
    B, T, D = x.shape
    dq = N_HEADS_C * HEAD_DIM_C
    dkv = N_KV_C * HEAD_DIM_C
    dqi = N_IDX_HEADS * D_IDX
    z = matmul(x.reshape(B * T, D), p['w_in']).reshape(B, T, -1)
    k = z[..., dq:dq + dkv]
    v = z[..., dq + dkv:dq + 2 * dkv]
    o = dq + 2 * dkv + dqi
    ki = z[..., o:o + D_IDX]
    wi = z[..., o + D_IDX:o + D_IDX + N_IDX_HEADS] * (dqi ** -0.5)
    Tp = -(-T // LANES) * LANES
    if cache is None:
        past = 0
        n_sel = min(TOPK_MAX, T // 4)
        parts = lambda c, n: [n]
    else:
        past = cache[0].shape[1]
        n_sel = min(TOPK_MAX, (past + T) // 4)
        parts = lambda c, n: [c.reshape(B, past, -1), n]
    L = -(-(past + Tp) // LANES) * LANES
    tk = _pick(L, (512, 384, 128))
    cat = lambda c, n: jnp.concatenate(
        parts(c, n) + [jnp.zeros((B, L - past - T, n.shape[-1]), n.dtype)], axis=1).astype(BF16)
    ck, cv, cki = cache if cache is not None else (None, None, None)
    keys, vals, kidx = cat(ck, k), cat(cv, v), cat(cki, ki)
    vals_t = jnp.swapaxes(vals.reshape(B, L // tk, tk, dkv), 2, 3)
    rowpad = lambda t: jnp.pad(t, ((0, 0), (0, Tp - T), (0, 0)))
    wi_t = jnp.swapaxes(rowpad(wi), 1, 2)
    o_att = dsa_attention(rowpad(z), 0, (dq + 2 * dkv) // dqi, wi_t, keys, vals_t, kidx, past, n_sel, tk)
    x_new = matmul_ln(o_att[:, :T].reshape(B * T, dq), p['w_out'], x.reshape(B * T, D), ln_g, ln_b)
    new = (k.reshape(B, T, N_KV_C, HEAD_DIM_C), v.reshape(B, T, N_KV_C, HEAD_DIM_C), ki)
    return x_new.reshape(B, T, D), new


def run_trunk(x, even_state, odd_cache, ev, od, norm_w, moe_w):
    ln1_g, ln1_b, ln2_g, ln2_b = norm_w
    router_w, router_b, wg, wu, wd = moe_w
    x, new_even = even_layer(x, *even_state, ev, ln1_g[0], ln1_b[0])
    x = moe_layer(x, router_w, router_b, 0, wg, wu, wd, ln2_g[0], ln2_b[0])
    x, new_odd = odd_layer(x, odd_cache, od, ln1_g[1], ln1_b[1])
    x = moe_layer(x, router_w, router_b, 1, wg, wu, wd, ln2_g[1], ln2_b[1])
    return x, new_even, new_odd


def kernel(x_prompt, x_sample, state_lru_h, state_lru_conv, state_rwkv_wkv, state_rwkv_shift, cache_k, cache_v, cache_kidx, ev_w_in, ev_conv_w, ev_conv_b, ev_gate_r_w, ev_gate_r_b, ev_gate_i_w, ev_gate_i_b, ev_lambda, ev_shift_mu, ev_w0, ev_w2, ev_a0, ev_a2, ev_g2, ev_k_k, ev_k_a, ev_r_k, ev_gn_g, ev_gn_b, ev_w_out, od_w_in, od_w_out, ln1_g, ln1_b, ln2_g, ln2_b, router_w, router_b, moe_w_gate, moe_w_up, moe_w_down):
    ev = dict(
        w_in=ev_w_in[0].astype(BF16), conv_w=ev_conv_w[0], conv_b=ev_conv_b[0][None],
        gate_w=jnp.concatenate([_block_diag(ev_gate_r_w[0]), _block_diag(ev_gate_i_w[0])], axis=1).astype(BF16),
        gate_b=jnp.concatenate([ev_gate_r_b[0], ev_gate_i_b[0]])[None],
        lru_c=(-LRU_C * jax.nn.softplus(-ev_lambda[0]))[None],
        shift_mu=ev_shift_mu[0][None], w0=ev_w0[0][None], a0=ev_a0[0][None],
        w2=ev_w2[0].astype(BF16), a2=ev_a2[0].astype(BF16), g2=ev_g2[0].astype(BF16),
        k_k=ev_k_k[0][None], k_a=ev_k_a[0][None], r_k=ev_r_k[0], gn_g=ev_gn_g[0], gn_b=ev_gn_b[0],
        w_out=ev_w_out[0].astype(BF16))
    n_in = od_w_in.shape[2]
    od = dict(w_in=jnp.pad(od_w_in[0], ((0, 0), (0, (-n_in) % LANES))).astype(BF16),
              w_out=od_w_out[0].astype(BF16))
    norm_w = (ln1_g, ln1_b, ln2_g, ln2_b)
    moe_w = (router_w, router_b, cast_bf16(moe_w_gate), cast_bf16(moe_w_up), cast_bf16(moe_w_down))

    B = x_prompt.shape[0]
    fresh = (jnp.zeros((B,) + state_lru_h.shape[2:], F32),
             jnp.zeros((B,) + state_lru_conv.shape[2:], F32),
             jnp.zeros((B,) + state_rwkv_wkv.shape[2:], F32),
             jnp.zeros((B,) + state_rwkv_shift.shape[2:], F32))
    y_p, pe, po = run_trunk(x_prompt, fresh, None, ev, od, norm_w, moe_w)
    y_s, se, so = run_trunk(
        x_sample, (state_lru_h[0], state_lru_conv[0], state_rwkv_wkv[0], state_rwkv_shift[0]),
        (cache_k[0], cache_v[0], cache_kidx[0]), ev, od, norm_w, moe_w)
    st = lambda t: t[None]
    return (y_p, y_s, *map(st, pe), *map(st, po), *map(st, se), *map(st, so))
```

```python
import functools
import math

import jax
import jax.numpy as jnp
from jax import lax
from jax.experimental import pallas as pl
from jax.experimental.pallas import tpu as pltpu

F32 = jnp.float32
BF16 = jnp.bfloat16

CHUNK = 64
CONV_W = 4
LRU_C = 8.0
RWKV_HEAD = 64
DECAY_LORA = 64
AAA_LORA = 64
GATE_LORA = 128
N_HEADS_C = 16
HEAD_DIM_C = 128
N_KV_C = 4
N_IDX_HEADS = 16
D_IDX = 64
TOPK_MAX = 256
N_EXPERTS = 16
N_GROUPS = 4
EXPERTS_PER_GROUP = N_EXPERTS // N_GROUPS
TOP_K = 2
DEPTH = 2
DN_ALPHA = (2 * DEPTH) ** 0.25
LN_EPS = 1e-5
GN_EPS = 64e-5

LANES = 128
SUBLANES = 8
VMEM_LIMIT = 56 * 1024 * 1024
MOE_TM = 256


def _params(sem):
    return pltpu.CompilerParams(dimension_semantics=sem, vmem_limit_bytes=VMEM_LIMIT)


def _mm_kernel(x_ref, w_ref, o_ref, xb_ref):
    @pl.when(pl.program_id(1) == 0)
    def _():
        xb_ref[...] = x_ref[...].astype(BF16)

    o_ref[...] = jnp.dot(xb_ref[...], w_ref[...], preferred_element_type=F32)


def _pick(n, cands):
    for c in cands:
        if n % c == 0:
            return c
    return n


def matmul(x, w):
    M, K = x.shape
    N = w.shape[1]
    tm = _pick(M, (1024, 512, 256, 128, 64, 32, 16, 8))
    tn = _pick(N, (1792, 1408, 1024, 768, 512, 384, 256, 128))
    return pl.pallas_call(
        _mm_kernel,
        grid=(M // tm, N // tn),
        in_specs=[pl.BlockSpec((tm, K), lambda i, j: (i, 0)),
                  pl.BlockSpec((K, tn), lambda i, j: (0, j))],
        out_specs=pl.BlockSpec((tm, tn), lambda i, j: (i, j)),
        out_shape=jax.ShapeDtypeStruct((M, N), F32),
        scratch_shapes=[pltpu.VMEM((tm, K), BF16)],
        compiler_params=_params(("arbitrary", "arbitrary")),
    )(x, w)


def _deepnorm(resid, y, g, b):
    h = DN_ALPHA * resid + y
    mu = jnp.mean(h, axis=-1, keepdims=True)
    hc = h - mu
    var = jnp.mean(hc * hc, axis=-1, keepdims=True)
    return hc * lax.rsqrt(var + LN_EPS) * g + b


def _mm_ln_kernel(x_ref, w_ref, r_ref, g_ref, b_ref, o_ref):
    y = jnp.dot(x_ref[...].astype(BF16), w_ref[...], preferred_element_type=F32)
    o_ref[...] = _deepnorm(r_ref[...], y, g_ref[...], b_ref[...])


def _mix_ln_kernel(ya_ref, yb_ref, gate_ref, w_ref, r_ref, g_ref, b_ref, o_ref):
    mix = jnp.concatenate([ya_ref[...], yb_ref[...] * gate_ref[...]], axis=1).astype(BF16)
    y = jnp.dot(mix, w_ref[...], preferred_element_type=F32)
    o_ref[...] = _deepnorm(r_ref[...], y, g_ref[...], b_ref[...])


def _ln_call(kern, lhs, w, resid, g, b):
    M, N = resid.shape
    tm = _pick(M, (512, 256, 128, 64, 32, 16, 8))
    full = lambda a: pl.BlockSpec(a.shape, lambda i: (0,) * a.ndim)
    rows = lambda a: pl.BlockSpec((tm, a.shape[1]), lambda i: (i, 0))
    consts = [g.reshape(1, N), b.reshape(1, N)]
    return pl.pallas_call(
        kern,
        grid=(M // tm,),
        in_specs=[rows(a) for a in lhs] + [full(w), rows(resid)] + [full(c) for c in consts],
        out_specs=rows(resid),
        out_shape=jax.ShapeDtypeStruct((M, N), F32),
        compiler_params=_params(("arbitrary",)),
    )(*lhs, w, resid, *consts)


def matmul_ln(x, w, resid, g, b):
    return _ln_call(_mm_ln_kernel, [x], w, resid, g, b)


def mix_ln(ya, yb, gate, w, resid, g, b):
    return _ln_call(_mix_ln_kernel, [ya, yb, gate], w, resid, g, b)


def _expm1(x):
    e = jnp.exp(x)
    safe = jnp.where((e == 1.0) | (e == 0.0), 2.0, e)
    return jnp.where(e == 1.0, x, jnp.where(e == 0.0, -1.0, (e - 1.0) * x / jnp.log(safe)))


def _softplus(x):
    return jnp.maximum(x, 0.0) + jnp.log(1.0 + jnp.exp(-jnp.abs(x)))


HALO = SUBLANES


def _even_pre_kernel(ax_ref, ag_ref, zr_ref, zk_ref, zv_ref, zt_ref,
                     conv_ref, sr_ref, sk_ref, sv_ref, st_ref, h0_ref,
                     cw_ref, cb_ref, gw_ref, gb_ref, lc_ref, mu_ref,
                     w0_ref, a0_ref, w2_ref, a2_ref, g2_ref,
                     ya_ref, r_ref, w_ref, k_ref, v_ref, a_ref, g_ref, hT_ref,
                     xp_scr, pr_scr, pk_scr, pv_scr, pt_scr, a_scr, u_scr, hs_scr, h_scr):
    i = pl.program_id(1)
    tm = ax_ref.shape[1]
    C = ax_ref.shape[2]
    prev = ((xp_scr, None), (pr_scr, sr_ref), (pk_scr, sk_ref), (pv_scr, sv_ref), (pt_scr, st_ref))

    @pl.when(i == 0)
    def _():
        xp_scr[...] = jnp.zeros((HALO, C), F32)
        xp_scr[HALO - (CONV_W - 1):HALO, :] = conv_ref[0]
        for scr, ref in prev[1:]:
            scr[...] = jnp.broadcast_to(ref[0], scr.shape)
        h_scr[...] = jnp.broadcast_to(h0_ref[0], h_scr.shape)

    def earlier(x, scr, d):
        rot = pltpu.roll(x, d, axis=0)
        halo = pltpu.roll(scr[...], d, axis=0)
        sub = lax.broadcasted_iota(jnp.int32, halo.shape, 0)
        return jnp.concatenate([jnp.where(sub < d, halo, rot[:HALO]), rot[HALO:]], axis=0)

    ax = ax_ref[0]
    xa = cb_ref[...] + cw_ref[CONV_W - 1:CONV_W, :] * ax
    for d in range(1, CONV_W):
        xa = xa + cw_ref[CONV_W - 1 - d:CONV_W - d, :] * earlier(ax, xp_scr, d)
    gates = jnp.dot(xa.astype(BF16), gw_ref[...], preferred_element_type=F32) + gb_ref[...]
    log_a = lc_ref[...] * jax.nn.sigmoid(gates[:, :C])
    a_scr[...] = jnp.exp(log_a)
    u_scr[...] = jnp.sqrt(-_expm1(2.0 * log_a)) * (jax.nn.sigmoid(gates[:, C:]) * xa)

    def step(t, h):
        h = a_scr[pl.ds(t, 1), :] * h + u_scr[pl.ds(t, 1), :]
        hs_scr[pl.ds(t, 1), :] = h
        return h

    h = lax.fori_loop(0, tm, step, h_scr[0:1, :], unroll=8)
    h_scr[...] = jnp.broadcast_to(h, h_scr.shape)
    hT_ref[0] = h
    ya_ref[0] = hs_scr[...] * jax.nn.gelu(ag_ref[0], approximate=True)

    zr, zk, zv, zt = zr_ref[0], zk_ref[0], zv_ref[0], zt_ref[0]

    def shifted(zb, scr, lo, hi):
        return zb + mu_ref[:, lo:hi] * (earlier(zb, scr, 1) - zb)

    r = shifted(zr, pr_scr, 0, C)
    k = shifted(zk, pk_scr, C, 2 * C)
    v = shifted(zv, pv_scr, 2 * C, 3 * C)
    tail = shifted(zt, pt_scr, 3 * C, 3 * C + pt_scr.shape[1])
    wl = jnp.tanh(tail[:, :DECAY_LORA])
    al = tail[:, DECAY_LORA:DECAY_LORA + AAA_LORA]
    gl = jax.nn.sigmoid(tail[:, DECAY_LORA + AAA_LORA:])
    lora = lambda x, w_ref_: jnp.dot(x.astype(BF16), w_ref_[...], preferred_element_type=F32)
    w_log = -_softplus(-(w0_ref[...] + lora(wl, w2_ref))) - 0.5
    a = jax.nn.sigmoid(a0_ref[...] + lora(al, a2_ref))
    r_ref[0] = r.astype(r_ref.dtype)
    w_ref[0] = jnp.exp(-jnp.exp(w_log))
    k_ref[0] = k.astype(k_ref.dtype)
    v_ref[0] = v
    a_ref[0] = a
    g_ref[0] = lora(gl, g2_ref)

    for (scr, _), x in zip(prev, (ax, zr, zk, zv, zt)):
        scr[...] = x[tm - HALO:]


def even_pre(z, lru_conv, shift, lru_h, p):
    B, T, _ = z.shape
    C = lru_h.shape[-1]
    tail = DECAY_LORA + AAA_LORA + GATE_LORA
    tm = _pick(T, (256, 128, 64, 32, 16, 8))
    zblk = lambda w, col: pl.BlockSpec((1, tm, w), lambda b, i: (b, i, col))
    sblk = lambda rows, w, col: pl.BlockSpec((1, rows, w), lambda b, i: (b, 0, col))
    full = lambda a: pl.BlockSpec(a.shape, lambda b, i: (0,) * a.ndim)
    weights = [p['conv_w'], p['conv_b'], p['gate_w'], p['gate_b'], p['lru_c'], p['shift_mu'],
               p['w0'], p['a0'], p['w2'], p['a2'], p['g2']]
    tok = jax.ShapeDtypeStruct((B, T, C), F32)
    tok16 = jax.ShapeDtypeStruct((B, T, C), BF16)
    return pl.pallas_call(
        _even_pre_kernel,
        grid=(B, T // tm),
        in_specs=[zblk(C, 0), zblk(C, 1), zblk(C, 2), zblk(C, 3), zblk(C, 4), zblk(tail, 5 * C // tail),
                  sblk(CONV_W - 1, C, 0), sblk(1, C, 0), sblk(1, C, 1), sblk(1, C, 2),
                  sblk(1, tail, 3 * C // tail), sblk(1, C, 0)] + [full(w) for w in weights],
        out_specs=[zblk(C, 0)] * 7 + [sblk(1, C, 0)],
        out_shape=[tok, tok16, tok, tok16, tok, tok, tok] + [jax.ShapeDtypeStruct((B, 1, C), F32)],
        scratch_shapes=[pltpu.VMEM((HALO, C), F32)] * 4 + [pltpu.VMEM((HALO, tail), F32)]
                       + [pltpu.VMEM((tm, C), F32)] * 3 + [pltpu.VMEM((SUBLANES, C), F32)],
        compiler_params=_params(("arbitrary", "arbitrary")),
    )(z, z, z, z, z, z, lru_conv, shift, shift, shift, shift, lru_h[:, None, :], *weights)


def _colsum(x):
    n = x.shape[0] // SUBLANES
    p = x.reshape(n, SUBLANES, x.shape[1]).sum(axis=0)
    p = p + pltpu.roll(p, 4, axis=0)
    p = p + pltpu.roll(p, 2, axis=0)
    p = p + pltpu.roll(p, 1, axis=0)
    return p


def _bcast_rows(p, n):
    return jnp.broadcast_to(p[None], (n // SUBLANES,) + p.shape).reshape(n, p.shape[1])


def _rwkv_kernel(r_ref, w_ref, k_ref, v_ref, a_ref, kk_ref, ka_ref, gng_ref, gnb_ref, rk_ref, s0_ref,
                 y_ref, sT_ref, s_scr):
    tc = pl.program_id(1)
    N = RWKV_HEAD

    @pl.when(tc == 0)
    def _():
        s_scr[...] = s0_ref[...]

    sub = lax.broadcasted_iota(jnp.int32, (SUBLANES, LANES), 0)

    def step(t, carry):
        r = r_ref[t].astype(F32)
        w = w_ref[t]
        k_raw = k_ref[t].astype(F32)
        a = a_ref[t]
        kk = k_raw * kk_ref[...]
        kk = kk / jnp.maximum(_bcast_rows(jnp.sqrt(_colsum(kk * kk)), N), 1e-12)
        nkk = -kk
        kka = kk * a
        k = k_raw * (1.0 + (a - 1.0) * ka_ref[...])

        def vgroup(g, c2):
            yacc = jnp.zeros((SUBLANES, LANES), F32)
            for j in range(SUBLANES):
                vi = g * SUBLANES + j
                S = s_scr[vi]
                sa = _bcast_rows(_colsum(S * nkk), N)
                vb = jnp.broadcast_to(v_ref[t, pl.ds(vi, 1), :], (N, LANES))
                S = S * w + sa * kka + vb * k
                s_scr[vi] = S
                yacc = jnp.where(sub == j, _colsum(S * r), yacc)
            y_ref[t, pl.ds(pl.multiple_of(g * SUBLANES, SUBLANES), SUBLANES), :] = yacc
            return c2

        lax.fori_loop(0, N // SUBLANES, vgroup, 0, unroll=True)

        y = y_ref[t]
        d = y - _bcast_rows(_colsum(y), N) * (1.0 / N)
        var = _bcast_rows(_colsum(d * d), N) * (1.0 / N)
        bonus = _bcast_rows(_colsum(r * k * rk_ref[...]), N)
        y_ref[t] = d * lax.rsqrt(var + GN_EPS) * gng_ref[...] + gnb_ref[...] + bonus * v_ref[t]
        return carry

    lax.fori_loop(0, r_ref.shape[0], step, 0)

    @pl.when(tc == pl.num_programs(1) - 1)
    def _():
        sT_ref[...] = s_scr[...]


def rwkv_scan(r, w, k, v, a, k_k, k_a, gn_g, gn_b, r_k, s0):
    T, N, P = r.shape
    tc = _pick(T, (64, 32, 16, 8))
    blk = pl.BlockSpec((tc, N, LANES), lambda p, t: (t, 0, p))
    par = pl.BlockSpec((N, LANES), lambda p, t: (0, p))
    st = pl.BlockSpec((N, N, LANES), lambda p, t: (0, 0, p))
    return pl.pallas_call(
        _rwkv_kernel,
        grid=(P // LANES, T // tc),
        in_specs=[blk] * 5 + [par] * 5 + [st],
        out_specs=[blk, st],
        out_shape=[jax.ShapeDtypeStruct((T, N, P), F32), jax.ShapeDtypeStruct((N, N, P), F32)],
        scratch_shapes=[pltpu.VMEM((N, N, LANES), F32)],
        compiler_params=_params(("arbitrary", "arbitrary")),
    )(r, w, k, v, a, k_k, k_a, gn_g, gn_b, r_k, s0)


MASKED = -1e30


KEY_NEG_INF = -2139095041
QK_SPLIT = 4


def _tree_sum(parts):
    while len(parts) > 1:
        parts = [parts[i] + parts[i + 1] for i in range(0, len(parts) - 1, 2)] + parts[len(parts) & ~1:]
    return parts[0]


def _dsa_kernel(q_ref, qi_ref, wi_ref, k_ref, vt_ref, ki_ref, o_ref,
                key_scr, m_scr, l_scr, acc_scr, *, past, n_sel, scale, tk):
    qb = pl.program_id(1)
    QB = q_ref.shape[1]
    L = k_ref.shape[1]
    NP = N_HEADS_C // 2
    lane_chunk = lax.broadcasted_iota(jnp.int32, (1, QB), 1) // CHUNK
    limit = past + (qb * (QB // CHUNK) + lane_chunk + 1) * CHUNK
    n_tiles = (past + (qb + 1) * QB + tk - 1) // tk
    u = 2 if (tk < 512 and (L // tk) % 2 == 0) else 1
    n_count = (n_tiles + u - 1) // u
    row = lax.broadcasted_iota(jnp.int32, (tk, QB), 0)
    tile = lambda kt: pl.ds(pl.multiple_of(kt * tk, tk), tk)

    qi = qi_ref[0].astype(BF16)
    qi_pair = [jnp.concatenate([qi[:, (2 * p) * D_IDX:(2 * p + 1) * D_IDX],
                                qi[:, (2 * p + 1) * D_IDX:(2 * p + 2) * D_IDX]], axis=0)
               for p in range(N_IDX_HEADS // 2)]
    wi = wi_ref[0]

    def score_tile(kt, c):
        @pl.when(kt < n_tiles)
        def _():
            kidx = ki_ref[0, tile(kt), :]
            terms = []
            for p in range(N_IDX_HEADS // 2):
                s = lax.dot_general(kidx, qi_pair[p], (((1,), (1,)), ((), ())),
                                    preferred_element_type=F32)
                for e in range(2):
                    h = 2 * p + e
                    terms.append(wi[h:h + 1, :] * jnp.maximum(s[:, e * QB:(e + 1) * QB], 0.0))
            isc = terms[0]
            for t in terms[1:]:
                isc = isc + t
            ok = row + kt * tk < limit
            bits = lax.bitcast_convert_type(jnp.where(ok, isc + 0.0, -jnp.inf), jnp.int32)
            key_scr[tile(kt), :] = bits ^ ((bits >> 31) & jnp.int32(0x7FFFFFFF))

        @pl.when(kt >= n_tiles)
        def _():
            key_scr[tile(kt), :] = jnp.full((tk, QB), KEY_NEG_INF, jnp.int32)

        return c

    lax.fori_loop(0, n_count * u, score_tile, 0)

    int_min = jnp.int32(-2 ** 31)
    ct = tk * u

    def search(i, t):
        cand = t | lax.shift_left(jnp.int32(1), 31 - i)
        cs = cand ^ int_min

        def count_tile(kt, acc):
            keys = key_scr[pl.ds(pl.multiple_of(kt * ct, ct), ct), :]
            hit = jnp.where(keys >= cs, 1.0, 0.0)
            return acc + _tree_sum([hit[r * SUBLANES:(r + 1) * SUBLANES] for r in range(ct // SUBLANES)])

        acc = lax.fori_loop(0, n_count, count_tile, jnp.zeros((SUBLANES, QB), F32))
        cnt = jnp.sum(acc, axis=0, keepdims=True)
        return jnp.where(cnt >= n_sel, cand, t)

    thr = lax.fori_loop(0, 32, search, jnp.zeros((1, QB), jnp.int32)) ^ int_min

    G = N_HEADS_C // N_KV_C
    dh = HEAD_DIM_C
    q = q_ref[0]
    q_pair = [(jnp.concatenate([q[:, (2 * p) * dh:(2 * p + 1) * dh],
                                q[:, (2 * p + 1) * dh:(2 * p + 2) * dh]], axis=0) * scale).astype(BF16)
              for p in range(NP)]
    m_scr[...] = jnp.full(m_scr.shape, MASKED, F32)
    l_scr[...] = jnp.zeros(l_scr.shape, F32)
    acc_scr[...] = jnp.zeros(acc_scr.shape, F32)

    def att_tile(kt, c):
        sel = (key_scr[tile(kt), :] >= thr) & (row + kt * tk < limit)
        b = jnp.where(sel, 0.0, MASKED)
        b2 = jnp.concatenate([b, b], axis=1)
        def logits(p):
            j = (2 * p) // G
            kj = k_ref[0, tile(kt), j * dh:(j + 1) * dh]
            part = tk // QK_SPLIT
            return jnp.concatenate(
                [lax.dot_general(kj[r * part:(r + 1) * part], q_pair[p], (((1,), (1,)), ((), ())),
                                 preferred_element_type=F32) for r in range(QK_SPLIT)], axis=0)

        ahead = 4
        lg = {p: logits(p) for p in range(ahead)}
        for p in range(NP):
            j = (2 * p) // G
            x = lg.pop(p) + b2
            m = m_scr[p:p + 1, :]
            m_new = jnp.maximum(m, jnp.max(x, axis=0, keepdims=True))
            alpha = jnp.exp2(m - m_new)
            pr = jnp.exp2(x - m_new)
            l_scr[p:p + 1, :] = alpha * l_scr[p:p + 1, :] + jnp.sum(pr, axis=0, keepdims=True)
            m_scr[p:p + 1, :] = m_new
            vt = vt_ref[0, kt, j * dh:(j + 1) * dh, :]
            pv = jnp.dot(vt, pr.astype(BF16), preferred_element_type=F32)
            if p + ahead < NP:
                lg[p + ahead] = logits(p + ahead)
            acc_scr[p] = acc_scr[p] * alpha + pv
        return c

    lax.fori_loop(0, n_tiles, att_tile, 0)
    for p in range(NP):
        o2 = acc_scr[p] / l_scr[p:p + 1, :]
        for s in range(2):
            o_ref[0, :, (2 * p + s) * dh:(2 * p + s + 1) * dh] = o2[:, s * QB:(s + 1) * QB].T


def dsa_attention(z, q_col, qi_col, wi_t, keys, vals_t, kidx, past, n_sel, tk):
    B, T, _ = z.shape
    L = keys.shape[1]
    QB = LANES
    dq = N_HEADS_C * HEAD_DIM_C
    dqi = N_IDX_HEADS * D_IDX
    kern = functools.partial(_dsa_kernel, past=past, n_sel=n_sel,
                             scale=HEAD_DIM_C ** -0.5 * math.log2(math.e), tk=tk)
    return pl.pallas_call(
        kern,
        grid=(B, T // QB),
        in_specs=[pl.BlockSpec((1, QB, dq), lambda b, c: (b, c, q_col)),
                  pl.BlockSpec((1, QB, dqi), lambda b, c: (b, c, qi_col)),
                  pl.BlockSpec((1, N_IDX_HEADS, QB), lambda b, c: (b, 0, c)),
                  pl.BlockSpec((1, L, keys.shape[2]), lambda b, c: (b, 0, 0)),
                  pl.BlockSpec((1,) + vals_t.shape[1:], lambda b, c: (b, 0, 0, 0)),
                  pl.BlockSpec((1, L, kidx.shape[2]), lambda b, c: (b, 0, 0))],
        out_specs=pl.BlockSpec((1, QB, dq), lambda b, c: (b, c, 0)),
        out_shape=jax.ShapeDtypeStruct((B, T, dq), F32),
        scratch_shapes=[pltpu.VMEM((L, QB), jnp.int32),
                        pltpu.VMEM((N_HEADS_C // 2, 2 * QB), F32),
                        pltpu.VMEM((N_HEADS_C // 2, 2 * QB), F32),
                        pltpu.VMEM((N_HEADS_C // 2, HEAD_DIM_C, 2 * QB), F32)],
        compiler_params=_params(("arbitrary", "arbitrary")),
    )(z, z, wi_t, keys, vals_t, kidx)


def _route_tokens(x, w, bias):
    def split(t):
        hi = t.astype(BF16)
        return hi, (t - hi.astype(F32)).astype(BF16)

    nt = lambda a, c: lax.dot_general(a, c, (((1,), (1,)), ((), ())), preferred_element_type=F32)
    (w_hi, w_lo), (x_hi, x_lo) = split(w), split(x)
    logits = nt(w_hi, x_hi) + (nt(w_hi, x_lo) + nt(w_lo, x_hi))
    scores = jax.nn.sigmoid(logits)
    biased = scores + bias
    tm = scores.shape[1]
    P = EXPERTS_PER_GROUP

    def ranks(rows):
        out = []
        for i in range(len(rows)):
            rk = jnp.zeros((1, tm), jnp.int32)
            for j in range(len(rows)):
                if j == i:
                    continue
                beat = (rows[j] > rows[i]) | ((rows[j] == rows[i]) & (j < i))
                rk = rk + beat.astype(jnp.int32)
            out.append(rk)
        return out

    b = [biased[i:i + 1] for i in range(N_EXPERTS)]
    s = [scores[i:i + 1] for i in range(N_EXPERTS)]
    grp_rank, grp_score = [], []
    for g in range(N_GROUPS):
        rows = b[g * P:(g + 1) * P]
        rk = ranks(rows)
        grp_rank.append(rk)
        top1 = sum(jnp.where(rk[i] == 0, rows[i], 0.0) for i in range(P))
        top2 = sum(jnp.where(rk[i] == 1, rows[i], 0.0) for i in range(P))
        grp_score.append(top1 + top2)
    gr = ranks(grp_score)
    e1 = jnp.zeros((1, tm), jnp.int32)
    e2 = jnp.zeros((1, tm), jnp.int32)
    w1 = jnp.zeros((1, tm), F32)
    w2 = jnp.zeros((1, tm), F32)
    for g in range(N_GROUPS):
        best = gr[g] == 0
        for i in range(P):
            is1 = best & (grp_rank[g][i] == 0)
            is2 = best & (grp_rank[g][i] == 1)
            e1 = jnp.where(is1, g * P + i, e1)
            e2 = jnp.where(is2, g * P + i, e2)
            w1 = jnp.where(is1, s[g * P + i], w1)
            w2 = jnp.where(is2, s[g * P + i], w2)
    tot = w1 + w2
    return jnp.concatenate([e1, e2], axis=0), jnp.concatenate([w1 / tot, w2 / tot], axis=0)


def _router_kernel(x_ref, w_ref, b_ref, e_ref, g_ref):
    e_ref[...], g_ref[...] = _route_tokens(x_ref[...], w_ref[...], b_ref[...])


def route(xt, router_w, router_b):
    T, D = xt.shape
    tm = _pick(T, (512, 256, 128))
    return pl.pallas_call(
        _router_kernel,
        grid=(T // tm,),
        in_specs=[pl.BlockSpec((tm, D), lambda i: (i, 0)),
                  pl.BlockSpec((N_EXPERTS, D), lambda i: (0, 0)),
                  pl.BlockSpec((N_EXPERTS, 1), lambda i: (0, 0))],
        out_specs=[pl.BlockSpec((TOP_K, tm), lambda i: (0, i)),
                   pl.BlockSpec((TOP_K, tm), lambda i: (0, i))],
        out_shape=[jax.ShapeDtypeStruct((TOP_K, T), jnp.int32),
                   jax.ShapeDtypeStruct((TOP_K, T), F32)],
        compiler_params=_params(("arbitrary",)),
    )(xt, router_w.T, router_b.reshape(N_EXPERTS, 1))


def _ffn_kernel(be_ref, x_ref, wg_ref, wu_ref, wd_ref, o_ref):
    i = pl.program_id(0)
    used = be_ref[pl.num_programs(0)]

    @pl.when(i < used)
    def _():
        x = x_ref[...].astype(BF16)
        a = jnp.dot(x, wg_ref[0, 0], preferred_element_type=F32)
        u = jnp.dot(x, wu_ref[0, 0], preferred_element_type=F32)
        h = (a * jax.nn.sigmoid(a)) * u
        o_ref[...] = jnp.dot(h.astype(BF16), wd_ref[0, 0], preferred_element_type=F32)

    @pl.when(i >= used)
    def _():
        o_ref[...] = jnp.zeros(o_ref.shape, F32)


def _cast_kernel(x_ref, o_ref):
    o_ref[...] = x_ref[...].astype(o_ref.dtype)


def cast_bf16(w):
    n = w.shape[-1]
    w2 = w.reshape(-1, n)
    rows = w2.shape[0]
    tr = _pick(rows, (1024 * 1024 // n, 512, 256, 128, 64, 32, 16))
    out = pl.pallas_call(
        _cast_kernel,
        grid=(rows // tr,),
        in_specs=[pl.BlockSpec((tr, n), lambda i: (i, 0))],
        out_specs=pl.BlockSpec((tr, n), lambda i: (i, 0)),
        out_shape=jax.ShapeDtypeStruct(w2.shape, BF16),
        compiler_params=_params(("arbitrary",)),
    )(w2)
    return out.reshape(w.shape)


def expert_ffn(xs, block_exp, layer, wg, wu, wd):
    R, D = xs.shape
    F = wg.shape[3]
    nb = R // MOE_TM
    grid_spec = pltpu.PrefetchScalarGridSpec(
        num_scalar_prefetch=1,
        grid=(nb,),
        in_specs=[pl.BlockSpec((MOE_TM, D), lambda i, be: (i, 0)),
                  pl.BlockSpec((1, 1, D, F), lambda i, be: (layer, be[i], 0, 0)),
                  pl.BlockSpec((1, 1, D, F), lambda i, be: (layer, be[i], 0, 0)),
                  pl.BlockSpec((1, 1, F, D), lambda i, be: (layer, be[i], 0, 0))],
        out_specs=pl.BlockSpec((MOE_TM, D), lambda i, be: (i, 0)),
    )
    return pl.pallas_call(
        _ffn_kernel,
        grid_spec=grid_spec,
        out_shape=jax.ShapeDtypeStruct((R, D), F32),
        compiler_params=_params(("arbitrary",)),
    )(block_exp, xs, wg, wu, wd)


def _combine_ln_kernel(ya_ref, yb_ref, ga_ref, gb_ref, r_ref, g_ref, b_ref, o_ref):
    y = ya_ref[...] * ga_ref[...] + yb_ref[...] * gb_ref[...]
    o_ref[...] = _deepnorm(r_ref[...], y, g_ref[...], b_ref[...])


def combine_ln(ya, yb, ga, gb, resid, g, b):
    M, N = ya.shape
    tm = _pick(M, (256, 128, 64, 32, 16, 8))
    blk = pl.BlockSpec((tm, N), lambda i: (i, 0))
    col = pl.BlockSpec((tm, 1), lambda i: (i, 0))
    row = pl.BlockSpec((1, N), lambda i: (0, 0))
    return pl.pallas_call(
        _combine_ln_kernel,
        grid=(M // tm,),
        in_specs=[blk, blk, col, col, blk, row, row],
        out_specs=blk,
        out_shape=jax.ShapeDtypeStruct((M, N), F32),
        compiler_params=_params(("arbitrary",)),
    )(ya, yb, ga, gb, resid, g.reshape(1, N), b.reshape(1, N))


def moe_layer(x, router_w, router_b, layer, wg, wu, wd, ln_g, ln_b):
    shp = x.shape
    xt = x.reshape(-1, shp[-1])
    T = xt.shape[0]
    expert, gate = route(xt, router_w, router_b)
    flat_e = expert.T.reshape(-1)
    n_assign = T * TOP_K
    onehot = (flat_e[:, None] == jnp.arange(N_EXPERTS, dtype=jnp.int32)[None]).astype(jnp.int32)
    csum = jnp.cumsum(onehot, axis=0)
    rank = jnp.take_along_axis(csum, flat_e[:, None], axis=1)[:, 0] - 1
    counts = csum[-1]
    padded = (counts + MOE_TM - 1) // MOE_TM * MOE_TM
    pad_end = jnp.cumsum(padded)
    pad_start = pad_end - padded
    dest = pad_start[flat_e] + rank
    nb = -(-n_assign // MOE_TM) + N_EXPERTS
    rows = nb * MOE_TM
    row_tok = jnp.zeros(rows, jnp.int32).at[dest].set(jnp.arange(n_assign, dtype=jnp.int32) // TOP_K)
    first_row = jnp.arange(nb, dtype=jnp.int32) * MOE_TM
    block_exp = jnp.minimum(jnp.sum((pad_end[None, :] <= first_row[:, None]).astype(jnp.int32), axis=1),
                            N_EXPERTS - 1)
    block_exp = jnp.concatenate([block_exp, pad_end[-1:] // MOE_TM])
    yb = expert_ffn(xt[row_tok], block_exp, layer, wg, wu, wd)
    d2 = dest.reshape(T, TOP_K)
    out = combine_ln(yb[d2[:, 0]], yb[d2[:, 1]], gate[0][:, None], gate[1][:, None], xt, ln_g, ln_b)
    return out.reshape(shp)


def _block_diag(w):
    G, I, O = w.shape
    eye = jnp.eye(G, dtype=w.dtype)
    return (eye[:, None, :, None] * w[:, :, None, :]).reshape(G * I, G * O)


def even_layer(x, lru_h, lru_conv, wkv, shift, p, ln_g, ln_b):
    B, T, D = x.shape
    D_LRU = lru_h.shape[-1]
    D_RWKV = p['w0'].shape[1]
    H = D_RWKV // RWKV_HEAD
    N = RWKV_HEAD
    z = matmul(x.reshape(B * T, D), p['w_in']).reshape(B, T, -1)
    y_a, r, decay, k, v, a, g, h_T = even_pre(z, lru_conv, shift, lru_h, p)

    lanes = lambda t: jnp.transpose(t.reshape(B, T, H, N), (1, 3, 0, 2)).reshape(T, N, B * H)
    per_chain = lambda w: jnp.tile(w.reshape(H, N).T, (1, B))
    s0 = jnp.transpose(wkv, (2, 3, 0, 1)).reshape(N, N, B * H)
    y, s_T = rwkv_scan(lanes(r), lanes(decay), lanes(k), lanes(v), lanes(a),
                       per_chain(p['k_k']), per_chain(p['k_a']), per_chain(p['gn_g']),
                       per_chain(p['gn_b']), per_chain(p['r_k']), s0)
    y = jnp.transpose(y.reshape(T, N, B, H), (2, 0, 3, 1)).reshape(B * T, D_RWKV)
    S_T = jnp.transpose(s_T.reshape(N, N, B, H), (2, 3, 0, 1))

    x_new = mix_ln(y_a.reshape(B * T, D_LRU), y, g.reshape(B * T, D_RWKV), p['w_out'],
                   x.reshape(B * T, D), ln_g, ln_b)
    conv_state = jnp.concatenate([lru_conv, z[..., :D_LRU]], axis=1)[:, T:]
    return x_new.reshape(B, T, D), (h_T[:, 0], conv_state, S_T, z[:, -1:, 2 * D_LRU:])


def odd_layer(x, cache, p, ln_g, ln_b):
    B, T, D = x.shape
    dq = N_HEADS_C * HEAD_DIM_C
    dkv = N_KV_C * HEAD_DIM_C
    dqi = N_IDX_HEADS * D_IDX
    z = matmul(x.reshape(B * T, D), p['w_in']).reshape(B, T, -1)
    k = z[..., dq:dq + dkv]
    v = z[..., dq + dkv:dq + 2 * dkv]
    o = dq + 2 * dkv + dqi
    ki = z[..., o:o + D_IDX]
    wi = z[..., o + D_IDX:o + D_IDX + N_IDX_HEADS] * (dqi ** -0.5)
    Tp = -(-T // LANES) * LANES
    if cache is None:
        past = 0
        n_sel = min(TOPK_MAX, T // 4)
        parts = lambda c, n: [n]
    else:
        past = cache[0].shape[1]
        n_sel = min(TOPK_MAX, (past + T) // 4)
        parts = lambda c, n: [c.reshape(B, past, -1), n]
    L = -(-(past + Tp) // LANES) * LANES
    tk = _pick(L, (512, 384, 128))
    cat = lambda c, n: jnp.concatenate(
        parts(c, n) + [jnp.zeros((B, L - past - T, n.shape[-1]), n.dtype)], axis=1).astype(BF16)
    ck, cv, cki = cache if cache is not None else (None, None, None)
    keys, vals, kidx = cat(ck, k), cat(cv, v), cat(cki, ki)
    vals_t = jnp.swapaxes(vals.reshape(B, L // tk, tk, dkv), 2, 3)
    rowpad = lambda t: jnp.pad(t, ((0, 0), (0, Tp - T), (0, 0)))
    wi_t = jnp.swapaxes(rowpad(wi), 1, 2)
    o_att = dsa_attention(rowpad(z), 0, (dq + 2 * dkv) // dqi, wi_t, keys, vals_t, kidx, past, n_sel, tk)
    x_new = matmul_ln(o_att[:, :T].reshape(B * T, dq), p['w_out'], x.reshape(B * T, D), ln_g, ln_b)
    new = (k.reshape(B, T, N_KV_C, HEAD_DIM_C), v.reshape(B, T, N_KV_C, HEAD_DIM_C), ki)
    return x_new.reshape(B, T, D), new


def run_trunk(x, even_state, odd_cache, ev, od, norm_w, moe_w):
    ln1_g, ln1_b, ln2_g, ln2_b = norm_w
    router_w, router_b, wg, wu, wd = moe_w
    x, new_even = even_layer(x, *even_state, ev, ln1_g[0], ln1_b[0])
    x = moe_layer(x, router_w, router_b, 0, wg, wu, wd, ln2_g[0], ln2_b[0])
    x, new_odd = odd_layer(x, odd_cache, od, ln1_g[1], ln1_b[1])
    x = moe_layer(x, router_w, router_b, 1, wg, wu, wd, ln2_g[1], ln2_b[1])
    return x, new_even, new_odd


def kernel(x_prompt, x_sample, state_lru_h, state_lru_conv, state_rwkv_wkv, state_rwkv_shift, cache_k, cache_v, cache_kidx, ev_w_in, ev_conv_w, ev_conv_b, ev_gate_r_w, ev_gate_r_b, ev_gate_i_w, ev_gate_i_b, ev_lambda, ev_shift_mu, ev_w0, ev_w2, ev_a0, ev_a2, ev_g2, ev_k_k, ev_k_a, ev_r_k, ev_gn_g, ev_gn_b, ev_w_out, od_w_in, od_w_out, ln1_g, ln1_b, ln2_g, ln2_b, router_w, router_b, moe_w_gate, moe_w_up, moe_w_down):
    ev = dict(
        w_in=ev_w_in[0].astype(BF16), conv_w=ev_conv_w[0], conv_b=ev_conv_b[0][None],
        gate_w=jnp.concatenate([_block_diag(ev_gate_r_w[0]), _block_diag(ev_gate_i_w[0])], axis=1).astype(BF16),
        gate_b=jnp.concatenate([ev_gate_r_b[0], ev_gate_i_b[0]])[None],
        lru_c=(-LRU_C * jax.nn.softplus(-ev_lambda[0]))[None],
        shift_mu=ev_shift_mu[0][None], w0=ev_w0[0][None], a0=ev_a0[0][None],
        w2=ev_w2[0].astype(BF16), a2=ev_a2[0].astype(BF16), g2=ev_g2[0].astype(BF16),
        k_k=ev_k_k[0][None], k_a=ev_k_a[0][None], r_k=ev_r_k[0], gn_g=ev_gn_g[0], gn_b=ev_gn_b[0],
        w_out=ev_w_out[0].astype(BF16))
    n_in = od_w_in.shape[2]
    od = dict(w_in=jnp.pad(od_w_in[0], ((0, 0), (0, (-n_in) % LANES))).astype(BF16),
              w_out=od_w_out[0].astype(BF16))
    norm_w = (ln1_g, ln1_b, ln2_g, ln2_b)
    moe_w = (router_w, router_b, cast_bf16(moe_w_gate), cast_bf16(moe_w_up), cast_bf16(moe_w_down))

    B = x_prompt.shape[0]
    fresh = (jnp.zeros((B,) + state_lru_h.shape[2:], F32),
             jnp.zeros((B,) + state_lru_conv.shape[2:], F32),
             jnp.zeros((B,) + state_rwkv_wkv.shape[2:], F32),
             jnp.zeros((B,) + state_rwkv_shift.shape[2:], F32))
    y_p, pe, po = run_trunk(x_prompt, fresh, None, ev, od, norm_w, moe_w)
    y_s, se, so = run_trunk(
        x_sample, (state_lru_h[0], state_lru_conv[0], state_rwkv_wkv[0], state_rwkv_shift[0]),
        (cache_k[0], cache_v[0], cache_kidx[0]), ev, od, norm_w, moe_w)
    st = lambda t: t[None]
    return (y_p, y_s, *map(st, pe), *map(st, po), *map(st, se), *map(st, so))
```
